```python
import jax, jax.numpy as jnp
from jax import lax
import numpy as np

D_MODEL = 1024
BATCH = 8
SEQ = 4096
DEPTH = 1

HEAD_DIM = 64
SB_HEADS = 6
NSA_HEADS = 6
NSA_KV_GROUPS = 2
NSA_GROUP = NSA_HEADS // NSA_KV_GROUPS
MEM_HEADS = 4
MEM_TOKENS = 256
N_BRANCHES = 3
SB_W = SB_HEADS * HEAD_DIM
NSA_W = NSA_HEADS * HEAD_DIM
NSA_KV_W = NSA_KV_GROUPS * HEAD_DIM
MEM_W = MEM_HEADS * HEAD_DIM
IN_WIDTHS = (SB_W, SB_W, SB_W, NSA_W, 6 * NSA_KV_W, NSA_HEADS * 3, MEM_W, N_BRANCHES * D_MODEL)
IN_DIM = sum(IN_WIDTHS)
SB_Q_BLOCK = 128
NSA_Q_BLOCK = 64
CMP_LEN = 32
CMP_STRIDE = 16
SEL_BLOCK = 64
N_SELECT = 16
WINDOW = 512
FORCED_SCORE = 1e4
PEER_HEADS = 8
PEER_N_KEYS = 128
PEER_N_EXPERTS = PEER_N_KEYS * PEER_N_KEYS
PEER_QUERY_DIM = 256
PEER_TOPK = 16
PEER_T_BLOCK = 32
RMS_EPS = 1e-6
NEG_INF = -1e30

kernel_name = 'hybrid_sb_nsa_mem_peer_block'


def rmsnorm(x, g):
    xf = x.astype(jnp.float32)
    y = xf * lax.rsqrt(jnp.mean(xf * xf, axis=-1, keepdims=True) + RMS_EPS)
    return (y * g.astype(jnp.float32)).astype(x.dtype)


def alibi_slopes(n):
    return jnp.asarray([2.0 ** (-8.0 * (h + 1) / n) for h in range(n)], dtype=jnp.float32)


def split_offsets():
    return [int(o) for o in np.cumsum(IN_WIDTHS)[:-1]]


def stick_breaking_attention(q, k, v):
    B, H, T, dh = q.shape
    nblk = T // SB_Q_BLOCK
    scale = dh ** -0.5
    qb = q.reshape(B, H, nblk, SB_Q_BLOCK, dh).transpose(2, 0, 1, 3, 4)
    kpos = jnp.arange(T)

    def block(args):
        i, qi = args
        qpos = i * SB_Q_BLOCK + jnp.arange(SB_Q_BLOCK)
        z = jnp.einsum('bhqd,bhkd->bhqk', qi, k).astype(jnp.float32) * scale
        before = kpos[None, :] < qpos[:, None]
        log_keep = jnp.where(before, jax.nn.log_sigmoid(-z), 0.0)
        rev = lax.cumsum(log_keep, axis=3, reverse=True)
        excl = jnp.concatenate([rev[..., 1:], jnp.zeros_like(rev[..., :1])], axis=-1)
        a = jnp.where(before, jnp.exp(jax.nn.log_sigmoid(z) + excl), 0.0)
        return jnp.einsum('bhqk,bhkd->bhqd', a.astype(v.dtype), v)

    out = lax.map(block, (jnp.arange(nblk), qb))
    return out.transpose(1, 0, 3, 2, 4).reshape(B, T, H * dh)


def nsa_attention(q, k_cmp, v_cmp, k_slc, v_slc, k_win, v_win, gates, pe_k, pe_v, w_cmp_k, w_cmp_v, slopes):
    B, T, _, dh = q.shape
    G, R = NSA_KV_GROUPS, NSA_GROUP
    scale = dh ** -0.5

    def compress(z, pe, w):
        chunks = z.reshape(B, T // CMP_STRIDE, CMP_STRIDE, G, dh)
        blocks = jnp.concatenate([chunks[:, :-1], chunks[:, 1:]], axis=2)
        blocks = blocks + pe[None, None, :, None, :]
        return jnp.einsum('bnlgd,lde->bgne', blocks, w)

    kc = compress(k_cmp, pe_k, w_cmp_k)
    vc = compress(v_cmp, pe_v, w_cmp_v)
    n_cmp = T // CMP_STRIDE - 1
    cmp_end = jnp.arange(n_cmp) * CMP_STRIDE + CMP_LEN - 1
    n_blk = T // SEL_BLOCK
    n_sel = min(N_SELECT, n_blk)
    ks = k_slc.reshape(B, n_blk, SEL_BLOCK, G, dh).transpose(0, 3, 1, 2, 4)
    vs = v_slc.reshape(B, n_blk, SEL_BLOCK, G, dh).transpose(0, 3, 1, 2, 4)
    kw = jnp.pad(k_win, ((0, 0), (WINDOW, 0), (0, 0), (0, 0))).transpose(0, 2, 1, 3)
    vw = jnp.pad(v_win, ((0, 0), (WINDOW, 0), (0, 0), (0, 0))).transpose(0, 2, 1, 3)
    nq = T // NSA_Q_BLOCK
    qg = q.reshape(B, nq, NSA_Q_BLOCK, G, R, dh).transpose(1, 0, 3, 4, 2, 5)
    gb = gates.reshape(B, nq, NSA_Q_BLOCK, G, R, 3).transpose(1, 0, 3, 4, 2, 5)
    slope = slopes.reshape(G, R)[None, :, :, None, None]
    b_idx = jnp.arange(B)[:, None, None, None]
    g_idx = jnp.arange(G)[None, :, None, None]
    blk = jnp.arange(n_blk)
    offs = jnp.arange(SEL_BLOCK)

    def block(args):
        i, qi, gi = args
        t = i * NSA_Q_BLOCK + jnp.arange(NSA_Q_BLOCK)
        s_c = jnp.einsum('bgrqd,bgnd->bgrqn', qi, kc).astype(jnp.float32) * scale
        dist_c = (t[:, None] - cmp_end[None, :]).astype(jnp.float32)
        valid_c = dist_c >= 0
        s_c = jnp.where(valid_c, s_c - slope * dist_c, NEG_INF)
        p_c = jax.nn.softmax(s_c, axis=-1)
        p_c = jnp.where(valid_c.any(-1)[:, None], p_c, 0.0)
        o_c = jnp.einsum('bgrqn,bgnd->bgrqd', p_c.astype(vc.dtype), vc)
        imp = jnp.pad(p_c.sum(axis=2), ((0, 0), (0, 0), (0, 0), (0, 1)))
        imp = imp.reshape(B, G, NSA_Q_BLOCK, n_blk, SEL_BLOCK // CMP_STRIDE).sum(-1)
        cur = t // SEL_BLOCK
        forced = (blk[None, :] == 0) | (blk[None, :] == cur[:, None]) | (blk[None, :] == cur[:, None] - 1)
        imp = jnp.where(forced, FORCED_SCORE, jnp.where(blk[None, :] <= cur[:, None], imp, -1.0))
        _, idx = lax.top_k(imp, n_sel)
        k_g = ks[b_idx, g_idx, idx]
        v_g = vs[b_idx, g_idx, idx]
        pos = idx[..., None] * SEL_BLOCK + offs
        s_s = jnp.einsum('bgrqd,bgqnld->bgrqnl', qi, k_g).astype(jnp.float32) * scale
        dist_s = (t[:, None, None] - pos).astype(jnp.float32)[:, :, None]
        s_s = jnp.where(dist_s >= 0, s_s - slope[..., None] * dist_s, NEG_INF)
        p_s = jax.nn.softmax(s_s.reshape(B, G, R, NSA_Q_BLOCK, n_sel * SEL_BLOCK), axis=-1)
        p_s = p_s.reshape(B, G, R, NSA_Q_BLOCK, n_sel, SEL_BLOCK)
        o_s = jnp.einsum('bgrqnl,bgqnld->bgrqd', p_s.astype(v_g.dtype), v_g)
        kwi = lax.dynamic_slice_in_dim(kw, i * NSA_Q_BLOCK, WINDOW + NSA_Q_BLOCK, axis=2)
        vwi = lax.dynamic_slice_in_dim(vw, i * NSA_Q_BLOCK, WINDOW + NSA_Q_BLOCK, axis=2)
        kpos = i * NSA_Q_BLOCK - WINDOW + jnp.arange(WINDOW + NSA_Q_BLOCK)
        dist_w = t[:, None] - kpos[None, :]
        valid_w = (dist_w >= 0) & (dist_w < WINDOW) & (kpos[None, :] >= 0)
        s_w = jnp.einsum('bgrqd,bgkd->bgrqk', qi, kwi).astype(jnp.float32) * scale
        s_w = jnp.where(valid_w, s_w - slope * dist_w.astype(jnp.float32), NEG_INF)
        p_w = jax.nn.softmax(s_w, axis=-1)
        o_w = jnp.einsum('bgrqk,bgkd->bgrqd', p_w.astype(vwi.dtype), vwi)
        return gi[..., 0:1] * o_c + gi[..., 1:2] * o_s + gi[..., 2:3] * o_w

    out = lax.map(block, (jnp.arange(nq), qg, gb))
    return out.transpose(1, 0, 4, 2, 3, 5).reshape(B, T, NSA_HEADS * dh)


def memory_attention(q, mk, mv):
    B, T, H, dh = q.shape
    s = jnp.einsum('bthd,bmhd->bhtm', q, mk).astype(jnp.float32) * (dh ** -0.5)
    p = jax.nn.softmax(s, axis=-1)
    o = jnp.einsum('bhtm,bmhd->bthd', p.astype(mv.dtype), mv)
    return o.reshape(B, T, H * dh)


def peer_ffn(h, w_q, subkeys, u_tab, v_tab):
    B, T, D = h.shape
    nb = T // PEER_T_BLOCK
    half = PEER_QUERY_DIM // 2
    K = PEER_TOPK
    hb = h.reshape(B, nb, PEER_T_BLOCK, D).transpose(1, 0, 2, 3)

    def block(hi):
        q = (hi @ w_q).reshape(B, PEER_T_BLOCK, PEER_HEADS, 2, half)
        s = jnp.einsum('bthpd,hpkd->bthpk', q, subkeys).astype(jnp.float32)
        s_top, i_top = lax.top_k(s, K)
        cand = s_top[..., 0, :, None] + s_top[..., 1, None, :]
        cand_idx = i_top[..., 0, :, None] * PEER_N_KEYS + i_top[..., 1, None, :]
        best, pos = lax.top_k(cand.reshape(B, PEER_T_BLOCK, PEER_HEADS, K * K), K)
        expert = jnp.take_along_axis(cand_idx.reshape(B, PEER_T_BLOCK, PEER_HEADS, K * K), pos, axis=-1)
        g = jax.nn.softmax(best, axis=-1)
        a = jax.nn.gelu(jnp.einsum('btd,bthkd->bthk', hi, u_tab[expert]).astype(jnp.float32), approximate=False)
        w = (g * a).astype(v_tab.dtype)
        return jnp.einsum('bthk,bthkd->btd', w, v_tab[expert])

    out = lax.map(block, hb)
    return out.transpose(1, 0, 2, 3).reshape(B, T, D)


def setup_inputs(seed: int = 0) -> dict:
    key = jax.random.key(seed)
    ks = jax.random.split(key, 24)
    f32 = jnp.float32
    L, D = DEPTH, D_MODEL
    half = PEER_QUERY_DIM // 2

    def nrm(k, shape, scale):
        return jax.random.normal(k, shape, f32) * scale

    return {
        'x': nrm(ks[0], (BATCH, SEQ, D), 1.0),
        'mem': nrm(ks[1], (BATCH, MEM_TOKENS, D), 1.0),
        'mix_norm_g': 1.0 + nrm(ks[2], (L, D), 0.02),
        'mem_norm_g': 1.0 + nrm(ks[3], (L, D), 0.02),
        'w_in': nrm(ks[4], (L, D, IN_DIM), D ** -0.5),
        'b_merge': nrm(ks[5], (L, N_BRANCHES * D), 0.02),
        'cmp_pe_k': nrm(ks[6], (L, CMP_LEN, HEAD_DIM), 0.02),
        'cmp_pe_v': nrm(ks[7], (L, CMP_LEN, HEAD_DIM), 0.02),
        'cmp_w_k': nrm(ks[8], (L, CMP_LEN, HEAD_DIM, HEAD_DIM), (CMP_LEN * HEAD_DIM) ** -0.5),
        'cmp_w_v': nrm(ks[9], (L, CMP_LEN, HEAD_DIM, HEAD_DIM), (CMP_LEN * HEAD_DIM) ** -0.5),
        'w_mem_kv': nrm(ks[10], (L, D, 2 * MEM_W), D ** -0.5),
        'w_sb_br': nrm(ks[11], (L, SB_W, D), SB_W ** -0.5),
        'w_nsa_br': nrm(ks[12], (L, NSA_W, D), NSA_W ** -0.5),
        'w_mem_br': nrm(ks[13], (L, MEM_W, D), MEM_W ** -0.5),
        'w_out': nrm(ks[14], (L, D, D), D ** -0.5),
        'ffn_norm_g': 1.0 + nrm(ks[15], (L, D), 0.02),
        'peer_w_q': nrm(ks[16], (L, D, PEER_HEADS * PEER_QUERY_DIM), D ** -0.5),
        'peer_subkeys': nrm(ks[17], (L, PEER_HEADS, 2, PEER_N_KEYS, half), half ** -0.5),
        'peer_u': nrm(ks[18], (L, PEER_N_EXPERTS, D), D ** -0.5),
        'peer_v': nrm(ks[19], (L, PEER_N_EXPERTS, D), 0.3),
        'final_norm_g': 1.0 + nrm(ks[20], (D,), 0.02),
    }


def reference(x, mem, mix_norm_g, mem_norm_g, w_in, b_merge, cmp_pe_k, cmp_pe_v, cmp_w_k, cmp_w_v, w_mem_kv, w_sb_br, w_nsa_br, w_mem_br, w_out, ffn_norm_g, peer_w_q, peer_subkeys, peer_u, peer_v, final_norm_g):
    B, T, D = x.shape
    M = mem.shape[1]
    G = NSA_KV_GROUPS
    slopes = alibi_slopes(NSA_HEADS)
    offsets = split_offsets()
    for l in range(DEPTH):
        h = rmsnorm(x, mix_norm_g[l])
        proj = h @ w_in[l]
        sb_q, sb_k, sb_v, nsa_q, nsa_kv, nsa_g, mem_q, merge_g = jnp.split(proj, offsets, axis=-1)
        sbh = lambda z: z.reshape(B, T, SB_HEADS, HEAD_DIM).transpose(0, 2, 1, 3)
        sb_out = stick_breaking_attention(sbh(sb_q), sbh(sb_k), sbh(sb_v))
        kv = nsa_kv.reshape(B, T, 6, G, HEAD_DIM)
        nsa_out = nsa_attention(nsa_q.reshape(B, T, NSA_HEADS, HEAD_DIM), kv[:, :, 0], kv[:, :, 1], kv[:, :, 2], kv[:, :, 3], kv[:, :, 4], kv[:, :, 5], jax.nn.sigmoid(nsa_g.reshape(B, T, NSA_HEADS, 3)), cmp_pe_k[l], cmp_pe_v[l], cmp_w_k[l], cmp_w_v[l], slopes)
        mkv = (rmsnorm(mem, mem_norm_g[l]) @ w_mem_kv[l]).reshape(B, M, 2, MEM_HEADS, HEAD_DIM)
        mem_out = memory_attention(mem_q.reshape(B, T, MEM_HEADS, HEAD_DIM), mkv[:, :, 0], mkv[:, :, 1])
        gates = jax.nn.sigmoid(merge_g + b_merge[l]).reshape(B, T, N_BRANCHES, D)
        merged = gates[:, :, 0] * (sb_out @ w_sb_br[l]) + gates[:, :, 1] * (nsa_out @ w_nsa_br[l]) + gates[:, :, 2] * (mem_out @ w_mem_br[l])
        x = x + merged @ w_out[l]
        x = x + peer_ffn(rmsnorm(x, ffn_norm_g[l]), peer_w_q[l], peer_subkeys[l], peer_u[l], peer_v[l])
    return rmsnorm(x, final_norm_g)
```

```python
import functools
import math

import jax
import jax.numpy as jnp
from jax import lax
from jax.experimental import pallas as pl
from jax.experimental.pallas import tpu as pltpu

F32 = jnp.float32
BF16 = jnp.bfloat16
I32 = jnp.int32

HEAD_DIM = 64
SB_HEADS = 6
NSA_HEADS = 6
NSA_KV_GROUPS = 2
NSA_GROUP = NSA_HEADS // NSA_KV_GROUPS
MEM_HEADS = 4
N_BRANCHES = 3
SB_W = SB_HEADS * HEAD_DIM
NSA_W = NSA_HEADS * HEAD_DIM
NSA_KV_W = NSA_KV_GROUPS * HEAD_DIM
MEM_W = MEM_HEADS * HEAD_DIM
CMP_LEN = 32
CMP_STRIDE = 16
SEL_BLOCK = 64
N_SELECT = 16
WINDOW = 512
FORCED_SCORE = 1e4
PEER_HEADS = 8
PEER_N_KEYS = 128
PEER_QUERY_DIM = 256
PEER_TOPK = 16
RMS_EPS = 1e-6
NEG_INF = -1e30
SCALE = HEAD_DIM ** -0.5

LANES = 128
VMEM_LIMIT_BYTES = 56 * 1024 * 1024

_NT = (((1,), (1,)), ((), ()))


def _params(*sem):
    return pltpu.CompilerParams(dimension_semantics=sem, vmem_limit_bytes=VMEM_LIMIT_BYTES)


def _dot(a, b):
    return jnp.dot(a, b, preferred_element_type=F32)


def _dot_nt(a, b):
    return lax.dot_general(a, b, _NT, preferred_element_type=F32)


def _sigmoid(x):
    return 1.0 / (1.0 + jnp.exp(-x))


def _rms(x, g):
    return x * lax.rsqrt(jnp.mean(x * x, axis=-1, keepdims=True) + RMS_EPS) * g


def _norm_matmul_kernel(x_ref, g_ref, w_ref, o_ref, *, tn):
    h = _rms(x_ref[...], g_ref[...]).astype(BF16)
    for c in range(0, o_ref.shape[1], tn):
        o_ref[:, c:c + tn] = _dot(h, w_ref[:, c:c + tn]).astype(o_ref.dtype)


def _norm_matmul(x2d, g, w, out_dtype, tm, tn):
    n, d = x2d.shape
    m = w.shape[1]
    return pl.pallas_call(
        functools.partial(_norm_matmul_kernel, tn=tn),
        grid=(n // tm,),
        in_specs=[
            pl.BlockSpec((tm, d), lambda i: (i, 0)),
            pl.BlockSpec((1, d), lambda i: (0, 0)),
            pl.BlockSpec((d, m), lambda i: (0, 0)),
        ],
        out_specs=pl.BlockSpec((tm, m), lambda i: (i, 0)),
        out_shape=jax.ShapeDtypeStruct((n, m), out_dtype),
        compiler_params=_params("parallel"),
        name="norm_matmul",
    )(x2d, g.reshape(1, d), w)


def _sb_kernel(q_ref, k_ref, v_ref, o_ref, *, tq, tk):
    q0 = pl.program_id(2) * tq
    qpos = q0 + lax.broadcasted_iota(I32, (tq, 1), 0)
    row = lax.broadcasted_iota(I32, (tk, tk), 0)
    col = lax.broadcasted_iota(I32, (tk, tk), 1)
    later = (row > col).astype(BF16)
    nkt = (q0 + tq) // tk
    outs = []
    for hh in range(q_ref.shape[1]):
        q = q_ref[0, hh]

        def body(i, carry, hh=hh, q=q):
            c, acc = carry
            ks = pl.multiple_of((nkt - 1 - i) * tk, tk)
            k = k_ref[0, hh, pl.ds(ks, tk), :]
            v = v_ref[0, hh, pl.ds(ks, tk), :]
            kpos = ks + lax.broadcasted_iota(I32, (1, tk), 1)
            z = _dot_nt(q, k) * SCALE
            before = kpos < qpos
            softplus = jnp.maximum(z, 0.0) + jnp.log1p(jnp.exp(-jnp.abs(z)))
            log_keep = jnp.where(before, -softplus, 0.0)
            hi = log_keep.astype(BF16)
            lo = (log_keep - hi.astype(F32)).astype(BF16)
            excl = _dot(hi, later) + _dot(lo, later)
            a = jnp.where(before, jnp.exp(z + log_keep + excl + c), 0.0)
            acc = acc + _dot(a.astype(BF16), v)
            return c + jnp.sum(log_keep, axis=1, keepdims=True), acc

        init = (jnp.zeros((tq, 1), F32), jnp.zeros((tq, HEAD_DIM), F32))
        outs.append(lax.fori_loop(0, nkt, body, init)[1])
    o_ref[0] = jnp.concatenate(outs, axis=1).astype(o_ref.dtype)


def _sb_attention(q, k, v, tq=128, tk=128):
    b, h, t, dh = q.shape
    hp = 2
    return pl.pallas_call(
        functools.partial(_sb_kernel, tq=tq, tk=tk),
        grid=(b, h // hp, t // tq),
        in_specs=[
            pl.BlockSpec((1, hp, tq, dh), lambda bi, hi, qi: (bi, hi, qi, 0)),
            pl.BlockSpec((1, hp, t, dh), lambda bi, hi, qi: (bi, hi, 0, 0)),
            pl.BlockSpec((1, hp, t, dh), lambda bi, hi, qi: (bi, hi, 0, 0)),
        ],
        out_specs=pl.BlockSpec((1, tq, hp * dh), lambda bi, hi, qi: (bi, qi, hi)),
        out_shape=jax.ShapeDtypeStruct((b, t, h * dh), BF16),
        compiler_params=_params("parallel", "parallel", "arbitrary"),
        name="sb_attn",
    )(q, k, v)


def _compress_kernel(x_ref, pe_ref, w_ref, o_ref):
    x = x_ref[0, 0]
    nc = x.shape[0]
    w_lo, w_hi = w_ref[0], w_ref[1]
    first = _dot(x, w_lo)
    second = _dot(x, w_hi)
    feat = pe_ref.shape[1]
    pe_lo = jnp.broadcast_to(pe_ref[0:1, :], (8, feat)).astype(BF16)
    pe_hi = jnp.broadcast_to(pe_ref[1:2, :], (8, feat)).astype(BF16)
    bias = _dot(pe_lo, w_lo)[0:1] + _dot(pe_hi, w_hi)[0:1]
    o_ref[0, 0] = (first + pltpu.roll(second, nc - 1, 0) + bias).astype(o_ref.dtype)


def _nsa_compress(z, pe, w):
    b, g, t, dh = z.shape
    nc = t // CMP_STRIDE
    feat = CMP_STRIDE * dh
    x = z.reshape(b, g, nc, feat)
    pe2 = pe.reshape(2, feat)
    w2 = w.reshape(2, feat, dh).astype(BF16)
    return pl.pallas_call(
        _compress_kernel,
        grid=(b, g),
        in_specs=[
            pl.BlockSpec((1, 1, nc, feat), lambda bi, gi: (bi, gi, 0, 0)),
            pl.BlockSpec((2, feat), lambda bi, gi: (0, 0)),
            pl.BlockSpec((2, feat, dh), lambda bi, gi: (0, 0, 0)),
        ],
        out_specs=pl.BlockSpec((1, 1, nc, dh), lambda bi, gi: (bi, gi, 0, 0)),
        out_shape=jax.ShapeDtypeStruct((b, g, nc, dh), BF16),
        compiler_params=_params("parallel", "parallel"),
        name="nsa_compress",
    )(x, pe2, w2)


def _online_softmax_step(s, valid, v, m, l, acc):
    m_new = jnp.maximum(m, jnp.max(s, axis=1, keepdims=True))
    alpha = jnp.exp(m - m_new)
    p = jnp.where(valid, jnp.exp(s - m_new), 0.0)
    l = alpha * l + jnp.sum(p, axis=1, keepdims=True)
    acc = alpha * acc + _dot(p.astype(BF16), v)
    return m_new, l, acc


def _nsa_kernel(slopes_ref, q_ref, kc_ref, vc_ref, ks_ref, vs_ref, kw_ref, vw_ref, gl_ref, o_ref,
                *, tq, tk, n_sel):
    grp = pl.program_id(1)
    t0 = pl.program_id(2) * tq
    nc = kc_ref.shape[2]
    rr = NSA_GROUP
    t = t0 + lax.broadcasted_iota(I32, (tq, 1), 0)
    slopes = [slopes_ref[grp * rr + r] for r in range(rr)]

    lane = lax.broadcasted_iota(I32, (1, nc), 1)
    dist_c = (t - (lane * CMP_STRIDE + (CMP_LEN - 1))).astype(F32)
    valid_c = dist_c >= 0
    kc = kc_ref[0, 0]
    vc = vc_ref[0, 0]
    psum = jnp.zeros((tq, nc), F32)
    o_cmp = []
    for r in range(rr):
        s = _dot_nt(q_ref[0, r], kc) * SCALE
        s = jnp.where(valid_c, s - slopes[r] * dist_c, NEG_INF)
        p = jnp.where(valid_c, jnp.exp(s - jnp.max(s, axis=1, keepdims=True)), 0.0)
        denom = jnp.sum(p, axis=1, keepdims=True)
        p = p / jnp.where(denom > 0, denom, 1.0)
        psum = psum + p
        o_cmp.append(_dot(p.astype(BF16), vc))

    per_blk = SEL_BLOCK // CMP_STRIDE
    imp = psum + pltpu.roll(psum, nc - 1, 1)
    imp = imp + pltpu.roll(imp, nc - 2, 1)
    blk = lane // per_blk
    cur = t // SEL_BLOCK
    forced = (blk == 0) | (blk == cur) | (blk == cur - 1)
    imp = jnp.where(forced, FORCED_SCORE, jnp.where(blk <= cur, imp, -1.0))
    imp = jnp.where(lane % per_blk == 0, imp, -jnp.inf)
    lane_f = lane.astype(F32)

    def pick(_, carry):
        imp, sel = carry
        best = jnp.max(imp, axis=1, keepdims=True)
        first = jnp.min(jnp.where(imp == best, lane_f, float(nc)), axis=1, keepdims=True)
        hit = lane_f == first
        return jnp.where(hit, -jnp.inf, imp), jnp.where(hit, 1.0, sel)

    _, sel = lax.fori_loop(0, n_sel, pick, (imp, jnp.zeros((tq, nc), F32)))
    sel = sel.astype(BF16)

    crow = lax.broadcasted_iota(I32, (nc, tk), 0)
    scol = lax.broadcasted_iota(I32, (nc, tk), 1)
    kcol = lax.broadcasted_iota(I32, (1, tk), 1)
    nkt = (t0 + tq) // tk

    def init():
        return tuple((jnp.full((tq, 1), NEG_INF, F32), jnp.zeros((tq, 1), F32),
                      jnp.zeros((tq, HEAD_DIM), F32)) for _ in range(rr))

    def sel_body(kb, carry):
        ks0 = pl.multiple_of(kb * tk, tk)
        k = ks_ref[0, 0, pl.ds(ks0, tk), :]
        v = vs_ref[0, 0, pl.ds(ks0, tk), :]
        expand = (crow == ((ks0 + scol) // SEL_BLOCK) * per_blk).astype(BF16)
        chosen = _dot(sel, expand) > 0.5
        dist = (t - (ks0 + kcol)).astype(F32)
        valid = chosen & (dist >= 0)
        out = []
        for r in range(rr):
            s = _dot_nt(q_ref[0, r], k) * SCALE - slopes[r] * dist
            out.append(_online_softmax_step(jnp.where(valid, s, NEG_INF), valid, v, *carry[r]))
        return tuple(out)

    st_sel = lax.fori_loop(0, nkt, sel_body, init())

    def win_body(kb, carry):
        ks0 = pl.multiple_of(kb * tk, tk)
        k = kw_ref[0, 0, pl.ds(ks0, tk), :]
        v = vw_ref[0, 0, pl.ds(ks0, tk), :]
        dist = (t - (ks0 + kcol)).astype(F32)
        valid = (dist >= 0) & (dist < WINDOW)
        out = []
        for r in range(rr):
            s = _dot_nt(q_ref[0, r], k) * SCALE - slopes[r] * dist
            out.append(_online_softmax_step(jnp.where(valid, s, NEG_INF), valid, v, *carry[r]))
        return tuple(out)

    st_win = lax.fori_loop(jnp.maximum(t0 // tk - WINDOW // tk, 0), nkt, win_body, init())

    gates = _sigmoid(gl_ref[0, 0])
    outs = []
    for r in range(rr):
        o_sel = st_sel[r][2] / st_sel[r][1]
        o_win = st_win[r][2] / st_win[r][1]
        outs.append(gates[:, 3 * r:3 * r + 1] * o_cmp[r] + gates[:, 3 * r + 1:3 * r + 2] * o_sel
                    + gates[:, 3 * r + 2:3 * r + 3] * o_win)
    o_ref[0, 0] = jnp.concatenate(outs, axis=1).astype(o_ref.dtype)


def _nsa_attention(q, kc, vc, ks, vs, kw, vw, gate_logits, slopes, tq=128, tk=128):
    b, _, t, dh = q.shape
    g = NSA_KV_GROUPS
    nc = kc.shape[2]
    n_sel = min(N_SELECT, t // SEL_BLOCK)
    kv_spec = pl.BlockSpec((1, 1, t, dh), lambda bi, gi, qi: (bi, gi, 0, 0))
    c_spec = pl.BlockSpec((1, 1, nc, dh), lambda bi, gi, qi: (bi, gi, 0, 0))
    return pl.pallas_call(
        functools.partial(_nsa_kernel, tq=tq, tk=tk, n_sel=n_sel),
        grid=(b, g, t // tq),
        in_specs=[
            pl.BlockSpec(memory_space=pltpu.SMEM),
            pl.BlockSpec((1, NSA_GROUP, tq, dh), lambda bi, gi, qi: (bi, gi, qi, 0)),
            c_spec, c_spec, kv_spec, kv_spec, kv_spec, kv_spec,
            pl.BlockSpec((1, 1, tq, LANES), lambda bi, gi, qi: (bi, gi, qi, 0)),
        ],
        out_specs=pl.BlockSpec((1, 1, tq, NSA_GROUP * dh), lambda bi, gi, qi: (bi, gi, qi, 0)),
        out_shape=jax.ShapeDtypeStruct((b, g, t, NSA_GROUP * dh), BF16),
        compiler_params=_params("parallel", "parallel", "arbitrary"),
        name="nsa_attn",
    )(slopes, q, kc, vc, ks, vs, kw, vw, gate_logits)


def _mem_kernel(q_ref, kv_ref, o_ref):
    q = q_ref[0]
    kv = kv_ref[0]
    outs = []
    for h in range(MEM_HEADS):
        sl = slice(h * HEAD_DIM, (h + 1) * HEAD_DIM)
        s = _dot_nt(q[:, sl], kv[:, sl]) * SCALE
        p = jnp.exp(s - jnp.max(s, axis=1, keepdims=True))
        p = p / jnp.sum(p, axis=1, keepdims=True)
        outs.append(_dot(p.astype(BF16), kv[:, MEM_W + h * HEAD_DIM:MEM_W + (h + 1) * HEAD_DIM]))
    o_ref[0] = jnp.concatenate(outs, axis=1).astype(o_ref.dtype)


def _mem_attention(q, mkv, tq=512):
    b, t, w = q.shape
    m = mkv.shape[1]
    return pl.pallas_call(
        _mem_kernel,
        grid=(b, t // tq),
        in_specs=[
            pl.BlockSpec((1, tq, w), lambda bi, qi: (bi, qi, 0)),
            pl.BlockSpec((1, m, 2 * w), lambda bi, qi: (bi, 0, 0)),
        ],
        out_specs=pl.BlockSpec((1, tq, w), lambda bi, qi: (bi, qi, 0)),
        out_shape=jax.ShapeDtypeStruct((b, t, w), BF16),
        compiler_params=_params("parallel", "parallel"),
        name="mem_attn",
    )(q, mkv)


def _merge_kernel(sb_ref, nsa_ref, mem_ref, mg_ref, bm_ref, x_ref, wsb_ref, wnsa_ref, wmem_ref,
                  wout_ref, fg_ref, wq_ref, x1_ref, h2_ref, q_ref):
    d = x_ref.shape[1]
    branches = (_dot(sb_ref[...], wsb_ref[...]), _dot(nsa_ref[...], wnsa_ref[...]),
                _dot(mem_ref[...], wmem_ref[...]))
    merged = jnp.zeros_like(branches[0])
    for j in range(N_BRANCHES):
        gate = _sigmoid(mg_ref[:, j * d:(j + 1) * d] + bm_ref[:, j * d:(j + 1) * d])
        merged = merged + gate * branches[j]
    x1 = x_ref[...] + _dot(merged.astype(BF16), wout_ref[...])
    x1_ref[...] = x1
    h2 = _rms(x1, fg_ref[...])
    h2_ref[...] = h2
    q = _dot(h2.astype(BF16), wq_ref[...]).astype(q_ref.dtype)
    for c in range(q_ref.shape[0]):
        q_ref[c] = q[:, c * LANES:(c + 1) * LANES]


def _merge(sb, nsa, mem, gates_slab, b_merge, x2d, w_sb, w_nsa, w_mem, w_out, ffn_g, w_q, tm=256):
    n, d = x2d.shape
    nq = w_q.shape[1] // LANES
    row = lambda w: pl.BlockSpec((tm, w), lambda i: (i, 0))
    full = lambda a: pl.BlockSpec(a.shape, lambda i: (0,) * a.ndim)
    bm = b_merge.reshape(1, -1)
    fg = ffn_g.reshape(1, d)
    return pl.pallas_call(
        _merge_kernel,
        grid=(n // tm,),
        in_specs=[row(sb.shape[1]), row(nsa.shape[1]), row(mem.shape[1]), row(N_BRANCHES * d),
                  full(bm), row(d), full(w_sb), full(w_nsa), full(w_mem), full(w_out), full(fg),
                  full(w_q)],
        out_specs=[row(d), row(d), pl.BlockSpec((nq, tm, LANES), lambda i: (0, i, 0))],
        out_shape=[jax.ShapeDtypeStruct((n, d), F32), jax.ShapeDtypeStruct((n, d), F32),
                   jax.ShapeDtypeStruct((nq, n, LANES), BF16)],
        compiler_params=_params("parallel"),
        name="merge",
    )(sb, nsa, mem, gates_slab, bm, x2d, w_sb, w_nsa, w_mem, w_out, fg, w_q)


def _peer_topk_kernel(q_ref, sk_ref, idx_ref, gw_ref, tv_ref, ti_ref, bv_ref):
    tt = q_ref.shape[1]
    kk = PEER_TOPK
    nk = PEER_N_KEYS
    rid = lax.broadcasted_iota(I32, (nk, tt), 0).astype(F32)
    for c in range(2 * PEER_HEADS):
        scores = _dot_nt(sk_ref[c], q_ref[c])

        def pick(k, s, c=c):
            best = jnp.max(s, axis=0, keepdims=True)
            first = jnp.min(jnp.where(s == best, rid, float(nk)), axis=0, keepdims=True)
            tv_ref[c, pl.ds(k, 1), :] = best
            ti_ref[c, pl.ds(k, 1), :] = first
            return jnp.where(rid == first, -jnp.inf, s)

        lax.fori_loop(0, kk, pick, scores)

    pos = lax.broadcasted_iota(I32, (kk * kk, tt), 0).astype(F32)
    for h in range(PEER_HEADS):
        s0, s1 = tv_ref[2 * h], tv_ref[2 * h + 1]
        i0, i1 = ti_ref[2 * h], ti_ref[2 * h + 1]
        cand = jnp.concatenate([s0[i:i + 1, :] + s1 for i in range(kk)], axis=0)
        cidx = jnp.concatenate([i0[i:i + 1, :] * float(nk) + i1 for i in range(kk)], axis=0)

        def pick2(k, s, h=h, cidx=cidx):
            best = jnp.max(s, axis=0, keepdims=True)
            first = jnp.min(jnp.where(s == best, pos, float(kk * kk)), axis=0, keepdims=True)
            hit = pos == first
            bv_ref[pl.ds(k, 1), :] = best
            expert = jnp.max(jnp.where(hit, cidx, -1.0), axis=0, keepdims=True)
            idx_ref[pl.ds(h * kk + k, 1), :] = expert.astype(I32)
            return jnp.where(hit, -jnp.inf, s)

        lax.fori_loop(0, kk, pick2, cand)
        best = bv_ref[...]
        e = jnp.exp(best - best[0:1, :])
        gw_ref[h * kk:(h + 1) * kk, :] = e / jnp.sum(e, axis=0, keepdims=True)


def _peer_topk(q_chunks, subkeys, tt=128):
    nchunk, n, half = q_chunks.shape
    slots = PEER_HEADS * PEER_TOPK
    return pl.pallas_call(
        _peer_topk_kernel,
        grid=(n // tt,),
        in_specs=[
            pl.BlockSpec((nchunk, tt, half), lambda i: (0, i, 0)),
            pl.BlockSpec(subkeys.shape, lambda i: (0, 0, 0)),
        ],
        out_specs=[pl.BlockSpec((slots, tt), lambda i: (0, i)),
                   pl.BlockSpec((slots, tt), lambda i: (0, i))],
        out_shape=[jax.ShapeDtypeStruct((slots, n), I32), jax.ShapeDtypeStruct((slots, n), F32)],
        scratch_shapes=[pltpu.VMEM((nchunk, PEER_TOPK, tt), F32),
                        pltpu.VMEM((nchunk, PEER_TOPK, tt), F32),
                        pltpu.VMEM((PEER_TOPK, tt), F32)],
        compiler_params=_params("parallel"),
        name="peer_topk",
    )(q_chunks, subkeys)


def _peer_ffn_kernel(idx_hbm, uv_hbm, gw_ref, h_ref, x1_ref, fg_ref, y_ref, idx_smem, buf, idx_sem,
                     row_sem):
    tt, d = h_ref.shape
    slots = gw_ref.shape[0]
    step = pl.program_id(0)

    idx_copy = pltpu.make_async_copy(idx_hbm.at[step], idx_smem, idx_sem)
    idx_copy.start()
    idx_copy.wait()

    def row_copy(tok, e, slot):
        return pltpu.make_async_copy(uv_hbm.at[pl.ds(idx_smem[tok * slots + e], 1)],
                                     buf.at[slot, pl.ds(e, 1)], row_sem.at[slot])

    def start(tok, slot):
        for e in range(slots):
            row_copy(tok, e, slot).start()

    def wait(slot):
        pltpu.make_async_copy(uv_hbm.at[pl.ds(0, slots)], buf.at[slot], row_sem.at[slot]).wait()

    tok_lane = lax.broadcasted_iota(I32, (1, tt), 1)

    def compute(tok, slot):
        h = h_ref[pl.ds(tok, 1), :]
        a = jnp.sum(buf[slot, :, 0:d] * h, axis=1, keepdims=True)
        act = 0.5 * a * (1.0 + lax.erf(a * (2.0 ** -0.5)))
        gate = jnp.sum(jnp.where(tok_lane == tok, gw_ref[...], 0.0), axis=1, keepdims=True)
        y_ref[pl.ds(tok, 1), :] = jnp.sum(buf[slot, :, d:2 * d] * (gate * act), axis=0, keepdims=True)

    start(0, 0)
    start(1, 1)

    def pair(i, _):
        for slot in range(2):
            tok = 2 * i + slot
            wait(slot)
            compute(tok, slot)

            @pl.when(tok + 2 < tt)
            def _():
                start(tok + 2, slot)
        return 0

    lax.fori_loop(0, tt // 2, pair, 0)
    y_ref[...] = _rms(x1_ref[...] + y_ref[...], fg_ref[...])


def _peer_ffn(idx_tiles, uv, gw, h2, x1, final_g, tt):
    n, d = h2.shape
    slots = gw.shape[0]
    return pl.pallas_call(
        _peer_ffn_kernel,
        grid=(n // tt,),
        in_specs=[
            pl.BlockSpec(memory_space=pl.ANY),
            pl.BlockSpec(memory_space=pl.ANY),
            pl.BlockSpec((slots, tt), lambda i: (0, i)),
            pl.BlockSpec((tt, d), lambda i: (i, 0)),
            pl.BlockSpec((tt, d), lambda i: (i, 0)),
            pl.BlockSpec((1, d), lambda i: (0, 0)),
        ],
        out_specs=pl.BlockSpec((tt, d), lambda i: (i, 0)),
        out_shape=jax.ShapeDtypeStruct((n, d), F32),
        scratch_shapes=[pltpu.SMEM((tt * slots,), I32), pltpu.VMEM((2, slots, 2 * d), F32),
                        pltpu.SemaphoreType.DMA, pltpu.SemaphoreType.DMA((2,))],
        compiler_params=_params("arbitrary"),
        name="peer_ffn",
    )(idx_tiles, uv, gw, h2, x1, final_g.reshape(1, d))


def _layer(x, mem, mix_g, mem_g, w_in, b_merge, pe_k, pe_v, cw_k, cw_v, w_mem_kv, w_sb_br, w_nsa_br,
           w_mem_br, w_out, ffn_g, peer_w_q, subkeys, uv, out_g):
    b, t, d = x.shape
    m = mem.shape[1]
    g = NSA_KV_GROUPS
    n = b * t
    x2d = x.reshape(n, d)

    o_sbq, o_sbk, o_sbv = 0, SB_W, 2 * SB_W
    o_nq = 3 * SB_W
    o_nkv = o_nq + NSA_W
    o_ng = o_nkv + 6 * NSA_KV_W
    o_mq = o_ng + NSA_HEADS * 3
    o_mg = o_mq + MEM_W
    w_act = jnp.concatenate([w_in[:, :o_ng], w_in[:, o_mq:o_mg]], axis=1).astype(BF16)
    gate_pad = LANES - NSA_HEADS * 3
    w_gate = jnp.concatenate([w_in[:, o_mg:], w_in[:, o_ng:o_mq], jnp.zeros((d, gate_pad), w_in.dtype)],
                             axis=1).astype(BF16)
    act = _norm_matmul(x2d, mix_g, w_act, BF16, tm=256, tn=512)
    gates_slab = _norm_matmul(x2d, mix_g, w_gate, F32, tm=256, tn=640)

    def heads(lo, nh):
        return act[:, lo:lo + nh * HEAD_DIM].reshape(b, t, nh, HEAD_DIM).transpose(0, 2, 1, 3)

    sb_out = _sb_attention(heads(o_sbq, SB_HEADS), heads(o_sbk, SB_HEADS), heads(o_sbv, SB_HEADS))

    kv = [heads(o_nkv + j * NSA_KV_W, g) for j in range(6)]
    kc = _nsa_compress(kv[0], pe_k, cw_k)
    vc = _nsa_compress(kv[1], pe_v, cw_v)
    gl = gates_slab[:, N_BRANCHES * d:N_BRANCHES * d + NSA_HEADS * 3]
    gl = gl.reshape(b, t, g, NSA_GROUP * 3).transpose(0, 2, 1, 3)
    gl = jnp.pad(gl, ((0, 0), (0, 0), (0, 0), (0, LANES - NSA_GROUP * 3)))
    slopes = jnp.asarray([2.0 ** (-8.0 * (h + 1) / NSA_HEADS) for h in range(NSA_HEADS)], F32)
    nsa_out = _nsa_attention(heads(o_nq, NSA_HEADS), kc, vc, kv[2], kv[3], kv[4], kv[5], gl, slopes)
    nsa_out = nsa_out.transpose(0, 2, 1, 3).reshape(n, NSA_W)

    mkv = _norm_matmul(mem.reshape(b * m, d), mem_g, w_mem_kv.astype(BF16), BF16, tm=256, tn=512)
    mem_q = act[:, o_nkv + 6 * NSA_KV_W:].reshape(b, t, MEM_W)
    mem_out = _mem_attention(mem_q, mkv.reshape(b, m, 2 * MEM_W))

    x1, h2, q_chunks = _merge(
        sb_out.reshape(n, SB_W), nsa_out, mem_out.reshape(n, MEM_W), gates_slab, b_merge, x2d,
        w_sb_br.astype(BF16), w_nsa_br.astype(BF16), w_mem_br.astype(BF16), w_out.astype(BF16), ffn_g,
        peer_w_q.astype(BF16))

    half = PEER_QUERY_DIM // 2
    idx, gw = _peer_topk(q_chunks, subkeys.reshape(2 * PEER_HEADS, PEER_N_KEYS, half).astype(BF16))
    tt = 128
    idx_tiles = idx.T.reshape(n // tt, tt * idx.shape[0])
    y = _peer_ffn(idx_tiles, uv, gw, h2, x1, out_g, tt)
    return y.reshape(b, t, d)


def kernel(x, mem, mix_norm_g, mem_norm_g, w_in, b_merge, cmp_pe_k, cmp_pe_v, cmp_w_k, cmp_w_v, w_mem_kv, w_sb_br, w_nsa_br, w_mem_br, w_out, ffn_norm_g, peer_w_q, peer_subkeys, peer_u, peer_v, final_norm_g):
    depth = w_in.shape[0]
    assert depth == 1, "the final rmsnorm is fused into the last layer's PEER kernel"
    l = 0
    uv = jnp.concatenate([peer_u[l], peer_v[l]], axis=1)
    return _layer(x, mem, mix_norm_g[l], mem_norm_g[l], w_in[l], b_merge[l], cmp_pe_k[l], cmp_pe_v[l],
                  cmp_w_k[l], cmp_w_v[l], w_mem_kv[l], w_sb_br[l], w_nsa_br[l], w_mem_br[l], w_out[l],
                  ffn_norm_g[l], peer_w_q[l], peer_subkeys[l], uv, final_norm_g)
```

```python
import functools
import math

import jax
import jax.numpy as jnp
from jax import lax
from jax.experimental import pallas as pl
from jax.experimental.pallas import tpu as pltpu

F32 = jnp.float32
BF16 = jnp.bfloat16
I32 = jnp.int32

HEAD_DIM = 64
SB_HEADS = 6
NSA_HEADS = 6
NSA_KV_GROUPS = 2
NSA_GROUP = NSA_HEADS // NSA_KV_GROUPS
MEM_HEADS = 4
N_BRANCHES = 3
SB_W = SB_HEADS * HEAD_DIM
NSA_W = NSA_HEADS * HEAD_DIM
NSA_KV_W = NSA_KV_GROUPS * HEAD_DIM
MEM_W = MEM_HEADS * HEAD_DIM
CMP_LEN = 32
CMP_STRIDE = 16
SEL_BLOCK = 64
N_SELECT = 16
WINDOW = 512
FORCED_SCORE = 1e4
PEER_HEADS = 8
PEER_N_KEYS = 128
PEER_QUERY_DIM = 256
PEER_TOPK = 16
RMS_EPS = 1e-6
NEG_INF = -1e30
SCALE = HEAD_DIM ** -0.5

LANES = 128
PEER_GATHER_BUFFERS = 8
VMEM_LIMIT_BYTES = 56 * 1024 * 1024

_NT = (((1,), (1,)), ((), ()))


def _params(*sem):
    return pltpu.CompilerParams(dimension_semantics=sem, vmem_limit_bytes=VMEM_LIMIT_BYTES)


def _dot(a, b):
    return jnp.dot(a, b, preferred_element_type=F32)


def _dot_nt(a, b):
    return lax.dot_general(a, b, _NT, preferred_element_type=F32)


def _sigmoid(x):
    return 1.0 / (1.0 + jnp.exp(-x))


def _rms(x, g):
    return x * lax.rsqrt(jnp.mean(x * x, axis=-1, keepdims=True) + RMS_EPS) * g


def _norm_matmul_kernel(x_ref, g_ref, w_ref, o_ref, *, tn):
    h = _rms(x_ref[...], g_ref[...]).astype(BF16)
    for c in range(0, o_ref.shape[1], tn):
        o_ref[:, c:c + tn] = _dot(h, w_ref[:, c:c + tn]).astype(o_ref.dtype)


def _norm_matmul(x2d, g, w, out_dtype, tm, tn):
    n, d = x2d.shape
    m = w.shape[1]
    return pl.pallas_call(
        functools.partial(_norm_matmul_kernel, tn=tn),
        grid=(n // tm,),
        in_specs=[
            pl.BlockSpec((tm, d), lambda i: (i, 0)),
            pl.BlockSpec((1, d), lambda i: (0, 0)),
            pl.BlockSpec((d, m), lambda i: (0, 0)),
        ],
        out_specs=pl.BlockSpec((tm, m), lambda i: (i, 0)),
        out_shape=jax.ShapeDtypeStruct((n, m), out_dtype),
        compiler_params=_params("parallel"),
        name="norm_matmul",
    )(x2d, g.reshape(1, d), w)


def _sb_kernel(q_ref, k_ref, v_ref, o_ref, *, tile):
    qi = pl.program_id(2)
    nh = q_ref.shape[1]
    row = lax.broadcasted_iota(I32, (tile, tile), 0)
    col = lax.broadcasted_iota(I32, (tile, tile), 1)
    lower = row > col
    later = lower.astype(BF16)
    qs = [q_ref[0, hh] for hh in range(nh)]

    def visit(hh, ks, c, acc, diagonal):
        k = k_ref[0, hh, pl.ds(ks, tile), :]
        v = v_ref[0, hh, pl.ds(ks, tile), :]
        z = _dot_nt(qs[hh], k) * SCALE
        sp = jnp.maximum(z, 0.0) + jnp.log(1.0 + jnp.exp(-jnp.abs(z)))
        if diagonal:
            sp = jnp.where(lower, sp, 0.0)
        hi = sp.astype(BF16)
        lo = (sp - hi.astype(F32)).astype(BF16)
        after = _dot(hi, later) + _dot(lo, later)
        a = jnp.exp(z - sp - after - c)
        if diagonal:
            a = jnp.where(lower, a, 0.0)
        return c + jnp.sum(sp, axis=1, keepdims=True), acc + _dot(a.astype(BF16), v)

    q0 = pl.multiple_of(qi * tile, tile)
    state = []
    for hh in range(nh):
        state.extend(visit(hh, q0, jnp.zeros((tile, 1), F32), jnp.zeros((tile, HEAD_DIM), F32), True))

    def body(i, carry):
        ks = pl.multiple_of((qi - 1 - i) * tile, tile)
        out = []
        for hh in range(nh):
            out.extend(visit(hh, ks, carry[2 * hh], carry[2 * hh + 1], False))
        return tuple(out)

    state = lax.fori_loop(0, qi, body, tuple(state))
    o_ref[0] = jnp.concatenate([state[2 * hh + 1] for hh in range(nh)], axis=1).astype(o_ref.dtype)


def _sb_attention(q, k, v, tile=256):
    b, h, t, dh = q.shape
    hp = 2
    tq = tile
    return pl.pallas_call(
        functools.partial(_sb_kernel, tile=tile),
        grid=(b, h // hp, t // tq),
        in_specs=[
            pl.BlockSpec((1, hp, tq, dh), lambda bi, hi, qi: (bi, hi, qi, 0)),
            pl.BlockSpec((1, hp, t, dh), lambda bi, hi, qi: (bi, hi, 0, 0)),
            pl.BlockSpec((1, hp, t, dh), lambda bi, hi, qi: (bi, hi, 0, 0)),
        ],
        out_specs=pl.BlockSpec((1, tq, hp * dh), lambda bi, hi, qi: (bi, qi, hi)),
        out_shape=jax.ShapeDtypeStruct((b, t, h * dh), BF16),
        compiler_params=_params("parallel", "parallel", "arbitrary"),
        name="sb_attn",
    )(q, k, v)


def _compress_kernel(x_ref, pe_ref, w_ref, o_ref):
    x = x_ref[0, 0]
    nc = x.shape[0]
    w_lo, w_hi = w_ref[0], w_ref[1]
    first = _dot(x, w_lo)
    second = _dot(x, w_hi)
    feat = pe_ref.shape[1]
    pe_lo = jnp.broadcast_to(pe_ref[0:1, :], (8, feat)).astype(BF16)
    pe_hi = jnp.broadcast_to(pe_ref[1:2, :], (8, feat)).astype(BF16)
    bias = _dot(pe_lo, w_lo)[0:1] + _dot(pe_hi, w_hi)[0:1]
    o_ref[0, 0] = (first + pltpu.roll(second, nc - 1, 0) + bias).astype(o_ref.dtype)


def _nsa_compress(z, pe, w):
    b, g, t, dh = z.shape
    nc = t // CMP_STRIDE
    feat = CMP_STRIDE * dh
    x = z.reshape(b, g, nc, feat)
    pe2 = pe.reshape(2, feat)
    w2 = w.reshape(2, feat, dh).astype(BF16)
    return pl.pallas_call(
        _compress_kernel,
        grid=(b, g),
        in_specs=[
            pl.BlockSpec((1, 1, nc, feat), lambda bi, gi: (bi, gi, 0, 0)),
            pl.BlockSpec((2, feat), lambda bi, gi: (0, 0)),
            pl.BlockSpec((2, feat, dh), lambda bi, gi: (0, 0, 0)),
        ],
        out_specs=pl.BlockSpec((1, 1, nc, dh), lambda bi, gi: (bi, gi, 0, 0)),
        out_shape=jax.ShapeDtypeStruct((b, g, nc, dh), BF16),
        compiler_params=_params("parallel", "parallel"),
        name="nsa_compress",
    )(x, pe2, w2)


def _nsa_kernel(slopes_ref, q_ref, kc_ref, vc_ref, ks_ref, vs_ref, kw_ref, vw_ref, gl_ref, pool_ref, ex_ref,
                o_ref, *, tq, tk, n_sel, real_blocks):
    grp = pl.program_id(1)
    t0 = pl.program_id(2) * tq
    nc = kc_ref.shape[2]
    rr = NSA_GROUP
    t = t0 + lax.broadcasted_iota(I32, (tq, 1), 0)

    def stack(x):
        return jnp.concatenate([x] * rr, axis=0)

    q = q_ref[0].reshape(rr * tq, HEAD_DIM)
    slope = jnp.concatenate([jnp.full((tq, 1), slopes_ref[grp * rr + r], F32) for r in range(rr)], axis=0)

    def heads_sum(x):
        out = x[0:tq]
        for r in range(1, rr):
            out = out + x[r * tq:(r + 1) * tq]
        return out

    def masked_softmax(s, valid):
        p = jnp.where(valid, jnp.exp(s - jnp.max(s, axis=1, keepdims=True)), 0.0)
        denom = jnp.sum(p, axis=1, keepdims=True)
        return p / jnp.where(denom > 0, denom, 1.0)

    lane = lax.broadcasted_iota(I32, (1, nc), 1)
    dist_c = (t - (lane * CMP_STRIDE + (CMP_LEN - 1))).astype(F32)
    valid_c = stack(dist_c >= 0)
    s = _dot_nt(q, kc_ref[0, 0]) * SCALE - slope * stack(dist_c)
    p = masked_softmax(jnp.where(valid_c, s, NEG_INF), valid_c)
    o_cmp = _dot(p.astype(BF16), vc_ref[0, 0])
    psum = heads_sum(p)

    n_blk = pool_ref.shape[0]
    hi = psum.astype(BF16)
    rest = psum - hi.astype(F32)
    mid = rest.astype(BF16)
    lo = (rest - mid.astype(F32)).astype(BF16)
    pool = pool_ref[...]
    imp = _dot_nt(pool, hi) + _dot_nt(pool, mid) + _dot_nt(pool, lo)
    blk = lax.broadcasted_iota(I32, (n_blk, 1), 0)
    cur = (t0 + lax.broadcasted_iota(I32, (1, tq), 1)) // SEL_BLOCK
    forced = (blk == 0) | (blk == cur) | (blk == cur - 1)
    imp = jnp.where(forced, FORCED_SCORE, jnp.where(blk <= cur, imp, -1.0))
    imp = jnp.where(blk < real_blocks, imp, -jnp.inf)
    blk_f = blk.astype(F32)

    def pick(_, carry):
        imp, sel = carry
        best = jnp.max(imp, axis=0, keepdims=True)
        first = jnp.min(jnp.where(imp == best, blk_f, float(n_blk)), axis=0, keepdims=True)
        hit = blk_f == first
        return jnp.where(hit, -jnp.inf, imp), jnp.where(hit, 1.0, sel)

    _, sel = lax.fori_loop(0, n_sel, pick, (imp, jnp.zeros((n_blk, tq), F32)))
    sel = sel.T.astype(BF16)

    span = WINDOW + tq
    w0 = pl.multiple_of(jnp.maximum(t0 - WINDOW, 0), tq)
    dist_w = t - (w0 + lax.broadcasted_iota(I32, (1, span), 1))
    valid_w = stack((dist_w >= 0) & (dist_w < WINDOW))
    s = _dot_nt(q, kw_ref[0, 0, pl.ds(w0, span), :]) * SCALE - slope * stack(dist_w.astype(F32))
    p = masked_softmax(jnp.where(valid_w, s, NEG_INF), valid_w)
    o_win = _dot(p.astype(BF16), vw_ref[0, 0, pl.ds(w0, span), :])

    kcol = lax.broadcasted_iota(I32, (1, tk), 1)

    def sel_body(kb, carry):
        m, l, acc = carry
        ks0 = pl.multiple_of(kb * tk, tk)
        chosen = _dot(sel, ex_ref[kb]) > 0.5
        dist = t - (ks0 + kcol)
        valid = stack(chosen & (dist >= 0))
        s = _dot_nt(q, ks_ref[0, 0, pl.ds(ks0, tk), :]) * SCALE - slope * stack(dist.astype(F32))
        s = jnp.where(valid, s, NEG_INF)
        m_new = jnp.maximum(m, jnp.max(s, axis=1, keepdims=True))
        alpha = jnp.exp(m - m_new)
        p = jnp.where(valid, jnp.exp(s - m_new), 0.0)
        l = alpha * l + jnp.sum(p, axis=1, keepdims=True)
        acc = alpha * acc + _dot(p.astype(BF16), vs_ref[0, 0, pl.ds(ks0, tk), :])
        return m_new, l, acc

    init = (jnp.full((rr * tq, 1), NEG_INF, F32), jnp.zeros((rr * tq, 1), F32),
            jnp.zeros((rr * tq, HEAD_DIM), F32))
    _, l_sel, acc_sel = lax.fori_loop(0, (t0 + tq + tk - 1) // tk, sel_body, init)
    o_sel = acc_sel / l_sel

    gates = _sigmoid(gl_ref[0, 0])
    outs = []
    for r in range(rr):
        rows = slice(r * tq, (r + 1) * tq)
        outs.append(gates[:, 3 * r:3 * r + 1] * o_cmp[rows] + gates[:, 3 * r + 1:3 * r + 2] * o_sel[rows]
                    + gates[:, 3 * r + 2:3 * r + 3] * o_win[rows])
    o_ref[0, 0] = jnp.concatenate(outs, axis=1).astype(o_ref.dtype)


def _nsa_attention(q, kc, vc, ks, vs, kw, vw, gate_logits, slopes, tq=128, tk=256):
    b, _, t, dh = q.shape
    g = NSA_KV_GROUPS
    nc = kc.shape[2]
    n_sel = min(N_SELECT, t // SEL_BLOCK)
    real_blocks = t // SEL_BLOCK
    n_blk = LANES
    assert t % tk == 0 and t >= WINDOW + tq and real_blocks <= n_blk
    blk_ids = jnp.arange(n_blk, dtype=I32)
    pool = (blk_ids[:, None] == jnp.arange(nc, dtype=I32)[None, :] // (SEL_BLOCK // CMP_STRIDE)).astype(BF16)
    key_blk = (jnp.arange(t, dtype=I32) // SEL_BLOCK).reshape(t // tk, 1, tk)
    expand = (blk_ids[None, :, None] == key_blk).astype(BF16)
    kv_spec = pl.BlockSpec((1, 1, t, dh), lambda bi, gi, qi: (bi, gi, 0, 0))
    c_spec = pl.BlockSpec((1, 1, nc, dh), lambda bi, gi, qi: (bi, gi, 0, 0))
    return pl.pallas_call(
        functools.partial(_nsa_kernel, tq=tq, tk=tk, n_sel=n_sel, real_blocks=real_blocks),
        grid=(b, g, t // tq),
        in_specs=[
            pl.BlockSpec(memory_space=pltpu.SMEM),
            pl.BlockSpec((1, NSA_GROUP, tq, dh), lambda bi, gi, qi: (bi, gi, qi, 0)),
            c_spec, c_spec, kv_spec, kv_spec, kv_spec, kv_spec,
            pl.BlockSpec((1, 1, tq, LANES), lambda bi, gi, qi: (bi, gi, qi, 0)),
            pl.BlockSpec(pool.shape, lambda bi, gi, qi: (0, 0)),
            pl.BlockSpec(expand.shape, lambda bi, gi, qi: (0, 0, 0)),
        ],
        out_specs=pl.BlockSpec((1, 1, tq, NSA_GROUP * dh), lambda bi, gi, qi: (bi, gi, qi, 0)),
        out_shape=jax.ShapeDtypeStruct((b, g, t, NSA_GROUP * dh), BF16),
        compiler_params=_params("parallel", "parallel", "arbitrary"),
        name="nsa_attn",
    )(slopes, q, kc, vc, ks, vs, kw, vw, gate_logits, pool, expand)


def _mem_kernel(q_ref, kv_ref, o_ref):
    q = q_ref[0]
    kv = kv_ref[0]
    outs = []
    for h in range(MEM_HEADS):
        sl = slice(h * HEAD_DIM, (h + 1) * HEAD_DIM)
        s = _dot_nt(q[:, sl], kv[:, sl]) * SCALE
        p = jnp.exp(s - jnp.max(s, axis=1, keepdims=True))
        p = p / jnp.sum(p, axis=1, keepdims=True)
        outs.append(_dot(p.astype(BF16), kv[:, MEM_W + h * HEAD_DIM:MEM_W + (h + 1) * HEAD_DIM]))
    o_ref[0] = jnp.concatenate(outs, axis=1).astype(o_ref.dtype)


def _mem_attention(q, mkv, tq=512):
    b, t, w = q.shape
    m = mkv.shape[1]
    return pl.pallas_call(
        _mem_kernel,
        grid=(b, t // tq),
        in_specs=[
            pl.BlockSpec((1, tq, w), lambda bi, qi: (bi, qi, 0)),
            pl.BlockSpec((1, m, 2 * w), lambda bi, qi: (bi, 0, 0)),
        ],
        out_specs=pl.BlockSpec((1, tq, w), lambda bi, qi: (bi, qi, 0)),
        out_shape=jax.ShapeDtypeStruct((b, t, w), BF16),
        compiler_params=_params("parallel", "parallel"),
        name="mem_attn",
    )(q, mkv)


def _merge_kernel(sb_ref, nsa_ref, mem_ref, mg_ref, bm_ref, x_ref, wsb_ref, wnsa_ref, wmem_ref,
                  wout_ref, fg_ref, wq_ref, x1_ref, h2_ref, q_ref):
    d = x_ref.shape[1]
    branches = (_dot(sb_ref[...], wsb_ref[...]), _dot(nsa_ref[...], wnsa_ref[...]),
                _dot(mem_ref[...], wmem_ref[...]))
    merged = jnp.zeros_like(branches[0])
    for j in range(N_BRANCHES):
        gate = _sigmoid(mg_ref[:, j * d:(j + 1) * d] + bm_ref[:, j * d:(j + 1) * d])
        merged = merged + gate * branches[j]
    x1 = x_ref[...] + _dot(merged.astype(BF16), wout_ref[...])
    x1_ref[...] = x1
    h2 = _rms(x1, fg_ref[...])
    h2_ref[...] = h2
    q = _dot(h2.astype(BF16), wq_ref[...]).astype(q_ref.dtype)
    for c in range(q_ref.shape[0]):
        q_ref[c] = q[:, c * LANES:(c + 1) * LANES]


def _merge(sb, nsa, mem, gates_slab, b_merge, x2d, w_sb, w_nsa, w_mem, w_out, ffn_g, w_q, tm=256):
    n, d = x2d.shape
    nq = w_q.shape[1] // LANES
    row = lambda w: pl.BlockSpec((tm, w), lambda i: (i, 0))
    full = lambda a: pl.BlockSpec(a.shape, lambda i: (0,) * a.ndim)
    bm = b_merge.reshape(1, -1)
    fg = ffn_g.reshape(1, d)
    return pl.pallas_call(
        _merge_kernel,
        grid=(n // tm,),
        in_specs=[row(sb.shape[1]), row(nsa.shape[1]), row(mem.shape[1]), row(N_BRANCHES * d),
                  full(bm), row(d), full(w_sb), full(w_nsa), full(w_mem), full(w_out), full(fg),
                  full(w_q)],
        out_specs=[row(d), row(d), pl.BlockSpec((nq, tm, LANES), lambda i: (0, i, 0))],
        out_shape=[jax.ShapeDtypeStruct((n, d), F32), jax.ShapeDtypeStruct((n, d), F32),
                   jax.ShapeDtypeStruct((nq, n, LANES), BF16)],
        compiler_params=_params("parallel"),
        name="merge",
    )(sb, nsa, mem, gates_slab, bm, x2d, w_sb, w_nsa, w_mem, w_out, fg, w_q)


def _peer_topk_kernel(q_ref, sk_ref, idx_ref, gw_ref, tv_ref, ti_ref, bv_ref):
    tt = q_ref.shape[1]
    kk = PEER_TOPK
    nk = PEER_N_KEYS
    rid = lax.broadcasted_iota(I32, (nk, tt), 0).astype(F32)
    for c in range(2 * PEER_HEADS):
        scores = _dot_nt(sk_ref[c], q_ref[c])

        def pick(k, s, c=c):
            best = jnp.max(s, axis=0, keepdims=True)
            first = jnp.min(jnp.where(s == best, rid, float(nk)), axis=0, keepdims=True)
            tv_ref[c, pl.ds(k, 1), :] = best
            ti_ref[c, pl.ds(k, 1), :] = first
            return jnp.where(rid == first, -jnp.inf, s)

        lax.fori_loop(0, kk, pick, scores)

    sub = 8
    widths = [min(kk, -(-(kk // (i + 1)) // sub) * sub) for i in range(kk // 2)]
    n_cand = sum(widths) + kk // 2
    pos = lax.broadcasted_iota(I32, (n_cand, tt), 0).astype(F32)
    for h in range(PEER_HEADS):
        s0, s1 = tv_ref[2 * h], tv_ref[2 * h + 1]
        i0, i1 = ti_ref[2 * h], ti_ref[2 * h + 1]
        cand, cidx = [], []
        for i, wd in enumerate(widths):
            keep = lax.broadcasted_iota(I32, (wd, 1), 0) < kk // (i + 1)
            cand.append(jnp.where(keep, s0[i:i + 1, :] + s1[0:wd, :], -jnp.inf))
            cidx.append(i0[i:i + 1, :] * float(nk) + i1[0:wd, :])
        cand.append(s0[kk // 2:, :] + s1[0:1, :])
        cidx.append(i0[kk // 2:, :] * float(nk) + i1[0:1, :])
        cand = jnp.concatenate(cand, axis=0)
        cidx = jnp.concatenate(cidx, axis=0)

        def pick2(k, s, h=h, cidx=cidx):
            best = jnp.max(s, axis=0, keepdims=True)
            first = jnp.min(jnp.where(s == best, pos, float(n_cand)), axis=0, keepdims=True)
            hit = pos == first
            bv_ref[pl.ds(k, 1), :] = best
            expert = jnp.max(jnp.where(hit, cidx, -1.0), axis=0, keepdims=True)
            idx_ref[pl.ds(h * kk + k, 1), :] = expert.astype(I32)
            return jnp.where(hit, -jnp.inf, s)

        lax.fori_loop(0, kk, pick2, cand)
        best = bv_ref[...]
        e = jnp.exp(best - best[0:1, :])
        gw_ref[h * kk:(h + 1) * kk, :] = e / jnp.sum(e, axis=0, keepdims=True)


def _peer_topk(q_chunks, subkeys, tt=256):
    nchunk, n, half = q_chunks.shape
    slots = PEER_HEADS * PEER_TOPK
    return pl.pallas_call(
        _peer_topk_kernel,
        grid=(n // tt,),
        in_specs=[
            pl.BlockSpec((nchunk, tt, half), lambda i: (0, i, 0)),
            pl.BlockSpec(subkeys.shape, lambda i: (0, 0, 0)),
        ],
        out_specs=[pl.BlockSpec((slots, tt), lambda i: (0, i)),
                   pl.BlockSpec((slots, tt), lambda i: (0, i))],
        out_shape=[jax.ShapeDtypeStruct((slots, n), I32), jax.ShapeDtypeStruct((slots, n), F32)],
        scratch_shapes=[pltpu.VMEM((nchunk, PEER_TOPK, tt), F32),
                        pltpu.VMEM((nchunk, PEER_TOPK, tt), F32),
                        pltpu.VMEM((PEER_TOPK, tt), F32)],
        compiler_params=_params("parallel"),
        name="peer_topk",
    )(q_chunks, subkeys)


def _peer_ffn_kernel(idx_hbm, uv_hbm, gw_ref, h_ref, x1_ref, fg_ref, y_ref, idx_smem, buf, idx_sem,
                     row_sem):
    tt, d = h_ref.shape
    slots = gw_ref.shape[0]
    rows = d // LANES
    step = pl.program_id(0)

    idx_copy = pltpu.make_async_copy(idx_hbm.at[step], idx_smem, idx_sem)
    idx_copy.start()
    idx_copy.wait()

    nbuf = buf.shape[0]

    def start(tok, slot):
        for e in range(slots):
            pltpu.make_async_copy(uv_hbm.at[idx_smem[tok * slots + e]], buf.at[slot, :, e, :],
                                  row_sem.at[slot]).start()

    def wait(slot):
        pltpu.make_async_copy(buf.at[slot], buf.at[slot], row_sem.at[slot]).wait()

    tok_lane = lax.broadcasted_iota(I32, (1, tt), 1)

    def expert_weights(tok, slot):
        h = h_ref[pl.ds(tok, 1), :]
        prod = jnp.zeros((slots, LANES), F32)
        for r in range(rows):
            prod = prod + buf[slot, r] * h[:, r * LANES:(r + 1) * LANES]
        a = jnp.sum(prod, axis=1, keepdims=True)
        act = 0.5 * a * (1.0 + lax.erf(a * (2.0 ** -0.5)))
        gate = jnp.sum(jnp.where(tok_lane == tok, gw_ref[...], 0.0), axis=1, keepdims=True)
        return gate * act

    def weighted_values(tok, slot, w):
        y_ref[pl.ds(tok, 1), :] = jnp.concatenate(
            [jnp.sum(buf[slot, rows + r] * w, axis=0, keepdims=True) for r in range(rows)], axis=1)

    for tok in range(nbuf):
        start(tok, tok)
    wait(0)
    w = expert_weights(0, 0)

    def step(tok, slot, w, refill):
        nxt = (slot + 1) % nbuf
        wait(nxt)
        w_next = expert_weights(tok + 1, nxt)
        weighted_values(tok, slot, w)
        if refill:
            start(tok + nbuf, slot)
        return w_next

    def round_of_steps(i, w):
        for slot in range(nbuf):
            w = step(i * nbuf + slot, slot, w, True)
        return w

    n_rounds = tt // nbuf - 1
    w = lax.fori_loop(0, n_rounds, round_of_steps, w)
    for slot in range(nbuf - 1):
        w = step(n_rounds * nbuf + slot, slot, w, False)
    weighted_values(tt - 1, nbuf - 1, w)
    y_ref[...] = _rms(x1_ref[...] + y_ref[...], fg_ref[...])


def _peer_ffn(idx_tiles, uv, gw, h2, x1, final_g, tt):
    n, d = h2.shape
    slots = gw.shape[0]
    return pl.pallas_call(
        _peer_ffn_kernel,
        grid=(n // tt,),
        in_specs=[
            pl.BlockSpec(memory_space=pl.ANY),
            pl.BlockSpec(memory_space=pl.ANY),
            pl.BlockSpec((slots, tt), lambda i: (0, i)),
            pl.BlockSpec((tt, d), lambda i: (i, 0)),
            pl.BlockSpec((tt, d), lambda i: (i, 0)),
            pl.BlockSpec((1, d), lambda i: (0, 0)),
        ],
        out_specs=pl.BlockSpec((tt, d), lambda i: (i, 0)),
        out_shape=jax.ShapeDtypeStruct((n, d), F32),
        scratch_shapes=[pltpu.SMEM((tt * slots,), I32),
                        pltpu.VMEM((PEER_GATHER_BUFFERS, uv.shape[1], slots, uv.shape[2]), F32),
                        pltpu.SemaphoreType.DMA, pltpu.SemaphoreType.DMA((PEER_GATHER_BUFFERS,))],
        compiler_params=_params("arbitrary"),
        name="peer_ffn",
    )(idx_tiles, uv, gw, h2, x1, final_g.reshape(1, d))


def _layer(x, mem, mix_g, mem_g, w_in, b_merge, pe_k, pe_v, cw_k, cw_v, w_mem_kv, w_sb_br, w_nsa_br,
           w_mem_br, w_out, ffn_g, peer_w_q, subkeys, uv, out_g):
    b, t, d = x.shape
    m = mem.shape[1]
    g = NSA_KV_GROUPS
    n = b * t
    x2d = x.reshape(n, d)

    o_sbq, o_sbk, o_sbv = 0, SB_W, 2 * SB_W
    o_nq = 3 * SB_W
    o_nkv = o_nq + NSA_W
    o_ng = o_nkv + 6 * NSA_KV_W
    o_mq = o_ng + NSA_HEADS * 3
    o_mg = o_mq + MEM_W
    w_act = jnp.concatenate([w_in[:, :o_ng], w_in[:, o_mq:o_mg]], axis=1).astype(BF16)
    gate_pad = LANES - NSA_HEADS * 3
    w_gate = jnp.concatenate([w_in[:, o_mg:], w_in[:, o_ng:o_mq], jnp.zeros((d, gate_pad), w_in.dtype)],
                             axis=1).astype(BF16)
    act = _norm_matmul(x2d, mix_g, w_act, BF16, tm=256, tn=512)
    gates_slab = _norm_matmul(x2d, mix_g, w_gate, F32, tm=256, tn=640)

    def heads(lo, nh):
        return act[:, lo:lo + nh * HEAD_DIM].reshape(b, t, nh, HEAD_DIM).transpose(0, 2, 1, 3)

    sb_out = _sb_attention(heads(o_sbq, SB_HEADS), heads(o_sbk, SB_HEADS), heads(o_sbv, SB_HEADS))

    kv = [heads(o_nkv + j * NSA_KV_W, g) for j in range(6)]
    kc = _nsa_compress(kv[0], pe_k, cw_k)
    vc = _nsa_compress(kv[1], pe_v, cw_v)
    gl = gates_slab[:, N_BRANCHES * d:N_BRANCHES * d + NSA_HEADS * 3]
    gl = gl.reshape(b, t, g, NSA_GROUP * 3).transpose(0, 2, 1, 3)
    gl = jnp.pad(gl, ((0, 0), (0, 0), (0, 0), (0, LANES - NSA_GROUP * 3)))
    slopes = jnp.asarray([2.0 ** (-8.0 * (h + 1) / NSA_HEADS) for h in range(NSA_HEADS)], F32)
    nsa_out = _nsa_attention(heads(o_nq, NSA_HEADS), kc, vc, kv[2], kv[3], kv[4], kv[5], gl, slopes)
    nsa_out = nsa_out.transpose(0, 2, 1, 3).reshape(n, NSA_W)

    mkv = _norm_matmul(mem.reshape(b * m, d), mem_g, w_mem_kv.astype(BF16), BF16, tm=256, tn=512)
    mem_q = act[:, o_nkv + 6 * NSA_KV_W:].reshape(b, t, MEM_W)
    mem_out = _mem_attention(mem_q, mkv.reshape(b, m, 2 * MEM_W))

    x1, h2, q_chunks = _merge(
        sb_out.reshape(n, SB_W), nsa_out, mem_out.reshape(n, MEM_W), gates_slab, b_merge, x2d,
        w_sb_br.astype(BF16), w_nsa_br.astype(BF16), w_mem_br.astype(BF16), w_out.astype(BF16), ffn_g,
        peer_w_q.astype(BF16))

    half = PEER_QUERY_DIM // 2
    idx, gw = _peer_topk(q_chunks, subkeys.reshape(2 * PEER_HEADS, PEER_N_KEYS, half).astype(BF16))
    tt = 128
    idx_tiles = idx.T.reshape(n // tt, tt * idx.shape[0])
    y = _peer_ffn(idx_tiles, uv, gw, h2, x1, out_g, tt)
    return y.reshape(b, t, d)


def kernel(x, mem, mix_norm_g, mem_norm_g, w_in, b_merge, cmp_pe_k, cmp_pe_v, cmp_w_k, cmp_w_v, w_mem_kv, w_sb_br, w_nsa_br, w_mem_br, w_out, ffn_norm_g, peer_w_q, peer_subkeys, peer_u, peer_v, final_norm_g):
    depth = w_in.shape[0]
    assert depth == 1, "the final rmsnorm is fused into the last layer's PEER kernel"
    l = 0
    n_exp, d = peer_u[l].shape
    uv = jnp.concatenate([peer_u[l].reshape(n_exp, d // LANES, LANES),
                          peer_v[l].reshape(n_exp, d // LANES, LANES)], axis=1)
    return _layer(x, mem, mix_norm_g[l], mem_norm_g[l], w_in[l], b_merge[l], cmp_pe_k[l], cmp_pe_v[l],
                  cmp_w_k[l], cmp_w_v[l], w_mem_kv[l], w_sb_br[l], w_nsa_br[l], w_mem_br[l], w_out[l],
                  ffn_norm_g[l], peer_w_q[l], peer_subkeys[l], uv, final_norm_g)
```

```python
import functools
import math

import jax
import jax.numpy as jnp
from jax import lax
from jax.experimental import pallas as pl
from jax.experimental.pallas import tpu as pltpu

F32 = jnp.float32
BF16 = jnp.bfloat16
I32 = jnp.int32

HEAD_DIM = 64
SB_HEADS = 6
NSA_HEADS = 6
NSA_KV_GROUPS = 2
NSA_GROUP = NSA_HEADS // NSA_KV_GROUPS
MEM_HEADS = 4
N_BRANCHES = 3
SB_W = SB_HEADS * HEAD_DIM
NSA_W = NSA_HEADS * HEAD_DIM
NSA_KV_W = NSA_KV_GROUPS * HEAD_DIM
MEM_W = MEM_HEADS * HEAD_DIM
CMP_LEN = 32
CMP_STRIDE = 16
SEL_BLOCK = 64
N_SELECT = 16
WINDOW = 512
FORCED_SCORE = 1e4
PEER_HEADS = 8
PEER_N_KEYS = 128
PEER_QUERY_DIM = 256
PEER_TOPK = 16
RMS_EPS = 1e-6
NEG_INF = -1e30
SCALE = HEAD_DIM ** -0.5

LANES = 128
PEER_GATHER_BUFFERS = 8
VMEM_LIMIT_BYTES = 56 * 1024 * 1024

_NT = (((1,), (1,)), ((), ()))


def _params(*sem):
    return pltpu.CompilerParams(dimension_semantics=sem, vmem_limit_bytes=VMEM_LIMIT_BYTES)


def _dot(a, b):
    return jnp.dot(a, b, preferred_element_type=F32)


def _dot_nt(a, b):
    return lax.dot_general(a, b, _NT, preferred_element_type=F32)


def _sigmoid(x):
    return 1.0 / (1.0 + jnp.exp(-x))


def _rms(x, g):
    return x * lax.rsqrt(jnp.mean(x * x, axis=-1, keepdims=True) + RMS_EPS) * g


def _norm_matmul_kernel(x_ref, g_ref, w_ref, o_ref, *, tn):
    h = _rms(x_ref[...], g_ref[...]).astype(BF16)
    for c in range(0, o_ref.shape[1], tn):
        o_ref[:, c:c + tn] = _dot(h, w_ref[:, c:c + tn]).astype(o_ref.dtype)


def _norm_matmul(x2d, g, w, out_dtype, tm, tn):
    n, d = x2d.shape
    m = w.shape[1]
    return pl.pallas_call(
        functools.partial(_norm_matmul_kernel, tn=tn),
        grid=(n // tm,),
        in_specs=[
            pl.BlockSpec((tm, d), lambda i: (i, 0)),
            pl.BlockSpec((1, d), lambda i: (0, 0)),
            pl.BlockSpec((d, m), lambda i: (0, 0)),
        ],
        out_specs=pl.BlockSpec((tm, m), lambda i: (i, 0)),
        out_shape=jax.ShapeDtypeStruct((n, m), out_dtype),
        compiler_params=_params("parallel"),
        name="norm_matmul",
    )(x2d, g.reshape(1, d), w)


def _sb_kernel(q_ref, k_ref, v_ref, o_ref, *, tile):
    qi = pl.program_id(2)
    nh = q_ref.shape[1]
    row = lax.broadcasted_iota(I32, (tile, tile), 0)
    col = lax.broadcasted_iota(I32, (tile, tile), 1)
    lower = row > col
    later = lower.astype(BF16)
    qs = [q_ref[0, hh] for hh in range(nh)]

    def visit(hh, ks, c, acc, diagonal):
        k = k_ref[0, hh, pl.ds(ks, tile), :]
        v = v_ref[0, hh, pl.ds(ks, tile), :]
        z = _dot_nt(qs[hh], k) * SCALE
        sp = jnp.maximum(z, 0.0) + jnp.log(1.0 + jnp.exp(-jnp.abs(z)))
        if diagonal:
            sp = jnp.where(lower, sp, 0.0)
        hi = sp.astype(BF16)
        lo = (sp - hi.astype(F32)).astype(BF16)
        after = _dot(hi, later) + _dot(lo, later)
        a = jnp.exp(z - sp - after - c)
        if diagonal:
            a = jnp.where(lower, a, 0.0)
        return c + jnp.sum(sp, axis=1, keepdims=True), acc + _dot(a.astype(BF16), v)

    q0 = pl.multiple_of(qi * tile, tile)
    state = []
    for hh in range(nh):
        state.extend(visit(hh, q0, jnp.zeros((tile, 1), F32), jnp.zeros((tile, HEAD_DIM), F32), True))

    def body(i, carry):
        ks = pl.multiple_of((qi - 1 - i) * tile, tile)
        out = []
        for hh in range(nh):
            out.extend(visit(hh, ks, carry[2 * hh], carry[2 * hh + 1], False))
        return tuple(out)

    state = lax.fori_loop(0, qi, body, tuple(state))
    o_ref[0] = jnp.concatenate([state[2 * hh + 1] for hh in range(nh)], axis=1).astype(o_ref.dtype)


def _sb_attention(q, k, v, tile=256):
    b, h, t, dh = q.shape
    hp = 2
    tq = tile
    return pl.pallas_call(
        functools.partial(_sb_kernel, tile=tile),
        grid=(b, h // hp, t // tq),
        in_specs=[
            pl.BlockSpec((1, hp, tq, dh), lambda bi, hi, qi: (bi, hi, qi, 0)),
            pl.BlockSpec((1, hp, t, dh), lambda bi, hi, qi: (bi, hi, 0, 0)),
            pl.BlockSpec((1, hp, t, dh), lambda bi, hi, qi: (bi, hi, 0, 0)),
        ],
        out_specs=pl.BlockSpec((1, tq, hp * dh), lambda bi, hi, qi: (bi, qi, hi)),
        out_shape=jax.ShapeDtypeStruct((b, t, h * dh), BF16),
        compiler_params=_params("parallel", "parallel", "arbitrary"),
        name="sb_attn",
    )(q, k, v)


def _compress_kernel(x_ref, pe_ref, w_ref, o_ref):
    x = x_ref[0, 0]
    nc = x.shape[0]
    w_lo, w_hi = w_ref[0], w_ref[1]
    first = _dot(x, w_lo)
    second = _dot(x, w_hi)
    feat = pe_ref.shape[1]
    pe_lo = jnp.broadcast_to(pe_ref[0:1, :], (8, feat)).astype(BF16)
    pe_hi = jnp.broadcast_to(pe_ref[1:2, :], (8, feat)).astype(BF16)
    bias = _dot(pe_lo, w_lo)[0:1] + _dot(pe_hi, w_hi)[0:1]
    o_ref[0, 0] = (first + pltpu.roll(second, nc - 1, 0) + bias).astype(o_ref.dtype)


def _nsa_compress(z, pe, w):
    b, g, t, dh = z.shape
    nc = t // CMP_STRIDE
    feat = CMP_STRIDE * dh
    x = z.reshape(b, g, nc, feat)
    pe2 = pe.reshape(2, feat)
    w2 = w.reshape(2, feat, dh).astype(BF16)
    return pl.pallas_call(
        _compress_kernel,
        grid=(b, g),
        in_specs=[
            pl.BlockSpec((1, 1, nc, feat), lambda bi, gi: (bi, gi, 0, 0)),
            pl.BlockSpec((2, feat), lambda bi, gi: (0, 0)),
            pl.BlockSpec((2, feat, dh), lambda bi, gi: (0, 0, 0)),
        ],
        out_specs=pl.BlockSpec((1, 1, nc, dh), lambda bi, gi: (bi, gi, 0, 0)),
        out_shape=jax.ShapeDtypeStruct((b, g, nc, dh), BF16),
        compiler_params=_params("parallel", "parallel"),
        name="nsa_compress",
    )(x, pe2, w2)


def _nsa_kernel(slopes_ref, q_ref, kc_ref, vc_ref, ks_ref, vs_ref, kw_ref, vw_ref, gl_ref, pool_ref, ex_ref,
                o_ref, *, tq, tk, n_sel, real_blocks):
    grp = pl.program_id(1)
    t0 = pl.program_id(2) * tq
    nc = kc_ref.shape[2]
    rr = NSA_GROUP
    t = t0 + lax.broadcasted_iota(I32, (tq, 1), 0)

    def stack(x):
        return jnp.concatenate([x] * rr, axis=0)

    q = q_ref[0].reshape(rr * tq, HEAD_DIM)
    slope = jnp.concatenate([jnp.full((tq, 1), slopes_ref[grp * rr + r], F32) for r in range(rr)], axis=0)

    def heads_sum(x):
        out = x[0:tq]
        for r in range(1, rr):
            out = out + x[r * tq:(r + 1) * tq]
        return out

    def masked_softmax(s, valid):
        p = jnp.where(valid, jnp.exp(s - jnp.max(s, axis=1, keepdims=True)), 0.0)
        denom = jnp.sum(p, axis=1, keepdims=True)
        return p / jnp.where(denom > 0, denom, 1.0)

    lane = lax.broadcasted_iota(I32, (1, nc), 1)
    dist_c = (t - (lane * CMP_STRIDE + (CMP_LEN - 1))).astype(F32)
    valid_c = stack(dist_c >= 0)
    s = _dot_nt(q, kc_ref[0, 0]) * SCALE - slope * stack(dist_c)
    p = masked_softmax(jnp.where(valid_c, s, NEG_INF), valid_c)
    o_cmp = _dot(p.astype(BF16), vc_ref[0, 0])
    psum = heads_sum(p)

    n_blk = pool_ref.shape[0]
    hi = psum.astype(BF16)
    rest = psum - hi.astype(F32)
    mid = rest.astype(BF16)
    lo = (rest - mid.astype(F32)).astype(BF16)
    pool = pool_ref[...]
    imp = _dot_nt(pool, hi) + _dot_nt(pool, mid) + _dot_nt(pool, lo)
    blk = lax.broadcasted_iota(I32, (n_blk, 1), 0)
    cur = (t0 + lax.broadcasted_iota(I32, (1, tq), 1)) // SEL_BLOCK
    forced = (blk == 0) | (blk == cur) | (blk == cur - 1)
    imp = jnp.where(forced, FORCED_SCORE, jnp.where(blk <= cur, imp, -1.0))
    imp = jnp.where(blk < real_blocks, imp, -jnp.inf)
    blk_f = blk.astype(F32)

    def pick(_, carry):
        imp, sel = carry
        best = jnp.max(imp, axis=0, keepdims=True)
        first = jnp.min(jnp.where(imp == best, blk_f, float(n_blk)), axis=0, keepdims=True)
        hit = blk_f == first
        return jnp.where(hit, -jnp.inf, imp), jnp.where(hit, 1.0, sel)

    _, sel = lax.fori_loop(0, n_sel, pick, (imp, jnp.zeros((n_blk, tq), F32)))
    sel = sel.T.astype(BF16)

    span = WINDOW + tq
    w0 = pl.multiple_of(jnp.maximum(t0 - WINDOW, 0), tq)
    dist_w = t - (w0 + lax.broadcasted_iota(I32, (1, span), 1))
    valid_w = stack((dist_w >= 0) & (dist_w < WINDOW))
    s = _dot_nt(q, kw_ref[0, 0, pl.ds(w0, span), :]) * SCALE - slope * stack(dist_w.astype(F32))
    p = masked_softmax(jnp.where(valid_w, s, NEG_INF), valid_w)
    o_win = _dot(p.astype(BF16), vw_ref[0, 0, pl.ds(w0, span), :])

    kcol = lax.broadcasted_iota(I32, (1, tk), 1)

    def sel_body(kb, carry):
        m, l, acc = carry
        ks0 = pl.multiple_of(kb * tk, tk)
        chosen = _dot(sel, ex_ref[kb]) > 0.5
        dist = t - (ks0 + kcol)
        valid = stack(chosen & (dist >= 0))
        s = _dot_nt(q, ks_ref[0, 0, pl.ds(ks0, tk), :]) * SCALE - slope * stack(dist.astype(F32))
        s = jnp.where(valid, s, NEG_INF)
        m_new = jnp.maximum(m, jnp.max(s, axis=1, keepdims=True))
        alpha = jnp.exp(m - m_new)
        p = jnp.where(valid, jnp.exp(s - m_new), 0.0)
        l = alpha * l + jnp.sum(p, axis=1, keepdims=True)
        acc = alpha * acc + _dot(p.astype(BF16), vs_ref[0, 0, pl.ds(ks0, tk), :])
        return m_new, l, acc

    init = (jnp.full((rr * tq, 1), NEG_INF, F32), jnp.zeros((rr * tq, 1), F32),
            jnp.zeros((rr * tq, HEAD_DIM), F32))
    _, l_sel, acc_sel = lax.fori_loop(0, (t0 + tq + tk - 1) // tk, sel_body, init)
    o_sel = acc_sel / l_sel

    gates = _sigmoid(gl_ref[0, 0])
    outs = []
    for r in range(rr):
        rows = slice(r * tq, (r + 1) * tq)
        outs.append(gates[:, 3 * r:3 * r + 1] * o_cmp[rows] + gates[:, 3 * r + 1:3 * r + 2] * o_sel[rows]
                    + gates[:, 3 * r + 2:3 * r + 3] * o_win[rows])
    o_ref[0, 0] = jnp.concatenate(outs, axis=1).astype(o_ref.dtype)


def _nsa_attention(q, kc, vc, ks, vs, kw, vw, gate_logits, slopes, tq=128, tk=256):
    b, _, t, dh = q.shape
    g = NSA_KV_GROUPS
    nc = kc.shape[2]
    n_sel = min(N_SELECT, t // SEL_BLOCK)
    real_blocks = t // SEL_BLOCK
    n_blk = LANES
    assert t % tk == 0 and t >= WINDOW + tq and real_blocks <= n_blk
    blk_ids = jnp.arange(n_blk, dtype=I32)
    pool = (blk_ids[:, None] == jnp.arange(nc, dtype=I32)[None, :] // (SEL_BLOCK // CMP_STRIDE)).astype(BF16)
    key_blk = (jnp.arange(t, dtype=I32) // SEL_BLOCK).reshape(t // tk, 1, tk)
    expand = (blk_ids[None, :, None] == key_blk).astype(BF16)
    kv_spec = pl.BlockSpec((1, 1, t, dh), lambda bi, gi, qi: (bi, gi, 0, 0))
    c_spec = pl.BlockSpec((1, 1, nc, dh), lambda bi, gi, qi: (bi, gi, 0, 0))
    return pl.pallas_call(
        functools.partial(_nsa_kernel, tq=tq, tk=tk, n_sel=n_sel, real_blocks=real_blocks),
        grid=(b, g, t // tq),
        in_specs=[
            pl.BlockSpec(memory_space=pltpu.SMEM),
            pl.BlockSpec((1, NSA_GROUP, tq, dh), lambda bi, gi, qi: (bi, gi, qi, 0)),
            c_spec, c_spec, kv_spec, kv_spec, kv_spec, kv_spec,
            pl.BlockSpec((1, 1, tq, LANES), lambda bi, gi, qi: (bi, gi, qi, 0)),
            pl.BlockSpec(pool.shape, lambda bi, gi, qi: (0, 0)),
            pl.BlockSpec(expand.shape, lambda bi, gi, qi: (0, 0, 0)),
        ],
        out_specs=pl.BlockSpec((1, 1, tq, NSA_GROUP * dh), lambda bi, gi, qi: (bi, gi, qi, 0)),
        out_shape=jax.ShapeDtypeStruct((b, g, t, NSA_GROUP * dh), BF16),
        compiler_params=_params("parallel", "parallel", "arbitrary"),
        name="nsa_attn",
    )(slopes, q, kc, vc, ks, vs, kw, vw, gate_logits, pool, expand)


def _mem_kernel(q_ref, kv_ref, o_ref):
    q = q_ref[0]
    kv = kv_ref[0]
    outs = []
    for h in range(MEM_HEADS):
        sl = slice(h * HEAD_DIM, (h + 1) * HEAD_DIM)
        s = _dot_nt(q[:, sl], kv[:, sl]) * SCALE
        p = jnp.exp(s - jnp.max(s, axis=1, keepdims=True))
        p = p / jnp.sum(p, axis=1, keepdims=True)
        outs.append(_dot(p.astype(BF16), kv[:, MEM_W + h * HEAD_DIM:MEM_W + (h + 1) * HEAD_DIM]))
    o_ref[0] = jnp.concatenate(outs, axis=1).astype(o_ref.dtype)


def _mem_attention(q, mkv, tq=512):
    b, t, w = q.shape
    m = mkv.shape[1]
    return pl.pallas_call(
        _mem_kernel,
        grid=(b, t // tq),
        in_specs=[
            pl.BlockSpec((1, tq, w), lambda bi, qi: (bi, qi, 0)),
            pl.BlockSpec((1, m, 2 * w), lambda bi, qi: (bi, 0, 0)),
        ],
        out_specs=pl.BlockSpec((1, tq, w), lambda bi, qi: (bi, qi, 0)),
        out_shape=jax.ShapeDtypeStruct((b, t, w), BF16),
        compiler_params=_params("parallel", "parallel"),
        name="mem_attn",
    )(q, mkv)


def _merge_kernel(sb_ref, nsa_ref, mem_ref, mg_ref, bm_ref, x_ref, wsb_ref, wnsa_ref, wmem_ref,
                  wout_ref, fg_ref, wq_ref, x1_ref, h2_ref, q_ref):
    d = x_ref.shape[1]
    branches = (_dot(sb_ref[...], wsb_ref[...]), _dot(nsa_ref[...], wnsa_ref[...]),
                _dot(mem_ref[...], wmem_ref[...]))
    merged = jnp.zeros_like(branches[0])
    for j in range(N_BRANCHES):
        gate = _sigmoid(mg_ref[:, j * d:(j + 1) * d] + bm_ref[:, j * d:(j + 1) * d])
        merged = merged + gate * branches[j]
    x1 = x_ref[...] + _dot(merged.astype(BF16), wout_ref[...])
    x1_ref[...] = x1
    h2 = _rms(x1, fg_ref[...])
    h2_ref[...] = h2
    q = _dot(h2.astype(BF16), wq_ref[...]).astype(q_ref.dtype)
    for c in range(q_ref.shape[0]):
        q_ref[c] = q[:, c * LANES:(c + 1) * LANES]


def _merge(sb, nsa, mem, gates_slab, b_merge, x2d, w_sb, w_nsa, w_mem, w_out, ffn_g, w_q, tm=256):
    n, d = x2d.shape
    nq = w_q.shape[1] // LANES
    row = lambda w: pl.BlockSpec((tm, w), lambda i: (i, 0))
    full = lambda a: pl.BlockSpec(a.shape, lambda i: (0,) * a.ndim)
    bm = b_merge.reshape(1, -1)
    fg = ffn_g.reshape(1, d)
    return pl.pallas_call(
        _merge_kernel,
        grid=(n // tm,),
        in_specs=[row(sb.shape[1]), row(nsa.shape[1]), row(mem.shape[1]), row(N_BRANCHES * d),
                  full(bm), row(d), full(w_sb), full(w_nsa), full(w_mem), full(w_out), full(fg),
                  full(w_q)],
        out_specs=[row(d), row(d), pl.BlockSpec((nq, tm, LANES), lambda i: (0, i, 0))],
        out_shape=[jax.ShapeDtypeStruct((n, d), F32), jax.ShapeDtypeStruct((n, d), F32),
                   jax.ShapeDtypeStruct((nq, n, LANES), BF16)],
        compiler_params=_params("parallel"),
        name="merge",
    )(sb, nsa, mem, gates_slab, bm, x2d, w_sb, w_nsa, w_mem, w_out, fg, w_q)


def _peer_topk_kernel(q_ref, sk_ref, idx_ref, gw_ref, tv_ref, ti_ref, bv_ref):
    tt = q_ref.shape[1]
    kk = PEER_TOPK
    nk = PEER_N_KEYS
    rid = lax.broadcasted_iota(I32, (nk, tt), 0).astype(F32)
    for c in range(2 * PEER_HEADS):
        scores = _dot_nt(sk_ref[c], q_ref[c])

        def pick(k, s, c=c):
            best = jnp.max(s, axis=0, keepdims=True)
            first = jnp.min(jnp.where(s == best, rid, float(nk)), axis=0, keepdims=True)
            tv_ref[c, pl.ds(k, 1), :] = best
            ti_ref[c, pl.ds(k, 1), :] = first
            return jnp.where(rid == first, -jnp.inf, s)

        lax.fori_loop(0, kk, pick, scores)

    sub = 8
    widths = [min(kk, -(-(kk // (i + 1)) // sub) * sub) for i in range(kk // 2)]
    n_cand = sum(widths) + kk // 2
    pos = lax.broadcasted_iota(I32, (n_cand, tt), 0).astype(F32)
    for h in range(PEER_HEADS):
        s0, s1 = tv_ref[2 * h], tv_ref[2 * h + 1]
        i0, i1 = ti_ref[2 * h], ti_ref[2 * h + 1]
        cand, cidx = [], []
        for i, wd in enumerate(widths):
            keep = lax.broadcasted_iota(I32, (wd, 1), 0) < kk // (i + 1)
            cand.append(jnp.where(keep, s0[i:i + 1, :] + s1[0:wd, :], -jnp.inf))
            cidx.append(i0[i:i + 1, :] * float(nk) + i1[0:wd, :])
        cand.append(s0[kk // 2:, :] + s1[0:1, :])
        cidx.append(i0[kk // 2:, :] * float(nk) + i1[0:1, :])
        cand = jnp.concatenate(cand, axis=0)
        cidx = jnp.concatenate(cidx, axis=0)

        def pick2(k, s, h=h, cidx=cidx):
            best = jnp.max(s, axis=0, keepdims=True)
            first = jnp.min(jnp.where(s == best, pos, float(n_cand)), axis=0, keepdims=True)
            hit = pos == first
            bv_ref[pl.ds(k, 1), :] = best
            expert = jnp.max(jnp.where(hit, cidx, -1.0), axis=0, keepdims=True)
            idx_ref[pl.ds(h * kk + k, 1), :] = expert.astype(I32)
            return jnp.where(hit, -jnp.inf, s)

        lax.fori_loop(0, kk, pick2, cand)
        best = bv_ref[...]
        e = jnp.exp(best - best[0:1, :])
        gw_ref[h * kk:(h + 1) * kk, :] = e / jnp.sum(e, axis=0, keepdims=True)


def _peer_topk(q_chunks, subkeys, tt=256):
    nchunk, n, half = q_chunks.shape
    slots = PEER_HEADS * PEER_TOPK
    return pl.pallas_call(
        _peer_topk_kernel,
        grid=(n // tt,),
        in_specs=[
            pl.BlockSpec((nchunk, tt, half), lambda i: (0, i, 0)),
            pl.BlockSpec(subkeys.shape, lambda i: (0, 0, 0)),
        ],
        out_specs=[pl.BlockSpec((slots, tt), lambda i: (0, i)),
                   pl.BlockSpec((slots, tt), lambda i: (0, i))],
        out_shape=[jax.ShapeDtypeStruct((slots, n), I32), jax.ShapeDtypeStruct((slots, n), F32)],
        scratch_shapes=[pltpu.VMEM((nchunk, PEER_TOPK, tt), F32),
                        pltpu.VMEM((nchunk, PEER_TOPK, tt), F32),
                        pltpu.VMEM((PEER_TOPK, tt), F32)],
        compiler_params=_params("parallel"),
        name="peer_topk",
    )(q_chunks, subkeys)


def _peer_ffn_kernel(idx_hbm, uv_hbm, gw_ref, h_ref, x1_ref, fg_ref, y_ref, idx_smem, buf, idx_sem,
                     row_sem):
    tt, d = h_ref.shape
    slots = gw_ref.shape[0]
    rows = d // LANES
    step = pl.program_id(0)

    idx_copy = pltpu.make_async_copy(idx_hbm.at[step], idx_smem, idx_sem)
    idx_copy.start()
    idx_copy.wait()

    nbuf = buf.shape[0]

    def start(tok, slot):
        for e in range(slots):
            pltpu.make_async_copy(uv_hbm.at[idx_smem[tok * slots + e]], buf.at[slot, :, e, :],
                                  row_sem.at[slot]).start(priority=e % 2)

    def wait(slot):
        pltpu.make_async_copy(buf.at[slot], buf.at[slot], row_sem.at[slot]).wait()

    tok_lane = lax.broadcasted_iota(I32, (1, tt), 1)

    def expert_weights(tok, slot):
        h = h_ref[pl.ds(tok, 1), :]
        prod = jnp.zeros((slots, LANES), F32)
        for r in range(rows):
            prod = prod + buf[slot, r] * h[:, r * LANES:(r + 1) * LANES]
        a = jnp.sum(prod, axis=1, keepdims=True)
        act = 0.5 * a * (1.0 + lax.erf(a * (2.0 ** -0.5)))
        gate = jnp.sum(jnp.where(tok_lane == tok, gw_ref[...], 0.0), axis=1, keepdims=True)
        return gate * act

    def weighted_values(tok, slot, w):
        y_ref[pl.ds(tok, 1), :] = jnp.concatenate(
            [jnp.sum(buf[slot, rows + r] * w, axis=0, keepdims=True) for r in range(rows)], axis=1)

    for tok in range(nbuf):
        start(tok, tok)
    wait(0)
    w = expert_weights(0, 0)

    def step(tok, slot, w, refill):
        nxt = (slot + 1) % nbuf
        wait(nxt)
        w_next = expert_weights(tok + 1, nxt)
        weighted_values(tok, slot, w)
        if refill:
            start(tok + nbuf, slot)
        return w_next

    def round_of_steps(i, w):
        for slot in range(nbuf):
            w = step(i * nbuf + slot, slot, w, True)
        return w

    n_rounds = tt // nbuf - 1
    w = lax.fori_loop(0, n_rounds, round_of_steps, w)
    for slot in range(nbuf - 1):
        w = step(n_rounds * nbuf + slot, slot, w, False)
    weighted_values(tt - 1, nbuf - 1, w)
    y_ref[...] = _rms(x1_ref[...] + y_ref[...], fg_ref[...])


def _peer_ffn(idx_tiles, uv, gw, h2, x1, final_g, tt):
    n, d = h2.shape
    slots = gw.shape[0]
    return pl.pallas_call(
        _peer_ffn_kernel,
        grid=(n // tt,),
        in_specs=[
            pl.BlockSpec(memory_space=pl.ANY),
            pl.BlockSpec(memory_space=pl.ANY),
            pl.BlockSpec((slots, tt), lambda i: (0, i)),
            pl.BlockSpec((tt, d), lambda i: (i, 0)),
            pl.BlockSpec((tt, d), lambda i: (i, 0)),
            pl.BlockSpec((1, d), lambda i: (0, 0)),
        ],
        out_specs=pl.BlockSpec((tt, d), lambda i: (i, 0)),
        out_shape=jax.ShapeDtypeStruct((n, d), F32),
        scratch_shapes=[pltpu.SMEM((tt * slots,), I32),
                        pltpu.VMEM((PEER_GATHER_BUFFERS, uv.shape[1], slots, uv.shape[2]), F32),
                        pltpu.SemaphoreType.DMA, pltpu.SemaphoreType.DMA((PEER_GATHER_BUFFERS,))],
        compiler_params=_params("arbitrary"),
        name="peer_ffn",
    )(idx_tiles, uv, gw, h2, x1, final_g.reshape(1, d))


def _layer(x, mem, mix_g, mem_g, w_in, b_merge, pe_k, pe_v, cw_k, cw_v, w_mem_kv, w_sb_br, w_nsa_br,
           w_mem_br, w_out, ffn_g, peer_w_q, subkeys, uv, out_g):
    b, t, d = x.shape
    m = mem.shape[1]
    g = NSA_KV_GROUPS
    n = b * t
    x2d = x.reshape(n, d)

    o_sbq, o_sbk, o_sbv = 0, SB_W, 2 * SB_W
    o_nq = 3 * SB_W
    o_nkv = o_nq + NSA_W
    o_ng = o_nkv + 6 * NSA_KV_W
    o_mq = o_ng + NSA_HEADS * 3
    o_mg = o_mq + MEM_W
    w_act = jnp.concatenate([w_in[:, :o_ng], w_in[:, o_mq:o_mg]], axis=1).astype(BF16)
    gate_pad = LANES - NSA_HEADS * 3
    w_gate = jnp.concatenate([w_in[:, o_mg:], w_in[:, o_ng:o_mq], jnp.zeros((d, gate_pad), w_in.dtype)],
                             axis=1).astype(BF16)
    act = _norm_matmul(x2d, mix_g, w_act, BF16, tm=256, tn=512)
    gates_slab = _norm_matmul(x2d, mix_g, w_gate, F32, tm=256, tn=640)

    def heads(lo, nh):
        return act[:, lo:lo + nh * HEAD_DIM].reshape(b, t, nh, HEAD_DIM).transpose(0, 2, 1, 3)

    sb_out = _sb_attention(heads(o_sbq, SB_HEADS), heads(o_sbk, SB_HEADS), heads(o_sbv, SB_HEADS))

    kv = [heads(o_nkv + j * NSA_KV_W, g) for j in range(6)]
    kc = _nsa_compress(kv[0], pe_k, cw_k)
    vc = _nsa_compress(kv[1], pe_v, cw_v)
    gl = gates_slab[:, N_BRANCHES * d:N_BRANCHES * d + NSA_HEADS * 3]
    gl = gl.reshape(b, t, g, NSA_GROUP * 3).transpose(0, 2, 1, 3)
    gl = jnp.pad(gl, ((0, 0), (0, 0), (0, 0), (0, LANES - NSA_GROUP * 3)))
    slopes = jnp.asarray([2.0 ** (-8.0 * (h + 1) / NSA_HEADS) for h in range(NSA_HEADS)], F32)
    nsa_out = _nsa_attention(heads(o_nq, NSA_HEADS), kc, vc, kv[2], kv[3], kv[4], kv[5], gl, slopes)
    nsa_out = nsa_out.transpose(0, 2, 1, 3).reshape(n, NSA_W)

    mkv = _norm_matmul(mem.reshape(b * m, d), mem_g, w_mem_kv.astype(BF16), BF16, tm=256, tn=512)
    mem_q = act[:, o_nkv + 6 * NSA_KV_W:].reshape(b, t, MEM_W)
    mem_out = _mem_attention(mem_q, mkv.reshape(b, m, 2 * MEM_W))

    x1, h2, q_chunks = _merge(
        sb_out.reshape(n, SB_W), nsa_out, mem_out.reshape(n, MEM_W), gates_slab, b_merge, x2d,
        w_sb_br.astype(BF16), w_nsa_br.astype(BF16), w_mem_br.astype(BF16), w_out.astype(BF16), ffn_g,
        peer_w_q.astype(BF16))

    half = PEER_QUERY_DIM // 2
    idx, gw = _peer_topk(q_chunks, subkeys.reshape(2 * PEER_HEADS, PEER_N_KEYS, half).astype(BF16))
    tt = 128
    idx_tiles = idx.T.reshape(n // tt, tt * idx.shape[0])
    y = _peer_ffn(idx_tiles, uv, gw, h2, x1, out_g, tt)
    return y.reshape(b, t, d)


def kernel(x, mem, mix_norm_g, mem_norm_g, w_in, b_merge, cmp_pe_k, cmp_pe_v, cmp_w_k, cmp_w_v, w_mem_kv, w_sb_br, w_nsa_br, w_mem_br, w_out, ffn_norm_g, peer_w_q, peer_subkeys, peer_u, peer_v, final_norm_g):
    depth = w_in.shape[0]
    assert depth == 1, "the final rmsnorm is fused into the last layer's PEER kernel"
    l = 0
    n_exp, d = peer_u[l].shape
    uv = jnp.concatenate([peer_u[l].reshape(n_exp, d // LANES, LANES),
                          peer_v[l].reshape(n_exp, d // LANES, LANES)], axis=1)
    return _layer(x, mem, mix_norm_g[l], mem_norm_g[l], w_in[l], b_merge[l], cmp_pe_k[l], cmp_pe_v[l],
                  cmp_w_k[l], cmp_w_v[l], w_mem_kv[l], w_sb_br[l], w_nsa_br[l], w_mem_br[l], w_out[l],
                  ffn_norm_g[l], peer_w_q[l], peer_subkeys[l], uv, final_norm_g)
```

```python
import functools
import math

import jax
import jax.numpy as jnp
from jax import lax
from jax.experimental import pallas as pl
from jax.experimental.pallas import tpu as pltpu

F32 = jnp.float32
BF16 = jnp.bfloat16
I32 = jnp.int32

HEAD_DIM = 64
SB_HEADS = 6
NSA_HEADS = 6
NSA_KV_GROUPS = 2
NSA_GROUP = NSA_HEADS // NSA_KV_GROUPS
MEM_HEADS = 4
N_BRANCHES = 3
SB_W = SB_HEADS * HEAD_DIM
NSA_W = NSA_HEADS * HEAD_DIM
NSA_KV_W = NSA_KV_GROUPS * HEAD_DIM
MEM_W = MEM_HEADS * HEAD_DIM
CMP_LEN = 32
CMP_STRIDE = 16
SEL_BLOCK = 64
N_SELECT = 16
WINDOW = 512
FORCED_SCORE = 1e4
PEER_HEADS = 8
PEER_N_KEYS = 128
PEER_QUERY_DIM = 256
PEER_TOPK = 16
RMS_EPS = 1e-6
NEG_INF = -1e30
SCALE = HEAD_DIM ** -0.5
SB_DEAD_LOG = 104.0

LANES = 128
PEER_GATHER_BUFFERS = 8
VMEM_LIMIT_BYTES = 56 * 1024 * 1024

_NT = (((1,), (1,)), ((), ()))


def _params(*sem):
    return pltpu.CompilerParams(dimension_semantics=sem, vmem_limit_bytes=VMEM_LIMIT_BYTES)


def _dot(a, b):
    return jnp.dot(a, b, preferred_element_type=F32)


def _dot_nt(a, b):
    return lax.dot_general(a, b, _NT, preferred_element_type=F32)


def _sigmoid(x):
    return 1.0 / (1.0 + jnp.exp(-x))


def _rms(x, g):
    return x * lax.rsqrt(jnp.mean(x * x, axis=-1, keepdims=True) + RMS_EPS) * g


def _norm_matmul_kernel(x_ref, g_ref, w_ref, o_ref, *, tn):
    h = _rms(x_ref[...], g_ref[...]).astype(BF16)
    for c in range(0, o_ref.shape[1], tn):
        o_ref[:, c:c + tn] = _dot(h, w_ref[:, c:c + tn]).astype(o_ref.dtype)


def _norm_matmul(x2d, g, w, out_dtype, tm, tn):
    n, d = x2d.shape
    m = w.shape[1]
    return pl.pallas_call(
        functools.partial(_norm_matmul_kernel, tn=tn),
        grid=(n // tm,),
        in_specs=[
            pl.BlockSpec((tm, d), lambda i: (i, 0)),
            pl.BlockSpec((1, d), lambda i: (0, 0)),
            pl.BlockSpec((d, m), lambda i: (0, 0)),
        ],
        out_specs=pl.BlockSpec((tm, m), lambda i: (i, 0)),
        out_shape=jax.ShapeDtypeStruct((n, m), out_dtype),
        compiler_params=_params("parallel"),
        name="norm_matmul",
    )(x2d, g.reshape(1, d), w)


def _sb_kernel(q_ref, k_ref, v_ref, o_ref, *, tile):
    qi = pl.program_id(2)
    nh = q_ref.shape[1]
    row = lax.broadcasted_iota(I32, (tile, tile), 0)
    col = lax.broadcasted_iota(I32, (tile, tile), 1)
    lower = row > col
    later = lower.astype(BF16)
    qs = [q_ref[0, hh] for hh in range(nh)]

    def visit(hh, ks, c, acc, diagonal):
        k = k_ref[0, hh, pl.ds(ks, tile), :]
        v = v_ref[0, hh, pl.ds(ks, tile), :]
        z = _dot_nt(qs[hh], k) * SCALE
        sp = jnp.maximum(z, 0.0) + jnp.log(1.0 + jnp.exp(-jnp.abs(z)))
        if diagonal:
            sp = jnp.where(lower, sp, 0.0)
        hi = sp.astype(BF16)
        lo = (sp - hi.astype(F32)).astype(BF16)
        after = _dot(hi, later) + _dot(lo, later)
        a = jnp.exp(z - sp - after - c)
        if diagonal:
            a = jnp.where(lower, a, 0.0)
        return c + jnp.sum(sp, axis=1, keepdims=True), acc + _dot(a.astype(BF16), v)

    q0 = pl.multiple_of(qi * tile, tile)
    state = []
    for hh in range(nh):
        state.extend(visit(hh, q0, jnp.zeros((tile, 1), F32), jnp.zeros((tile, HEAD_DIM), F32), True))

    def smallest_carry(state):
        c = state[0]
        for hh in range(1, nh):
            c = jnp.minimum(c, state[2 * hh])
        return jnp.min(c)

    def live(carry):
        i, c_min, _ = carry
        return (i < qi) & (c_min <= SB_DEAD_LOG)

    def body(carry):
        i, _, state = carry
        ks = pl.multiple_of((qi - 1 - i) * tile, tile)
        out = []
        for hh in range(nh):
            out.extend(visit(hh, ks, state[2 * hh], state[2 * hh + 1], False))
        return i + 1, smallest_carry(out), tuple(out)

    _, _, state = lax.while_loop(live, body, (0, smallest_carry(state), tuple(state)))
    o_ref[0] = jnp.concatenate([state[2 * hh + 1] for hh in range(nh)], axis=1).astype(o_ref.dtype)


def _sb_attention(q, k, v, tile=256):
    b, h, t, dh = q.shape
    hp = 2
    tq = tile
    return pl.pallas_call(
        functools.partial(_sb_kernel, tile=tile),
        grid=(b, h // hp, t // tq),
        in_specs=[
            pl.BlockSpec((1, hp, tq, dh), lambda bi, hi, qi: (bi, hi, qi, 0)),
            pl.BlockSpec((1, hp, t, dh), lambda bi, hi, qi: (bi, hi, 0, 0)),
            pl.BlockSpec((1, hp, t, dh), lambda bi, hi, qi: (bi, hi, 0, 0)),
        ],
        out_specs=pl.BlockSpec((1, tq, hp * dh), lambda bi, hi, qi: (bi, qi, hi)),
        out_shape=jax.ShapeDtypeStruct((b, t, h * dh), BF16),
        compiler_params=_params("parallel", "parallel", "arbitrary"),
        name="sb_attn",
    )(q, k, v)


def _compress_kernel(x_ref, pe_ref, w_ref, o_ref):
    x = x_ref[0, 0]
    nc = x.shape[0]
    w_lo, w_hi = w_ref[0], w_ref[1]
    first = _dot(x, w_lo)
    second = _dot(x, w_hi)
    feat = pe_ref.shape[1]
    pe_lo = jnp.broadcast_to(pe_ref[0:1, :], (8, feat)).astype(BF16)
    pe_hi = jnp.broadcast_to(pe_ref[1:2, :], (8, feat)).astype(BF16)
    bias = _dot(pe_lo, w_lo)[0:1] + _dot(pe_hi, w_hi)[0:1]
    o_ref[0, 0] = (first + pltpu.roll(second, nc - 1, 0) + bias).astype(o_ref.dtype)


def _nsa_compress(z, pe, w):
    b, g, t, dh = z.shape
    nc = t // CMP_STRIDE
    feat = CMP_STRIDE * dh
    x = z.reshape(b, g, nc, feat)
    pe2 = pe.reshape(2, feat)
    w2 = w.reshape(2, feat, dh).astype(BF16)
    return pl.pallas_call(
        _compress_kernel,
        grid=(b, g),
        in_specs=[
            pl.BlockSpec((1, 1, nc, feat), lambda bi, gi: (bi, gi, 0, 0)),
            pl.BlockSpec((2, feat), lambda bi, gi: (0, 0)),
            pl.BlockSpec((2, feat, dh), lambda bi, gi: (0, 0, 0)),
        ],
        out_specs=pl.BlockSpec((1, 1, nc, dh), lambda bi, gi: (bi, gi, 0, 0)),
        out_shape=jax.ShapeDtypeStruct((b, g, nc, dh), BF16),
        compiler_params=_params("parallel", "parallel"),
        name="nsa_compress",
    )(x, pe2, w2)


def _nsa_kernel(slopes_ref, q_ref, kc_ref, vc_ref, ks_ref, vs_ref, kw_ref, vw_ref, gl_ref, pool_ref, ex_ref,
                o_ref, used_ref, *, tq, tk, n_sel, real_blocks):
    grp = pl.program_id(1)
    t0 = pl.program_id(2) * tq
    nc = kc_ref.shape[2]
    rr = NSA_GROUP
    t = t0 + lax.broadcasted_iota(I32, (tq, 1), 0)

    def stack(x):
        return jnp.concatenate([x] * rr, axis=0)

    q = q_ref[0].reshape(rr * tq, HEAD_DIM)
    slope = jnp.concatenate([jnp.full((tq, 1), slopes_ref[grp * rr + r], F32) for r in range(rr)], axis=0)

    def heads_sum(x):
        out = x[0:tq]
        for r in range(1, rr):
            out = out + x[r * tq:(r + 1) * tq]
        return out

    def masked_softmax(s, valid):
        p = jnp.where(valid, jnp.exp(s - jnp.max(s, axis=1, keepdims=True)), 0.0)
        denom = jnp.sum(p, axis=1, keepdims=True)
        return p / jnp.where(denom > 0, denom, 1.0)

    lane = lax.broadcasted_iota(I32, (1, nc), 1)
    dist_c = (t - (lane * CMP_STRIDE + (CMP_LEN - 1))).astype(F32)
    valid_c = stack(dist_c >= 0)
    s = _dot_nt(q, kc_ref[0, 0]) * SCALE - slope * stack(dist_c)
    p = masked_softmax(jnp.where(valid_c, s, NEG_INF), valid_c)
    o_cmp = _dot(p.astype(BF16), vc_ref[0, 0])
    psum = heads_sum(p)

    n_blk = pool_ref.shape[0]
    hi = psum.astype(BF16)
    rest = psum - hi.astype(F32)
    mid = rest.astype(BF16)
    lo = (rest - mid.astype(F32)).astype(BF16)
    pool = pool_ref[...]
    imp = _dot_nt(pool, hi) + _dot_nt(pool, mid) + _dot_nt(pool, lo)
    blk = lax.broadcasted_iota(I32, (n_blk, 1), 0)
    cur = (t0 + lax.broadcasted_iota(I32, (1, tq), 1)) // SEL_BLOCK
    forced = (blk == 0) | (blk == cur) | (blk == cur - 1)
    imp = jnp.where(forced, FORCED_SCORE, jnp.where(blk <= cur, imp, -1.0))
    imp = jnp.where(blk < real_blocks, imp, -jnp.inf)
    blk_f = blk.astype(F32)

    def pick(_, carry):
        imp, sel = carry
        best = jnp.max(imp, axis=0, keepdims=True)
        first = jnp.min(jnp.where(imp == best, blk_f, float(n_blk)), axis=0, keepdims=True)
        hit = blk_f == first
        return jnp.where(hit, -jnp.inf, imp), jnp.where(hit, 1.0, sel)

    _, sel = lax.fori_loop(0, n_sel, pick, (imp, jnp.zeros((n_blk, tq), F32)))
    picked = jnp.max(sel, axis=1, keepdims=True)
    per_tile = tk // SEL_BLOCK
    n_causal = (t0 + tq + tk - 1) // tk
    n_used = jnp.int32(0)
    for j in range(used_ref.shape[0]):
        used_ref[n_used] = j
        hit = (jnp.max(picked[j * per_tile:(j + 1) * per_tile, :]) > 0) & (j < n_causal)
        n_used = n_used + hit.astype(I32)
    sel = sel.T.astype(BF16)

    span = WINDOW + tq
    w0 = pl.multiple_of(jnp.maximum(t0 - WINDOW, 0), tq)
    dist_w = t - (w0 + lax.broadcasted_iota(I32, (1, span), 1))
    valid_w = stack((dist_w >= 0) & (dist_w < WINDOW))
    s = _dot_nt(q, kw_ref[0, 0, pl.ds(w0, span), :]) * SCALE - slope * stack(dist_w.astype(F32))
    p = masked_softmax(jnp.where(valid_w, s, NEG_INF), valid_w)
    o_win = _dot(p.astype(BF16), vw_ref[0, 0, pl.ds(w0, span), :])

    kcol = lax.broadcasted_iota(I32, (1, tk), 1)

    def sel_body(i, carry):
        m, l, acc = carry
        kb = used_ref[i]
        ks0 = pl.multiple_of(kb * tk, tk)
        chosen = _dot(sel, ex_ref[kb]) > 0.5
        dist = t - (ks0 + kcol)
        valid = stack(chosen & (dist >= 0))
        s = _dot_nt(q, ks_ref[0, 0, pl.ds(ks0, tk), :]) * SCALE - slope * stack(dist.astype(F32))
        s = jnp.where(valid, s, NEG_INF)
        m_new = jnp.maximum(m, jnp.max(s, axis=1, keepdims=True))
        alpha = jnp.exp(m - m_new)
        p = jnp.where(valid, jnp.exp(s - m_new), 0.0)
        l = alpha * l + jnp.sum(p, axis=1, keepdims=True)
        acc = alpha * acc + _dot(p.astype(BF16), vs_ref[0, 0, pl.ds(ks0, tk), :])
        return m_new, l, acc

    init = (jnp.full((rr * tq, 1), NEG_INF, F32), jnp.zeros((rr * tq, 1), F32),
            jnp.zeros((rr * tq, HEAD_DIM), F32))
    _, l_sel, acc_sel = lax.fori_loop(0, n_used, sel_body, init)
    o_sel = acc_sel / l_sel

    gates = _sigmoid(gl_ref[0, 0])
    outs = []
    for r in range(rr):
        rows = slice(r * tq, (r + 1) * tq)
        outs.append(gates[:, 3 * r:3 * r + 1] * o_cmp[rows] + gates[:, 3 * r + 1:3 * r + 2] * o_sel[rows]
                    + gates[:, 3 * r + 2:3 * r + 3] * o_win[rows])
    o_ref[0, 0] = jnp.concatenate(outs, axis=1).astype(o_ref.dtype)


def _nsa_attention(q, kc, vc, ks, vs, kw, vw, gate_logits, slopes, tq=128, tk=256):
    b, _, t, dh = q.shape
    g = NSA_KV_GROUPS
    nc = kc.shape[2]
    n_sel = min(N_SELECT, t // SEL_BLOCK)
    real_blocks = t // SEL_BLOCK
    n_blk = LANES
    assert t % tk == 0 and t >= WINDOW + tq and real_blocks <= n_blk
    blk_ids = jnp.arange(n_blk, dtype=I32)
    pool = (blk_ids[:, None] == jnp.arange(nc, dtype=I32)[None, :] // (SEL_BLOCK // CMP_STRIDE)).astype(BF16)
    key_blk = (jnp.arange(t, dtype=I32) // SEL_BLOCK).reshape(t // tk, 1, tk)
    expand = (blk_ids[None, :, None] == key_blk).astype(BF16)
    kv_spec = pl.BlockSpec((1, 1, t, dh), lambda bi, gi, qi: (bi, gi, 0, 0))
    c_spec = pl.BlockSpec((1, 1, nc, dh), lambda bi, gi, qi: (bi, gi, 0, 0))
    return pl.pallas_call(
        functools.partial(_nsa_kernel, tq=tq, tk=tk, n_sel=n_sel, real_blocks=real_blocks),
        grid=(b, g, t // tq),
        in_specs=[
            pl.BlockSpec(memory_space=pltpu.SMEM),
            pl.BlockSpec((1, NSA_GROUP, tq, dh), lambda bi, gi, qi: (bi, gi, qi, 0)),
            c_spec, c_spec, kv_spec, kv_spec, kv_spec, kv_spec,
            pl.BlockSpec((1, 1, tq, LANES), lambda bi, gi, qi: (bi, gi, qi, 0)),
            pl.BlockSpec(pool.shape, lambda bi, gi, qi: (0, 0)),
            pl.BlockSpec(expand.shape, lambda bi, gi, qi: (0, 0, 0)),
        ],
        out_specs=pl.BlockSpec((1, 1, tq, NSA_GROUP * dh), lambda bi, gi, qi: (bi, gi, qi, 0)),
        out_shape=jax.ShapeDtypeStruct((b, g, t, NSA_GROUP * dh), BF16),
        scratch_shapes=[pltpu.SMEM((t // tk,), I32)],
        compiler_params=_params("parallel", "parallel", "arbitrary"),
        name="nsa_attn",
    )(slopes, q, kc, vc, ks, vs, kw, vw, gate_logits, pool, expand)


def _mem_kernel(q_ref, kv_ref, o_ref):
    q = q_ref[0]
    kv = kv_ref[0]
    outs = []
    for h in range(MEM_HEADS):
        sl = slice(h * HEAD_DIM, (h + 1) * HEAD_DIM)
        s = _dot_nt(q[:, sl], kv[:, sl]) * SCALE
        p = jnp.exp(s - jnp.max(s, axis=1, keepdims=True))
        p = p / jnp.sum(p, axis=1, keepdims=True)
        outs.append(_dot(p.astype(BF16), kv[:, MEM_W + h * HEAD_DIM:MEM_W + (h + 1) * HEAD_DIM]))
    o_ref[0] = jnp.concatenate(outs, axis=1).astype(o_ref.dtype)


def _mem_attention(q, mkv, tq=512):
    b, t, w = q.shape
    m = mkv.shape[1]
    return pl.pallas_call(
        _mem_kernel,
        grid=(b, t // tq),
        in_specs=[
            pl.BlockSpec((1, tq, w), lambda bi, qi: (bi, qi, 0)),
            pl.BlockSpec((1, m, 2 * w), lambda bi, qi: (bi, 0, 0)),
        ],
        out_specs=pl.BlockSpec((1, tq, w), lambda bi, qi: (bi, qi, 0)),
        out_shape=jax.ShapeDtypeStruct((b, t, w), BF16),
        compiler_params=_params("parallel", "parallel"),
        name="mem_attn",
    )(q, mkv)


def _merge_kernel(sb_ref, nsa_ref, mem_ref, mg_ref, bm_ref, x_ref, wsb_ref, wnsa_ref, wmem_ref,
                  wout_ref, fg_ref, wq_ref, x1_ref, h2_ref, q_ref):
    d = x_ref.shape[1]
    branches = (_dot(sb_ref[...], wsb_ref[...]), _dot(nsa_ref[...], wnsa_ref[...]),
                _dot(mem_ref[...], wmem_ref[...]))
    merged = jnp.zeros_like(branches[0])
    for j in range(N_BRANCHES):
        gate = _sigmoid(mg_ref[:, j * d:(j + 1) * d] + bm_ref[:, j * d:(j + 1) * d])
        merged = merged + gate * branches[j]
    x1 = x_ref[...] + _dot(merged.astype(BF16), wout_ref[...])
    x1_ref[...] = x1
    h2 = _rms(x1, fg_ref[...])
    h2_ref[...] = h2
    q = _dot(h2.astype(BF16), wq_ref[...]).astype(q_ref.dtype)
    for c in range(q_ref.shape[0]):
        q_ref[c] = q[:, c * LANES:(c + 1) * LANES]


def _merge(sb, nsa, mem, gates_slab, b_merge, x2d, w_sb, w_nsa, w_mem, w_out, ffn_g, w_q, tm=256):
    n, d = x2d.shape
    nq = w_q.shape[1] // LANES
    row = lambda w: pl.BlockSpec((tm, w), lambda i: (i, 0))
    full = lambda a: pl.BlockSpec(a.shape, lambda i: (0,) * a.ndim)
    bm = b_merge.reshape(1, -1)
    fg = ffn_g.reshape(1, d)
    return pl.pallas_call(
        _merge_kernel,
        grid=(n // tm,),
        in_specs=[row(sb.shape[1]), row(nsa.shape[1]), row(mem.shape[1]), row(N_BRANCHES * d),
                  full(bm), row(d), full(w_sb), full(w_nsa), full(w_mem), full(w_out), full(fg),
                  full(w_q)],
        out_specs=[row(d), row(d), pl.BlockSpec((nq, tm, LANES), lambda i: (0, i, 0))],
        out_shape=[jax.ShapeDtypeStruct((n, d), F32), jax.ShapeDtypeStruct((n, d), F32),
                   jax.ShapeDtypeStruct((nq, n, LANES), BF16)],
        compiler_params=_params("parallel"),
        name="merge",
    )(sb, nsa, mem, gates_slab, bm, x2d, w_sb, w_nsa, w_mem, w_out, fg, w_q)


def _peer_topk_kernel(q_ref, sk_ref, idx_ref, gw_ref, tv_ref, ti_ref, bv_ref):
    tt = q_ref.shape[1]
    kk = PEER_TOPK
    nk = PEER_N_KEYS
    rid = lax.broadcasted_iota(I32, (nk, tt), 0).astype(F32)
    for c in range(2 * PEER_HEADS):
        scores = _dot_nt(sk_ref[c], q_ref[c])

        def pick(k, s, c=c):
            best = jnp.max(s, axis=0, keepdims=True)
            first = jnp.min(jnp.where(s == best, rid, float(nk)), axis=0, keepdims=True)
            tv_ref[c, pl.ds(k, 1), :] = best
            ti_ref[c, pl.ds(k, 1), :] = first
            return jnp.where(rid == first, -jnp.inf, s)

        lax.fori_loop(0, kk, pick, scores)

    sub = 8
    widths = [min(kk, -(-(kk // (i + 1)) // sub) * sub) for i in range(kk // 2)]
    n_cand = sum(widths) + kk // 2
    pos = lax.broadcasted_iota(I32, (n_cand, tt), 0).astype(F32)
    for h in range(PEER_HEADS):
        s0, s1 = tv_ref[2 * h], tv_ref[2 * h + 1]
        i0, i1 = ti_ref[2 * h], ti_ref[2 * h + 1]
        cand, cidx = [], []
        for i, wd in enumerate(widths):
            keep = lax.broadcasted_iota(I32, (wd, 1), 0) < kk // (i + 1)
            cand.append(jnp.where(keep, s0[i:i + 1, :] + s1[0:wd, :], -jnp.inf))
            cidx.append(i0[i:i + 1, :] * float(nk) + i1[0:wd, :])
        cand.append(s0[kk // 2:, :] + s1[0:1, :])
        cidx.append(i0[kk // 2:, :] * float(nk) + i1[0:1, :])
        cand = jnp.concatenate(cand, axis=0)
        cidx = jnp.concatenate(cidx, axis=0)

        def pick2(k, s, h=h, cidx=cidx):
            best = jnp.max(s, axis=0, keepdims=True)
            first = jnp.min(jnp.where(s == best, pos, float(n_cand)), axis=0, keepdims=True)
            hit = pos == first
            bv_ref[pl.ds(k, 1), :] = best
            expert = jnp.max(jnp.where(hit, cidx, -1.0), axis=0, keepdims=True)
            idx_ref[pl.ds(h * kk + k, 1), :] = expert.astype(I32)
            return jnp.where(hit, -jnp.inf, s)

        lax.fori_loop(0, kk, pick2, cand)
        best = bv_ref[...]
        e = jnp.exp(best - best[0:1, :])
        gw_ref[h * kk:(h + 1) * kk, :] = e / jnp.sum(e, axis=0, keepdims=True)


def _peer_topk(q_chunks, subkeys, tt=256):
    nchunk, n, half = q_chunks.shape
    slots = PEER_HEADS * PEER_TOPK
    return pl.pallas_call(
        _peer_topk_kernel,
        grid=(n // tt,),
        in_specs=[
            pl.BlockSpec((nchunk, tt, half), lambda i: (0, i, 0)),
            pl.BlockSpec(subkeys.shape, lambda i: (0, 0, 0)),
        ],
        out_specs=[pl.BlockSpec((slots, tt), lambda i: (0, i)),
                   pl.BlockSpec((slots, tt), lambda i: (0, i))],
        out_shape=[jax.ShapeDtypeStruct((slots, n), I32), jax.ShapeDtypeStruct((slots, n), F32)],
        scratch_shapes=[pltpu.VMEM((nchunk, PEER_TOPK, tt), F32),
                        pltpu.VMEM((nchunk, PEER_TOPK, tt), F32),
                        pltpu.VMEM((PEER_TOPK, tt), F32)],
        compiler_params=_params("parallel"),
        name="peer_topk",
    )(q_chunks, subkeys)


def _peer_ffn_kernel(idx_hbm, uv_hbm, gw_ref, h_ref, x1_ref, fg_ref, y_ref, idx_smem, buf, idx_sem,
                     row_sem):
    tt, d = h_ref.shape
    slots = gw_ref.shape[0]
    rows = d // (2 * LANES)
    step = pl.program_id(0)

    def halves(words):
        return tuple(pltpu.unpack_elementwise(words, index=i, packed_dtype=BF16, unpacked_dtype=F32)
                     for i in range(2))

    idx_copy = pltpu.make_async_copy(idx_hbm.at[step], idx_smem, idx_sem)
    idx_copy.start()
    idx_copy.wait()

    nbuf = buf.shape[0]

    def start(tok, slot):
        for e in range(slots):
            pltpu.make_async_copy(uv_hbm.at[idx_smem[tok * slots + e]], buf.at[slot, :, e, :],
                                  row_sem.at[slot]).start(priority=e % 2)

    def wait(slot):
        pltpu.make_async_copy(buf.at[slot], buf.at[slot], row_sem.at[slot]).wait()

    tok_lane = lax.broadcasted_iota(I32, (1, tt), 1)

    def expert_weights(tok, slot):
        h = h_ref[pl.ds(tok, 1), :]
        prod = jnp.zeros((slots, LANES), F32)
        for r in range(rows):
            lo, hi = halves(buf[slot, r])
            prod = prod + lo * h[:, r * LANES:(r + 1) * LANES]
            prod = prod + hi * h[:, d // 2 + r * LANES:d // 2 + (r + 1) * LANES]
        a = jnp.sum(prod, axis=1, keepdims=True)
        act = 0.5 * a * (1.0 + lax.erf(a * (2.0 ** -0.5)))
        gate = jnp.sum(jnp.where(tok_lane == tok, gw_ref[...], 0.0), axis=1, keepdims=True)
        return gate * act

    def weighted_values(tok, slot, w):
        parts = [halves(buf[slot, rows + r]) for r in range(rows)]
        y_ref[pl.ds(tok, 1), :] = jnp.concatenate(
            [jnp.sum(parts[r][i] * w, axis=0, keepdims=True) for i in range(2) for r in range(rows)], axis=1)

    for tok in range(nbuf):
        start(tok, tok)
    wait(0)
    w = expert_weights(0, 0)

    def step(tok, slot, w, refill):
        nxt = (slot + 1) % nbuf
        wait(nxt)
        w_next = expert_weights(tok + 1, nxt)
        weighted_values(tok, slot, w)
        if refill:
            start(tok + nbuf, slot)
        return w_next

    def round_of_steps(i, w):
        for slot in range(nbuf):
            w = step(i * nbuf + slot, slot, w, True)
        return w

    n_rounds = tt // nbuf - 1
    w = lax.fori_loop(0, n_rounds, round_of_steps, w)
    for slot in range(nbuf - 1):
        w = step(n_rounds * nbuf + slot, slot, w, False)
    weighted_values(tt - 1, nbuf - 1, w)
    y_ref[...] = _rms(x1_ref[...] + y_ref[...], fg_ref[...])


def _pack_tables_kernel(u_ref, v_ref, o_ref):
    d = u_ref.shape[1]
    rows = d // (2 * LANES)
    for base, src in ((0, u_ref), (rows, v_ref)):
        for r in range(rows):
            lo = src[:, r * LANES:(r + 1) * LANES]
            hi = src[:, d // 2 + r * LANES:d // 2 + (r + 1) * LANES]
            o_ref[:, base + r, :] = pltpu.pack_elementwise([lo, hi], packed_dtype=BF16)


def _pack_tables(u, v, te=512):
    n_exp, d = u.shape
    rows = d // LANES
    return pl.pallas_call(
        _pack_tables_kernel,
        grid=(n_exp // te,),
        in_specs=[pl.BlockSpec((te, d), lambda i: (i, 0)), pl.BlockSpec((te, d), lambda i: (i, 0))],
        out_specs=pl.BlockSpec((te, rows, LANES), lambda i: (i, 0, 0)),
        out_shape=jax.ShapeDtypeStruct((n_exp, rows, LANES), jnp.uint32),
        compiler_params=_params("parallel"),
        name="pack_tables",
    )(u, v)


def _peer_ffn(idx_tiles, uv, gw, h2, x1, final_g, tt):
    n, d = h2.shape
    slots = gw.shape[0]
    return pl.pallas_call(
        _peer_ffn_kernel,
        grid=(n // tt,),
        in_specs=[
            pl.BlockSpec(memory_space=pl.ANY),
            pl.BlockSpec(memory_space=pl.ANY),
            pl.BlockSpec((slots, tt), lambda i: (0, i)),
            pl.BlockSpec((tt, d), lambda i: (i, 0)),
            pl.BlockSpec((tt, d), lambda i: (i, 0)),
            pl.BlockSpec((1, d), lambda i: (0, 0)),
        ],
        out_specs=pl.BlockSpec((tt, d), lambda i: (i, 0)),
        out_shape=jax.ShapeDtypeStruct((n, d), F32),
        scratch_shapes=[pltpu.SMEM((tt * slots,), I32),
                        pltpu.VMEM((PEER_GATHER_BUFFERS, uv.shape[1], slots, uv.shape[2]), uv.dtype),
                        pltpu.SemaphoreType.DMA, pltpu.SemaphoreType.DMA((PEER_GATHER_BUFFERS,))],
        compiler_params=_params("arbitrary"),
        name="peer_ffn",
    )(idx_tiles, uv, gw, h2, x1, final_g.reshape(1, d))


def _layer(x, mem, mix_g, mem_g, w_in, b_merge, pe_k, pe_v, cw_k, cw_v, w_mem_kv, w_sb_br, w_nsa_br,
           w_mem_br, w_out, ffn_g, peer_w_q, subkeys, uv, out_g):
    b, t, d = x.shape
    m = mem.shape[1]
    g = NSA_KV_GROUPS
    n = b * t
    x2d = x.reshape(n, d)

    o_sbq, o_sbk, o_sbv = 0, SB_W, 2 * SB_W
    o_nq = 3 * SB_W
    o_nkv = o_nq + NSA_W
    o_ng = o_nkv + 6 * NSA_KV_W
    o_mq = o_ng + NSA_HEADS * 3
    o_mg = o_mq + MEM_W
    w_act = jnp.concatenate([w_in[:, :o_ng], w_in[:, o_mq:o_mg]], axis=1).astype(BF16)
    gate_pad = LANES - NSA_HEADS * 3
    w_gate = jnp.concatenate([w_in[:, o_mg:], w_in[:, o_ng:o_mq], jnp.zeros((d, gate_pad), w_in.dtype)],
                             axis=1).astype(BF16)
    act = _norm_matmul(x2d, mix_g, w_act, BF16, tm=256, tn=512)
    gates_slab = _norm_matmul(x2d, mix_g, w_gate, F32, tm=256, tn=640)

    def heads(lo, nh):
        return act[:, lo:lo + nh * HEAD_DIM].reshape(b, t, nh, HEAD_DIM).transpose(0, 2, 1, 3)

    sb_out = _sb_attention(heads(o_sbq, SB_HEADS), heads(o_sbk, SB_HEADS), heads(o_sbv, SB_HEADS))

    kv = [heads(o_nkv + j * NSA_KV_W, g) for j in range(6)]
    kc = _nsa_compress(kv[0], pe_k, cw_k)
    vc = _nsa_compress(kv[1], pe_v, cw_v)
    gl = gates_slab[:, N_BRANCHES * d:N_BRANCHES * d + NSA_HEADS * 3]
    gl = gl.reshape(b, t, g, NSA_GROUP * 3).transpose(0, 2, 1, 3)
    gl = jnp.pad(gl, ((0, 0), (0, 0), (0, 0), (0, LANES - NSA_GROUP * 3)))
    slopes = jnp.asarray([2.0 ** (-8.0 * (h + 1) / NSA_HEADS) for h in range(NSA_HEADS)], F32)
    nsa_out = _nsa_attention(heads(o_nq, NSA_HEADS), kc, vc, kv[2], kv[3], kv[4], kv[5], gl, slopes)
    nsa_out = nsa_out.transpose(0, 2, 1, 3).reshape(n, NSA_W)

    mkv = _norm_matmul(mem.reshape(b * m, d), mem_g, w_mem_kv.astype(BF16), BF16, tm=256, tn=512)
    mem_q = act[:, o_nkv + 6 * NSA_KV_W:].reshape(b, t, MEM_W)
    mem_out = _mem_attention(mem_q, mkv.reshape(b, m, 2 * MEM_W))

    x1, h2, q_chunks = _merge(
        sb_out.reshape(n, SB_W), nsa_out, mem_out.reshape(n, MEM_W), gates_slab, b_merge, x2d,
        w_sb_br.astype(BF16), w_nsa_br.astype(BF16), w_mem_br.astype(BF16), w_out.astype(BF16), ffn_g,
        peer_w_q.astype(BF16))

    half = PEER_QUERY_DIM // 2
    idx, gw = _peer_topk(q_chunks, subkeys.reshape(2 * PEER_HEADS, PEER_N_KEYS, half).astype(BF16))
    tt = 128
    idx_tiles = idx.T.reshape(n // tt, tt * idx.shape[0])
    y = _peer_ffn(idx_tiles, uv, gw, h2, x1, out_g, tt)
    return y.reshape(b, t, d)


def kernel(x, mem, mix_norm_g, mem_norm_g, w_in, b_merge, cmp_pe_k, cmp_pe_v, cmp_w_k, cmp_w_v, w_mem_kv, w_sb_br, w_nsa_br, w_mem_br, w_out, ffn_norm_g, peer_w_q, peer_subkeys, peer_u, peer_v, final_norm_g):
    depth = w_in.shape[0]
    assert depth == 1, "the final rmsnorm is fused into the last layer's PEER kernel"
    l = 0
    uv = _pack_tables(peer_u[l], peer_v[l])
    return _layer(x, mem, mix_norm_g[l], mem_norm_g[l], w_in[l], b_merge[l], cmp_pe_k[l], cmp_pe_v[l],
                  cmp_w_k[l], cmp_w_v[l], w_mem_kv[l], w_sb_br[l], w_nsa_br[l], w_mem_br[l], w_out[l],
                  ffn_norm_g[l], peer_w_q[l], peer_subkeys[l], uv, final_norm_g)
```

```python
import functools
import math

import jax
import jax.numpy as jnp
from jax import lax
from jax.experimental import pallas as pl
from jax.experimental.pallas import tpu as pltpu
from jax.experimental.pallas import tpu_sc as plsc

F32 = jnp.float32
BF16 = jnp.bfloat16
I32 = jnp.int32

HEAD_DIM = 64
SB_HEADS = 6
NSA_HEADS = 6
NSA_KV_GROUPS = 2
NSA_GROUP = NSA_HEADS // NSA_KV_GROUPS
MEM_HEADS = 4
N_BRANCHES = 3
SB_W = SB_HEADS * HEAD_DIM
NSA_W = NSA_HEADS * HEAD_DIM
NSA_KV_W = NSA_KV_GROUPS * HEAD_DIM
MEM_W = MEM_HEADS * HEAD_DIM
CMP_LEN = 32
CMP_STRIDE = 16
SEL_BLOCK = 64
N_SELECT = 16
WINDOW = 512
FORCED_SCORE = 1e4
PEER_HEADS = 8
PEER_N_KEYS = 128
PEER_QUERY_DIM = 256
PEER_TOPK = 16
RMS_EPS = 1e-6
NEG_INF = -1e30
SCALE = HEAD_DIM ** -0.5
SB_DEAD_LOG = 104.0

LANES = 128
PEER_GATHER_BUFFERS = 8
SC_GATHER_WINDOW = 32
SC_GATHER_SHARE = 0.3
VMEM_LIMIT_BYTES = 56 * 1024 * 1024

_NT = (((1,), (1,)), ((), ()))


def _params(*sem):
    return pltpu.CompilerParams(dimension_semantics=sem, vmem_limit_bytes=VMEM_LIMIT_BYTES)


def _dot(a, b):
    return jnp.dot(a, b, preferred_element_type=F32)


def _dot_nt(a, b):
    return lax.dot_general(a, b, _NT, preferred_element_type=F32)


def _sigmoid(x):
    return 1.0 / (1.0 + jnp.exp(-x))


def _rms(x, g):
    return x * lax.rsqrt(jnp.mean(x * x, axis=-1, keepdims=True) + RMS_EPS) * g


def _norm_matmul_kernel(x_ref, g_ref, w_ref, o_ref, *, tn):
    h = _rms(x_ref[...], g_ref[...]).astype(BF16)
    for c in range(0, o_ref.shape[1], tn):
        o_ref[:, c:c + tn] = _dot(h, w_ref[:, c:c + tn]).astype(o_ref.dtype)


def _norm_matmul(x2d, g, w, out_dtype, tm, tn):
    n, d = x2d.shape
    m = w.shape[1]
    return pl.pallas_call(
        functools.partial(_norm_matmul_kernel, tn=tn),
        grid=(n // tm,),
        in_specs=[
            pl.BlockSpec((tm, d), lambda i: (i, 0)),
            pl.BlockSpec((1, d), lambda i: (0, 0)),
            pl.BlockSpec((d, m), lambda i: (0, 0)),
        ],
        out_specs=pl.BlockSpec((tm, m), lambda i: (i, 0)),
        out_shape=jax.ShapeDtypeStruct((n, m), out_dtype),
        compiler_params=_params("parallel"),
        name="norm_matmul",
    )(x2d, g.reshape(1, d), w)


def _sb_kernel(q_ref, k_ref, v_ref, o_ref, *, tile):
    qi = pl.program_id(2)
    nh = q_ref.shape[1]
    row = lax.broadcasted_iota(I32, (tile, tile), 0)
    col = lax.broadcasted_iota(I32, (tile, tile), 1)
    lower = row > col
    later = lower.astype(BF16)
    qs = [q_ref[0, hh] for hh in range(nh)]

    def visit(hh, ks, c, acc, diagonal):
        k = k_ref[0, hh, pl.ds(ks, tile), :]
        v = v_ref[0, hh, pl.ds(ks, tile), :]
        z = _dot_nt(qs[hh], k) * SCALE
        sp = jnp.maximum(z, 0.0) + jnp.log(1.0 + jnp.exp(-jnp.abs(z)))
        if diagonal:
            sp = jnp.where(lower, sp, 0.0)
        hi = sp.astype(BF16)
        lo = (sp - hi.astype(F32)).astype(BF16)
        after = _dot(hi, later) + _dot(lo, later)
        a = jnp.exp(z - sp - after - c)
        if diagonal:
            a = jnp.where(lower, a, 0.0)
        return c + jnp.sum(sp, axis=1, keepdims=True), acc + _dot(a.astype(BF16), v)

    q0 = pl.multiple_of(qi * tile, tile)
    state = []
    for hh in range(nh):
        state.extend(visit(hh, q0, jnp.zeros((tile, 1), F32), jnp.zeros((tile, HEAD_DIM), F32), True))

    def smallest_carry(state):
        c = state[0]
        for hh in range(1, nh):
            c = jnp.minimum(c, state[2 * hh])
        return jnp.min(c)

    def live(carry):
        i, c_min, _ = carry
        return (i < qi) & (c_min <= SB_DEAD_LOG)

    def body(carry):
        i, _, state = carry
        ks = pl.multiple_of((qi - 1 - i) * tile, tile)
        out = []
        for hh in range(nh):
            out.extend(visit(hh, ks, state[2 * hh], state[2 * hh + 1], False))
        return i + 1, smallest_carry(out), tuple(out)

    _, _, state = lax.while_loop(live, body, (0, smallest_carry(state), tuple(state)))
    o_ref[0] = jnp.concatenate([state[2 * hh + 1] for hh in range(nh)], axis=1).astype(o_ref.dtype)


def _sb_attention(q, k, v, tile=256):
    b, h, t, dh = q.shape
    hp = 2
    tq = tile
    return pl.pallas_call(
        functools.partial(_sb_kernel, tile=tile),
        grid=(b, h // hp, t // tq),
        in_specs=[
            pl.BlockSpec((1, hp, tq, dh), lambda bi, hi, qi: (bi, hi, qi, 0)),
            pl.BlockSpec((1, hp, t, dh), lambda bi, hi, qi: (bi, hi, 0, 0)),
            pl.BlockSpec((1, hp, t, dh), lambda bi, hi, qi: (bi, hi, 0, 0)),
        ],
        out_specs=pl.BlockSpec((1, tq, hp * dh), lambda bi, hi, qi: (bi, qi, hi)),
        out_shape=jax.ShapeDtypeStruct((b, t, h * dh), BF16),
        compiler_params=_params("parallel", "parallel", "arbitrary"),
        name="sb_attn",
    )(q, k, v)


def _compress_kernel(x_ref, pe_ref, w_ref, o_ref):
    x = x_ref[0, 0]
    nc = x.shape[0]
    w_lo, w_hi = w_ref[0], w_ref[1]
    first = _dot(x, w_lo)
    second = _dot(x, w_hi)
    feat = pe_ref.shape[1]
    pe_lo = jnp.broadcast_to(pe_ref[0:1, :], (8, feat)).astype(BF16)
    pe_hi = jnp.broadcast_to(pe_ref[1:2, :], (8, feat)).astype(BF16)
    bias = _dot(pe_lo, w_lo)[0:1] + _dot(pe_hi, w_hi)[0:1]
    o_ref[0, 0] = (first + pltpu.roll(second, nc - 1, 0) + bias).astype(o_ref.dtype)


def _nsa_compress(z, pe, w):
    b, g, t, dh = z.shape
    nc = t // CMP_STRIDE
    feat = CMP_STRIDE * dh
    x = z.reshape(b, g, nc, feat)
    pe2 = pe.reshape(2, feat)
    w2 = w.reshape(2, feat, dh).astype(BF16)
    return pl.pallas_call(
        _compress_kernel,
        grid=(b, g),
        in_specs=[
            pl.BlockSpec((1, 1, nc, feat), lambda bi, gi: (bi, gi, 0, 0)),
            pl.BlockSpec((2, feat), lambda bi, gi: (0, 0)),
            pl.BlockSpec((2, feat, dh), lambda bi, gi: (0, 0, 0)),
        ],
        out_specs=pl.BlockSpec((1, 1, nc, dh), lambda bi, gi: (bi, gi, 0, 0)),
        out_shape=jax.ShapeDtypeStruct((b, g, nc, dh), BF16),
        compiler_params=_params("parallel", "parallel"),
        name="nsa_compress",
    )(x, pe2, w2)


def _nsa_kernel(slopes_ref, q_ref, kc_ref, vc_ref, ks_ref, vs_ref, kw_ref, vw_ref, gl_ref, pool_ref, ex_ref,
                o_ref, used_ref, *, tq, tk, n_sel, real_blocks):
    grp = pl.program_id(1)
    t0 = pl.program_id(2) * tq
    nc = kc_ref.shape[2]
    rr = NSA_GROUP
    t = t0 + lax.broadcasted_iota(I32, (tq, 1), 0)

    def stack(x):
        return jnp.concatenate([x] * rr, axis=0)

    q = q_ref[0].reshape(rr * tq, HEAD_DIM)
    slope = jnp.concatenate([jnp.full((tq, 1), slopes_ref[grp * rr + r], F32) for r in range(rr)], axis=0)

    def heads_sum(x):
        out = x[0:tq]
        for r in range(1, rr):
            out = out + x[r * tq:(r + 1) * tq]
        return out

    def masked_softmax(s, valid):
        p = jnp.where(valid, jnp.exp(s - jnp.max(s, axis=1, keepdims=True)), 0.0)
        denom = jnp.sum(p, axis=1, keepdims=True)
        return p / jnp.where(denom > 0, denom, 1.0)

    lane = lax.broadcasted_iota(I32, (1, nc), 1)
    dist_c = (t - (lane * CMP_STRIDE + (CMP_LEN - 1))).astype(F32)
    valid_c = stack(dist_c >= 0)
    s = _dot_nt(q, kc_ref[0, 0]) * SCALE - slope * stack(dist_c)
    p = masked_softmax(jnp.where(valid_c, s, NEG_INF), valid_c)
    o_cmp = _dot(p.astype(BF16), vc_ref[0, 0])
    psum = heads_sum(p)

    n_blk = pool_ref.shape[0]
    hi = psum.astype(BF16)
    rest = psum - hi.astype(F32)
    mid = rest.astype(BF16)
    lo = (rest - mid.astype(F32)).astype(BF16)
    pool = pool_ref[...]
    imp = _dot_nt(pool, hi) + _dot_nt(pool, mid) + _dot_nt(pool, lo)
    blk = lax.broadcasted_iota(I32, (n_blk, 1), 0)
    cur = (t0 + lax.broadcasted_iota(I32, (1, tq), 1)) // SEL_BLOCK
    forced = (blk == 0) | (blk == cur) | (blk == cur - 1)
    imp = jnp.where(forced, FORCED_SCORE, jnp.where(blk <= cur, imp, -1.0))
    imp = jnp.where(blk < real_blocks, imp, -jnp.inf)
    blk_f = blk.astype(F32)

    def pick(_, carry):
        imp, sel = carry
        best = jnp.max(imp, axis=0, keepdims=True)
        first = jnp.min(jnp.where(imp == best, blk_f, float(n_blk)), axis=0, keepdims=True)
        hit = blk_f == first
        return jnp.where(hit, -jnp.inf, imp), jnp.where(hit, 1.0, sel)

    _, sel = lax.fori_loop(0, n_sel, pick, (imp, jnp.zeros((n_blk, tq), F32)))
    picked = jnp.max(sel, axis=1, keepdims=True)
    per_tile = tk // SEL_BLOCK
    n_causal = (t0 + tq + tk - 1) // tk
    n_used = jnp.int32(0)
    for j in range(used_ref.shape[0]):
        used_ref[n_used] = j
        hit = (jnp.max(picked[j * per_tile:(j + 1) * per_tile, :]) > 0) & (j < n_causal)
        n_used = n_used + hit.astype(I32)
    sel = sel.T.astype(BF16)

    span = WINDOW + tq
    w0 = pl.multiple_of(jnp.maximum(t0 - WINDOW, 0), tq)
    dist_w = t - (w0 + lax.broadcasted_iota(I32, (1, span), 1))
    valid_w = stack((dist_w >= 0) & (dist_w < WINDOW))
    s = _dot_nt(q, kw_ref[0, 0, pl.ds(w0, span), :]) * SCALE - slope * stack(dist_w.astype(F32))
    p = masked_softmax(jnp.where(valid_w, s, NEG_INF), valid_w)
    o_win = _dot(p.astype(BF16), vw_ref[0, 0, pl.ds(w0, span), :])

    kcol = lax.broadcasted_iota(I32, (1, tk), 1)

    def sel_body(i, carry):
        m, l, acc = carry
        kb = used_ref[i]
        ks0 = pl.multiple_of(kb * tk, tk)
        chosen = _dot(sel, ex_ref[kb]) > 0.5
        dist = t - (ks0 + kcol)
        valid = stack(chosen & (dist >= 0))
        s = _dot_nt(q, ks_ref[0, 0, pl.ds(ks0, tk), :]) * SCALE - slope * stack(dist.astype(F32))
        s = jnp.where(valid, s, NEG_INF)
        m_new = jnp.maximum(m, jnp.max(s, axis=1, keepdims=True))
        alpha = jnp.exp(m - m_new)
        p = jnp.where(valid, jnp.exp(s - m_new), 0.0)
        l = alpha * l + jnp.sum(p, axis=1, keepdims=True)
        acc = alpha * acc + _dot(p.astype(BF16), vs_ref[0, 0, pl.ds(ks0, tk), :])
        return m_new, l, acc

    init = (jnp.full((rr * tq, 1), NEG_INF, F32), jnp.zeros((rr * tq, 1), F32),
            jnp.zeros((rr * tq, HEAD_DIM), F32))
    _, l_sel, acc_sel = lax.fori_loop(0, n_used, sel_body, init)
    o_sel = acc_sel / l_sel

    gates = _sigmoid(gl_ref[0, 0])
    outs = []
    for r in range(rr):
        rows = slice(r * tq, (r + 1) * tq)
        outs.append(gates[:, 3 * r:3 * r + 1] * o_cmp[rows] + gates[:, 3 * r + 1:3 * r + 2] * o_sel[rows]
                    + gates[:, 3 * r + 2:3 * r + 3] * o_win[rows])
    o_ref[0, 0] = jnp.concatenate(outs, axis=1).astype(o_ref.dtype)


def _nsa_attention(q, kc, vc, ks, vs, kw, vw, gate_logits, slopes, tq=128, tk=256):
    b, _, t, dh = q.shape
    g = NSA_KV_GROUPS
    nc = kc.shape[2]
    n_sel = min(N_SELECT, t // SEL_BLOCK)
    real_blocks = t // SEL_BLOCK
    n_blk = LANES
    assert t % tk == 0 and t >= WINDOW + tq and real_blocks <= n_blk
    blk_ids = jnp.arange(n_blk, dtype=I32)
    pool = (blk_ids[:, None] == jnp.arange(nc, dtype=I32)[None, :] // (SEL_BLOCK // CMP_STRIDE)).astype(BF16)
    key_blk = (jnp.arange(t, dtype=I32) // SEL_BLOCK).reshape(t // tk, 1, tk)
    expand = (blk_ids[None, :, None] == key_blk).astype(BF16)
    kv_spec = pl.BlockSpec((1, 1, t, dh), lambda bi, gi, qi: (bi, gi, 0, 0))
    c_spec = pl.BlockSpec((1, 1, nc, dh), lambda bi, gi, qi: (bi, gi, 0, 0))
    return pl.pallas_call(
        functools.partial(_nsa_kernel, tq=tq, tk=tk, n_sel=n_sel, real_blocks=real_blocks),
        grid=(b, g, t // tq),
        in_specs=[
            pl.BlockSpec(memory_space=pltpu.SMEM),
            pl.BlockSpec((1, NSA_GROUP, tq, dh), lambda bi, gi, qi: (bi, gi, qi, 0)),
            c_spec, c_spec, kv_spec, kv_spec, kv_spec, kv_spec,
            pl.BlockSpec((1, 1, tq, LANES), lambda bi, gi, qi: (bi, gi, qi, 0)),
            pl.BlockSpec(pool.shape, lambda bi, gi, qi: (0, 0)),
            pl.BlockSpec(expand.shape, lambda bi, gi, qi: (0, 0, 0)),
        ],
        out_specs=pl.BlockSpec((1, 1, tq, NSA_GROUP * dh), lambda bi, gi, qi: (bi, gi, qi, 0)),
        out_shape=jax.ShapeDtypeStruct((b, g, t, NSA_GROUP * dh), BF16),
        scratch_shapes=[pltpu.SMEM((t // tk,), I32)],
        compiler_params=_params("parallel", "parallel", "arbitrary"),
        name="nsa_attn",
    )(slopes, q, kc, vc, ks, vs, kw, vw, gate_logits, pool, expand)


def _mem_kernel(q_ref, kv_ref, o_ref):
    q = q_ref[0]
    kv = kv_ref[0]
    outs = []
    for h in range(MEM_HEADS):
        sl = slice(h * HEAD_DIM, (h + 1) * HEAD_DIM)
        s = _dot_nt(q[:, sl], kv[:, sl]) * SCALE
        p = jnp.exp(s - jnp.max(s, axis=1, keepdims=True))
        p = p / jnp.sum(p, axis=1, keepdims=True)
        outs.append(_dot(p.astype(BF16), kv[:, MEM_W + h * HEAD_DIM:MEM_W + (h + 1) * HEAD_DIM]))
    o_ref[0] = jnp.concatenate(outs, axis=1).astype(o_ref.dtype)


def _mem_attention(q, mkv, tq=512):
    b, t, w = q.shape
    m = mkv.shape[1]
    return pl.pallas_call(
        _mem_kernel,
        grid=(b, t // tq),
        in_specs=[
            pl.BlockSpec((1, tq, w), lambda bi, qi: (bi, qi, 0)),
            pl.BlockSpec((1, m, 2 * w), lambda bi, qi: (bi, 0, 0)),
        ],
        out_specs=pl.BlockSpec((1, tq, w), lambda bi, qi: (bi, qi, 0)),
        out_shape=jax.ShapeDtypeStruct((b, t, w), BF16),
        compiler_params=_params("parallel", "parallel"),
        name="mem_attn",
    )(q, mkv)


def _merge_kernel(sb_ref, nsa_ref, mem_ref, mg_ref, bm_ref, x_ref, wsb_ref, wnsa_ref, wmem_ref,
                  wout_ref, fg_ref, wq_ref, x1_ref, h2_ref, q_ref):
    d = x_ref.shape[1]
    branches = (_dot(sb_ref[...], wsb_ref[...]), _dot(nsa_ref[...], wnsa_ref[...]),
                _dot(mem_ref[...], wmem_ref[...]))
    merged = jnp.zeros_like(branches[0])
    for j in range(N_BRANCHES):
        gate = _sigmoid(mg_ref[:, j * d:(j + 1) * d] + bm_ref[:, j * d:(j + 1) * d])
        merged = merged + gate * branches[j]
    x1 = x_ref[...] + _dot(merged.astype(BF16), wout_ref[...])
    x1_ref[...] = x1
    h2 = _rms(x1, fg_ref[...])
    h2_ref[...] = h2
    q = _dot(h2.astype(BF16), wq_ref[...]).astype(q_ref.dtype)
    for c in range(q_ref.shape[0]):
        q_ref[c] = q[:, c * LANES:(c + 1) * LANES]


def _merge(sb, nsa, mem, gates_slab, b_merge, x2d, w_sb, w_nsa, w_mem, w_out, ffn_g, w_q, tm=256):
    n, d = x2d.shape
    nq = w_q.shape[1] // LANES
    row = lambda w: pl.BlockSpec((tm, w), lambda i: (i, 0))
    full = lambda a: pl.BlockSpec(a.shape, lambda i: (0,) * a.ndim)
    bm = b_merge.reshape(1, -1)
    fg = ffn_g.reshape(1, d)
    return pl.pallas_call(
        _merge_kernel,
        grid=(n // tm,),
        in_specs=[row(sb.shape[1]), row(nsa.shape[1]), row(mem.shape[1]), row(N_BRANCHES * d),
                  full(bm), row(d), full(w_sb), full(w_nsa), full(w_mem), full(w_out), full(fg),
                  full(w_q)],
        out_specs=[row(d), row(d), pl.BlockSpec((nq, tm, LANES), lambda i: (0, i, 0))],
        out_shape=[jax.ShapeDtypeStruct((n, d), F32), jax.ShapeDtypeStruct((n, d), F32),
                   jax.ShapeDtypeStruct((nq, n, LANES), BF16)],
        compiler_params=_params("parallel"),
        name="merge",
    )(sb, nsa, mem, gates_slab, bm, x2d, w_sb, w_nsa, w_mem, w_out, fg, w_q)


def _peer_topk_kernel(q_ref, sk_ref, idx_ref, gw_ref, tv_ref, ti_ref, bv_ref):
    tt = q_ref.shape[1]
    kk = PEER_TOPK
    nk = PEER_N_KEYS
    rid = lax.broadcasted_iota(I32, (nk, tt), 0).astype(F32)
    for c in range(2 * PEER_HEADS):
        scores = _dot_nt(sk_ref[c], q_ref[c])

        def pick(k, s, c=c):
            best = jnp.max(s, axis=0, keepdims=True)
            first = jnp.min(jnp.where(s == best, rid, float(nk)), axis=0, keepdims=True)
            tv_ref[c, pl.ds(k, 1), :] = best
            ti_ref[c, pl.ds(k, 1), :] = first
            return jnp.where(rid == first, -jnp.inf, s)

        lax.fori_loop(0, kk, pick, scores)

    sub = 8
    widths = [min(kk, -(-(kk // (i + 1)) // sub) * sub) for i in range(kk // 2)]
    n_cand = sum(widths) + kk // 2
    pos = lax.broadcasted_iota(I32, (n_cand, tt), 0).astype(F32)
    for h in range(PEER_HEADS):
        s0, s1 = tv_ref[2 * h], tv_ref[2 * h + 1]
        i0, i1 = ti_ref[2 * h], ti_ref[2 * h + 1]
        cand, cidx = [], []
        for i, wd in enumerate(widths):
            keep = lax.broadcasted_iota(I32, (wd, 1), 0) < kk // (i + 1)
            cand.append(jnp.where(keep, s0[i:i + 1, :] + s1[0:wd, :], -jnp.inf))
            cidx.append(i0[i:i + 1, :] * float(nk) + i1[0:wd, :])
        cand.append(s0[kk // 2:, :] + s1[0:1, :])
        cidx.append(i0[kk // 2:, :] * float(nk) + i1[0:1, :])
        cand = jnp.concatenate(cand, axis=0)
        cidx = jnp.concatenate(cidx, axis=0)

        def pick2(k, s, h=h, cidx=cidx):
            best = jnp.max(s, axis=0, keepdims=True)
            first = jnp.min(jnp.where(s == best, pos, float(n_cand)), axis=0, keepdims=True)
            hit = pos == first
            bv_ref[pl.ds(k, 1), :] = best
            expert = jnp.max(jnp.where(hit, cidx, -1.0), axis=0, keepdims=True)
            idx_ref[pl.ds(h * kk + k, 1), :] = expert.astype(I32)
            return jnp.where(hit, -jnp.inf, s)

        lax.fori_loop(0, kk, pick2, cand)
        best = bv_ref[...]
        e = jnp.exp(best - best[0:1, :])
        gw_ref[h * kk:(h + 1) * kk, :] = e / jnp.sum(e, axis=0, keepdims=True)


def _peer_topk(q_chunks, subkeys, tt=256):
    nchunk, n, half = q_chunks.shape
    slots = PEER_HEADS * PEER_TOPK
    return pl.pallas_call(
        _peer_topk_kernel,
        grid=(n // tt,),
        in_specs=[
            pl.BlockSpec((nchunk, tt, half), lambda i: (0, i, 0)),
            pl.BlockSpec(subkeys.shape, lambda i: (0, 0, 0)),
        ],
        out_specs=[pl.BlockSpec((slots, tt), lambda i: (0, i)),
                   pl.BlockSpec((slots, tt), lambda i: (0, i))],
        out_shape=[jax.ShapeDtypeStruct((slots, n), I32), jax.ShapeDtypeStruct((slots, n), F32)],
        scratch_shapes=[pltpu.VMEM((nchunk, PEER_TOPK, tt), F32),
                        pltpu.VMEM((nchunk, PEER_TOPK, tt), F32),
                        pltpu.VMEM((PEER_TOPK, tt), F32)],
        compiler_params=_params("parallel"),
        name="peer_topk",
    )(q_chunks, subkeys)


def _peer_ffn_kernel(idx_hbm, uv_hbm, gw_ref, h_ref, x1_ref, fg_ref, y_ref, idx_smem, buf, idx_sem,
                     row_sem, *, first_tile, staged):
    tt, d = h_ref.shape
    slots = gw_ref.shape[0]
    rows = d // (2 * LANES)
    tile = pl.program_id(0)

    def halves(words):
        return tuple(pltpu.unpack_elementwise(words, index=i, packed_dtype=BF16, unpacked_dtype=F32)
                     for i in range(2))

    nbuf = buf.shape[0]

    if staged:
        def start(tok, slot):
            for r in range(buf.shape[1]):
                pltpu.make_async_copy(uv_hbm.at[tile * tt + tok, :, r, :], buf.at[slot, r],
                                      row_sem.at[slot]).start(priority=r % 2)
    else:
        idx_copy = pltpu.make_async_copy(idx_hbm.at[tile + first_tile], idx_smem, idx_sem)
        idx_copy.start()
        idx_copy.wait()

        def start(tok, slot):
            for e in range(slots):
                pltpu.make_async_copy(uv_hbm.at[idx_smem[tok * slots + e]], buf.at[slot, :, e, :],
                                      row_sem.at[slot]).start(priority=e % 2)

    def wait(slot):
        pltpu.make_async_copy(buf.at[slot], buf.at[slot], row_sem.at[slot]).wait()

    tok_lane = lax.broadcasted_iota(I32, (1, tt), 1)

    def expert_weights(tok, slot):
        h = h_ref[pl.ds(tok, 1), :]
        prod = jnp.zeros((slots, LANES), F32)
        for r in range(rows):
            lo, hi = halves(buf[slot, r])
            prod = prod + lo * h[:, r * LANES:(r + 1) * LANES]
            prod = prod + hi * h[:, d // 2 + r * LANES:d // 2 + (r + 1) * LANES]
        a = jnp.sum(prod, axis=1, keepdims=True)
        act = 0.5 * a * (1.0 + lax.erf(a * (2.0 ** -0.5)))
        gate = jnp.sum(jnp.where(tok_lane == tok, gw_ref[...], 0.0), axis=1, keepdims=True)
        return gate * act

    def weighted_values(tok, slot, w):
        parts = [halves(buf[slot, rows + r]) for r in range(rows)]
        y_ref[pl.ds(tok, 1), :] = jnp.concatenate(
            [jnp.sum(parts[r][i] * w, axis=0, keepdims=True) for i in range(2) for r in range(rows)], axis=1)

    for tok in range(nbuf):
        start(tok, tok)
    wait(0)
    w = expert_weights(0, 0)

    def step(tok, slot, w, refill):
        nxt = (slot + 1) % nbuf
        wait(nxt)
        w_next = expert_weights(tok + 1, nxt)
        weighted_values(tok, slot, w)
        if refill:
            start(tok + nbuf, slot)
        return w_next

    def round_of_steps(i, w):
        for slot in range(nbuf):
            w = step(i * nbuf + slot, slot, w, True)
        return w

    n_rounds = tt // nbuf - 1
    w = lax.fori_loop(0, n_rounds, round_of_steps, w)
    for slot in range(nbuf - 1):
        w = step(n_rounds * nbuf + slot, slot, w, False)
    weighted_values(tt - 1, nbuf - 1, w)
    y_ref[...] = _rms(x1_ref[...] + y_ref[...], fg_ref[...])


def _pack_tables_kernel(u_ref, v_ref, o_ref):
    d = u_ref.shape[1]
    rows = d // (2 * LANES)
    for base, src in ((0, u_ref), (rows, v_ref)):
        for r in range(rows):
            lo = src[:, r * LANES:(r + 1) * LANES]
            hi = src[:, d // 2 + r * LANES:d // 2 + (r + 1) * LANES]
            o_ref[:, base + r, :] = pltpu.pack_elementwise([lo, hi], packed_dtype=BF16)


def _pack_tables(u, v, te=512):
    n_exp, d = u.shape
    rows = d // LANES
    return pl.pallas_call(
        _pack_tables_kernel,
        grid=(n_exp // te,),
        in_specs=[pl.BlockSpec((te, d), lambda i: (i, 0)), pl.BlockSpec((te, d), lambda i: (i, 0))],
        out_specs=pl.BlockSpec((te, rows, LANES), lambda i: (i, 0, 0)),
        out_shape=jax.ShapeDtypeStruct((n_exp, rows, LANES), jnp.uint32),
        compiler_params=_params("parallel"),
        name="pack_tables",
    )(u, v)


def _sc_gather_slabs(slabs, slab_idx):
    m = slab_idx.shape[0]
    mesh = plsc.VectorSubcoreMesh(core_axis_name="core", subcore_axis_name="subcore")
    idx_rows = jnp.pad(slab_idx.reshape(m // SC_GATHER_WINDOW, SC_GATHER_WINDOW),
                       ((0, 0), (0, LANES - SC_GATHER_WINDOW)))

    @pl.kernel(out_type=jax.ShapeDtypeStruct((m,) + slabs.shape[1:], slabs.dtype), mesh=mesh)
    def gather(slabs_hbm, idx_hbm, out_hbm):
        def window(idx_vmem, out_vmem):
            pltpu.sync_copy(slabs_hbm.at[idx_vmem.at[0, pl.ds(0, SC_GATHER_WINDOW)]], out_vmem)

        pltpu.emit_pipeline(
            window,
            grid=(m // SC_GATHER_WINDOW,),
            in_specs=[pl.BlockSpec((1, LANES), index_map=lambda i: (i, 0))],
            out_specs=[pl.BlockSpec((SC_GATHER_WINDOW,) + slabs.shape[1:], index_map=lambda i: (i, 0, 0))],
            core_axis_name=("core", "subcore"),
            dimension_semantics=(pltpu.PARALLEL,),
            trace_scopes=False,
        )(idx_hbm, out_hbm)

    return gather(slabs, idx_rows)


def _peer_ffn(idx_tiles, table, gw, h2, x1, final_g, tt, first_tile, n_tiles, staged):
    _, d = h2.shape
    slots = gw.shape[0]
    rows = d // LANES
    return pl.pallas_call(
        functools.partial(_peer_ffn_kernel, first_tile=first_tile, staged=staged),
        grid=(n_tiles,),
        in_specs=[
            pl.BlockSpec(memory_space=pl.ANY),
            pl.BlockSpec(memory_space=pl.ANY),
            pl.BlockSpec((slots, tt), lambda i: (0, i + first_tile)),
            pl.BlockSpec((tt, d), lambda i: (i + first_tile, 0)),
            pl.BlockSpec((tt, d), lambda i: (i + first_tile, 0)),
            pl.BlockSpec((1, d), lambda i: (0, 0)),
        ],
        out_specs=pl.BlockSpec((tt, d), lambda i: (i, 0)),
        out_shape=jax.ShapeDtypeStruct((n_tiles * tt, d), F32),
        scratch_shapes=[pltpu.SMEM((tt * slots,), I32),
                        pltpu.VMEM((PEER_GATHER_BUFFERS, rows, slots, LANES), table.dtype),
                        pltpu.SemaphoreType.DMA, pltpu.SemaphoreType.DMA((PEER_GATHER_BUFFERS,))],
        compiler_params=_params("arbitrary"),
        name="peer_ffn_staged" if staged else "peer_ffn",
    )(idx_tiles, table, gw, h2, x1, final_g.reshape(1, d))


def _layer(x, mem, mix_g, mem_g, w_in, b_merge, pe_k, pe_v, cw_k, cw_v, w_mem_kv, w_sb_br, w_nsa_br,
           w_mem_br, w_out, ffn_g, peer_w_q, subkeys, uv, out_g):
    b, t, d = x.shape
    m = mem.shape[1]
    g = NSA_KV_GROUPS
    n = b * t
    x2d = x.reshape(n, d)

    o_sbq, o_sbk, o_sbv = 0, SB_W, 2 * SB_W
    o_nq = 3 * SB_W
    o_nkv = o_nq + NSA_W
    o_ng = o_nkv + 6 * NSA_KV_W
    o_mq = o_ng + NSA_HEADS * 3
    o_mg = o_mq + MEM_W
    w_act = jnp.concatenate([w_in[:, :o_ng], w_in[:, o_mq:o_mg]], axis=1).astype(BF16)
    gate_pad = LANES - NSA_HEADS * 3
    w_gate = jnp.concatenate([w_in[:, o_mg:], w_in[:, o_ng:o_mq], jnp.zeros((d, gate_pad), w_in.dtype)],
                             axis=1).astype(BF16)
    act = _norm_matmul(x2d, mix_g, w_act, BF16, tm=256, tn=512)
    gates_slab = _norm_matmul(x2d, mix_g, w_gate, F32, tm=256, tn=640)

    def heads(lo, nh):
        return act[:, lo:lo + nh * HEAD_DIM].reshape(b, t, nh, HEAD_DIM).transpose(0, 2, 1, 3)

    sb_out = _sb_attention(heads(o_sbq, SB_HEADS), heads(o_sbk, SB_HEADS), heads(o_sbv, SB_HEADS))

    kv = [heads(o_nkv + j * NSA_KV_W, g) for j in range(6)]
    kc = _nsa_compress(kv[0], pe_k, cw_k)
    vc = _nsa_compress(kv[1], pe_v, cw_v)
    gl = gates_slab[:, N_BRANCHES * d:N_BRANCHES * d + NSA_HEADS * 3]
    gl = gl.reshape(b, t, g, NSA_GROUP * 3).transpose(0, 2, 1, 3)
    gl = jnp.pad(gl, ((0, 0), (0, 0), (0, 0), (0, LANES - NSA_GROUP * 3)))
    slopes = jnp.asarray([2.0 ** (-8.0 * (h + 1) / NSA_HEADS) for h in range(NSA_HEADS)], F32)
    nsa_out = _nsa_attention(heads(o_nq, NSA_HEADS), kc, vc, kv[2], kv[3], kv[4], kv[5], gl, slopes)
    nsa_out = nsa_out.transpose(0, 2, 1, 3).reshape(n, NSA_W)

    mkv = _norm_matmul(mem.reshape(b * m, d), mem_g, w_mem_kv.astype(BF16), BF16, tm=256, tn=512)
    mem_q = act[:, o_nkv + 6 * NSA_KV_W:].reshape(b, t, MEM_W)
    mem_out = _mem_attention(mem_q, mkv.reshape(b, m, 2 * MEM_W))

    x1, h2, q_chunks = _merge(
        sb_out.reshape(n, SB_W), nsa_out, mem_out.reshape(n, MEM_W), gates_slab, b_merge, x2d,
        w_sb_br.astype(BF16), w_nsa_br.astype(BF16), w_mem_br.astype(BF16), w_out.astype(BF16), ffn_g,
        peer_w_q.astype(BF16))

    half = PEER_QUERY_DIM // 2
    idx, gw = _peer_topk(q_chunks, subkeys.reshape(2 * PEER_HEADS, PEER_N_KEYS, half).astype(BF16))
    tt = 128
    slots = idx.shape[0]
    idx_tok = idx.T
    idx_tiles = idx_tok.reshape(n // tt, tt * slots)
    n_tiles = n // tt
    sc_tiles = int(n_tiles * SC_GATHER_SHARE)
    tc_tiles = n_tiles - sc_tiles
    staged = _sc_gather_slabs(uv, idx_tok[tc_tiles * tt:].reshape(-1))
    staged = staged.reshape((sc_tiles * tt, slots) + uv.shape[1:])
    y_tc = _peer_ffn(idx_tiles, uv, gw, h2, x1, out_g, tt, 0, tc_tiles, False)
    y_sc = _peer_ffn(idx_tiles, staged, gw, h2, x1, out_g, tt, tc_tiles, sc_tiles, True)
    return jnp.concatenate([y_tc, y_sc], axis=0).reshape(b, t, d)


def kernel(x, mem, mix_norm_g, mem_norm_g, w_in, b_merge, cmp_pe_k, cmp_pe_v, cmp_w_k, cmp_w_v, w_mem_kv, w_sb_br, w_nsa_br, w_mem_br, w_out, ffn_norm_g, peer_w_q, peer_subkeys, peer_u, peer_v, final_norm_g):
    depth = w_in.shape[0]
    assert depth == 1, "the final rmsnorm is fused into the last layer's PEER kernel"
    l = 0
    uv = _pack_tables(peer_u[l], peer_v[l])
    return _layer(x, mem, mix_norm_g[l], mem_norm_g[l], w_in[l], b_merge[l], cmp_pe_k[l], cmp_pe_v[l],
                  cmp_w_k[l], cmp_w_v[l], w_mem_kv[l], w_sb_br[l], w_nsa_br[l], w_mem_br[l], w_out[l],
                  ffn_norm_g[l], peer_w_q[l], peer_subkeys[l], uv, final_norm_g)
```

```python
import functools
import math

import jax
import jax.numpy as jnp
from jax import lax
from jax.experimental import pallas as pl
from jax.experimental.pallas import tpu as pltpu
from jax.experimental.pallas import tpu_sc as plsc

F32 = jnp.float32
BF16 = jnp.bfloat16
I32 = jnp.int32

HEAD_DIM = 64
SB_HEADS = 6
NSA_HEADS = 6
NSA_KV_GROUPS = 2
NSA_GROUP = NSA_HEADS // NSA_KV_GROUPS
MEM_HEADS = 4
N_BRANCHES = 3
SB_W = SB_HEADS * HEAD_DIM
NSA_W = NSA_HEADS * HEAD_DIM
NSA_KV_W = NSA_KV_GROUPS * HEAD_DIM
MEM_W = MEM_HEADS * HEAD_DIM
CMP_LEN = 32
CMP_STRIDE = 16
SEL_BLOCK = 64
N_SELECT = 16
WINDOW = 512
FORCED_SCORE = 1e4
PEER_HEADS = 8
PEER_N_KEYS = 128
PEER_QUERY_DIM = 256
PEER_TOPK = 16
RMS_EPS = 1e-6
NEG_INF = -1e30
SCALE = HEAD_DIM ** -0.5
SB_DEAD_LOG = 104.0

LANES = 128
PEER_GATHER_BUFFERS = 8
SC_GATHER_WINDOW = 32
SC_GATHER_SHARE = 0.45
VMEM_LIMIT_BYTES = 56 * 1024 * 1024

_NT = (((1,), (1,)), ((), ()))


def _params(*sem):
    return pltpu.CompilerParams(dimension_semantics=sem, vmem_limit_bytes=VMEM_LIMIT_BYTES)


def _dot(a, b):
    return jnp.dot(a, b, preferred_element_type=F32)


def _dot_nt(a, b):
    return lax.dot_general(a, b, _NT, preferred_element_type=F32)


def _sigmoid(x):
    return 1.0 / (1.0 + jnp.exp(-x))


def _rms(x, g):
    return x * lax.rsqrt(jnp.mean(x * x, axis=-1, keepdims=True) + RMS_EPS) * g


def _norm_matmul_kernel(x_ref, g_ref, w_ref, o_ref, *, tn):
    h = _rms(x_ref[...], g_ref[...]).astype(BF16)
    for c in range(0, o_ref.shape[1], tn):
        o_ref[:, c:c + tn] = _dot(h, w_ref[:, c:c + tn]).astype(o_ref.dtype)


def _norm_matmul(x2d, g, w, out_dtype, tm, tn):
    n, d = x2d.shape
    m = w.shape[1]
    return pl.pallas_call(
        functools.partial(_norm_matmul_kernel, tn=tn),
        grid=(n // tm,),
        in_specs=[
            pl.BlockSpec((tm, d), lambda i: (i, 0)),
            pl.BlockSpec((1, d), lambda i: (0, 0)),
            pl.BlockSpec((d, m), lambda i: (0, 0)),
        ],
        out_specs=pl.BlockSpec((tm, m), lambda i: (i, 0)),
        out_shape=jax.ShapeDtypeStruct((n, m), out_dtype),
        compiler_params=_params("parallel"),
        name="norm_matmul",
    )(x2d, g.reshape(1, d), w)


def _sb_kernel(q_ref, k_ref, v_ref, o_ref, *, tile):
    qi = pl.program_id(2)
    nh = q_ref.shape[1]
    row = lax.broadcasted_iota(I32, (tile, tile), 0)
    col = lax.broadcasted_iota(I32, (tile, tile), 1)
    lower = row > col
    later = lower.astype(BF16)
    qs = [q_ref[0, hh] for hh in range(nh)]

    def visit(hh, ks, c, acc, diagonal):
        k = k_ref[0, hh, pl.ds(ks, tile), :]
        v = v_ref[0, hh, pl.ds(ks, tile), :]
        z = _dot_nt(qs[hh], k) * SCALE
        sp = jnp.maximum(z, 0.0) + jnp.log(1.0 + jnp.exp(-jnp.abs(z)))
        if diagonal:
            sp = jnp.where(lower, sp, 0.0)
        hi = sp.astype(BF16)
        lo = (sp - hi.astype(F32)).astype(BF16)
        after = _dot(hi, later) + _dot(lo, later)
        a = jnp.exp(z - sp - after - c)
        if diagonal:
            a = jnp.where(lower, a, 0.0)
        return c + jnp.sum(sp, axis=1, keepdims=True), acc + _dot(a.astype(BF16), v)

    q0 = pl.multiple_of(qi * tile, tile)
    state = []
    for hh in range(nh):
        state.extend(visit(hh, q0, jnp.zeros((tile, 1), F32), jnp.zeros((tile, HEAD_DIM), F32), True))

    def smallest_carry(state):
        c = state[0]
        for hh in range(1, nh):
            c = jnp.minimum(c, state[2 * hh])
        return jnp.min(c)

    def live(carry):
        i, c_min, _ = carry
        return (i < qi) & (c_min <= SB_DEAD_LOG)

    def body(carry):
        i, _, state = carry
        ks = pl.multiple_of((qi - 1 - i) * tile, tile)
        out = []
        for hh in range(nh):
            out.extend(visit(hh, ks, state[2 * hh], state[2 * hh + 1], False))
        return i + 1, smallest_carry(out), tuple(out)

    _, _, state = lax.while_loop(live, body, (0, smallest_carry(state), tuple(state)))
    o_ref[0] = jnp.concatenate([state[2 * hh + 1] for hh in range(nh)], axis=1).astype(o_ref.dtype)


def _sb_attention(q, k, v, tile=256):
    b, h, t, dh = q.shape
    hp = 2
    tq = tile
    return pl.pallas_call(
        functools.partial(_sb_kernel, tile=tile),
        grid=(b, h // hp, t // tq),
        in_specs=[
            pl.BlockSpec((1, hp, tq, dh), lambda bi, hi, qi: (bi, hi, qi, 0)),
            pl.BlockSpec((1, hp, t, dh), lambda bi, hi, qi: (bi, hi, 0, 0)),
            pl.BlockSpec((1, hp, t, dh), lambda bi, hi, qi: (bi, hi, 0, 0)),
        ],
        out_specs=pl.BlockSpec((1, tq, hp * dh), lambda bi, hi, qi: (bi, qi, hi)),
        out_shape=jax.ShapeDtypeStruct((b, t, h * dh), BF16),
        compiler_params=_params("parallel", "parallel", "arbitrary"),
        name="sb_attn",
    )(q, k, v)


def _compress_kernel(x_ref, pe_ref, w_ref, o_ref):
    x = x_ref[0, 0]
    nc = x.shape[0]
    w_lo, w_hi = w_ref[0], w_ref[1]
    first = _dot(x, w_lo)
    second = _dot(x, w_hi)
    feat = pe_ref.shape[1]
    pe_lo = jnp.broadcast_to(pe_ref[0:1, :], (8, feat)).astype(BF16)
    pe_hi = jnp.broadcast_to(pe_ref[1:2, :], (8, feat)).astype(BF16)
    bias = _dot(pe_lo, w_lo)[0:1] + _dot(pe_hi, w_hi)[0:1]
    o_ref[0, 0] = (first + pltpu.roll(second, nc - 1, 0) + bias).astype(o_ref.dtype)


def _nsa_compress(z, pe, w):
    b, g, t, dh = z.shape
    nc = t // CMP_STRIDE
    feat = CMP_STRIDE * dh
    x = z.reshape(b, g, nc, feat)
    pe2 = pe.reshape(2, feat)
    w2 = w.reshape(2, feat, dh).astype(BF16)
    return pl.pallas_call(
        _compress_kernel,
        grid=(b, g),
        in_specs=[
            pl.BlockSpec((1, 1, nc, feat), lambda bi, gi: (bi, gi, 0, 0)),
            pl.BlockSpec((2, feat), lambda bi, gi: (0, 0)),
            pl.BlockSpec((2, feat, dh), lambda bi, gi: (0, 0, 0)),
        ],
        out_specs=pl.BlockSpec((1, 1, nc, dh), lambda bi, gi: (bi, gi, 0, 0)),
        out_shape=jax.ShapeDtypeStruct((b, g, nc, dh), BF16),
        compiler_params=_params("parallel", "parallel"),
        name="nsa_compress",
    )(x, pe2, w2)


def _nsa_kernel(slopes_ref, q_ref, kc_ref, vc_ref, ks_ref, vs_ref, kw_ref, vw_ref, gl_ref, pool_ref, ex_ref,
                o_ref, used_ref, *, tq, tk, n_sel, real_blocks):
    grp = pl.program_id(1)
    t0 = pl.program_id(2) * tq
    nc = kc_ref.shape[2]
    rr = NSA_GROUP
    t = t0 + lax.broadcasted_iota(I32, (tq, 1), 0)

    def stack(x):
        return jnp.concatenate([x] * rr, axis=0)

    q = q_ref[0].reshape(rr * tq, HEAD_DIM)
    slope = jnp.concatenate([jnp.full((tq, 1), slopes_ref[grp * rr + r], F32) for r in range(rr)], axis=0)

    def heads_sum(x):
        out = x[0:tq]
        for r in range(1, rr):
            out = out + x[r * tq:(r + 1) * tq]
        return out

    def masked_softmax(s, valid):
        p = jnp.where(valid, jnp.exp(s - jnp.max(s, axis=1, keepdims=True)), 0.0)
        denom = jnp.sum(p, axis=1, keepdims=True)
        return p / jnp.where(denom > 0, denom, 1.0)

    lane = lax.broadcasted_iota(I32, (1, nc), 1)
    dist_c = (t - (lane * CMP_STRIDE + (CMP_LEN - 1))).astype(F32)
    valid_c = stack(dist_c >= 0)
    s = _dot_nt(q, kc_ref[0, 0]) * SCALE - slope * stack(dist_c)
    p = masked_softmax(jnp.where(valid_c, s, NEG_INF), valid_c)
    o_cmp = _dot(p.astype(BF16), vc_ref[0, 0])
    psum = heads_sum(p)

    n_blk = pool_ref.shape[0]
    hi = psum.astype(BF16)
    rest = psum - hi.astype(F32)
    mid = rest.astype(BF16)
    lo = (rest - mid.astype(F32)).astype(BF16)
    pool = pool_ref[...]
    imp = _dot_nt(pool, hi) + _dot_nt(pool, mid) + _dot_nt(pool, lo)
    blk = lax.broadcasted_iota(I32, (n_blk, 1), 0)
    cur = (t0 + lax.broadcasted_iota(I32, (1, tq), 1)) // SEL_BLOCK
    forced = (blk == 0) | (blk == cur) | (blk == cur - 1)
    imp = jnp.where(forced, FORCED_SCORE, jnp.where(blk <= cur, imp, -1.0))
    imp = jnp.where(blk < real_blocks, imp, -jnp.inf)
    blk_f = blk.astype(F32)

    def pick(_, carry):
        imp, sel = carry
        best = jnp.max(imp, axis=0, keepdims=True)
        first = jnp.min(jnp.where(imp == best, blk_f, float(n_blk)), axis=0, keepdims=True)
        hit = blk_f == first
        return jnp.where(hit, -jnp.inf, imp), jnp.where(hit, 1.0, sel)

    _, sel = lax.fori_loop(0, n_sel, pick, (imp, jnp.zeros((n_blk, tq), F32)))
    picked = jnp.max(sel, axis=1, keepdims=True)
    per_tile = tk // SEL_BLOCK
    n_causal = (t0 + tq + tk - 1) // tk
    n_used = jnp.int32(0)
    for j in range(used_ref.shape[0]):
        used_ref[n_used] = j
        hit = (jnp.max(picked[j * per_tile:(j + 1) * per_tile, :]) > 0) & (j < n_causal)
        n_used = n_used + hit.astype(I32)
    sel = sel.T.astype(BF16)

    span = WINDOW + tq
    w0 = pl.multiple_of(jnp.maximum(t0 - WINDOW, 0), tq)
    dist_w = t - (w0 + lax.broadcasted_iota(I32, (1, span), 1))
    valid_w = stack((dist_w >= 0) & (dist_w < WINDOW))
    s = _dot_nt(q, kw_ref[0, 0, pl.ds(w0, span), :]) * SCALE - slope * stack(dist_w.astype(F32))
    p = masked_softmax(jnp.where(valid_w, s, NEG_INF), valid_w)
    o_win = _dot(p.astype(BF16), vw_ref[0, 0, pl.ds(w0, span), :])

    kcol = lax.broadcasted_iota(I32, (1, tk), 1)

    def sel_body(i, carry):
        m, l, acc = carry
        kb = used_ref[i]
        ks0 = pl.multiple_of(kb * tk, tk)
        chosen = _dot(sel, ex_ref[kb]) > 0.5
        dist = t - (ks0 + kcol)
        valid = stack(chosen & (dist >= 0))
        s = _dot_nt(q, ks_ref[0, 0, pl.ds(ks0, tk), :]) * SCALE - slope * stack(dist.astype(F32))
        s = jnp.where(valid, s, NEG_INF)
        m_new = jnp.maximum(m, jnp.max(s, axis=1, keepdims=True))
        alpha = jnp.exp(m - m_new)
        p = jnp.where(valid, jnp.exp(s - m_new), 0.0)
        l = alpha * l + jnp.sum(p, axis=1, keepdims=True)
        acc = alpha * acc + _dot(p.astype(BF16), vs_ref[0, 0, pl.ds(ks0, tk), :])
        return m_new, l, acc

    init = (jnp.full((rr * tq, 1), NEG_INF, F32), jnp.zeros((rr * tq, 1), F32),
            jnp.zeros((rr * tq, HEAD_DIM), F32))
    _, l_sel, acc_sel = lax.fori_loop(0, n_used, sel_body, init)
    o_sel = acc_sel / l_sel

    gates = _sigmoid(gl_ref[0, 0])
    outs = []
    for r in range(rr):
        rows = slice(r * tq, (r + 1) * tq)
        outs.append(gates[:, 3 * r:3 * r + 1] * o_cmp[rows] + gates[:, 3 * r + 1:3 * r + 2] * o_sel[rows]
                    + gates[:, 3 * r + 2:3 * r + 3] * o_win[rows])
    o_ref[0, 0] = jnp.concatenate(outs, axis=1).astype(o_ref.dtype)


def _nsa_attention(q, kc, vc, ks, vs, kw, vw, gate_logits, slopes, tq=128, tk=256):
    b, _, t, dh = q.shape
    g = NSA_KV_GROUPS
    nc = kc.shape[2]
    n_sel = min(N_SELECT, t // SEL_BLOCK)
    real_blocks = t // SEL_BLOCK
    n_blk = LANES
    assert t % tk == 0 and t >= WINDOW + tq and real_blocks <= n_blk
    blk_ids = jnp.arange(n_blk, dtype=I32)
    pool = (blk_ids[:, None] == jnp.arange(nc, dtype=I32)[None, :] // (SEL_BLOCK // CMP_STRIDE)).astype(BF16)
    key_blk = (jnp.arange(t, dtype=I32) // SEL_BLOCK).reshape(t // tk, 1, tk)
    expand = (blk_ids[None, :, None] == key_blk).astype(BF16)
    kv_spec = pl.BlockSpec((1, 1, t, dh), lambda bi, gi, qi: (bi, gi, 0, 0))
    c_spec = pl.BlockSpec((1, 1, nc, dh), lambda bi, gi, qi: (bi, gi, 0, 0))
    return pl.pallas_call(
        functools.partial(_nsa_kernel, tq=tq, tk=tk, n_sel=n_sel, real_blocks=real_blocks),
        grid=(b, g, t // tq),
        in_specs=[
            pl.BlockSpec(memory_space=pltpu.SMEM),
            pl.BlockSpec((1, NSA_GROUP, tq, dh), lambda bi, gi, qi: (bi, gi, qi, 0)),
            c_spec, c_spec, kv_spec, kv_spec, kv_spec, kv_spec,
            pl.BlockSpec((1, 1, tq, LANES), lambda bi, gi, qi: (bi, gi, qi, 0)),
            pl.BlockSpec(pool.shape, lambda bi, gi, qi: (0, 0)),
            pl.BlockSpec(expand.shape, lambda bi, gi, qi: (0, 0, 0)),
        ],
        out_specs=pl.BlockSpec((1, 1, tq, NSA_GROUP * dh), lambda bi, gi, qi: (bi, gi, qi, 0)),
        out_shape=jax.ShapeDtypeStruct((b, g, t, NSA_GROUP * dh), BF16),
        scratch_shapes=[pltpu.SMEM((t // tk,), I32)],
        compiler_params=_params("parallel", "parallel", "arbitrary"),
        name="nsa_attn",
    )(slopes, q, kc, vc, ks, vs, kw, vw, gate_logits, pool, expand)


def _mem_kernel(q_ref, kv_ref, o_ref):
    q = q_ref[0]
    kv = kv_ref[0]
    outs = []
    for h in range(MEM_HEADS):
        sl = slice(h * HEAD_DIM, (h + 1) * HEAD_DIM)
        s = _dot_nt(q[:, sl], kv[:, sl]) * SCALE
        p = jnp.exp(s - jnp.max(s, axis=1, keepdims=True))
        p = p / jnp.sum(p, axis=1, keepdims=True)
        outs.append(_dot(p.astype(BF16), kv[:, MEM_W + h * HEAD_DIM:MEM_W + (h + 1) * HEAD_DIM]))
    o_ref[0] = jnp.concatenate(outs, axis=1).astype(o_ref.dtype)


def _mem_attention(q, mkv, tq=512):
    b, t, w = q.shape
    m = mkv.shape[1]
    return pl.pallas_call(
        _mem_kernel,
        grid=(b, t // tq),
        in_specs=[
            pl.BlockSpec((1, tq, w), lambda bi, qi: (bi, qi, 0)),
            pl.BlockSpec((1, m, 2 * w), lambda bi, qi: (bi, 0, 0)),
        ],
        out_specs=pl.BlockSpec((1, tq, w), lambda bi, qi: (bi, qi, 0)),
        out_shape=jax.ShapeDtypeStruct((b, t, w), BF16),
        compiler_params=_params("parallel", "parallel"),
        name="mem_attn",
    )(q, mkv)


def _merge_kernel(sb_ref, nsa_ref, mem_ref, mg_ref, bm_ref, x_ref, wsb_ref, wnsa_ref, wmem_ref,
                  wout_ref, fg_ref, wq_ref, x1_ref, h2_ref, q_ref):
    d = x_ref.shape[1]
    branches = (_dot(sb_ref[...], wsb_ref[...]), _dot(nsa_ref[...], wnsa_ref[...]),
                _dot(mem_ref[...], wmem_ref[...]))
    merged = jnp.zeros_like(branches[0])
    for j in range(N_BRANCHES):
        gate = _sigmoid(mg_ref[:, j * d:(j + 1) * d] + bm_ref[:, j * d:(j + 1) * d])
        merged = merged + gate * branches[j]
    x1 = x_ref[...] + _dot(merged.astype(BF16), wout_ref[...])
    x1_ref[...] = x1
    h2 = _rms(x1, fg_ref[...])
    h2_ref[...] = h2
    q = _dot(h2.astype(BF16), wq_ref[...]).astype(q_ref.dtype)
    for c in range(q_ref.shape[0]):
        q_ref[c] = q[:, c * LANES:(c + 1) * LANES]


def _merge(sb, nsa, mem, gates_slab, b_merge, x2d, w_sb, w_nsa, w_mem, w_out, ffn_g, w_q, tm=256):
    n, d = x2d.shape
    nq = w_q.shape[1] // LANES
    row = lambda w: pl.BlockSpec((tm, w), lambda i: (i, 0))
    full = lambda a: pl.BlockSpec(a.shape, lambda i: (0,) * a.ndim)
    bm = b_merge.reshape(1, -1)
    fg = ffn_g.reshape(1, d)
    return pl.pallas_call(
        _merge_kernel,
        grid=(n // tm,),
        in_specs=[row(sb.shape[1]), row(nsa.shape[1]), row(mem.shape[1]), row(N_BRANCHES * d),
                  full(bm), row(d), full(w_sb), full(w_nsa), full(w_mem), full(w_out), full(fg),
                  full(w_q)],
        out_specs=[row(d), row(d), pl.BlockSpec((nq, tm, LANES), lambda i: (0, i, 0))],
        out_shape=[jax.ShapeDtypeStruct((n, d), F32), jax.ShapeDtypeStruct((n, d), F32),
                   jax.ShapeDtypeStruct((nq, n, LANES), BF16)],
        compiler_params=_params("parallel"),
        name="merge",
    )(sb, nsa, mem, gates_slab, bm, x2d, w_sb, w_nsa, w_mem, w_out, fg, w_q)


def _peer_topk_kernel(q_ref, sk_ref, idx_ref, gw_ref, tv_ref, ti_ref, bv_ref):
    tt = q_ref.shape[1]
    kk = PEER_TOPK
    nk = PEER_N_KEYS
    rid = lax.broadcasted_iota(I32, (nk, tt), 0).astype(F32)
    for c in range(2 * PEER_HEADS):
        scores = _dot_nt(sk_ref[c], q_ref[c])

        def pick(k, s, c=c):
            best = jnp.max(s, axis=0, keepdims=True)
            first = jnp.min(jnp.where(s == best, rid, float(nk)), axis=0, keepdims=True)
            tv_ref[c, pl.ds(k, 1), :] = best
            ti_ref[c, pl.ds(k, 1), :] = first
            return jnp.where(rid == first, -jnp.inf, s)

        lax.fori_loop(0, kk, pick, scores)

    sub = 8
    widths = [min(kk, -(-(kk // (i + 1)) // sub) * sub) for i in range(kk // 2)]
    n_cand = sum(widths) + kk // 2
    pos = lax.broadcasted_iota(I32, (n_cand, tt), 0).astype(F32)
    for h in range(PEER_HEADS):
        s0, s1 = tv_ref[2 * h], tv_ref[2 * h + 1]
        i0, i1 = ti_ref[2 * h], ti_ref[2 * h + 1]
        cand, cidx = [], []
        for i, wd in enumerate(widths):
            keep = lax.broadcasted_iota(I32, (wd, 1), 0) < kk // (i + 1)
            cand.append(jnp.where(keep, s0[i:i + 1, :] + s1[0:wd, :], -jnp.inf))
            cidx.append(i0[i:i + 1, :] * float(nk) + i1[0:wd, :])
        cand.append(s0[kk // 2:, :] + s1[0:1, :])
        cidx.append(i0[kk // 2:, :] * float(nk) + i1[0:1, :])
        cand = jnp.concatenate(cand, axis=0)
        cidx = jnp.concatenate(cidx, axis=0)

        def pick2(k, s, h=h, cidx=cidx):
            best = jnp.max(s, axis=0, keepdims=True)
            first = jnp.min(jnp.where(s == best, pos, float(n_cand)), axis=0, keepdims=True)
            hit = pos == first
            bv_ref[pl.ds(k, 1), :] = best
            expert = jnp.max(jnp.where(hit, cidx, -1.0), axis=0, keepdims=True)
            idx_ref[pl.ds(h * kk + k, 1), :] = expert.astype(I32)
            return jnp.where(hit, -jnp.inf, s)

        lax.fori_loop(0, kk, pick2, cand)
        best = bv_ref[...]
        e = jnp.exp(best - best[0:1, :])
        gw_ref[h * kk:(h + 1) * kk, :] = e / jnp.sum(e, axis=0, keepdims=True)


def _peer_topk(q_chunks, subkeys, tt=256):
    nchunk, n, half = q_chunks.shape
    slots = PEER_HEADS * PEER_TOPK
    return pl.pallas_call(
        _peer_topk_kernel,
        grid=(n // tt,),
        in_specs=[
            pl.BlockSpec((nchunk, tt, half), lambda i: (0, i, 0)),
            pl.BlockSpec(subkeys.shape, lambda i: (0, 0, 0)),
        ],
        out_specs=[pl.BlockSpec((slots, tt), lambda i: (0, i)),
                   pl.BlockSpec((slots, tt), lambda i: (0, i))],
        out_shape=[jax.ShapeDtypeStruct((slots, n), I32), jax.ShapeDtypeStruct((slots, n), F32)],
        scratch_shapes=[pltpu.VMEM((nchunk, PEER_TOPK, tt), F32),
                        pltpu.VMEM((nchunk, PEER_TOPK, tt), F32),
                        pltpu.VMEM((PEER_TOPK, tt), F32)],
        compiler_params=_params("parallel"),
        name="peer_topk",
    )(q_chunks, subkeys)


def _peer_ffn_kernel(idx_hbm, uv_hbm, gw_ref, h_ref, x1_ref, fg_ref, y_ref, idx_smem, buf, idx_sem,
                     row_sem, *, first_tile, staged):
    tt, d = h_ref.shape
    slots = gw_ref.shape[0]
    rows = d // (2 * LANES)
    tile = pl.program_id(0)

    def halves(words):
        return tuple(pltpu.unpack_elementwise(words, index=i, packed_dtype=BF16, unpacked_dtype=F32)
                     for i in range(2))

    nbuf = buf.shape[0]

    if staged:
        def start(tok, slot):
            for r in range(buf.shape[1]):
                pltpu.make_async_copy(uv_hbm.at[tile * tt + tok, :, r, :], buf.at[slot, r],
                                      row_sem.at[slot]).start(priority=r % 2)
    else:
        idx_copy = pltpu.make_async_copy(idx_hbm.at[tile + first_tile], idx_smem, idx_sem)
        idx_copy.start()
        idx_copy.wait()

        def start(tok, slot):
            for e in range(slots):
                pltpu.make_async_copy(uv_hbm.at[idx_smem[tok * slots + e]], buf.at[slot, :, e, :],
                                      row_sem.at[slot]).start(priority=e % 2)

    def wait(slot):
        pltpu.make_async_copy(buf.at[slot], buf.at[slot], row_sem.at[slot]).wait()

    tok_lane = lax.broadcasted_iota(I32, (1, tt), 1)

    def expert_weights(tok, slot):
        h = h_ref[pl.ds(tok, 1), :]
        prod = jnp.zeros((slots, LANES), F32)
        for r in range(rows):
            lo, hi = halves(buf[slot, r])
            prod = prod + lo * h[:, r * LANES:(r + 1) * LANES]
            prod = prod + hi * h[:, d // 2 + r * LANES:d // 2 + (r + 1) * LANES]
        a = jnp.sum(prod, axis=1, keepdims=True)
        act = 0.5 * a * (1.0 + lax.erf(a * (2.0 ** -0.5)))
        gate = jnp.sum(jnp.where(tok_lane == tok, gw_ref[...], 0.0), axis=1, keepdims=True)
        return gate * act

    def weighted_values(tok, slot, w):
        parts = [halves(buf[slot, rows + r]) for r in range(rows)]
        y_ref[pl.ds(tok, 1), :] = jnp.concatenate(
            [jnp.sum(parts[r][i] * w, axis=0, keepdims=True) for i in range(2) for r in range(rows)], axis=1)

    for tok in range(nbuf):
        start(tok, tok)
    wait(0)
    w = expert_weights(0, 0)

    def step(tok, slot, w, refill):
        nxt = (slot + 1) % nbuf
        wait(nxt)
        w_next = expert_weights(tok + 1, nxt)
        weighted_values(tok, slot, w)
        if refill:
            start(tok + nbuf, slot)
        return w_next

    def round_of_steps(i, w):
        for slot in range(nbuf):
            w = step(i * nbuf + slot, slot, w, True)
        return w

    n_rounds = tt // nbuf - 1
    w = lax.fori_loop(0, n_rounds, round_of_steps, w)
    for slot in range(nbuf - 1):
        w = step(n_rounds * nbuf + slot, slot, w, False)
    weighted_values(tt - 1, nbuf - 1, w)
    y_ref[...] = _rms(x1_ref[...] + y_ref[...], fg_ref[...])


def _pack_tables_kernel(u_ref, v_ref, o_ref):
    d = u_ref.shape[1]
    rows = d // (2 * LANES)
    for base, src in ((0, u_ref), (rows, v_ref)):
        for r in range(rows):
            lo = src[:, r * LANES:(r + 1) * LANES]
            hi = src[:, d // 2 + r * LANES:d // 2 + (r + 1) * LANES]
            o_ref[:, base + r, :] = pltpu.pack_elementwise([lo, hi], packed_dtype=BF16)


def _pack_tables(u, v, te=512):
    n_exp, d = u.shape
    rows = d // LANES
    return pl.pallas_call(
        _pack_tables_kernel,
        grid=(n_exp // te,),
        in_specs=[pl.BlockSpec((te, d), lambda i: (i, 0)), pl.BlockSpec((te, d), lambda i: (i, 0))],
        out_specs=pl.BlockSpec((te, rows, LANES), lambda i: (i, 0, 0)),
        out_shape=jax.ShapeDtypeStruct((n_exp, rows, LANES), jnp.uint32),
        compiler_params=_params("parallel"),
        name="pack_tables",
    )(u, v)


def _sc_gather_slabs(slabs, slab_idx):
    m = slab_idx.shape[0]
    mesh = plsc.VectorSubcoreMesh(core_axis_name="core", subcore_axis_name="subcore")
    idx_rows = jnp.pad(slab_idx.reshape(m // SC_GATHER_WINDOW, SC_GATHER_WINDOW),
                       ((0, 0), (0, LANES - SC_GATHER_WINDOW)))

    @pl.kernel(out_type=jax.ShapeDtypeStruct((m,) + slabs.shape[1:], slabs.dtype), mesh=mesh)
    def gather(slabs_hbm, idx_hbm, out_hbm):
        def window(idx_vmem, out_vmem):
            pltpu.sync_copy(slabs_hbm.at[idx_vmem.at[0, pl.ds(0, SC_GATHER_WINDOW)]], out_vmem)

        pltpu.emit_pipeline(
            window,
            grid=(m // SC_GATHER_WINDOW,),
            in_specs=[pl.BlockSpec((1, LANES), index_map=lambda i: (i, 0))],
            out_specs=[pl.BlockSpec((SC_GATHER_WINDOW,) + slabs.shape[1:], index_map=lambda i: (i, 0, 0))],
            core_axis_name=("core", "subcore"),
            dimension_semantics=(pltpu.PARALLEL,),
            trace_scopes=False,
        )(idx_hbm, out_hbm)

    return gather(slabs, idx_rows)


def _peer_ffn(idx_tiles, table, gw, h2, x1, final_g, tt, first_tile, n_tiles, staged):
    _, d = h2.shape
    slots = gw.shape[0]
    rows = d // LANES
    return pl.pallas_call(
        functools.partial(_peer_ffn_kernel, first_tile=first_tile, staged=staged),
        grid=(n_tiles,),
        in_specs=[
            pl.BlockSpec(memory_space=pl.ANY),
            pl.BlockSpec(memory_space=pl.ANY),
            pl.BlockSpec((slots, tt), lambda i: (0, i + first_tile)),
            pl.BlockSpec((tt, d), lambda i: (i + first_tile, 0)),
            pl.BlockSpec((tt, d), lambda i: (i + first_tile, 0)),
            pl.BlockSpec((1, d), lambda i: (0, 0)),
        ],
        out_specs=pl.BlockSpec((tt, d), lambda i: (i, 0)),
        out_shape=jax.ShapeDtypeStruct((n_tiles * tt, d), F32),
        scratch_shapes=[pltpu.SMEM((tt * slots,), I32),
                        pltpu.VMEM((PEER_GATHER_BUFFERS, rows, slots, LANES), table.dtype),
                        pltpu.SemaphoreType.DMA, pltpu.SemaphoreType.DMA((PEER_GATHER_BUFFERS,))],
        compiler_params=_params("arbitrary"),
        name="peer_ffn_staged" if staged else "peer_ffn",
    )(idx_tiles, table, gw, h2, x1, final_g.reshape(1, d))


def _layer(x, mem, mix_g, mem_g, w_in, b_merge, pe_k, pe_v, cw_k, cw_v, w_mem_kv, w_sb_br, w_nsa_br,
           w_mem_br, w_out, ffn_g, peer_w_q, subkeys, uv, out_g):
    b, t, d = x.shape
    m = mem.shape[1]
    g = NSA_KV_GROUPS
    n = b * t
    x2d = x.reshape(n, d)

    o_sbq, o_sbk, o_sbv = 0, SB_W, 2 * SB_W
    o_nq = 3 * SB_W
    o_nkv = o_nq + NSA_W
    o_ng = o_nkv + 6 * NSA_KV_W
    o_mq = o_ng + NSA_HEADS * 3
    o_mg = o_mq + MEM_W
    w_act = jnp.concatenate([w_in[:, :o_ng], w_in[:, o_mq:o_mg]], axis=1).astype(BF16)
    gate_pad = LANES - NSA_HEADS * 3
    w_gate = jnp.concatenate([w_in[:, o_mg:], w_in[:, o_ng:o_mq], jnp.zeros((d, gate_pad), w_in.dtype)],
                             axis=1).astype(BF16)
    act = _norm_matmul(x2d, mix_g, w_act, BF16, tm=256, tn=512)
    gates_slab = _norm_matmul(x2d, mix_g, w_gate, F32, tm=256, tn=640)

    def heads(lo, nh):
        return act[:, lo:lo + nh * HEAD_DIM].reshape(b, t, nh, HEAD_DIM).transpose(0, 2, 1, 3)

    sb_out = _sb_attention(heads(o_sbq, SB_HEADS), heads(o_sbk, SB_HEADS), heads(o_sbv, SB_HEADS))

    kv = [heads(o_nkv + j * NSA_KV_W, g) for j in range(6)]
    kc = _nsa_compress(kv[0], pe_k, cw_k)
    vc = _nsa_compress(kv[1], pe_v, cw_v)
    gl = gates_slab[:, N_BRANCHES * d:N_BRANCHES * d + NSA_HEADS * 3]
    gl = gl.reshape(b, t, g, NSA_GROUP * 3).transpose(0, 2, 1, 3)
    gl = jnp.pad(gl, ((0, 0), (0, 0), (0, 0), (0, LANES - NSA_GROUP * 3)))
    slopes = jnp.asarray([2.0 ** (-8.0 * (h + 1) / NSA_HEADS) for h in range(NSA_HEADS)], F32)
    nsa_out = _nsa_attention(heads(o_nq, NSA_HEADS), kc, vc, kv[2], kv[3], kv[4], kv[5], gl, slopes)
    nsa_out = nsa_out.transpose(0, 2, 1, 3).reshape(n, NSA_W)

    mkv = _norm_matmul(mem.reshape(b * m, d), mem_g, w_mem_kv.astype(BF16), BF16, tm=256, tn=512)
    mem_q = act[:, o_nkv + 6 * NSA_KV_W:].reshape(b, t, MEM_W)
    mem_out = _mem_attention(mem_q, mkv.reshape(b, m, 2 * MEM_W))

    x1, h2, q_chunks = _merge(
        sb_out.reshape(n, SB_W), nsa_out, mem_out.reshape(n, MEM_W), gates_slab, b_merge, x2d,
        w_sb_br.astype(BF16), w_nsa_br.astype(BF16), w_mem_br.astype(BF16), w_out.astype(BF16), ffn_g,
        peer_w_q.astype(BF16))

    half = PEER_QUERY_DIM // 2
    idx, gw = _peer_topk(q_chunks, subkeys.reshape(2 * PEER_HEADS, PEER_N_KEYS, half).astype(BF16))
    tt = 128
    slots = idx.shape[0]
    idx_tok = idx.T
    idx_tiles = idx_tok.reshape(n // tt, tt * slots)
    n_tiles = n // tt
    sc_tiles = int(n_tiles * SC_GATHER_SHARE)
    tc_tiles = n_tiles - sc_tiles
    staged = _sc_gather_slabs(uv, idx_tok[tc_tiles * tt:].reshape(-1))
    staged = staged.reshape((sc_tiles * tt, slots) + uv.shape[1:])
    y_tc = _peer_ffn(idx_tiles, uv, gw, h2, x1, out_g, tt, 0, tc_tiles, False)
    y_sc = _peer_ffn(idx_tiles, staged, gw, h2, x1, out_g, tt, tc_tiles, sc_tiles, True)
    return jnp.concatenate([y_tc, y_sc], axis=0).reshape(b, t, d)


def kernel(x, mem, mix_norm_g, mem_norm_g, w_in, b_merge, cmp_pe_k, cmp_pe_v, cmp_w_k, cmp_w_v, w_mem_kv, w_sb_br, w_nsa_br, w_mem_br, w_out, ffn_norm_g, peer_w_q, peer_subkeys, peer_u, peer_v, final_norm_g):
    depth = w_in.shape[0]
    assert depth == 1, "the final rmsnorm is fused into the last layer's PEER kernel"
    l = 0
    uv = _pack_tables(peer_u[l], peer_v[l])
    return _layer(x, mem, mix_norm_g[l], mem_norm_g[l], w_in[l], b_merge[l], cmp_pe_k[l], cmp_pe_v[l],
                  cmp_w_k[l], cmp_w_v[l], w_mem_kv[l], w_sb_br[l], w_nsa_br[l], w_mem_br[l], w_out[l],
                  ffn_norm_g[l], peer_w_q[l], peer_subkeys[l], uv, final_norm_g)
```

```python
import functools
import math

import jax
import jax.numpy as jnp
from jax import lax
from jax.experimental import pallas as pl
from jax.experimental.pallas import tpu as pltpu
from jax.experimental.pallas import tpu_sc as plsc

F32 = jnp.float32
BF16 = jnp.bfloat16
I32 = jnp.int32

HEAD_DIM = 64
SB_HEADS = 6
NSA_HEADS = 6
NSA_KV_GROUPS = 2
NSA_GROUP = NSA_HEADS // NSA_KV_GROUPS
MEM_HEADS = 4
N_BRANCHES = 3
SB_W = SB_HEADS * HEAD_DIM
NSA_W = NSA_HEADS * HEAD_DIM
NSA_KV_W = NSA_KV_GROUPS * HEAD_DIM
MEM_W = MEM_HEADS * HEAD_DIM
CMP_LEN = 32
CMP_STRIDE = 16
SEL_BLOCK = 64
N_SELECT = 16
WINDOW = 512
FORCED_SCORE = 1e4
PEER_HEADS = 8
PEER_N_KEYS = 128
PEER_QUERY_DIM = 256
PEER_TOPK = 16
RMS_EPS = 1e-6
NEG_INF = -1e30
SCALE = HEAD_DIM ** -0.5
SB_DEAD_LOG = 104.0

LANES = 128
PEER_GATHER_BUFFERS = 8
SC_GATHER_WINDOW = 32
SC_GATHER_SHARES = (0.6, 0.6)
VMEM_LIMIT_BYTES = 56 * 1024 * 1024

_NT = (((1,), (1,)), ((), ()))


def _params(*sem):
    return pltpu.CompilerParams(dimension_semantics=sem, vmem_limit_bytes=VMEM_LIMIT_BYTES)


def _dot(a, b):
    return jnp.dot(a, b, preferred_element_type=F32)


def _dot_nt(a, b):
    return lax.dot_general(a, b, _NT, preferred_element_type=F32)


def _sigmoid(x):
    return 1.0 / (1.0 + jnp.exp(-x))


def _rms(x, g):
    return x * lax.rsqrt(jnp.mean(x * x, axis=-1, keepdims=True) + RMS_EPS) * g


def _norm_matmul_kernel(x_ref, g_ref, w_ref, o_ref, *, tn):
    h = _rms(x_ref[...], g_ref[...]).astype(BF16)
    for c in range(0, o_ref.shape[1], tn):
        o_ref[:, c:c + tn] = _dot(h, w_ref[:, c:c + tn]).astype(o_ref.dtype)


def _norm_matmul(x2d, g, w, out_dtype, tm, tn):
    n, d = x2d.shape
    m = w.shape[1]
    return pl.pallas_call(
        functools.partial(_norm_matmul_kernel, tn=tn),
        grid=(n // tm,),
        in_specs=[
            pl.BlockSpec((tm, d), lambda i: (i, 0)),
            pl.BlockSpec((1, d), lambda i: (0, 0)),
            pl.BlockSpec((d, m), lambda i: (0, 0)),
        ],
        out_specs=pl.BlockSpec((tm, m), lambda i: (i, 0)),
        out_shape=jax.ShapeDtypeStruct((n, m), out_dtype),
        compiler_params=_params("parallel"),
        name="norm_matmul",
    )(x2d, g.reshape(1, d), w)


def _sb_kernel(q_ref, k_ref, v_ref, o_ref, *, tile):
    qi = pl.program_id(2)
    nh = q_ref.shape[1]
    row = lax.broadcasted_iota(I32, (tile, tile), 0)
    col = lax.broadcasted_iota(I32, (tile, tile), 1)
    lower = row > col
    later = lower.astype(BF16)
    qs = [q_ref[0, hh] for hh in range(nh)]

    def visit(hh, ks, c, acc, diagonal):
        k = k_ref[0, hh, pl.ds(ks, tile), :]
        v = v_ref[0, hh, pl.ds(ks, tile), :]
        z = _dot_nt(qs[hh], k) * SCALE
        sp = jnp.maximum(z, 0.0) + jnp.log(1.0 + jnp.exp(-jnp.abs(z)))
        if diagonal:
            sp = jnp.where(lower, sp, 0.0)
        hi = sp.astype(BF16)
        lo = (sp - hi.astype(F32)).astype(BF16)
        after = _dot(hi, later) + _dot(lo, later)
        a = jnp.exp(z - sp - after - c)
        if diagonal:
            a = jnp.where(lower, a, 0.0)
        return c + jnp.sum(sp, axis=1, keepdims=True), acc + _dot(a.astype(BF16), v)

    q0 = pl.multiple_of(qi * tile, tile)
    state = []
    for hh in range(nh):
        state.extend(visit(hh, q0, jnp.zeros((tile, 1), F32), jnp.zeros((tile, HEAD_DIM), F32), True))

    def smallest_carry(state):
        c = state[0]
        for hh in range(1, nh):
            c = jnp.minimum(c, state[2 * hh])
        return jnp.min(c)

    def live(carry):
        i, c_min, _ = carry
        return (i < qi) & (c_min <= SB_DEAD_LOG)

    def body(carry):
        i, _, state = carry
        ks = pl.multiple_of((qi - 1 - i) * tile, tile)
        out = []
        for hh in range(nh):
            out.extend(visit(hh, ks, state[2 * hh], state[2 * hh + 1], False))
        return i + 1, smallest_carry(out), tuple(out)

    _, _, state = lax.while_loop(live, body, (0, smallest_carry(state), tuple(state)))
    o_ref[0] = jnp.concatenate([state[2 * hh + 1] for hh in range(nh)], axis=1).astype(o_ref.dtype)


def _sb_attention(q, k, v, tile=256):
    b, h, t, dh = q.shape
    hp = 2
    tq = tile
    return pl.pallas_call(
        functools.partial(_sb_kernel, tile=tile),
        grid=(b, h // hp, t // tq),
        in_specs=[
            pl.BlockSpec((1, hp, tq, dh), lambda bi, hi, qi: (bi, hi, qi, 0)),
            pl.BlockSpec((1, hp, t, dh), lambda bi, hi, qi: (bi, hi, 0, 0)),
            pl.BlockSpec((1, hp, t, dh), lambda bi, hi, qi: (bi, hi, 0, 0)),
        ],
        out_specs=pl.BlockSpec((1, tq, hp * dh), lambda bi, hi, qi: (bi, qi, hi)),
        out_shape=jax.ShapeDtypeStruct((b, t, h * dh), BF16),
        compiler_params=_params("parallel", "parallel", "arbitrary"),
        name="sb_attn",
    )(q, k, v)


def _compress_kernel(x_ref, pe_ref, w_ref, o_ref):
    x = x_ref[0, 0]
    nc = x.shape[0]
    w_lo, w_hi = w_ref[0], w_ref[1]
    first = _dot(x, w_lo)
    second = _dot(x, w_hi)
    feat = pe_ref.shape[1]
    pe_lo = jnp.broadcast_to(pe_ref[0:1, :], (8, feat)).astype(BF16)
    pe_hi = jnp.broadcast_to(pe_ref[1:2, :], (8, feat)).astype(BF16)
    bias = _dot(pe_lo, w_lo)[0:1] + _dot(pe_hi, w_hi)[0:1]
    o_ref[0, 0] = (first + pltpu.roll(second, nc - 1, 0) + bias).astype(o_ref.dtype)


def _nsa_compress(z, pe, w):
    b, g, t, dh = z.shape
    nc = t // CMP_STRIDE
    feat = CMP_STRIDE * dh
    x = z.reshape(b, g, nc, feat)
    pe2 = pe.reshape(2, feat)
    w2 = w.reshape(2, feat, dh).astype(BF16)
    return pl.pallas_call(
        _compress_kernel,
        grid=(b, g),
        in_specs=[
            pl.BlockSpec((1, 1, nc, feat), lambda bi, gi: (bi, gi, 0, 0)),
            pl.BlockSpec((2, feat), lambda bi, gi: (0, 0)),
            pl.BlockSpec((2, feat, dh), lambda bi, gi: (0, 0, 0)),
        ],
        out_specs=pl.BlockSpec((1, 1, nc, dh), lambda bi, gi: (bi, gi, 0, 0)),
        out_shape=jax.ShapeDtypeStruct((b, g, nc, dh), BF16),
        compiler_params=_params("parallel", "parallel"),
        name="nsa_compress",
    )(x, pe2, w2)


def _nsa_kernel(slopes_ref, q_ref, kc_ref, vc_ref, ks_ref, vs_ref, kw_ref, vw_ref, gl_ref, pool_ref, ex_ref,
                o_ref, used_ref, *, tq, tk, n_sel, real_blocks):
    grp = pl.program_id(1)
    t0 = pl.program_id(2) * tq
    nc = kc_ref.shape[2]
    rr = NSA_GROUP
    t = t0 + lax.broadcasted_iota(I32, (tq, 1), 0)

    def stack(x):
        return jnp.concatenate([x] * rr, axis=0)

    q = q_ref[0].reshape(rr * tq, HEAD_DIM)
    slope = jnp.concatenate([jnp.full((tq, 1), slopes_ref[grp * rr + r], F32) for r in range(rr)], axis=0)

    def heads_sum(x):
        out = x[0:tq]
        for r in range(1, rr):
            out = out + x[r * tq:(r + 1) * tq]
        return out

    def masked_softmax(s, valid):
        p = jnp.where(valid, jnp.exp(s - jnp.max(s, axis=1, keepdims=True)), 0.0)
        denom = jnp.sum(p, axis=1, keepdims=True)
        return p / jnp.where(denom > 0, denom, 1.0)

    lane = lax.broadcasted_iota(I32, (1, nc), 1)
    dist_c = (t - (lane * CMP_STRIDE + (CMP_LEN - 1))).astype(F32)
    valid_c = stack(dist_c >= 0)
    s = _dot_nt(q, kc_ref[0, 0]) * SCALE - slope * stack(dist_c)
    p = masked_softmax(jnp.where(valid_c, s, NEG_INF), valid_c)
    o_cmp = _dot(p.astype(BF16), vc_ref[0, 0])
    psum = heads_sum(p)

    n_blk = pool_ref.shape[0]
    hi = psum.astype(BF16)
    rest = psum - hi.astype(F32)
    mid = rest.astype(BF16)
    lo = (rest - mid.astype(F32)).astype(BF16)
    pool = pool_ref[...]
    imp = _dot_nt(pool, hi) + _dot_nt(pool, mid) + _dot_nt(pool, lo)
    blk = lax.broadcasted_iota(I32, (n_blk, 1), 0)
    cur = (t0 + lax.broadcasted_iota(I32, (1, tq), 1)) // SEL_BLOCK
    forced = (blk == 0) | (blk == cur) | (blk == cur - 1)
    imp = jnp.where(forced, FORCED_SCORE, jnp.where(blk <= cur, imp, -1.0))
    imp = jnp.where(blk < real_blocks, imp, -jnp.inf)
    blk_f = blk.astype(F32)

    def pick(_, carry):
        imp, sel = carry
        best = jnp.max(imp, axis=0, keepdims=True)
        first = jnp.min(jnp.where(imp == best, blk_f, float(n_blk)), axis=0, keepdims=True)
        hit = blk_f == first
        return jnp.where(hit, -jnp.inf, imp), jnp.where(hit, 1.0, sel)

    _, sel = lax.fori_loop(0, n_sel, pick, (imp, jnp.zeros((n_blk, tq), F32)))
    picked = jnp.max(sel, axis=1, keepdims=True)
    per_tile = tk // SEL_BLOCK
    n_causal = (t0 + tq + tk - 1) // tk
    n_used = jnp.int32(0)
    for j in range(used_ref.shape[0]):
        used_ref[n_used] = j
        hit = (jnp.max(picked[j * per_tile:(j + 1) * per_tile, :]) > 0) & (j < n_causal)
        n_used = n_used + hit.astype(I32)
    sel = sel.T.astype(BF16)

    span = WINDOW + tq
    w0 = pl.multiple_of(jnp.maximum(t0 - WINDOW, 0), tq)
    dist_w = t - (w0 + lax.broadcasted_iota(I32, (1, span), 1))
    valid_w = stack((dist_w >= 0) & (dist_w < WINDOW))
    s = _dot_nt(q, kw_ref[0, 0, pl.ds(w0, span), :]) * SCALE - slope * stack(dist_w.astype(F32))
    p = masked_softmax(jnp.where(valid_w, s, NEG_INF), valid_w)
    o_win = _dot(p.astype(BF16), vw_ref[0, 0, pl.ds(w0, span), :])

    kcol = lax.broadcasted_iota(I32, (1, tk), 1)

    def sel_body(i, carry):
        m, l, acc = carry
        kb = used_ref[i]
        ks0 = pl.multiple_of(kb * tk, tk)
        chosen = _dot(sel, ex_ref[kb]) > 0.5
        dist = t - (ks0 + kcol)
        valid = stack(chosen & (dist >= 0))
        s = _dot_nt(q, ks_ref[0, 0, pl.ds(ks0, tk), :]) * SCALE - slope * stack(dist.astype(F32))
        s = jnp.where(valid, s, NEG_INF)
        m_new = jnp.maximum(m, jnp.max(s, axis=1, keepdims=True))
        alpha = jnp.exp(m - m_new)
        p = jnp.where(valid, jnp.exp(s - m_new), 0.0)
        l = alpha * l + jnp.sum(p, axis=1, keepdims=True)
        acc = alpha * acc + _dot(p.astype(BF16), vs_ref[0, 0, pl.ds(ks0, tk), :])
        return m_new, l, acc

    init = (jnp.full((rr * tq, 1), NEG_INF, F32), jnp.zeros((rr * tq, 1), F32),
            jnp.zeros((rr * tq, HEAD_DIM), F32))
    _, l_sel, acc_sel = lax.fori_loop(0, n_used, sel_body, init)
    o_sel = acc_sel / l_sel

    gates = _sigmoid(gl_ref[0, 0])
    outs = []
    for r in range(rr):
        rows = slice(r * tq, (r + 1) * tq)
        outs.append(gates[:, 3 * r:3 * r + 1] * o_cmp[rows] + gates[:, 3 * r + 1:3 * r + 2] * o_sel[rows]
                    + gates[:, 3 * r + 2:3 * r + 3] * o_win[rows])
    o_ref[0, 0] = jnp.concatenate(outs, axis=1).astype(o_ref.dtype)


def _nsa_attention(q, kc, vc, ks, vs, kw, vw, gate_logits, slopes, tq=128, tk=256):
    b, _, t, dh = q.shape
    g = NSA_KV_GROUPS
    nc = kc.shape[2]
    n_sel = min(N_SELECT, t // SEL_BLOCK)
    real_blocks = t // SEL_BLOCK
    n_blk = LANES
    assert t % tk == 0 and t >= WINDOW + tq and real_blocks <= n_blk
    blk_ids = jnp.arange(n_blk, dtype=I32)
    pool = (blk_ids[:, None] == jnp.arange(nc, dtype=I32)[None, :] // (SEL_BLOCK // CMP_STRIDE)).astype(BF16)
    key_blk = (jnp.arange(t, dtype=I32) // SEL_BLOCK).reshape(t // tk, 1, tk)
    expand = (blk_ids[None, :, None] == key_blk).astype(BF16)
    kv_spec = pl.BlockSpec((1, 1, t, dh), lambda bi, gi, qi: (bi, gi, 0, 0))
    c_spec = pl.BlockSpec((1, 1, nc, dh), lambda bi, gi, qi: (bi, gi, 0, 0))
    return pl.pallas_call(
        functools.partial(_nsa_kernel, tq=tq, tk=tk, n_sel=n_sel, real_blocks=real_blocks),
        grid=(b, g, t // tq),
        in_specs=[
            pl.BlockSpec(memory_space=pltpu.SMEM),
            pl.BlockSpec((1, NSA_GROUP, tq, dh), lambda bi, gi, qi: (bi, gi, qi, 0)),
            c_spec, c_spec, kv_spec, kv_spec, kv_spec, kv_spec,
            pl.BlockSpec((1, 1, tq, LANES), lambda bi, gi, qi: (bi, gi, qi, 0)),
            pl.BlockSpec(pool.shape, lambda bi, gi, qi: (0, 0)),
            pl.BlockSpec(expand.shape, lambda bi, gi, qi: (0, 0, 0)),
        ],
        out_specs=pl.BlockSpec((1, 1, tq, NSA_GROUP * dh), lambda bi, gi, qi: (bi, gi, qi, 0)),
        out_shape=jax.ShapeDtypeStruct((b, g, t, NSA_GROUP * dh), BF16),
        scratch_shapes=[pltpu.SMEM((t // tk,), I32)],
        compiler_params=_params("parallel", "parallel", "arbitrary"),
        name="nsa_attn",
    )(slopes, q, kc, vc, ks, vs, kw, vw, gate_logits, pool, expand)


def _mem_kernel(q_ref, kv_ref, o_ref):
    q = q_ref[0]
    kv = kv_ref[0]
    outs = []
    for h in range(MEM_HEADS):
        sl = slice(h * HEAD_DIM, (h + 1) * HEAD_DIM)
        s = _dot_nt(q[:, sl], kv[:, sl]) * SCALE
        p = jnp.exp(s - jnp.max(s, axis=1, keepdims=True))
        p = p / jnp.sum(p, axis=1, keepdims=True)
        outs.append(_dot(p.astype(BF16), kv[:, MEM_W + h * HEAD_DIM:MEM_W + (h + 1) * HEAD_DIM]))
    o_ref[0] = jnp.concatenate(outs, axis=1).astype(o_ref.dtype)


def _mem_attention(q, mkv, tq=512):
    b, t, w = q.shape
    m = mkv.shape[1]
    return pl.pallas_call(
        _mem_kernel,
        grid=(b, t // tq),
        in_specs=[
            pl.BlockSpec((1, tq, w), lambda bi, qi: (bi, qi, 0)),
            pl.BlockSpec((1, m, 2 * w), lambda bi, qi: (bi, 0, 0)),
        ],
        out_specs=pl.BlockSpec((1, tq, w), lambda bi, qi: (bi, qi, 0)),
        out_shape=jax.ShapeDtypeStruct((b, t, w), BF16),
        compiler_params=_params("parallel", "parallel"),
        name="mem_attn",
    )(q, mkv)


def _merge_kernel(sb_ref, nsa_ref, mem_ref, mg_ref, bm_ref, x_ref, wsb_ref, wnsa_ref, wmem_ref,
                  wout_ref, fg_ref, wq_ref, x1_ref, h2_ref, q_ref):
    d = x_ref.shape[1]
    branches = (_dot(sb_ref[...], wsb_ref[...]), _dot(nsa_ref[...], wnsa_ref[...]),
                _dot(mem_ref[...], wmem_ref[...]))
    merged = jnp.zeros_like(branches[0])
    for j in range(N_BRANCHES):
        gate = _sigmoid(mg_ref[:, j * d:(j + 1) * d] + bm_ref[:, j * d:(j + 1) * d])
        merged = merged + gate * branches[j]
    x1 = x_ref[...] + _dot(merged.astype(BF16), wout_ref[...])
    x1_ref[...] = x1
    h2 = _rms(x1, fg_ref[...])
    h2_ref[...] = h2
    q = _dot(h2.astype(BF16), wq_ref[...]).astype(q_ref.dtype)
    for c in range(q_ref.shape[0]):
        q_ref[c] = q[:, c * LANES:(c + 1) * LANES]


def _merge(sb, nsa, mem, gates_slab, b_merge, x2d, w_sb, w_nsa, w_mem, w_out, ffn_g, w_q, tm=256):
    n, d = x2d.shape
    nq = w_q.shape[1] // LANES
    row = lambda w: pl.BlockSpec((tm, w), lambda i: (i, 0))
    full = lambda a: pl.BlockSpec(a.shape, lambda i: (0,) * a.ndim)
    bm = b_merge.reshape(1, -1)
    fg = ffn_g.reshape(1, d)
    return pl.pallas_call(
        _merge_kernel,
        grid=(n // tm,),
        in_specs=[row(sb.shape[1]), row(nsa.shape[1]), row(mem.shape[1]), row(N_BRANCHES * d),
                  full(bm), row(d), full(w_sb), full(w_nsa), full(w_mem), full(w_out), full(fg),
                  full(w_q)],
        out_specs=[row(d), row(d), pl.BlockSpec((nq, tm, LANES), lambda i: (0, i, 0))],
        out_shape=[jax.ShapeDtypeStruct((n, d), F32), jax.ShapeDtypeStruct((n, d), F32),
                   jax.ShapeDtypeStruct((nq, n, LANES), BF16)],
        compiler_params=_params("parallel"),
        name="merge",
    )(sb, nsa, mem, gates_slab, bm, x2d, w_sb, w_nsa, w_mem, w_out, fg, w_q)


def _peer_topk_kernel(q_ref, sk_ref, idx_ref, gw_ref, tv_ref, ti_ref, bv_ref):
    tt = q_ref.shape[1]
    kk = PEER_TOPK
    nk = PEER_N_KEYS
    rid = lax.broadcasted_iota(I32, (nk, tt), 0).astype(F32)
    for c in range(2 * PEER_HEADS):
        scores = _dot_nt(sk_ref[c], q_ref[c])

        def pick(k, s, c=c):
            best = jnp.max(s, axis=0, keepdims=True)
            first = jnp.min(jnp.where(s == best, rid, float(nk)), axis=0, keepdims=True)
            tv_ref[c, pl.ds(k, 1), :] = best
            ti_ref[c, pl.ds(k, 1), :] = first
            return jnp.where(rid == first, -jnp.inf, s)

        lax.fori_loop(0, kk, pick, scores)

    sub = 8
    widths = [min(kk, -(-(kk // (i + 1)) // sub) * sub) for i in range(kk // 2)]
    n_cand = sum(widths) + kk // 2
    pos = lax.broadcasted_iota(I32, (n_cand, tt), 0).astype(F32)
    for h in range(PEER_HEADS):
        s0, s1 = tv_ref[2 * h], tv_ref[2 * h + 1]
        i0, i1 = ti_ref[2 * h], ti_ref[2 * h + 1]
        cand, cidx = [], []
        for i, wd in enumerate(widths):
            keep = lax.broadcasted_iota(I32, (wd, 1), 0) < kk // (i + 1)
            cand.append(jnp.where(keep, s0[i:i + 1, :] + s1[0:wd, :], -jnp.inf))
            cidx.append(i0[i:i + 1, :] * float(nk) + i1[0:wd, :])
        cand.append(s0[kk // 2:, :] + s1[0:1, :])
        cidx.append(i0[kk // 2:, :] * float(nk) + i1[0:1, :])
        cand = jnp.concatenate(cand, axis=0)
        cidx = jnp.concatenate(cidx, axis=0)

        def pick2(k, s, h=h, cidx=cidx):
            best = jnp.max(s, axis=0, keepdims=True)
            first = jnp.min(jnp.where(s == best, pos, float(n_cand)), axis=0, keepdims=True)
            hit = pos == first
            bv_ref[pl.ds(k, 1), :] = best
            expert = jnp.max(jnp.where(hit, cidx, -1.0), axis=0, keepdims=True)
            idx_ref[pl.ds(h * kk + k, 1), :] = expert.astype(I32)
            return jnp.where(hit, -jnp.inf, s)

        lax.fori_loop(0, kk, pick2, cand)
        best = bv_ref[...]
        e = jnp.exp(best - best[0:1, :])
        gw_ref[h * kk:(h + 1) * kk, :] = e / jnp.sum(e, axis=0, keepdims=True)


def _peer_topk(q_chunks, subkeys, tt=256):
    nchunk, n, half = q_chunks.shape
    slots = PEER_HEADS * PEER_TOPK
    return pl.pallas_call(
        _peer_topk_kernel,
        grid=(n // tt,),
        in_specs=[
            pl.BlockSpec((nchunk, tt, half), lambda i: (0, i, 0)),
            pl.BlockSpec(subkeys.shape, lambda i: (0, 0, 0)),
        ],
        out_specs=[pl.BlockSpec((slots, tt), lambda i: (0, i)),
                   pl.BlockSpec((slots, tt), lambda i: (0, i))],
        out_shape=[jax.ShapeDtypeStruct((slots, n), I32), jax.ShapeDtypeStruct((slots, n), F32)],
        scratch_shapes=[pltpu.VMEM((nchunk, PEER_TOPK, tt), F32),
                        pltpu.VMEM((nchunk, PEER_TOPK, tt), F32),
                        pltpu.VMEM((PEER_TOPK, tt), F32)],
        compiler_params=_params("parallel"),
        name="peer_topk",
    )(q_chunks, subkeys)


def _peer_ffn_kernel(idx_hbm, uv_hbm, gw_ref, h_ref, x1_ref, fg_ref, y_ref, idx_smem, buf, idx_sem,
                     row_sem, *, first_tile, staged):
    tt, d = h_ref.shape
    slots = gw_ref.shape[0]
    rows = d // (2 * LANES)
    tile = pl.program_id(0)

    def halves(words):
        return tuple(pltpu.unpack_elementwise(words, index=i, packed_dtype=BF16, unpacked_dtype=F32)
                     for i in range(2))

    nbuf = buf.shape[0]

    if staged:
        def start(tok, slot):
            for r in range(buf.shape[1]):
                pltpu.make_async_copy(uv_hbm.at[tile * tt + tok, :, r, :], buf.at[slot, r],
                                      row_sem.at[slot]).start(priority=r % 2)
    else:
        idx_copy = pltpu.make_async_copy(idx_hbm.at[tile + first_tile], idx_smem, idx_sem)
        idx_copy.start()
        idx_copy.wait()

        def start(tok, slot):
            for e in range(slots):
                pltpu.make_async_copy(uv_hbm.at[idx_smem[tok * slots + e]], buf.at[slot, :, e, :],
                                      row_sem.at[slot]).start(priority=e % 2)

    def wait(slot):
        pltpu.make_async_copy(buf.at[slot], buf.at[slot], row_sem.at[slot]).wait()

    tok_lane = lax.broadcasted_iota(I32, (1, tt), 1)

    def expert_weights(tok, slot):
        h = h_ref[pl.ds(tok, 1), :]
        prod = jnp.zeros((slots, LANES), F32)
        for r in range(rows):
            lo, hi = halves(buf[slot, r])
            prod = prod + lo * h[:, r * LANES:(r + 1) * LANES]
            prod = prod + hi * h[:, d // 2 + r * LANES:d // 2 + (r + 1) * LANES]
        a = jnp.sum(prod, axis=1, keepdims=True)
        act = 0.5 * a * (1.0 + lax.erf(a * (2.0 ** -0.5)))
        gate = jnp.sum(jnp.where(tok_lane == tok, gw_ref[...], 0.0), axis=1, keepdims=True)
        return gate * act

    def weighted_values(tok, slot, w):
        parts = [halves(buf[slot, rows + r]) for r in range(rows)]
        y_ref[pl.ds(tok, 1), :] = jnp.concatenate(
            [jnp.sum(parts[r][i] * w, axis=0, keepdims=True) for i in range(2) for r in range(rows)], axis=1)

    for tok in range(nbuf):
        start(tok, tok)
    wait(0)
    w = expert_weights(0, 0)

    def step(tok, slot, w, refill):
        nxt = (slot + 1) % nbuf
        wait(nxt)
        w_next = expert_weights(tok + 1, nxt)
        weighted_values(tok, slot, w)
        if refill:
            start(tok + nbuf, slot)
        return w_next

    def round_of_steps(i, w):
        for slot in range(nbuf):
            w = step(i * nbuf + slot, slot, w, True)
        return w

    n_rounds = tt // nbuf - 1
    w = lax.fori_loop(0, n_rounds, round_of_steps, w)
    for slot in range(nbuf - 1):
        w = step(n_rounds * nbuf + slot, slot, w, False)
    weighted_values(tt - 1, nbuf - 1, w)
    y_ref[...] = _rms(x1_ref[...] + y_ref[...], fg_ref[...])


def _pack_tables_kernel(u_ref, v_ref, o_ref):
    d = u_ref.shape[1]
    rows = d // (2 * LANES)
    for base, src in ((0, u_ref), (rows, v_ref)):
        for r in range(rows):
            lo = src[:, r * LANES:(r + 1) * LANES]
            hi = src[:, d // 2 + r * LANES:d // 2 + (r + 1) * LANES]
            o_ref[:, base + r, :] = pltpu.pack_elementwise([lo, hi], packed_dtype=BF16)


def _pack_tables(u, v, te=512):
    n_exp, d = u.shape
    rows = d // LANES
    return pl.pallas_call(
        _pack_tables_kernel,
        grid=(n_exp // te,),
        in_specs=[pl.BlockSpec((te, d), lambda i: (i, 0)), pl.BlockSpec((te, d), lambda i: (i, 0))],
        out_specs=pl.BlockSpec((te, rows, LANES), lambda i: (i, 0, 0)),
        out_shape=jax.ShapeDtypeStruct((n_exp, rows, LANES), jnp.uint32),
        compiler_params=_params("parallel"),
        name="pack_tables",
    )(u, v)


def _sc_gather_slabs(slabs, slab_idx):
    m = slab_idx.shape[0]
    mesh = plsc.VectorSubcoreMesh(core_axis_name="core", subcore_axis_name="subcore")
    idx_rows = jnp.pad(slab_idx.reshape(m // SC_GATHER_WINDOW, SC_GATHER_WINDOW),
                       ((0, 0), (0, LANES - SC_GATHER_WINDOW)))

    @pl.kernel(out_type=jax.ShapeDtypeStruct((m,) + slabs.shape[1:], slabs.dtype), mesh=mesh)
    def gather(slabs_hbm, idx_hbm, out_hbm):
        def window(idx_vmem, out_vmem):
            pltpu.sync_copy(slabs_hbm.at[idx_vmem.at[0, pl.ds(0, SC_GATHER_WINDOW)]], out_vmem)

        pltpu.emit_pipeline(
            window,
            grid=(m // SC_GATHER_WINDOW,),
            in_specs=[pl.BlockSpec((1, LANES), index_map=lambda i: (i, 0))],
            out_specs=[pl.BlockSpec((SC_GATHER_WINDOW,) + slabs.shape[1:], index_map=lambda i: (i, 0, 0))],
            core_axis_name=("core", "subcore"),
            dimension_semantics=(pltpu.PARALLEL,),
            trace_scopes=False,
        )(idx_hbm, out_hbm)

    return gather(slabs, idx_rows)


def _peer_ffn(idx_tiles, table, gw, h2, x1, final_g, tt, first_tile, n_tiles, staged):
    _, d = h2.shape
    slots = gw.shape[0]
    rows = d // LANES
    return pl.pallas_call(
        functools.partial(_peer_ffn_kernel, first_tile=first_tile, staged=staged),
        grid=(n_tiles,),
        in_specs=[
            pl.BlockSpec(memory_space=pl.ANY),
            pl.BlockSpec(memory_space=pl.ANY),
            pl.BlockSpec((slots, tt), lambda i: (0, i + first_tile)),
            pl.BlockSpec((tt, d), lambda i: (i + first_tile, 0)),
            pl.BlockSpec((tt, d), lambda i: (i + first_tile, 0)),
            pl.BlockSpec((1, d), lambda i: (0, 0)),
        ],
        out_specs=pl.BlockSpec((tt, d), lambda i: (i, 0)),
        out_shape=jax.ShapeDtypeStruct((n_tiles * tt, d), F32),
        scratch_shapes=[pltpu.SMEM((tt * slots,), I32),
                        pltpu.VMEM((PEER_GATHER_BUFFERS, rows, slots, LANES), table.dtype),
                        pltpu.SemaphoreType.DMA, pltpu.SemaphoreType.DMA((PEER_GATHER_BUFFERS,))],
        compiler_params=_params("arbitrary"),
        name="peer_ffn_staged" if staged else "peer_ffn",
    )(idx_tiles, table, gw, h2, x1, final_g.reshape(1, d))


def _mixers(x, mem, mix_g, mem_g, w_in, b_merge, pe_k, pe_v, cw_k, cw_v, w_mem_kv, w_sb_br, w_nsa_br,
            w_mem_br, w_out, ffn_g, peer_w_q, subkeys):
    b, t, d = x.shape
    m = mem.shape[1]
    g = NSA_KV_GROUPS
    n = b * t
    x2d = x.reshape(n, d)

    o_sbq, o_sbk, o_sbv = 0, SB_W, 2 * SB_W
    o_nq = 3 * SB_W
    o_nkv = o_nq + NSA_W
    o_ng = o_nkv + 6 * NSA_KV_W
    o_mq = o_ng + NSA_HEADS * 3
    o_mg = o_mq + MEM_W
    w_act = jnp.concatenate([w_in[:, :o_ng], w_in[:, o_mq:o_mg]], axis=1).astype(BF16)
    gate_pad = LANES - NSA_HEADS * 3
    w_gate = jnp.concatenate([w_in[:, o_mg:], w_in[:, o_ng:o_mq], jnp.zeros((d, gate_pad), w_in.dtype)],
                             axis=1).astype(BF16)
    act = _norm_matmul(x2d, mix_g, w_act, BF16, tm=256, tn=512)
    gates_slab = _norm_matmul(x2d, mix_g, w_gate, F32, tm=256, tn=640)

    def heads(lo, nh):
        return act[:, lo:lo + nh * HEAD_DIM].reshape(b, t, nh, HEAD_DIM).transpose(0, 2, 1, 3)

    sb_out = _sb_attention(heads(o_sbq, SB_HEADS), heads(o_sbk, SB_HEADS), heads(o_sbv, SB_HEADS))

    kv = [heads(o_nkv + j * NSA_KV_W, g) for j in range(6)]
    kc = _nsa_compress(kv[0], pe_k, cw_k)
    vc = _nsa_compress(kv[1], pe_v, cw_v)
    gl = gates_slab[:, N_BRANCHES * d:N_BRANCHES * d + NSA_HEADS * 3]
    gl = gl.reshape(b, t, g, NSA_GROUP * 3).transpose(0, 2, 1, 3)
    gl = jnp.pad(gl, ((0, 0), (0, 0), (0, 0), (0, LANES - NSA_GROUP * 3)))
    slopes = jnp.asarray([2.0 ** (-8.0 * (h + 1) / NSA_HEADS) for h in range(NSA_HEADS)], F32)
    nsa_out = _nsa_attention(heads(o_nq, NSA_HEADS), kc, vc, kv[2], kv[3], kv[4], kv[5], gl, slopes)
    nsa_out = nsa_out.transpose(0, 2, 1, 3).reshape(n, NSA_W)

    mkv = _norm_matmul(mem.reshape(b * m, d), mem_g, w_mem_kv.astype(BF16), BF16, tm=256, tn=512)
    mem_q = act[:, o_nkv + 6 * NSA_KV_W:].reshape(b, t, MEM_W)
    mem_out = _mem_attention(mem_q, mkv.reshape(b, m, 2 * MEM_W))

    x1, h2, q_chunks = _merge(
        sb_out.reshape(n, SB_W), nsa_out, mem_out.reshape(n, MEM_W), gates_slab, b_merge, x2d,
        w_sb_br.astype(BF16), w_nsa_br.astype(BF16), w_mem_br.astype(BF16), w_out.astype(BF16), ffn_g,
        peer_w_q.astype(BF16))

    half = PEER_QUERY_DIM // 2
    idx, gw = _peer_topk(q_chunks, subkeys.reshape(2 * PEER_HEADS, PEER_N_KEYS, half).astype(BF16))
    return x1, h2, idx.T, gw


def _layer(x, mem, *params_and_tables):
    uv, out_g = params_and_tables[-2:]
    b, t, d = x.shape
    n_groups = len(SC_GATHER_SHARES) if b % len(SC_GATHER_SHARES) == 0 else 1
    bg = b // n_groups
    tt = 128
    work = []
    for gi in range(n_groups):
        x1, h2, idx_tok, gw = _mixers(x[gi * bg:(gi + 1) * bg], mem[gi * bg:(gi + 1) * bg],
                                      *params_and_tables[:-2])
        n, slots = idx_tok.shape
        n_tiles = n // tt
        sc_tiles = int(n_tiles * SC_GATHER_SHARES[gi])
        tc_tiles = n_tiles - sc_tiles
        staged = _sc_gather_slabs(uv, idx_tok[tc_tiles * tt:].reshape(-1))
        staged = staged.reshape((sc_tiles * tt, slots) + uv.shape[1:])
        work.append((idx_tok.reshape(n_tiles, tt * slots), staged, gw, h2, x1, tc_tiles, sc_tiles))
    outs = []
    for idx_tiles, staged, gw, h2, x1, tc_tiles, sc_tiles in work:
        outs.append(_peer_ffn(idx_tiles, uv, gw, h2, x1, out_g, tt, 0, tc_tiles, False))
        outs.append(_peer_ffn(idx_tiles, staged, gw, h2, x1, out_g, tt, tc_tiles, sc_tiles, True))
    return jnp.concatenate(outs, axis=0).reshape(b, t, d)


def kernel(x, mem, mix_norm_g, mem_norm_g, w_in, b_merge, cmp_pe_k, cmp_pe_v, cmp_w_k, cmp_w_v, w_mem_kv, w_sb_br, w_nsa_br, w_mem_br, w_out, ffn_norm_g, peer_w_q, peer_subkeys, peer_u, peer_v, final_norm_g):
    depth = w_in.shape[0]
    assert depth == 1, "the final rmsnorm is fused into the last layer's PEER kernel"
    l = 0
    uv = _pack_tables(peer_u[l], peer_v[l])
    return _layer(x, mem, mix_norm_g[l], mem_norm_g[l], w_in[l], b_merge[l], cmp_pe_k[l], cmp_pe_v[l],
                  cmp_w_k[l], cmp_w_v[l], w_mem_kv[l], w_sb_br[l], w_nsa_br[l], w_mem_br[l], w_out[l],
                  ffn_norm_g[l], peer_w_q[l], peer_subkeys[l], uv, final_norm_g)
```

```python
import functools
import math

import jax
import jax.numpy as jnp
from jax import lax
from jax.experimental import pallas as pl
from jax.experimental.pallas import tpu as pltpu
from jax.experimental.pallas import tpu_sc as plsc

F32 = jnp.float32
BF16 = jnp.bfloat16
I32 = jnp.int32

HEAD_DIM = 64
SB_HEADS = 6
NSA_HEADS = 6
NSA_KV_GROUPS = 2
NSA_GROUP = NSA_HEADS // NSA_KV_GROUPS
MEM_HEADS = 4
N_BRANCHES = 3
SB_W = SB_HEADS * HEAD_DIM
NSA_W = NSA_HEADS * HEAD_DIM
NSA_KV_W = NSA_KV_GROUPS * HEAD_DIM
MEM_W = MEM_HEADS * HEAD_DIM
CMP_LEN = 32
CMP_STRIDE = 16
SEL_BLOCK = 64
N_SELECT = 16
WINDOW = 512
FORCED_SCORE = 1e4
PEER_HEADS = 8
PEER_N_KEYS = 128
PEER_QUERY_DIM = 256
PEER_TOPK = 16
RMS_EPS = 1e-6
NEG_INF = -1e30
SCALE = HEAD_DIM ** -0.5
SB_DEAD_LOG = 104.0

LANES = 128
PEER_GATHER_BUFFERS = 8
SC_GATHER_WINDOW = 32
SC_GATHER_SHARES = (0.5,)
VMEM_LIMIT_BYTES = 56 * 1024 * 1024

_NT = (((1,), (1,)), ((), ()))


def _params(*sem):
    return pltpu.CompilerParams(dimension_semantics=sem, vmem_limit_bytes=VMEM_LIMIT_BYTES)


def _dot(a, b):
    return jnp.dot(a, b, preferred_element_type=F32)


def _dot_nt(a, b):
    return lax.dot_general(a, b, _NT, preferred_element_type=F32)


def _sigmoid(x):
    return 1.0 / (1.0 + jnp.exp(-x))


def _rms(x, g):
    return x * lax.rsqrt(jnp.mean(x * x, axis=-1, keepdims=True) + RMS_EPS) * g


def _norm_matmul_kernel(x_ref, g_ref, w_ref, o_ref, *, tn):
    h = _rms(x_ref[...], g_ref[...]).astype(BF16)
    for c in range(0, o_ref.shape[1], tn):
        o_ref[:, c:c + tn] = _dot(h, w_ref[:, c:c + tn]).astype(o_ref.dtype)


def _norm_matmul(x2d, g, w, out_dtype, tm, tn):
    n, d = x2d.shape
    m = w.shape[1]
    return pl.pallas_call(
        functools.partial(_norm_matmul_kernel, tn=tn),
        grid=(n // tm,),
        in_specs=[
            pl.BlockSpec((tm, d), lambda i: (i, 0)),
            pl.BlockSpec((1, d), lambda i: (0, 0)),
            pl.BlockSpec((d, m), lambda i: (0, 0)),
        ],
        out_specs=pl.BlockSpec((tm, m), lambda i: (i, 0)),
        out_shape=jax.ShapeDtypeStruct((n, m), out_dtype),
        compiler_params=_params("parallel"),
        name="norm_matmul",
    )(x2d, g.reshape(1, d), w)


def _sb_kernel(q_ref, k_ref, v_ref, o_ref, *, tile):
    qi = pl.program_id(2)
    nh = q_ref.shape[1]
    row = lax.broadcasted_iota(I32, (tile, tile), 0)
    col = lax.broadcasted_iota(I32, (tile, tile), 1)
    lower = row > col
    later = lower.astype(BF16)
    qs = [q_ref[0, hh] for hh in range(nh)]

    def visit(hh, ks, c, acc, diagonal):
        k = k_ref[0, hh, pl.ds(ks, tile), :]
        v = v_ref[0, hh, pl.ds(ks, tile), :]
        z = _dot_nt(qs[hh], k) * SCALE
        sp = jnp.maximum(z, 0.0) + jnp.log(1.0 + jnp.exp(-jnp.abs(z)))
        if diagonal:
            sp = jnp.where(lower, sp, 0.0)
        hi = sp.astype(BF16)
        lo = (sp - hi.astype(F32)).astype(BF16)
        after = _dot(hi, later) + _dot(lo, later)
        a = jnp.exp(z - sp - after - c)
        if diagonal:
            a = jnp.where(lower, a, 0.0)
        return c + jnp.sum(sp, axis=1, keepdims=True), acc + _dot(a.astype(BF16), v)

    q0 = pl.multiple_of(qi * tile, tile)
    state = []
    for hh in range(nh):
        state.extend(visit(hh, q0, jnp.zeros((tile, 1), F32), jnp.zeros((tile, HEAD_DIM), F32), True))

    def smallest_carry(state):
        c = state[0]
        for hh in range(1, nh):
            c = jnp.minimum(c, state[2 * hh])
        return jnp.min(c)

    def live(carry):
        i, c_min, _ = carry
        return (i < qi) & (c_min <= SB_DEAD_LOG)

    def body(carry):
        i, _, state = carry
        ks = pl.multiple_of((qi - 1 - i) * tile, tile)
        out = []
        for hh in range(nh):
            out.extend(visit(hh, ks, state[2 * hh], state[2 * hh + 1], False))
        return i + 1, smallest_carry(out), tuple(out)

    _, _, state = lax.while_loop(live, body, (0, smallest_carry(state), tuple(state)))
    o_ref[0] = jnp.concatenate([state[2 * hh + 1] for hh in range(nh)], axis=1).astype(o_ref.dtype)


def _sb_attention(q, k, v, tile=256):
    b, h, t, dh = q.shape
    hp = 2
    tq = tile
    return pl.pallas_call(
        functools.partial(_sb_kernel, tile=tile),
        grid=(b, h // hp, t // tq),
        in_specs=[
            pl.BlockSpec((1, hp, tq, dh), lambda bi, hi, qi: (bi, hi, qi, 0)),
            pl.BlockSpec((1, hp, t, dh), lambda bi, hi, qi: (bi, hi, 0, 0)),
            pl.BlockSpec((1, hp, t, dh), lambda bi, hi, qi: (bi, hi, 0, 0)),
        ],
        out_specs=pl.BlockSpec((1, tq, hp * dh), lambda bi, hi, qi: (bi, qi, hi)),
        out_shape=jax.ShapeDtypeStruct((b, t, h * dh), BF16),
        compiler_params=_params("parallel", "parallel", "arbitrary"),
        name="sb_attn",
    )(q, k, v)


def _compress_kernel(x_ref, pe_ref, w_ref, o_ref):
    x = x_ref[0, 0]
    nc = x.shape[0]
    w_lo, w_hi = w_ref[0], w_ref[1]
    first = _dot(x, w_lo)
    second = _dot(x, w_hi)
    feat = pe_ref.shape[1]
    pe_lo = jnp.broadcast_to(pe_ref[0:1, :], (8, feat)).astype(BF16)
    pe_hi = jnp.broadcast_to(pe_ref[1:2, :], (8, feat)).astype(BF16)
    bias = _dot(pe_lo, w_lo)[0:1] + _dot(pe_hi, w_hi)[0:1]
    o_ref[0, 0] = (first + pltpu.roll(second, nc - 1, 0) + bias).astype(o_ref.dtype)


def _nsa_compress(z, pe, w):
    b, g, t, dh = z.shape
    nc = t // CMP_STRIDE
    feat = CMP_STRIDE * dh
    x = z.reshape(b, g, nc, feat)
    pe2 = pe.reshape(2, feat)
    w2 = w.reshape(2, feat, dh).astype(BF16)
    return pl.pallas_call(
        _compress_kernel,
        grid=(b, g),
        in_specs=[
            pl.BlockSpec((1, 1, nc, feat), lambda bi, gi: (bi, gi, 0, 0)),
            pl.BlockSpec((2, feat), lambda bi, gi: (0, 0)),
            pl.BlockSpec((2, feat, dh), lambda bi, gi: (0, 0, 0)),
        ],
        out_specs=pl.BlockSpec((1, 1, nc, dh), lambda bi, gi: (bi, gi, 0, 0)),
        out_shape=jax.ShapeDtypeStruct((b, g, nc, dh), BF16),
        compiler_params=_params("parallel", "parallel"),
        name="nsa_compress",
    )(x, pe2, w2)


def _nsa_kernel(slopes_ref, q_ref, kc_ref, vc_ref, ks_ref, vs_ref, kw_ref, vw_ref, gl_ref, pool_ref, ex_ref,
                o_ref, used_ref, *, tq, tk, n_sel, real_blocks):
    grp = pl.program_id(1)
    t0 = pl.program_id(2) * tq
    nc = kc_ref.shape[2]
    rr = NSA_GROUP
    t = t0 + lax.broadcasted_iota(I32, (tq, 1), 0)

    def stack(x):
        return jnp.concatenate([x] * rr, axis=0)

    q = q_ref[0].reshape(rr * tq, HEAD_DIM)
    slope = jnp.concatenate([jnp.full((tq, 1), slopes_ref[grp * rr + r], F32) for r in range(rr)], axis=0)

    def heads_sum(x):
        out = x[0:tq]
        for r in range(1, rr):
            out = out + x[r * tq:(r + 1) * tq]
        return out

    def masked_softmax(s, valid):
        p = jnp.where(valid, jnp.exp(s - jnp.max(s, axis=1, keepdims=True)), 0.0)
        denom = jnp.sum(p, axis=1, keepdims=True)
        return p / jnp.where(denom > 0, denom, 1.0)

    lane = lax.broadcasted_iota(I32, (1, nc), 1)
    dist_c = (t - (lane * CMP_STRIDE + (CMP_LEN - 1))).astype(F32)
    valid_c = stack(dist_c >= 0)
    s = _dot_nt(q, kc_ref[0, 0]) * SCALE - slope * stack(dist_c)
    p = masked_softmax(jnp.where(valid_c, s, NEG_INF), valid_c)
    o_cmp = _dot(p.astype(BF16), vc_ref[0, 0])
    psum = heads_sum(p)

    n_blk = pool_ref.shape[0]
    hi = psum.astype(BF16)
    rest = psum - hi.astype(F32)
    mid = rest.astype(BF16)
    lo = (rest - mid.astype(F32)).astype(BF16)
    pool = pool_ref[...]
    imp = _dot_nt(pool, hi) + _dot_nt(pool, mid) + _dot_nt(pool, lo)
    blk = lax.broadcasted_iota(I32, (n_blk, 1), 0)
    cur = (t0 + lax.broadcasted_iota(I32, (1, tq), 1)) // SEL_BLOCK
    forced = (blk == 0) | (blk == cur) | (blk == cur - 1)
    imp = jnp.where(forced, FORCED_SCORE, jnp.where(blk <= cur, imp, -1.0))
    imp = jnp.where(blk < real_blocks, imp, -jnp.inf)
    blk_f = blk.astype(F32)

    def pick(_, carry):
        imp, sel = carry
        best = jnp.max(imp, axis=0, keepdims=True)
        first = jnp.min(jnp.where(imp == best, blk_f, float(n_blk)), axis=0, keepdims=True)
        hit = blk_f == first
        return jnp.where(hit, -jnp.inf, imp), jnp.where(hit, 1.0, sel)

    _, sel = lax.fori_loop(0, n_sel, pick, (imp, jnp.zeros((n_blk, tq), F32)))
    picked = jnp.max(sel, axis=1, keepdims=True)
    per_tile = tk // SEL_BLOCK
    n_causal = (t0 + tq + tk - 1) // tk
    n_used = jnp.int32(0)
    for j in range(used_ref.shape[0]):
        used_ref[n_used] = j
        hit = (jnp.max(picked[j * per_tile:(j + 1) * per_tile, :]) > 0) & (j < n_causal)
        n_used = n_used + hit.astype(I32)
    sel = sel.T.astype(BF16)

    span = WINDOW + tq
    w0 = pl.multiple_of(jnp.maximum(t0 - WINDOW, 0), tq)
    dist_w = t - (w0 + lax.broadcasted_iota(I32, (1, span), 1))
    valid_w = stack((dist_w >= 0) & (dist_w < WINDOW))
    s = _dot_nt(q, kw_ref[0, 0, pl.ds(w0, span), :]) * SCALE - slope * stack(dist_w.astype(F32))
    p = masked_softmax(jnp.where(valid_w, s, NEG_INF), valid_w)
    o_win = _dot(p.astype(BF16), vw_ref[0, 0, pl.ds(w0, span), :])

    kcol = lax.broadcasted_iota(I32, (1, tk), 1)

    def sel_body(i, carry):
        m, l, acc = carry
        kb = used_ref[i]
        ks0 = pl.multiple_of(kb * tk, tk)
        chosen = _dot(sel, ex_ref[kb]) > 0.5
        dist = t - (ks0 + kcol)
        valid = stack(chosen & (dist >= 0))
        s = _dot_nt(q, ks_ref[0, 0, pl.ds(ks0, tk), :]) * SCALE - slope * stack(dist.astype(F32))
        s = jnp.where(valid, s, NEG_INF)
        m_new = jnp.maximum(m, jnp.max(s, axis=1, keepdims=True))
        alpha = jnp.exp(m - m_new)
        p = jnp.where(valid, jnp.exp(s - m_new), 0.0)
        l = alpha * l + jnp.sum(p, axis=1, keepdims=True)
        acc = alpha * acc + _dot(p.astype(BF16), vs_ref[0, 0, pl.ds(ks0, tk), :])
        return m_new, l, acc

    init = (jnp.full((rr * tq, 1), NEG_INF, F32), jnp.zeros((rr * tq, 1), F32),
            jnp.zeros((rr * tq, HEAD_DIM), F32))
    _, l_sel, acc_sel = lax.fori_loop(0, n_used, sel_body, init)
    o_sel = acc_sel / l_sel

    gates = _sigmoid(gl_ref[0, 0])
    outs = []
    for r in range(rr):
        rows = slice(r * tq, (r + 1) * tq)
        outs.append(gates[:, 3 * r:3 * r + 1] * o_cmp[rows] + gates[:, 3 * r + 1:3 * r + 2] * o_sel[rows]
                    + gates[:, 3 * r + 2:3 * r + 3] * o_win[rows])
    o_ref[0, 0] = jnp.concatenate(outs, axis=1).astype(o_ref.dtype)


def _nsa_attention(q, kc, vc, ks, vs, kw, vw, gate_logits, slopes, tq=128, tk=256):
    b, _, t, dh = q.shape
    g = NSA_KV_GROUPS
    nc = kc.shape[2]
    n_sel = min(N_SELECT, t // SEL_BLOCK)
    real_blocks = t // SEL_BLOCK
    n_blk = LANES
    assert t % tk == 0 and t >= WINDOW + tq and real_blocks <= n_blk
    blk_ids = jnp.arange(n_blk, dtype=I32)
    pool = (blk_ids[:, None] == jnp.arange(nc, dtype=I32)[None, :] // (SEL_BLOCK // CMP_STRIDE)).astype(BF16)
    key_blk = (jnp.arange(t, dtype=I32) // SEL_BLOCK).reshape(t // tk, 1, tk)
    expand = (blk_ids[None, :, None] == key_blk).astype(BF16)
    kv_spec = pl.BlockSpec((1, 1, t, dh), lambda bi, gi, qi: (bi, gi, 0, 0))
    c_spec = pl.BlockSpec((1, 1, nc, dh), lambda bi, gi, qi: (bi, gi, 0, 0))
    return pl.pallas_call(
        functools.partial(_nsa_kernel, tq=tq, tk=tk, n_sel=n_sel, real_blocks=real_blocks),
        grid=(b, g, t // tq),
        in_specs=[
            pl.BlockSpec(memory_space=pltpu.SMEM),
            pl.BlockSpec((1, NSA_GROUP, tq, dh), lambda bi, gi, qi: (bi, gi, qi, 0)),
            c_spec, c_spec, kv_spec, kv_spec, kv_spec, kv_spec,
            pl.BlockSpec((1, 1, tq, LANES), lambda bi, gi, qi: (bi, gi, qi, 0)),
            pl.BlockSpec(pool.shape, lambda bi, gi, qi: (0, 0)),
            pl.BlockSpec(expand.shape, lambda bi, gi, qi: (0, 0, 0)),
        ],
        out_specs=pl.BlockSpec((1, 1, tq, NSA_GROUP * dh), lambda bi, gi, qi: (bi, gi, qi, 0)),
        out_shape=jax.ShapeDtypeStruct((b, g, t, NSA_GROUP * dh), BF16),
        scratch_shapes=[pltpu.SMEM((t // tk,), I32)],
        compiler_params=_params("parallel", "parallel", "arbitrary"),
        name="nsa_attn",
    )(slopes, q, kc, vc, ks, vs, kw, vw, gate_logits, pool, expand)


def _mem_kernel(q_ref, kv_ref, o_ref):
    q = q_ref[0]
    kv = kv_ref[0]
    outs = []
    for h in range(MEM_HEADS):
        sl = slice(h * HEAD_DIM, (h + 1) * HEAD_DIM)
        s = _dot_nt(q[:, sl], kv[:, sl]) * SCALE
        p = jnp.exp(s - jnp.max(s, axis=1, keepdims=True))
        p = p / jnp.sum(p, axis=1, keepdims=True)
        outs.append(_dot(p.astype(BF16), kv[:, MEM_W + h * HEAD_DIM:MEM_W + (h + 1) * HEAD_DIM]))
    o_ref[0] = jnp.concatenate(outs, axis=1).astype(o_ref.dtype)


def _mem_attention(q, mkv, tq=512):
    b, t, w = q.shape
    m = mkv.shape[1]
    return pl.pallas_call(
        _mem_kernel,
        grid=(b, t // tq),
        in_specs=[
            pl.BlockSpec((1, tq, w), lambda bi, qi: (bi, qi, 0)),
            pl.BlockSpec((1, m, 2 * w), lambda bi, qi: (bi, 0, 0)),
        ],
        out_specs=pl.BlockSpec((1, tq, w), lambda bi, qi: (bi, qi, 0)),
        out_shape=jax.ShapeDtypeStruct((b, t, w), BF16),
        compiler_params=_params("parallel", "parallel"),
        name="mem_attn",
    )(q, mkv)


def _merge_kernel(sb_ref, nsa_ref, mem_ref, mg_ref, bm_ref, x_ref, wsb_ref, wnsa_ref, wmem_ref,
                  wout_ref, fg_ref, wq_ref, x1_ref, h2_ref, q_ref):
    d = x_ref.shape[1]
    branches = (_dot(sb_ref[...], wsb_ref[...]), _dot(nsa_ref[...], wnsa_ref[...]),
                _dot(mem_ref[...], wmem_ref[...]))
    merged = jnp.zeros_like(branches[0])
    for j in range(N_BRANCHES):
        gate = _sigmoid(mg_ref[:, j * d:(j + 1) * d] + bm_ref[:, j * d:(j + 1) * d])
        merged = merged + gate * branches[j]
    x1 = x_ref[...] + _dot(merged.astype(BF16), wout_ref[...])
    x1_ref[...] = x1
    h2 = _rms(x1, fg_ref[...])
    h2_ref[...] = h2
    q = _dot(h2.astype(BF16), wq_ref[...]).astype(q_ref.dtype)
    for c in range(q_ref.shape[0]):
        q_ref[c] = q[:, c * LANES:(c + 1) * LANES]


def _merge(sb, nsa, mem, gates_slab, b_merge, x2d, w_sb, w_nsa, w_mem, w_out, ffn_g, w_q, tm=256):
    n, d = x2d.shape
    nq = w_q.shape[1] // LANES
    row = lambda w: pl.BlockSpec((tm, w), lambda i: (i, 0))
    full = lambda a: pl.BlockSpec(a.shape, lambda i: (0,) * a.ndim)
    bm = b_merge.reshape(1, -1)
    fg = ffn_g.reshape(1, d)
    return pl.pallas_call(
        _merge_kernel,
        grid=(n // tm,),
        in_specs=[row(sb.shape[1]), row(nsa.shape[1]), row(mem.shape[1]), row(N_BRANCHES * d),
                  full(bm), row(d), full(w_sb), full(w_nsa), full(w_mem), full(w_out), full(fg),
                  full(w_q)],
        out_specs=[row(d), row(d), pl.BlockSpec((nq, tm, LANES), lambda i: (0, i, 0))],
        out_shape=[jax.ShapeDtypeStruct((n, d), F32), jax.ShapeDtypeStruct((n, d), F32),
                   jax.ShapeDtypeStruct((nq, n, LANES), BF16)],
        compiler_params=_params("parallel"),
        name="merge",
    )(sb, nsa, mem, gates_slab, bm, x2d, w_sb, w_nsa, w_mem, w_out, fg, w_q)


def _peer_topk_kernel(q_ref, sk_ref, idx_ref, gw_ref, tv_ref, ti_ref, bv_ref):
    tt = q_ref.shape[1]
    kk = PEER_TOPK
    nk = PEER_N_KEYS
    rid = lax.broadcasted_iota(I32, (nk, tt), 0).astype(F32)
    for c in range(2 * PEER_HEADS):
        scores = _dot_nt(sk_ref[c], q_ref[c])

        def pick(k, s, c=c):
            best = jnp.max(s, axis=0, keepdims=True)
            first = jnp.min(jnp.where(s == best, rid, float(nk)), axis=0, keepdims=True)
            tv_ref[c, pl.ds(k, 1), :] = best
            ti_ref[c, pl.ds(k, 1), :] = first
            return jnp.where(rid == first, -jnp.inf, s)

        lax.fori_loop(0, kk, pick, scores)

    sub = 8
    widths = [min(kk, -(-(kk // (i + 1)) // sub) * sub) for i in range(kk // 2)]
    n_cand = sum(widths) + kk // 2
    pos = lax.broadcasted_iota(I32, (n_cand, tt), 0).astype(F32)
    for h in range(PEER_HEADS):
        s0, s1 = tv_ref[2 * h], tv_ref[2 * h + 1]
        i0, i1 = ti_ref[2 * h], ti_ref[2 * h + 1]
        cand, cidx = [], []
        for i, wd in enumerate(widths):
            keep = lax.broadcasted_iota(I32, (wd, 1), 0) < kk // (i + 1)
            cand.append(jnp.where(keep, s0[i:i + 1, :] + s1[0:wd, :], -jnp.inf))
            cidx.append(i0[i:i + 1, :] * float(nk) + i1[0:wd, :])
        cand.append(s0[kk // 2:, :] + s1[0:1, :])
        cidx.append(i0[kk // 2:, :] * float(nk) + i1[0:1, :])
        cand = jnp.concatenate(cand, axis=0)
        cidx = jnp.concatenate(cidx, axis=0)

        def pick2(k, s, h=h, cidx=cidx):
            best = jnp.max(s, axis=0, keepdims=True)
            first = jnp.min(jnp.where(s == best, pos, float(n_cand)), axis=0, keepdims=True)
            hit = pos == first
            bv_ref[pl.ds(k, 1), :] = best
            expert = jnp.max(jnp.where(hit, cidx, -1.0), axis=0, keepdims=True)
            idx_ref[pl.ds(h * kk + k, 1), :] = expert.astype(I32)
            return jnp.where(hit, -jnp.inf, s)

        lax.fori_loop(0, kk, pick2, cand)
        best = bv_ref[...]
        e = jnp.exp(best - best[0:1, :])
        gw_ref[h * kk:(h + 1) * kk, :] = e / jnp.sum(e, axis=0, keepdims=True)


def _peer_topk(q_chunks, subkeys, tt=256):
    nchunk, n, half = q_chunks.shape
    slots = PEER_HEADS * PEER_TOPK
    return pl.pallas_call(
        _peer_topk_kernel,
        grid=(n // tt,),
        in_specs=[
            pl.BlockSpec((nchunk, tt, half), lambda i: (0, i, 0)),
            pl.BlockSpec(subkeys.shape, lambda i: (0, 0, 0)),
        ],
        out_specs=[pl.BlockSpec((slots, tt), lambda i: (0, i)),
                   pl.BlockSpec((slots, tt), lambda i: (0, i))],
        out_shape=[jax.ShapeDtypeStruct((slots, n), I32), jax.ShapeDtypeStruct((slots, n), F32)],
        scratch_shapes=[pltpu.VMEM((nchunk, PEER_TOPK, tt), F32),
                        pltpu.VMEM((nchunk, PEER_TOPK, tt), F32),
                        pltpu.VMEM((PEER_TOPK, tt), F32)],
        compiler_params=_params("parallel"),
        name="peer_topk",
    )(q_chunks, subkeys)


def _peer_ffn_kernel(idx_hbm, uv_hbm, gw_ref, h_ref, x1_ref, fg_ref, y_ref, idx_smem, buf, idx_sem,
                     row_sem, *, first_tile, staged):
    tt, d = h_ref.shape
    slots = gw_ref.shape[0]
    rows = d // (2 * LANES)
    tile = pl.program_id(0)

    def halves(words):
        return tuple(pltpu.unpack_elementwise(words, index=i, packed_dtype=BF16, unpacked_dtype=F32)
                     for i in range(2))

    nbuf = buf.shape[0]

    if staged:
        def start(tok, slot):
            for r in range(buf.shape[1]):
                pltpu.make_async_copy(uv_hbm.at[tile * tt + tok, :, r, :], buf.at[slot, r],
                                      row_sem.at[slot]).start(priority=r % 2)
    else:
        idx_copy = pltpu.make_async_copy(idx_hbm.at[tile + first_tile], idx_smem, idx_sem)
        idx_copy.start()
        idx_copy.wait()

        def start(tok, slot):
            for e in range(slots):
                pltpu.make_async_copy(uv_hbm.at[idx_smem[tok * slots + e]], buf.at[slot, :, e, :],
                                      row_sem.at[slot]).start(priority=e % 2)

    def wait(slot):
        pltpu.make_async_copy(buf.at[slot], buf.at[slot], row_sem.at[slot]).wait()

    tok_lane = lax.broadcasted_iota(I32, (1, tt), 1)

    def expert_weights(tok, slot):
        h = h_ref[pl.ds(tok, 1), :]
        prod = jnp.zeros((slots, LANES), F32)
        for r in range(rows):
            lo, hi = halves(buf[slot, r])
            prod = prod + lo * h[:, r * LANES:(r + 1) * LANES]
            prod = prod + hi * h[:, d // 2 + r * LANES:d // 2 + (r + 1) * LANES]
        a = jnp.sum(prod, axis=1, keepdims=True)
        act = 0.5 * a * (1.0 + lax.erf(a * (2.0 ** -0.5)))
        gate = jnp.sum(jnp.where(tok_lane == tok, gw_ref[...], 0.0), axis=1, keepdims=True)
        return gate * act

    def weighted_values(tok, slot, w):
        parts = [halves(buf[slot, rows + r]) for r in range(rows)]
        y_ref[pl.ds(tok, 1), :] = jnp.concatenate(
            [jnp.sum(parts[r][i] * w, axis=0, keepdims=True) for i in range(2) for r in range(rows)], axis=1)

    for tok in range(nbuf):
        start(tok, tok)
    wait(0)
    w = expert_weights(0, 0)

    def step(tok, slot, w, refill):
        nxt = (slot + 1) % nbuf
        wait(nxt)
        w_next = expert_weights(tok + 1, nxt)
        weighted_values(tok, slot, w)
        if refill:
            start(tok + nbuf, slot)
        return w_next

    def round_of_steps(i, w):
        for slot in range(nbuf):
            w = step(i * nbuf + slot, slot, w, True)
        return w

    n_rounds = tt // nbuf - 1
    w = lax.fori_loop(0, n_rounds, round_of_steps, w)
    for slot in range(nbuf - 1):
        w = step(n_rounds * nbuf + slot, slot, w, False)
    weighted_values(tt - 1, nbuf - 1, w)
    y_ref[...] = _rms(x1_ref[...] + y_ref[...], fg_ref[...])


def _pack_tables_kernel(u_ref, v_ref, o_ref):
    d = u_ref.shape[1]
    rows = d // (2 * LANES)
    for base, src in ((0, u_ref), (rows, v_ref)):
        for r in range(rows):
            lo = src[:, r * LANES:(r + 1) * LANES]
            hi = src[:, d // 2 + r * LANES:d // 2 + (r + 1) * LANES]
            o_ref[:, base + r, :] = pltpu.pack_elementwise([lo, hi], packed_dtype=BF16)


def _pack_tables(u, v, te=512):
    n_exp, d = u.shape
    rows = d // LANES
    return pl.pallas_call(
        _pack_tables_kernel,
        grid=(n_exp // te,),
        in_specs=[pl.BlockSpec((te, d), lambda i: (i, 0)), pl.BlockSpec((te, d), lambda i: (i, 0))],
        out_specs=pl.BlockSpec((te, rows, LANES), lambda i: (i, 0, 0)),
        out_shape=jax.ShapeDtypeStruct((n_exp, rows, LANES), jnp.uint32),
        compiler_params=_params("parallel"),
        name="pack_tables",
    )(u, v)


def _sc_gather_slabs(slabs, slab_idx):
    m = slab_idx.shape[0]
    mesh = plsc.VectorSubcoreMesh(core_axis_name="core", subcore_axis_name="subcore")
    idx_rows = jnp.pad(slab_idx.reshape(m // SC_GATHER_WINDOW, SC_GATHER_WINDOW),
                       ((0, 0), (0, LANES - SC_GATHER_WINDOW)))

    @pl.kernel(out_type=jax.ShapeDtypeStruct((m,) + slabs.shape[1:], slabs.dtype), mesh=mesh)
    def gather(slabs_hbm, idx_hbm, out_hbm):
        def window(idx_vmem, out_vmem):
            pltpu.sync_copy(slabs_hbm.at[idx_vmem.at[0, pl.ds(0, SC_GATHER_WINDOW)]], out_vmem)

        pltpu.emit_pipeline(
            window,
            grid=(m // SC_GATHER_WINDOW,),
            in_specs=[pl.BlockSpec((1, LANES), index_map=lambda i: (i, 0))],
            out_specs=[pl.BlockSpec((SC_GATHER_WINDOW,) + slabs.shape[1:], index_map=lambda i: (i, 0, 0))],
            core_axis_name=("core", "subcore"),
            dimension_semantics=(pltpu.PARALLEL,),
            trace_scopes=False,
        )(idx_hbm, out_hbm)

    return gather(slabs, idx_rows)


def _peer_ffn(idx_tiles, table, gw, h2, x1, final_g, tt, first_tile, n_tiles, staged):
    _, d = h2.shape
    slots = gw.shape[0]
    rows = d // LANES
    return pl.pallas_call(
        functools.partial(_peer_ffn_kernel, first_tile=first_tile, staged=staged),
        grid=(n_tiles,),
        in_specs=[
            pl.BlockSpec(memory_space=pl.ANY),
            pl.BlockSpec(memory_space=pl.ANY),
            pl.BlockSpec((slots, tt), lambda i: (0, i + first_tile)),
            pl.BlockSpec((tt, d), lambda i: (i + first_tile, 0)),
            pl.BlockSpec((tt, d), lambda i: (i + first_tile, 0)),
            pl.BlockSpec((1, d), lambda i: (0, 0)),
        ],
        out_specs=pl.BlockSpec((tt, d), lambda i: (i, 0)),
        out_shape=jax.ShapeDtypeStruct((n_tiles * tt, d), F32),
        scratch_shapes=[pltpu.SMEM((tt * slots,), I32),
                        pltpu.VMEM((PEER_GATHER_BUFFERS, rows, slots, LANES), table.dtype),
                        pltpu.SemaphoreType.DMA, pltpu.SemaphoreType.DMA((PEER_GATHER_BUFFERS,))],
        compiler_params=_params("arbitrary"),
        name="peer_ffn_staged" if staged else "peer_ffn",
    )(idx_tiles, table, gw, h2, x1, final_g.reshape(1, d))


def _mixers(x, mem, mix_g, mem_g, w_in, b_merge, pe_k, pe_v, cw_k, cw_v, w_mem_kv, w_sb_br, w_nsa_br,
            w_mem_br, w_out, ffn_g, peer_w_q, subkeys):
    b, t, d = x.shape
    m = mem.shape[1]
    g = NSA_KV_GROUPS
    n = b * t
    x2d = x.reshape(n, d)

    o_sbq, o_sbk, o_sbv = 0, SB_W, 2 * SB_W
    o_nq = 3 * SB_W
    o_nkv = o_nq + NSA_W
    o_ng = o_nkv + 6 * NSA_KV_W
    o_mq = o_ng + NSA_HEADS * 3
    o_mg = o_mq + MEM_W
    w_act = jnp.concatenate([w_in[:, :o_ng], w_in[:, o_mq:o_mg]], axis=1).astype(BF16)
    gate_pad = LANES - NSA_HEADS * 3
    w_gate = jnp.concatenate([w_in[:, o_mg:], w_in[:, o_ng:o_mq], jnp.zeros((d, gate_pad), w_in.dtype)],
                             axis=1).astype(BF16)
    act = _norm_matmul(x2d, mix_g, w_act, BF16, tm=256, tn=512)
    gates_slab = _norm_matmul(x2d, mix_g, w_gate, F32, tm=256, tn=640)

    def heads(lo, nh):
        return act[:, lo:lo + nh * HEAD_DIM].reshape(b, t, nh, HEAD_DIM).transpose(0, 2, 1, 3)

    sb_out = _sb_attention(heads(o_sbq, SB_HEADS), heads(o_sbk, SB_HEADS), heads(o_sbv, SB_HEADS))

    kv = [heads(o_nkv + j * NSA_KV_W, g) for j in range(6)]
    kc = _nsa_compress(kv[0], pe_k, cw_k)
    vc = _nsa_compress(kv[1], pe_v, cw_v)
    gl = gates_slab[:, N_BRANCHES * d:N_BRANCHES * d + NSA_HEADS * 3]
    gl = gl.reshape(b, t, g, NSA_GROUP * 3).transpose(0, 2, 1, 3)
    gl = jnp.pad(gl, ((0, 0), (0, 0), (0, 0), (0, LANES - NSA_GROUP * 3)))
    slopes = jnp.asarray([2.0 ** (-8.0 * (h + 1) / NSA_HEADS) for h in range(NSA_HEADS)], F32)
    nsa_out = _nsa_attention(heads(o_nq, NSA_HEADS), kc, vc, kv[2], kv[3], kv[4], kv[5], gl, slopes)
    nsa_out = nsa_out.transpose(0, 2, 1, 3).reshape(n, NSA_W)

    mkv = _norm_matmul(mem.reshape(b * m, d), mem_g, w_mem_kv.astype(BF16), BF16, tm=256, tn=512)
    mem_q = act[:, o_nkv + 6 * NSA_KV_W:].reshape(b, t, MEM_W)
    mem_out = _mem_attention(mem_q, mkv.reshape(b, m, 2 * MEM_W))

    x1, h2, q_chunks = _merge(
        sb_out.reshape(n, SB_W), nsa_out, mem_out.reshape(n, MEM_W), gates_slab, b_merge, x2d,
        w_sb_br.astype(BF16), w_nsa_br.astype(BF16), w_mem_br.astype(BF16), w_out.astype(BF16), ffn_g,
        peer_w_q.astype(BF16))

    half = PEER_QUERY_DIM // 2
    idx, gw = _peer_topk(q_chunks, subkeys.reshape(2 * PEER_HEADS, PEER_N_KEYS, half).astype(BF16))
    return x1, h2, idx.T, gw


def _layer(x, mem, *params_and_tables):
    uv, out_g = params_and_tables[-2:]
    b, t, d = x.shape
    n_groups = len(SC_GATHER_SHARES) if b % len(SC_GATHER_SHARES) == 0 else 1
    bg = b // n_groups
    tt = 128
    work = []
    for gi in range(n_groups):
        x1, h2, idx_tok, gw = _mixers(x[gi * bg:(gi + 1) * bg], mem[gi * bg:(gi + 1) * bg],
                                      *params_and_tables[:-2])
        n, slots = idx_tok.shape
        n_tiles = n // tt
        sc_tiles = int(n_tiles * SC_GATHER_SHARES[gi])
        tc_tiles = n_tiles - sc_tiles
        staged = _sc_gather_slabs(uv, idx_tok[tc_tiles * tt:].reshape(-1))
        staged = staged.reshape((sc_tiles * tt, slots) + uv.shape[1:])
        work.append((idx_tok.reshape(n_tiles, tt * slots), staged, gw, h2, x1, tc_tiles, sc_tiles))
    outs = []
    for idx_tiles, staged, gw, h2, x1, tc_tiles, sc_tiles in work:
        outs.append(_peer_ffn(idx_tiles, uv, gw, h2, x1, out_g, tt, 0, tc_tiles, False))
        outs.append(_peer_ffn(idx_tiles, staged, gw, h2, x1, out_g, tt, tc_tiles, sc_tiles, True))
    return jnp.concatenate(outs, axis=0).reshape(b, t, d)


def kernel(x, mem, mix_norm_g, mem_norm_g, w_in, b_merge, cmp_pe_k, cmp_pe_v, cmp_w_k, cmp_w_v, w_mem_kv, w_sb_br, w_nsa_br, w_mem_br, w_out, ffn_norm_g, peer_w_q, peer_subkeys, peer_u, peer_v, final_norm_g):
    depth = w_in.shape[0]
    assert depth == 1, "the final rmsnorm is fused into the last layer's PEER kernel"
    l = 0
    uv = _pack_tables(peer_u[l], peer_v[l])
    return _layer(x, mem, mix_norm_g[l], mem_norm_g[l], w_in[l], b_merge[l], cmp_pe_k[l], cmp_pe_v[l],
                  cmp_w_k[l], cmp_w_v[l], w_mem_kv[l], w_sb_br[l], w_nsa_br[l], w_mem_br[l], w_out[l],
                  ffn_norm_g[l], peer_w_q[l], peer_subkeys[l], uv, final_norm_g)
```

```python
import functools
import math

import jax
import jax.numpy as jnp
from jax import lax
from jax.experimental import pallas as pl
from jax.experimental.pallas import tpu as pltpu
from jax.experimental.pallas import tpu_sc as plsc

F32 = jnp.float32
BF16 = jnp.bfloat16
I32 = jnp.int32

HEAD_DIM = 64
SB_HEADS = 6
NSA_HEADS = 6
NSA_KV_GROUPS = 2
NSA_GROUP = NSA_HEADS // NSA_KV_GROUPS
MEM_HEADS = 4
N_BRANCHES = 3
SB_W = SB_HEADS * HEAD_DIM
NSA_W = NSA_HEADS * HEAD_DIM
NSA_KV_W = NSA_KV_GROUPS * HEAD_DIM
MEM_W = MEM_HEADS * HEAD_DIM
CMP_LEN = 32
CMP_STRIDE = 16
SEL_BLOCK = 64
N_SELECT = 16
WINDOW = 512
FORCED_SCORE = 1e4
PEER_HEADS = 8
PEER_N_KEYS = 128
PEER_QUERY_DIM = 256
PEER_TOPK = 16
RMS_EPS = 1e-6
NEG_INF = -1e30
SCALE = HEAD_DIM ** -0.5
SB_DEAD_LOG = 104.0

LANES = 128
PEER_GATHER_BUFFERS = 8
SC_GATHER_WINDOW = 32
SC_CHUNKS = 4
SC_CHUNK_TILES = 28
VMEM_LIMIT_BYTES = 56 * 1024 * 1024

_NT = (((1,), (1,)), ((), ()))


def _params(*sem):
    return pltpu.CompilerParams(dimension_semantics=sem, vmem_limit_bytes=VMEM_LIMIT_BYTES)


def _dot(a, b):
    return jnp.dot(a, b, preferred_element_type=F32)


def _dot_nt(a, b):
    return lax.dot_general(a, b, _NT, preferred_element_type=F32)


def _sigmoid(x):
    return 1.0 / (1.0 + jnp.exp(-x))


def _rms(x, g):
    return x * lax.rsqrt(jnp.mean(x * x, axis=-1, keepdims=True) + RMS_EPS) * g


def _norm_matmul_kernel(x_ref, g_ref, w_ref, o_ref, *, tn):
    h = _rms(x_ref[...], g_ref[...]).astype(BF16)
    for c in range(0, o_ref.shape[1], tn):
        o_ref[:, c:c + tn] = _dot(h, w_ref[:, c:c + tn]).astype(o_ref.dtype)


def _norm_matmul(x2d, g, w, out_dtype, tm, tn):
    n, d = x2d.shape
    m = w.shape[1]
    return pl.pallas_call(
        functools.partial(_norm_matmul_kernel, tn=tn),
        grid=(n // tm,),
        in_specs=[
            pl.BlockSpec((tm, d), lambda i: (i, 0)),
            pl.BlockSpec((1, d), lambda i: (0, 0)),
            pl.BlockSpec((d, m), lambda i: (0, 0)),
        ],
        out_specs=pl.BlockSpec((tm, m), lambda i: (i, 0)),
        out_shape=jax.ShapeDtypeStruct((n, m), out_dtype),
        compiler_params=_params("parallel"),
        name="norm_matmul",
    )(x2d, g.reshape(1, d), w)


def _sb_kernel(q_ref, k_ref, v_ref, o_ref, *, tile):
    qi = pl.program_id(2)
    nh = q_ref.shape[1]
    row = lax.broadcasted_iota(I32, (tile, tile), 0)
    col = lax.broadcasted_iota(I32, (tile, tile), 1)
    lower = row > col
    later = lower.astype(BF16)
    qs = [q_ref[0, hh] for hh in range(nh)]

    def visit(hh, ks, c, acc, diagonal):
        k = k_ref[0, hh, pl.ds(ks, tile), :]
        v = v_ref[0, hh, pl.ds(ks, tile), :]
        z = _dot_nt(qs[hh], k) * SCALE
        sp = jnp.maximum(z, 0.0) + jnp.log(1.0 + jnp.exp(-jnp.abs(z)))
        if diagonal:
            sp = jnp.where(lower, sp, 0.0)
        hi = sp.astype(BF16)
        lo = (sp - hi.astype(F32)).astype(BF16)
        after = _dot(hi, later) + _dot(lo, later)
        a = jnp.exp(z - sp - after - c)
        if diagonal:
            a = jnp.where(lower, a, 0.0)
        return c + jnp.sum(sp, axis=1, keepdims=True), acc + _dot(a.astype(BF16), v)

    q0 = pl.multiple_of(qi * tile, tile)
    state = []
    for hh in range(nh):
        state.extend(visit(hh, q0, jnp.zeros((tile, 1), F32), jnp.zeros((tile, HEAD_DIM), F32), True))

    def smallest_carry(state):
        c = state[0]
        for hh in range(1, nh):
            c = jnp.minimum(c, state[2 * hh])
        return jnp.min(c)

    def live(carry):
        i, c_min, _ = carry
        return (i < qi) & (c_min <= SB_DEAD_LOG)

    def body(carry):
        i, _, state = carry
        ks = pl.multiple_of((qi - 1 - i) * tile, tile)
        out = []
        for hh in range(nh):
            out.extend(visit(hh, ks, state[2 * hh], state[2 * hh + 1], False))
        return i + 1, smallest_carry(out), tuple(out)

    _, _, state = lax.while_loop(live, body, (0, smallest_carry(state), tuple(state)))
    o_ref[0] = jnp.concatenate([state[2 * hh + 1] for hh in range(nh)], axis=1).astype(o_ref.dtype)


def _sb_attention(q, k, v, tile=256):
    b, h, t, dh = q.shape
    hp = 2
    tq = tile
    return pl.pallas_call(
        functools.partial(_sb_kernel, tile=tile),
        grid=(b, h // hp, t // tq),
        in_specs=[
            pl.BlockSpec((1, hp, tq, dh), lambda bi, hi, qi: (bi, hi, qi, 0)),
            pl.BlockSpec((1, hp, t, dh), lambda bi, hi, qi: (bi, hi, 0, 0)),
            pl.BlockSpec((1, hp, t, dh), lambda bi, hi, qi: (bi, hi, 0, 0)),
        ],
        out_specs=pl.BlockSpec((1, tq, hp * dh), lambda bi, hi, qi: (bi, qi, hi)),
        out_shape=jax.ShapeDtypeStruct((b, t, h * dh), BF16),
        compiler_params=_params("parallel", "parallel", "arbitrary"),
        name="sb_attn",
    )(q, k, v)


def _compress_kernel(x_ref, pe_ref, w_ref, o_ref):
    x = x_ref[0, 0]
    nc = x.shape[0]
    w_lo, w_hi = w_ref[0], w_ref[1]
    first = _dot(x, w_lo)
    second = _dot(x, w_hi)
    feat = pe_ref.shape[1]
    pe_lo = jnp.broadcast_to(pe_ref[0:1, :], (8, feat)).astype(BF16)
    pe_hi = jnp.broadcast_to(pe_ref[1:2, :], (8, feat)).astype(BF16)
    bias = _dot(pe_lo, w_lo)[0:1] + _dot(pe_hi, w_hi)[0:1]
    o_ref[0, 0] = (first + pltpu.roll(second, nc - 1, 0) + bias).astype(o_ref.dtype)


def _nsa_compress(z, pe, w):
    b, g, t, dh = z.shape
    nc = t // CMP_STRIDE
    feat = CMP_STRIDE * dh
    x = z.reshape(b, g, nc, feat)
    pe2 = pe.reshape(2, feat)
    w2 = w.reshape(2, feat, dh).astype(BF16)
    return pl.pallas_call(
        _compress_kernel,
        grid=(b, g),
        in_specs=[
            pl.BlockSpec((1, 1, nc, feat), lambda bi, gi: (bi, gi, 0, 0)),
            pl.BlockSpec((2, feat), lambda bi, gi: (0, 0)),
            pl.BlockSpec((2, feat, dh), lambda bi, gi: (0, 0, 0)),
        ],
        out_specs=pl.BlockSpec((1, 1, nc, dh), lambda bi, gi: (bi, gi, 0, 0)),
        out_shape=jax.ShapeDtypeStruct((b, g, nc, dh), BF16),
        compiler_params=_params("parallel", "parallel"),
        name="nsa_compress",
    )(x, pe2, w2)


def _nsa_kernel(slopes_ref, q_ref, kc_ref, vc_ref, ks_ref, vs_ref, kw_ref, vw_ref, gl_ref, pool_ref, ex_ref,
                o_ref, used_ref, *, tq, tk, n_sel, real_blocks):
    grp = pl.program_id(1)
    t0 = pl.program_id(2) * tq
    nc = kc_ref.shape[2]
    rr = NSA_GROUP
    t = t0 + lax.broadcasted_iota(I32, (tq, 1), 0)

    def stack(x):
        return jnp.concatenate([x] * rr, axis=0)

    q = q_ref[0].reshape(rr * tq, HEAD_DIM)
    slope = jnp.concatenate([jnp.full((tq, 1), slopes_ref[grp * rr + r], F32) for r in range(rr)], axis=0)

    def heads_sum(x):
        out = x[0:tq]
        for r in range(1, rr):
            out = out + x[r * tq:(r + 1) * tq]
        return out

    def masked_softmax(s, valid):
        p = jnp.where(valid, jnp.exp(s - jnp.max(s, axis=1, keepdims=True)), 0.0)
        denom = jnp.sum(p, axis=1, keepdims=True)
        return p / jnp.where(denom > 0, denom, 1.0)

    lane = lax.broadcasted_iota(I32, (1, nc), 1)
    dist_c = (t - (lane * CMP_STRIDE + (CMP_LEN - 1))).astype(F32)
    valid_c = stack(dist_c >= 0)
    s = _dot_nt(q, kc_ref[0, 0]) * SCALE - slope * stack(dist_c)
    p = masked_softmax(jnp.where(valid_c, s, NEG_INF), valid_c)
    o_cmp = _dot(p.astype(BF16), vc_ref[0, 0])
    psum = heads_sum(p)

    n_blk = pool_ref.shape[0]
    hi = psum.astype(BF16)
    rest = psum - hi.astype(F32)
    mid = rest.astype(BF16)
    lo = (rest - mid.astype(F32)).astype(BF16)
    pool = pool_ref[...]
    imp = _dot_nt(pool, hi) + _dot_nt(pool, mid) + _dot_nt(pool, lo)
    blk = lax.broadcasted_iota(I32, (n_blk, 1), 0)
    cur = (t0 + lax.broadcasted_iota(I32, (1, tq), 1)) // SEL_BLOCK
    forced = (blk == 0) | (blk == cur) | (blk == cur - 1)
    imp = jnp.where(forced, FORCED_SCORE, jnp.where(blk <= cur, imp, -1.0))
    imp = jnp.where(blk < real_blocks, imp, -jnp.inf)
    blk_f = blk.astype(F32)

    def pick(_, carry):
        imp, sel = carry
        best = jnp.max(imp, axis=0, keepdims=True)
        first = jnp.min(jnp.where(imp == best, blk_f, float(n_blk)), axis=0, keepdims=True)
        hit = blk_f == first
        return jnp.where(hit, -jnp.inf, imp), jnp.where(hit, 1.0, sel)

    _, sel = lax.fori_loop(0, n_sel, pick, (imp, jnp.zeros((n_blk, tq), F32)))
    picked = jnp.max(sel, axis=1, keepdims=True)
    per_tile = tk // SEL_BLOCK
    n_causal = (t0 + tq + tk - 1) // tk
    n_used = jnp.int32(0)
    for j in range(used_ref.shape[0]):
        used_ref[n_used] = j
        hit = (jnp.max(picked[j * per_tile:(j + 1) * per_tile, :]) > 0) & (j < n_causal)
        n_used = n_used + hit.astype(I32)
    sel = sel.T.astype(BF16)

    span = WINDOW + tq
    w0 = pl.multiple_of(jnp.maximum(t0 - WINDOW, 0), tq)
    dist_w = t - (w0 + lax.broadcasted_iota(I32, (1, span), 1))
    valid_w = stack((dist_w >= 0) & (dist_w < WINDOW))
    s = _dot_nt(q, kw_ref[0, 0, pl.ds(w0, span), :]) * SCALE - slope * stack(dist_w.astype(F32))
    p = masked_softmax(jnp.where(valid_w, s, NEG_INF), valid_w)
    o_win = _dot(p.astype(BF16), vw_ref[0, 0, pl.ds(w0, span), :])

    kcol = lax.broadcasted_iota(I32, (1, tk), 1)

    def sel_body(i, carry):
        m, l, acc = carry
        kb = used_ref[i]
        ks0 = pl.multiple_of(kb * tk, tk)
        chosen = _dot(sel, ex_ref[kb]) > 0.5
        dist = t - (ks0 + kcol)
        valid = stack(chosen & (dist >= 0))
        s = _dot_nt(q, ks_ref[0, 0, pl.ds(ks0, tk), :]) * SCALE - slope * stack(dist.astype(F32))
        s = jnp.where(valid, s, NEG_INF)
        m_new = jnp.maximum(m, jnp.max(s, axis=1, keepdims=True))
        alpha = jnp.exp(m - m_new)
        p = jnp.where(valid, jnp.exp(s - m_new), 0.0)
        l = alpha * l + jnp.sum(p, axis=1, keepdims=True)
        acc = alpha * acc + _dot(p.astype(BF16), vs_ref[0, 0, pl.ds(ks0, tk), :])
        return m_new, l, acc

    init = (jnp.full((rr * tq, 1), NEG_INF, F32), jnp.zeros((rr * tq, 1), F32),
            jnp.zeros((rr * tq, HEAD_DIM), F32))
    _, l_sel, acc_sel = lax.fori_loop(0, n_used, sel_body, init)
    o_sel = acc_sel / l_sel

    gates = _sigmoid(gl_ref[0, 0])
    outs = []
    for r in range(rr):
        rows = slice(r * tq, (r + 1) * tq)
        outs.append(gates[:, 3 * r:3 * r + 1] * o_cmp[rows] + gates[:, 3 * r + 1:3 * r + 2] * o_sel[rows]
                    + gates[:, 3 * r + 2:3 * r + 3] * o_win[rows])
    o_ref[0, 0] = jnp.concatenate(outs, axis=1).astype(o_ref.dtype)


def _nsa_attention(q, kc, vc, ks, vs, kw, vw, gate_logits, slopes, tq=128, tk=256):
    b, _, t, dh = q.shape
    g = NSA_KV_GROUPS
    nc = kc.shape[2]
    n_sel = min(N_SELECT, t // SEL_BLOCK)
    real_blocks = t // SEL_BLOCK
    n_blk = LANES
    assert t % tk == 0 and t >= WINDOW + tq and real_blocks <= n_blk
    blk_ids = jnp.arange(n_blk, dtype=I32)
    pool = (blk_ids[:, None] == jnp.arange(nc, dtype=I32)[None, :] // (SEL_BLOCK // CMP_STRIDE)).astype(BF16)
    key_blk = (jnp.arange(t, dtype=I32) // SEL_BLOCK).reshape(t // tk, 1, tk)
    expand = (blk_ids[None, :, None] == key_blk).astype(BF16)
    kv_spec = pl.BlockSpec((1, 1, t, dh), lambda bi, gi, qi: (bi, gi, 0, 0))
    c_spec = pl.BlockSpec((1, 1, nc, dh), lambda bi, gi, qi: (bi, gi, 0, 0))
    return pl.pallas_call(
        functools.partial(_nsa_kernel, tq=tq, tk=tk, n_sel=n_sel, real_blocks=real_blocks),
        grid=(b, g, t // tq),
        in_specs=[
            pl.BlockSpec(memory_space=pltpu.SMEM),
            pl.BlockSpec((1, NSA_GROUP, tq, dh), lambda bi, gi, qi: (bi, gi, qi, 0)),
            c_spec, c_spec, kv_spec, kv_spec, kv_spec, kv_spec,
            pl.BlockSpec((1, 1, tq, LANES), lambda bi, gi, qi: (bi, gi, qi, 0)),
            pl.BlockSpec(pool.shape, lambda bi, gi, qi: (0, 0)),
            pl.BlockSpec(expand.shape, lambda bi, gi, qi: (0, 0, 0)),
        ],
        out_specs=pl.BlockSpec((1, 1, tq, NSA_GROUP * dh), lambda bi, gi, qi: (bi, gi, qi, 0)),
        out_shape=jax.ShapeDtypeStruct((b, g, t, NSA_GROUP * dh), BF16),
        scratch_shapes=[pltpu.SMEM((t // tk,), I32)],
        compiler_params=_params("parallel", "parallel", "arbitrary"),
        name="nsa_attn",
    )(slopes, q, kc, vc, ks, vs, kw, vw, gate_logits, pool, expand)


def _mem_kernel(q_ref, kv_ref, o_ref):
    q = q_ref[0]
    kv = kv_ref[0]
    outs = []
    for h in range(MEM_HEADS):
        sl = slice(h * HEAD_DIM, (h + 1) * HEAD_DIM)
        s = _dot_nt(q[:, sl], kv[:, sl]) * SCALE
        p = jnp.exp(s - jnp.max(s, axis=1, keepdims=True))
        p = p / jnp.sum(p, axis=1, keepdims=True)
        outs.append(_dot(p.astype(BF16), kv[:, MEM_W + h * HEAD_DIM:MEM_W + (h + 1) * HEAD_DIM]))
    o_ref[0] = jnp.concatenate(outs, axis=1).astype(o_ref.dtype)


def _mem_attention(q, mkv, tq=512):
    b, t, w = q.shape
    m = mkv.shape[1]
    return pl.pallas_call(
        _mem_kernel,
        grid=(b, t // tq),
        in_specs=[
            pl.BlockSpec((1, tq, w), lambda bi, qi: (bi, qi, 0)),
            pl.BlockSpec((1, m, 2 * w), lambda bi, qi: (bi, 0, 0)),
        ],
        out_specs=pl.BlockSpec((1, tq, w), lambda bi, qi: (bi, qi, 0)),
        out_shape=jax.ShapeDtypeStruct((b, t, w), BF16),
        compiler_params=_params("parallel", "parallel"),
        name="mem_attn",
    )(q, mkv)


def _merge_kernel(sb_ref, nsa_ref, mem_ref, mg_ref, bm_ref, x_ref, wsb_ref, wnsa_ref, wmem_ref,
                  wout_ref, fg_ref, wq_ref, x1_ref, h2_ref, q_ref):
    d = x_ref.shape[1]
    branches = (_dot(sb_ref[...], wsb_ref[...]), _dot(nsa_ref[...], wnsa_ref[...]),
                _dot(mem_ref[...], wmem_ref[...]))
    merged = jnp.zeros_like(branches[0])
    for j in range(N_BRANCHES):
        gate = _sigmoid(mg_ref[:, j * d:(j + 1) * d] + bm_ref[:, j * d:(j + 1) * d])
        merged = merged + gate * branches[j]
    x1 = x_ref[...] + _dot(merged.astype(BF16), wout_ref[...])
    x1_ref[...] = x1
    h2 = _rms(x1, fg_ref[...])
    h2_ref[...] = h2
    q = _dot(h2.astype(BF16), wq_ref[...]).astype(q_ref.dtype)
    for c in range(q_ref.shape[0]):
        q_ref[c] = q[:, c * LANES:(c + 1) * LANES]


def _merge(sb, nsa, mem, gates_slab, b_merge, x2d, w_sb, w_nsa, w_mem, w_out, ffn_g, w_q, tm=256):
    n, d = x2d.shape
    nq = w_q.shape[1] // LANES
    row = lambda w: pl.BlockSpec((tm, w), lambda i: (i, 0))
    full = lambda a: pl.BlockSpec(a.shape, lambda i: (0,) * a.ndim)
    bm = b_merge.reshape(1, -1)
    fg = ffn_g.reshape(1, d)
    return pl.pallas_call(
        _merge_kernel,
        grid=(n // tm,),
        in_specs=[row(sb.shape[1]), row(nsa.shape[1]), row(mem.shape[1]), row(N_BRANCHES * d),
                  full(bm), row(d), full(w_sb), full(w_nsa), full(w_mem), full(w_out), full(fg),
                  full(w_q)],
        out_specs=[row(d), row(d), pl.BlockSpec((nq, tm, LANES), lambda i: (0, i, 0))],
        out_shape=[jax.ShapeDtypeStruct((n, d), F32), jax.ShapeDtypeStruct((n, d), F32),
                   jax.ShapeDtypeStruct((nq, n, LANES), BF16)],
        compiler_params=_params("parallel"),
        name="merge",
    )(sb, nsa, mem, gates_slab, bm, x2d, w_sb, w_nsa, w_mem, w_out, fg, w_q)


def _peer_topk_kernel(q_ref, sk_ref, idx_ref, gw_ref, tv_ref, ti_ref, bv_ref):
    tt = q_ref.shape[1]
    kk = PEER_TOPK
    nk = PEER_N_KEYS
    rid = lax.broadcasted_iota(I32, (nk, tt), 0).astype(F32)
    for c in range(2 * PEER_HEADS):
        scores = _dot_nt(sk_ref[c], q_ref[c])

        def pick(k, s, c=c):
            best = jnp.max(s, axis=0, keepdims=True)
            first = jnp.min(jnp.where(s == best, rid, float(nk)), axis=0, keepdims=True)
            tv_ref[c, pl.ds(k, 1), :] = best
            ti_ref[c, pl.ds(k, 1), :] = first
            return jnp.where(rid == first, -jnp.inf, s)

        lax.fori_loop(0, kk, pick, scores)

    sub = 8
    widths = [min(kk, -(-(kk // (i + 1)) // sub) * sub) for i in range(kk // 2)]
    n_cand = sum(widths) + kk // 2
    pos = lax.broadcasted_iota(I32, (n_cand, tt), 0).astype(F32)
    for h in range(PEER_HEADS):
        s0, s1 = tv_ref[2 * h], tv_ref[2 * h + 1]
        i0, i1 = ti_ref[2 * h], ti_ref[2 * h + 1]
        cand, cidx = [], []
        for i, wd in enumerate(widths):
            keep = lax.broadcasted_iota(I32, (wd, 1), 0) < kk // (i + 1)
            cand.append(jnp.where(keep, s0[i:i + 1, :] + s1[0:wd, :], -jnp.inf))
            cidx.append(i0[i:i + 1, :] * float(nk) + i1[0:wd, :])
        cand.append(s0[kk // 2:, :] + s1[0:1, :])
        cidx.append(i0[kk // 2:, :] * float(nk) + i1[0:1, :])
        cand = jnp.concatenate(cand, axis=0)
        cidx = jnp.concatenate(cidx, axis=0)

        def pick2(k, s, h=h, cidx=cidx):
            best = jnp.max(s, axis=0, keepdims=True)
            first = jnp.min(jnp.where(s == best, pos, float(n_cand)), axis=0, keepdims=True)
            hit = pos == first
            bv_ref[pl.ds(k, 1), :] = best
            expert = jnp.max(jnp.where(hit, cidx, -1.0), axis=0, keepdims=True)
            idx_ref[pl.ds(h * kk + k, 1), :] = expert.astype(I32)
            return jnp.where(hit, -jnp.inf, s)

        lax.fori_loop(0, kk, pick2, cand)
        best = bv_ref[...]
        e = jnp.exp(best - best[0:1, :])
        gw_ref[h * kk:(h + 1) * kk, :] = e / jnp.sum(e, axis=0, keepdims=True)


def _peer_topk(q_chunks, subkeys, tt=256):
    nchunk, n, half = q_chunks.shape
    slots = PEER_HEADS * PEER_TOPK
    return pl.pallas_call(
        _peer_topk_kernel,
        grid=(n // tt,),
        in_specs=[
            pl.BlockSpec((nchunk, tt, half), lambda i: (0, i, 0)),
            pl.BlockSpec(subkeys.shape, lambda i: (0, 0, 0)),
        ],
        out_specs=[pl.BlockSpec((slots, tt), lambda i: (0, i)),
                   pl.BlockSpec((slots, tt), lambda i: (0, i))],
        out_shape=[jax.ShapeDtypeStruct((slots, n), I32), jax.ShapeDtypeStruct((slots, n), F32)],
        scratch_shapes=[pltpu.VMEM((nchunk, PEER_TOPK, tt), F32),
                        pltpu.VMEM((nchunk, PEER_TOPK, tt), F32),
                        pltpu.VMEM((PEER_TOPK, tt), F32)],
        compiler_params=_params("parallel"),
        name="peer_topk",
    )(q_chunks, subkeys)


def _peer_ffn_kernel(*refs, first_tiles, staged):
    ns = len(staged)
    idx_hbm, tables = refs[0], refs[1:1 + ns]
    io = [refs[1 + ns + 3 * s:4 + ns + 3 * s] for s in range(ns)]
    fg_ref = refs[1 + 4 * ns]
    ys = refs[2 + 4 * ns:2 + 5 * ns]
    idx_smem, bufs = refs[2 + 5 * ns], refs[3 + 5 * ns:3 + 6 * ns]
    idx_sem, sems = refs[3 + 6 * ns], refs[4 + 6 * ns:4 + 7 * ns]
    tt, d = io[0][1].shape
    slots = io[0][0].shape[0]
    rows = d // (2 * LANES)
    tile = pl.program_id(0)
    nbuf = bufs[0].shape[0]
    tok_lane = lax.broadcasted_iota(I32, (1, tt), 1)

    def halves(words):
        return tuple(pltpu.unpack_elementwise(words, index=i, packed_dtype=BF16, unpacked_dtype=F32)
                     for i in range(2))

    def make_stream(s):
        table, buf, row_sem, y_ref = tables[s], bufs[s], sems[s], ys[s]
        gw_ref, h_ref, _ = io[s]

        if staged[s]:
            def start(tok, slot):
                for r in range(buf.shape[1]):
                    pltpu.make_async_copy(table.at[tile * tt + tok, :, r, :], buf.at[slot, r],
                                          row_sem.at[slot]).start(priority=r % 2)
        else:
            idx_copy = pltpu.make_async_copy(idx_hbm.at[tile + first_tiles[s]], idx_smem, idx_sem)
            idx_copy.start()
            idx_copy.wait()

            def start(tok, slot):
                for e in range(slots):
                    pltpu.make_async_copy(table.at[idx_smem[tok * slots + e]], buf.at[slot, :, e, :],
                                          row_sem.at[slot]).start(priority=e % 2)

        def wait(slot):
            pltpu.make_async_copy(buf.at[slot], buf.at[slot], row_sem.at[slot]).wait()

        def expert_weights(tok, slot):
            h = h_ref[pl.ds(tok, 1), :]
            prod = jnp.zeros((slots, LANES), F32)
            for r in range(rows):
                lo, hi = halves(buf[slot, r])
                prod = prod + lo * h[:, r * LANES:(r + 1) * LANES]
                prod = prod + hi * h[:, d // 2 + r * LANES:d // 2 + (r + 1) * LANES]
            a = jnp.sum(prod, axis=1, keepdims=True)
            act = 0.5 * a * (1.0 + lax.erf(a * (2.0 ** -0.5)))
            gate = jnp.sum(jnp.where(tok_lane == tok, gw_ref[...], 0.0), axis=1, keepdims=True)
            return gate * act

        def weighted_values(tok, slot, w):
            parts = [halves(buf[slot, rows + r]) for r in range(rows)]
            y_ref[pl.ds(tok, 1), :] = jnp.concatenate(
                [jnp.sum(parts[r][i] * w, axis=0, keepdims=True) for i in range(2) for r in range(rows)],
                axis=1)

        def step(tok, slot, w, refill):
            nxt = (slot + 1) % nbuf
            wait(nxt)
            w_next = expert_weights(tok + 1, nxt)
            weighted_values(tok, slot, w)
            if refill:
                start(tok + nbuf, slot)
            return w_next

        return start, wait, expert_weights, weighted_values, step

    streams = [make_stream(s) for s in range(ns)]

    ws = []
    for start, wait, expert_weights, _, _ in streams:
        for tok in range(nbuf):
            start(tok, tok)
    for start, wait, expert_weights, _, _ in streams:
        wait(0)
        ws.append(expert_weights(0, 0))

    def round_of_steps(i, ws):
        ws = list(ws)
        for slot in range(nbuf):
            for s in range(ns):
                ws[s] = streams[s][4](i * nbuf + slot, slot, ws[s], True)
        return tuple(ws)

    n_rounds = tt // nbuf - 1
    ws = list(lax.fori_loop(0, n_rounds, round_of_steps, tuple(ws)))
    for slot in range(nbuf - 1):
        for s in range(ns):
            ws[s] = streams[s][4](n_rounds * nbuf + slot, slot, ws[s], False)
    for s in range(ns):
        streams[s][3](tt - 1, nbuf - 1, ws[s])
        ys[s][...] = _rms(io[s][2][...] + ys[s][...], fg_ref[...])


def _pack_tables_kernel(u_ref, v_ref, o_ref):
    d = u_ref.shape[1]
    rows = d // (2 * LANES)
    for base, src in ((0, u_ref), (rows, v_ref)):
        for r in range(rows):
            lo = src[:, r * LANES:(r + 1) * LANES]
            hi = src[:, d // 2 + r * LANES:d // 2 + (r + 1) * LANES]
            o_ref[:, base + r, :] = pltpu.pack_elementwise([lo, hi], packed_dtype=BF16)


def _pack_tables(u, v, te=512):
    n_exp, d = u.shape
    rows = d // LANES
    return pl.pallas_call(
        _pack_tables_kernel,
        grid=(n_exp // te,),
        in_specs=[pl.BlockSpec((te, d), lambda i: (i, 0)), pl.BlockSpec((te, d), lambda i: (i, 0))],
        out_specs=pl.BlockSpec((te, rows, LANES), lambda i: (i, 0, 0)),
        out_shape=jax.ShapeDtypeStruct((n_exp, rows, LANES), jnp.uint32),
        compiler_params=_params("parallel"),
        name="pack_tables",
    )(u, v)


def _sc_gather_slabs(slabs, slab_idx):
    m = slab_idx.shape[0]
    mesh = plsc.VectorSubcoreMesh(core_axis_name="core", subcore_axis_name="subcore")
    idx_rows = jnp.pad(slab_idx.reshape(m // SC_GATHER_WINDOW, SC_GATHER_WINDOW),
                       ((0, 0), (0, LANES - SC_GATHER_WINDOW)))

    @pl.kernel(out_type=jax.ShapeDtypeStruct((m,) + slabs.shape[1:], slabs.dtype), mesh=mesh)
    def gather(slabs_hbm, idx_hbm, out_hbm):
        def window(idx_vmem, out_vmem):
            pltpu.sync_copy(slabs_hbm.at[idx_vmem.at[0, pl.ds(0, SC_GATHER_WINDOW)]], out_vmem)

        pltpu.emit_pipeline(
            window,
            grid=(m // SC_GATHER_WINDOW,),
            in_specs=[pl.BlockSpec((1, LANES), index_map=lambda i: (i, 0))],
            out_specs=[pl.BlockSpec((SC_GATHER_WINDOW,) + slabs.shape[1:], index_map=lambda i: (i, 0, 0))],
            core_axis_name=("core", "subcore"),
            dimension_semantics=(pltpu.PARALLEL,),
            trace_scopes=False,
        )(idx_hbm, out_hbm)

    return gather(slabs, idx_rows)


def _peer_ffn(idx_tiles, tables, gw, h2, x1, final_g, tt, first_tiles, n_tiles, staged):
    _, d = h2.shape
    slots = gw.shape[0]
    rows = d // LANES
    ns = len(staged)
    assert sum(not st for st in staged) <= 1, "one index buffer: at most one gathering stream"
    hbm = pl.BlockSpec(memory_space=pl.ANY)
    io_specs, io_args = [], []
    for s in range(ns):
        io_specs += [pl.BlockSpec((slots, tt), lambda i, f=first_tiles[s]: (0, i + f)),
                     pl.BlockSpec((tt, d), lambda i, f=first_tiles[s]: (i + f, 0)),
                     pl.BlockSpec((tt, d), lambda i, f=first_tiles[s]: (i + f, 0))]
        io_args += [gw, h2, x1]
    buf = pltpu.VMEM((PEER_GATHER_BUFFERS, rows, slots, LANES), tables[0].dtype)
    sem = pltpu.SemaphoreType.DMA((PEER_GATHER_BUFFERS,))
    outs = pl.pallas_call(
        functools.partial(_peer_ffn_kernel, first_tiles=tuple(first_tiles), staged=tuple(staged)),
        grid=(n_tiles,),
        in_specs=[hbm] + [hbm] * ns + io_specs + [pl.BlockSpec((1, d), lambda i: (0, 0))],
        out_specs=[pl.BlockSpec((tt, d), lambda i: (i, 0))] * ns,
        out_shape=[jax.ShapeDtypeStruct((n_tiles * tt, d), F32)] * ns,
        scratch_shapes=[pltpu.SMEM((tt * slots,), I32)] + [buf] * ns + [pltpu.SemaphoreType.DMA] + [sem] * ns,
        compiler_params=_params("arbitrary"),
        name="peer_ffn_" + "_".join("staged" if st else "gather" for st in staged),
    )(idx_tiles, *tables, *io_args, final_g.reshape(1, d))
    return list(outs)


def _mixers(x, mem, mix_g, mem_g, w_in, b_merge, pe_k, pe_v, cw_k, cw_v, w_mem_kv, w_sb_br, w_nsa_br,
            w_mem_br, w_out, ffn_g, peer_w_q, subkeys):
    b, t, d = x.shape
    m = mem.shape[1]
    g = NSA_KV_GROUPS
    n = b * t
    x2d = x.reshape(n, d)

    o_sbq, o_sbk, o_sbv = 0, SB_W, 2 * SB_W
    o_nq = 3 * SB_W
    o_nkv = o_nq + NSA_W
    o_ng = o_nkv + 6 * NSA_KV_W
    o_mq = o_ng + NSA_HEADS * 3
    o_mg = o_mq + MEM_W
    w_act = jnp.concatenate([w_in[:, :o_ng], w_in[:, o_mq:o_mg]], axis=1).astype(BF16)
    gate_pad = LANES - NSA_HEADS * 3
    w_gate = jnp.concatenate([w_in[:, o_mg:], w_in[:, o_ng:o_mq], jnp.zeros((d, gate_pad), w_in.dtype)],
                             axis=1).astype(BF16)
    act = _norm_matmul(x2d, mix_g, w_act, BF16, tm=256, tn=512)
    gates_slab = _norm_matmul(x2d, mix_g, w_gate, F32, tm=256, tn=640)

    def heads(lo, nh):
        return act[:, lo:lo + nh * HEAD_DIM].reshape(b, t, nh, HEAD_DIM).transpose(0, 2, 1, 3)

    sb_out = _sb_attention(heads(o_sbq, SB_HEADS), heads(o_sbk, SB_HEADS), heads(o_sbv, SB_HEADS))

    kv = [heads(o_nkv + j * NSA_KV_W, g) for j in range(6)]
    kc = _nsa_compress(kv[0], pe_k, cw_k)
    vc = _nsa_compress(kv[1], pe_v, cw_v)
    gl = gates_slab[:, N_BRANCHES * d:N_BRANCHES * d + NSA_HEADS * 3]
    gl = gl.reshape(b, t, g, NSA_GROUP * 3).transpose(0, 2, 1, 3)
    gl = jnp.pad(gl, ((0, 0), (0, 0), (0, 0), (0, LANES - NSA_GROUP * 3)))
    slopes = jnp.asarray([2.0 ** (-8.0 * (h + 1) / NSA_HEADS) for h in range(NSA_HEADS)], F32)
    nsa_out = _nsa_attention(heads(o_nq, NSA_HEADS), kc, vc, kv[2], kv[3], kv[4], kv[5], gl, slopes)
    nsa_out = nsa_out.transpose(0, 2, 1, 3).reshape(n, NSA_W)

    mkv = _norm_matmul(mem.reshape(b * m, d), mem_g, w_mem_kv.astype(BF16), BF16, tm=256, tn=512)
    mem_q = act[:, o_nkv + 6 * NSA_KV_W:].reshape(b, t, MEM_W)
    mem_out = _mem_attention(mem_q, mkv.reshape(b, m, 2 * MEM_W))

    x1, h2, q_chunks = _merge(
        sb_out.reshape(n, SB_W), nsa_out, mem_out.reshape(n, MEM_W), gates_slab, b_merge, x2d,
        w_sb_br.astype(BF16), w_nsa_br.astype(BF16), w_mem_br.astype(BF16), w_out.astype(BF16), ffn_g,
        peer_w_q.astype(BF16))

    half = PEER_QUERY_DIM // 2
    idx, gw = _peer_topk(q_chunks, subkeys.reshape(2 * PEER_HEADS, PEER_N_KEYS, half).astype(BF16))
    return x1, h2, idx.T, gw


def _layer(x, mem, *params_and_tables):
    uv, out_g = params_and_tables[-2:]
    b, t, d = x.shape
    tt = 128
    x1, h2, idx_tok, gw = _mixers(x, mem, *params_and_tables[:-2])
    n, slots = idx_tok.shape
    n_tiles = n // tt
    idx_tiles = idx_tok.reshape(n_tiles, tt * slots)
    chunk = SC_CHUNK_TILES
    sc_first = n_tiles - SC_CHUNKS * chunk
    lead = sc_first - SC_CHUNKS * chunk
    assert lead > 0

    def stage(c):
        lo = (sc_first + c * chunk) * tt
        slabs = _sc_gather_slabs(uv, idx_tok[lo:lo + chunk * tt].reshape(-1))
        return slabs.reshape((chunk * tt, slots) + uv.shape[1:])

    staged = [stage(c) for c in range(SC_CHUNKS)]
    ffn = functools.partial(_peer_ffn, idx_tiles, gw=gw, h2=h2, x1=x1, final_g=out_g, tt=tt)
    own = ffn(tables=[uv], first_tiles=[0], n_tiles=lead, staged=[False])
    theirs = []
    for c in range(SC_CHUNKS):
        mine, other = ffn(tables=[uv, staged[c]], first_tiles=[lead + c * chunk, sc_first + c * chunk],
                          n_tiles=chunk, staged=[False, True])
        own.append(mine)
        theirs.append(other)
    return jnp.concatenate(own + theirs, axis=0).reshape(b, t, d)


def kernel(x, mem, mix_norm_g, mem_norm_g, w_in, b_merge, cmp_pe_k, cmp_pe_v, cmp_w_k, cmp_w_v, w_mem_kv, w_sb_br, w_nsa_br, w_mem_br, w_out, ffn_norm_g, peer_w_q, peer_subkeys, peer_u, peer_v, final_norm_g):
    depth = w_in.shape[0]
    assert depth == 1, "the final rmsnorm is fused into the last layer's PEER kernel"
    l = 0
    uv = _pack_tables(peer_u[l], peer_v[l])
    return _layer(x, mem, mix_norm_g[l], mem_norm_g[l], w_in[l], b_merge[l], cmp_pe_k[l], cmp_pe_v[l],
                  cmp_w_k[l], cmp_w_v[l], w_mem_kv[l], w_sb_br[l], w_nsa_br[l], w_mem_br[l], w_out[l],
                  ffn_norm_g[l], peer_w_q[l], peer_subkeys[l], uv, final_norm_g)
```

```python
import functools
import math

import jax
import jax.numpy as jnp
from jax import lax
from jax.experimental import pallas as pl
from jax.experimental.pallas import tpu as pltpu
from jax.experimental.pallas import tpu_sc as plsc

F32 = jnp.float32
BF16 = jnp.bfloat16
I32 = jnp.int32

HEAD_DIM = 64
SB_HEADS = 6
NSA_HEADS = 6
NSA_KV_GROUPS = 2
NSA_GROUP = NSA_HEADS // NSA_KV_GROUPS
MEM_HEADS = 4
N_BRANCHES = 3
SB_W = SB_HEADS * HEAD_DIM
NSA_W = NSA_HEADS * HEAD_DIM
NSA_KV_W = NSA_KV_GROUPS * HEAD_DIM
MEM_W = MEM_HEADS * HEAD_DIM
CMP_LEN = 32
CMP_STRIDE = 16
SEL_BLOCK = 64
N_SELECT = 16
WINDOW = 512
FORCED_SCORE = 1e4
PEER_HEADS = 8
PEER_N_KEYS = 128
PEER_QUERY_DIM = 256
PEER_TOPK = 16
RMS_EPS = 1e-6
NEG_INF = -1e30
SCALE = HEAD_DIM ** -0.5
SB_DEAD_LOG = 104.0

LANES = 128
PEER_GATHER_BUFFERS = 8
SC_GATHER_WINDOW = 32
SC_CHUNKS = 4
SC_CHUNK_TILES = 28
VMEM_LIMIT_BYTES = 56 * 1024 * 1024

_NT = (((1,), (1,)), ((), ()))


def _params(*sem):
    return pltpu.CompilerParams(dimension_semantics=sem, vmem_limit_bytes=VMEM_LIMIT_BYTES)


def _dot(a, b):
    return jnp.dot(a, b, preferred_element_type=F32)


def _dot_nt(a, b):
    return lax.dot_general(a, b, _NT, preferred_element_type=F32)


def _sigmoid(x):
    return 1.0 / (1.0 + jnp.exp(-x))


def _rms(x, g):
    return x * lax.rsqrt(jnp.mean(x * x, axis=-1, keepdims=True) + RMS_EPS) * g


def _norm_matmul_kernel(x_ref, g_ref, w_ref, o_ref, *, tn):
    h = _rms(x_ref[...], g_ref[...]).astype(BF16)
    for c in range(0, o_ref.shape[1], tn):
        o_ref[:, c:c + tn] = _dot(h, w_ref[:, c:c + tn]).astype(o_ref.dtype)


def _norm_matmul(x2d, g, w, out_dtype, tm, tn):
    n, d = x2d.shape
    m = w.shape[1]
    return pl.pallas_call(
        functools.partial(_norm_matmul_kernel, tn=tn),
        grid=(n // tm,),
        in_specs=[
            pl.BlockSpec((tm, d), lambda i: (i, 0)),
            pl.BlockSpec((1, d), lambda i: (0, 0)),
            pl.BlockSpec((d, m), lambda i: (0, 0)),
        ],
        out_specs=pl.BlockSpec((tm, m), lambda i: (i, 0)),
        out_shape=jax.ShapeDtypeStruct((n, m), out_dtype),
        compiler_params=_params("parallel"),
        name="norm_matmul",
    )(x2d, g.reshape(1, d), w)


def _sb_kernel(q_ref, k_ref, v_ref, o_ref, *, tile):
    qi = pl.program_id(2)
    nh = q_ref.shape[1]
    row = lax.broadcasted_iota(I32, (tile, tile), 0)
    col = lax.broadcasted_iota(I32, (tile, tile), 1)
    lower = row > col
    later = lower.astype(BF16)
    qs = [q_ref[0, hh] for hh in range(nh)]

    def visit(hh, ks, c, acc, diagonal):
        k = k_ref[0, hh, pl.ds(ks, tile), :]
        v = v_ref[0, hh, pl.ds(ks, tile), :]
        z = _dot_nt(qs[hh], k) * SCALE
        sp = jnp.maximum(z, 0.0) + jnp.log(1.0 + jnp.exp(-jnp.abs(z)))
        if diagonal:
            sp = jnp.where(lower, sp, 0.0)
        hi = sp.astype(BF16)
        lo = (sp - hi.astype(F32)).astype(BF16)
        after = _dot(hi, later) + _dot(lo, later)
        a = jnp.exp(z - sp - after - c)
        if diagonal:
            a = jnp.where(lower, a, 0.0)
        return c + jnp.sum(sp, axis=1, keepdims=True), acc + _dot(a.astype(BF16), v)

    q0 = pl.multiple_of(qi * tile, tile)
    state = []
    for hh in range(nh):
        state.extend(visit(hh, q0, jnp.zeros((tile, 1), F32), jnp.zeros((tile, HEAD_DIM), F32), True))

    def smallest_carry(state):
        c = state[0]
        for hh in range(1, nh):
            c = jnp.minimum(c, state[2 * hh])
        return jnp.min(c)

    def live(carry):
        i, c_min, _ = carry
        return (i < qi) & (c_min <= SB_DEAD_LOG)

    def body(carry):
        i, _, state = carry
        ks = pl.multiple_of((qi - 1 - i) * tile, tile)
        out = []
        for hh in range(nh):
            out.extend(visit(hh, ks, state[2 * hh], state[2 * hh + 1], False))
        return i + 1, smallest_carry(out), tuple(out)

    _, _, state = lax.while_loop(live, body, (0, smallest_carry(state), tuple(state)))
    o_ref[0] = jnp.concatenate([state[2 * hh + 1] for hh in range(nh)], axis=1).astype(o_ref.dtype)


def _sb_attention(q, k, v, tile=256):
    b, h, t, dh = q.shape
    hp = 2
    tq = tile
    return pl.pallas_call(
        functools.partial(_sb_kernel, tile=tile),
        grid=(b, h // hp, t // tq),
        in_specs=[
            pl.BlockSpec((1, hp, tq, dh), lambda bi, hi, qi: (bi, hi, qi, 0)),
            pl.BlockSpec((1, hp, t, dh), lambda bi, hi, qi: (bi, hi, 0, 0)),
            pl.BlockSpec((1, hp, t, dh), lambda bi, hi, qi: (bi, hi, 0, 0)),
        ],
        out_specs=pl.BlockSpec((1, tq, hp * dh), lambda bi, hi, qi: (bi, qi, hi)),
        out_shape=jax.ShapeDtypeStruct((b, t, h * dh), BF16),
        compiler_params=_params("parallel", "parallel", "arbitrary"),
        name="sb_attn",
    )(q, k, v)


def _compress_kernel(x_ref, pe_ref, w_ref, o_ref):
    x = x_ref[0, 0]
    nc = x.shape[0]
    w_lo, w_hi = w_ref[0], w_ref[1]
    first = _dot(x, w_lo)
    second = _dot(x, w_hi)
    feat = pe_ref.shape[1]
    pe_lo = jnp.broadcast_to(pe_ref[0:1, :], (8, feat)).astype(BF16)
    pe_hi = jnp.broadcast_to(pe_ref[1:2, :], (8, feat)).astype(BF16)
    bias = _dot(pe_lo, w_lo)[0:1] + _dot(pe_hi, w_hi)[0:1]
    o_ref[0, 0] = (first + pltpu.roll(second, nc - 1, 0) + bias).astype(o_ref.dtype)


def _nsa_compress(z, pe, w):
    b, g, t, dh = z.shape
    nc = t // CMP_STRIDE
    feat = CMP_STRIDE * dh
    x = z.reshape(b, g, nc, feat)
    pe2 = pe.reshape(2, feat)
    w2 = w.reshape(2, feat, dh).astype(BF16)
    return pl.pallas_call(
        _compress_kernel,
        grid=(b, g),
        in_specs=[
            pl.BlockSpec((1, 1, nc, feat), lambda bi, gi: (bi, gi, 0, 0)),
            pl.BlockSpec((2, feat), lambda bi, gi: (0, 0)),
            pl.BlockSpec((2, feat, dh), lambda bi, gi: (0, 0, 0)),
        ],
        out_specs=pl.BlockSpec((1, 1, nc, dh), lambda bi, gi: (bi, gi, 0, 0)),
        out_shape=jax.ShapeDtypeStruct((b, g, nc, dh), BF16),
        compiler_params=_params("parallel", "parallel"),
        name="nsa_compress",
    )(x, pe2, w2)


def _nsa_kernel(slopes_ref, q_ref, kc_ref, vc_ref, ks_ref, vs_ref, kw_ref, vw_ref, gl_ref, pool_ref, ex_ref,
                o_ref, used_ref, *, tq, tk, n_sel, real_blocks):
    grp = pl.program_id(1)
    t0 = pl.program_id(2) * tq
    nc = kc_ref.shape[2]
    rr = NSA_GROUP
    t = t0 + lax.broadcasted_iota(I32, (tq, 1), 0)

    def stack(x):
        return jnp.concatenate([x] * rr, axis=0)

    q = q_ref[0].reshape(rr * tq, HEAD_DIM)
    slope = jnp.concatenate([jnp.full((tq, 1), slopes_ref[grp * rr + r], F32) for r in range(rr)], axis=0)

    def heads_sum(x):
        out = x[0:tq]
        for r in range(1, rr):
            out = out + x[r * tq:(r + 1) * tq]
        return out

    def masked_softmax(s, valid):
        p = jnp.where(valid, jnp.exp(s - jnp.max(s, axis=1, keepdims=True)), 0.0)
        denom = jnp.sum(p, axis=1, keepdims=True)
        return p / jnp.where(denom > 0, denom, 1.0)

    lane = lax.broadcasted_iota(I32, (1, nc), 1)
    dist_c = (t - (lane * CMP_STRIDE + (CMP_LEN - 1))).astype(F32)
    valid_c = stack(dist_c >= 0)
    s = _dot_nt(q, kc_ref[0, 0]) * SCALE - slope * stack(dist_c)
    p = masked_softmax(jnp.where(valid_c, s, NEG_INF), valid_c)
    o_cmp = _dot(p.astype(BF16), vc_ref[0, 0])
    psum = heads_sum(p)

    n_blk = pool_ref.shape[0]
    hi = psum.astype(BF16)
    rest = psum - hi.astype(F32)
    mid = rest.astype(BF16)
    lo = (rest - mid.astype(F32)).astype(BF16)
    pool = pool_ref[...]
    imp = _dot_nt(pool, hi) + _dot_nt(pool, mid) + _dot_nt(pool, lo)
    blk = lax.broadcasted_iota(I32, (n_blk, 1), 0)
    cur = (t0 + lax.broadcasted_iota(I32, (1, tq), 1)) // SEL_BLOCK
    forced = (blk == 0) | (blk == cur) | (blk == cur - 1)
    imp = jnp.where(forced, FORCED_SCORE, jnp.where(blk <= cur, imp, -1.0))
    imp = jnp.where(blk < real_blocks, imp, -jnp.inf)
    blk_f = blk.astype(F32)

    def pick(_, carry):
        imp, sel = carry
        best = jnp.max(imp, axis=0, keepdims=True)
        first = jnp.min(jnp.where(imp == best, blk_f, float(n_blk)), axis=0, keepdims=True)
        hit = blk_f == first
        return jnp.where(hit, -jnp.inf, imp), jnp.where(hit, 1.0, sel)

    _, sel = lax.fori_loop(0, n_sel, pick, (imp, jnp.zeros((n_blk, tq), F32)))
    picked = jnp.max(sel, axis=1, keepdims=True)
    per_tile = tk // SEL_BLOCK
    n_causal = (t0 + tq + tk - 1) // tk
    n_used = jnp.int32(0)
    for j in range(used_ref.shape[0]):
        used_ref[n_used] = j
        hit = (jnp.max(picked[j * per_tile:(j + 1) * per_tile, :]) > 0) & (j < n_causal)
        n_used = n_used + hit.astype(I32)
    sel = sel.T.astype(BF16)

    span = WINDOW + tq
    w0 = pl.multiple_of(jnp.maximum(t0 - WINDOW, 0), tq)
    dist_w = t - (w0 + lax.broadcasted_iota(I32, (1, span), 1))
    valid_w = stack((dist_w >= 0) & (dist_w < WINDOW))
    s = _dot_nt(q, kw_ref[0, 0, pl.ds(w0, span), :]) * SCALE - slope * stack(dist_w.astype(F32))
    p = masked_softmax(jnp.where(valid_w, s, NEG_INF), valid_w)
    o_win = _dot(p.astype(BF16), vw_ref[0, 0, pl.ds(w0, span), :])

    kcol = lax.broadcasted_iota(I32, (1, tk), 1)

    def sel_body(i, carry):
        m, l, acc = carry
        kb = used_ref[i]
        ks0 = pl.multiple_of(kb * tk, tk)
        chosen = _dot(sel, ex_ref[kb]) > 0.5
        dist = t - (ks0 + kcol)
        valid = stack(chosen & (dist >= 0))
        s = _dot_nt(q, ks_ref[0, 0, pl.ds(ks0, tk), :]) * SCALE - slope * stack(dist.astype(F32))
        s = jnp.where(valid, s, NEG_INF)
        m_new = jnp.maximum(m, jnp.max(s, axis=1, keepdims=True))
        alpha = jnp.exp(m - m_new)
        p = jnp.where(valid, jnp.exp(s - m_new), 0.0)
        l = alpha * l + jnp.sum(p, axis=1, keepdims=True)
        acc = alpha * acc + _dot(p.astype(BF16), vs_ref[0, 0, pl.ds(ks0, tk), :])
        return m_new, l, acc

    init = (jnp.full((rr * tq, 1), NEG_INF, F32), jnp.zeros((rr * tq, 1), F32),
            jnp.zeros((rr * tq, HEAD_DIM), F32))
    _, l_sel, acc_sel = lax.fori_loop(0, n_used, sel_body, init)
    o_sel = acc_sel / l_sel

    gates = _sigmoid(gl_ref[0, 0])
    outs = []
    for r in range(rr):
        rows = slice(r * tq, (r + 1) * tq)
        outs.append(gates[:, 3 * r:3 * r + 1] * o_cmp[rows] + gates[:, 3 * r + 1:3 * r + 2] * o_sel[rows]
                    + gates[:, 3 * r + 2:3 * r + 3] * o_win[rows])
    o_ref[0, 0] = jnp.concatenate(outs, axis=1).astype(o_ref.dtype)


def _nsa_attention(q, kc, vc, ks, vs, kw, vw, gate_logits, slopes, tq=256, tk=256):
    b, _, t, dh = q.shape
    g = NSA_KV_GROUPS
    nc = kc.shape[2]
    n_sel = min(N_SELECT, t // SEL_BLOCK)
    real_blocks = t // SEL_BLOCK
    n_blk = LANES
    assert t % tk == 0 and t >= WINDOW + tq and real_blocks <= n_blk
    blk_ids = jnp.arange(n_blk, dtype=I32)
    pool = (blk_ids[:, None] == jnp.arange(nc, dtype=I32)[None, :] // (SEL_BLOCK // CMP_STRIDE)).astype(BF16)
    key_blk = (jnp.arange(t, dtype=I32) // SEL_BLOCK).reshape(t // tk, 1, tk)
    expand = (blk_ids[None, :, None] == key_blk).astype(BF16)
    kv_spec = pl.BlockSpec((1, 1, t, dh), lambda bi, gi, qi: (bi, gi, 0, 0))
    c_spec = pl.BlockSpec((1, 1, nc, dh), lambda bi, gi, qi: (bi, gi, 0, 0))
    return pl.pallas_call(
        functools.partial(_nsa_kernel, tq=tq, tk=tk, n_sel=n_sel, real_blocks=real_blocks),
        grid=(b, g, t // tq),
        in_specs=[
            pl.BlockSpec(memory_space=pltpu.SMEM),
            pl.BlockSpec((1, NSA_GROUP, tq, dh), lambda bi, gi, qi: (bi, gi, qi, 0)),
            c_spec, c_spec, kv_spec, kv_spec, kv_spec, kv_spec,
            pl.BlockSpec((1, 1, tq, LANES), lambda bi, gi, qi: (bi, gi, qi, 0)),
            pl.BlockSpec(pool.shape, lambda bi, gi, qi: (0, 0)),
            pl.BlockSpec(expand.shape, lambda bi, gi, qi: (0, 0, 0)),
        ],
        out_specs=pl.BlockSpec((1, 1, tq, NSA_GROUP * dh), lambda bi, gi, qi: (bi, gi, qi, 0)),
        out_shape=jax.ShapeDtypeStruct((b, g, t, NSA_GROUP * dh), BF16),
        scratch_shapes=[pltpu.SMEM((t // tk,), I32)],
        compiler_params=_params("parallel", "parallel", "arbitrary"),
        name="nsa_attn",
    )(slopes, q, kc, vc, ks, vs, kw, vw, gate_logits, pool, expand)


def _mem_kernel(q_ref, kv_ref, o_ref):
    q = q_ref[0]
    kv = kv_ref[0]
    outs = []
    for h in range(MEM_HEADS):
        sl = slice(h * HEAD_DIM, (h + 1) * HEAD_DIM)
        s = _dot_nt(q[:, sl], kv[:, sl]) * SCALE
        p = jnp.exp(s - jnp.max(s, axis=1, keepdims=True))
        p = p / jnp.sum(p, axis=1, keepdims=True)
        outs.append(_dot(p.astype(BF16), kv[:, MEM_W + h * HEAD_DIM:MEM_W + (h + 1) * HEAD_DIM]))
    o_ref[0] = jnp.concatenate(outs, axis=1).astype(o_ref.dtype)


def _mem_attention(q, mkv, tq=512):
    b, t, w = q.shape
    m = mkv.shape[1]
    return pl.pallas_call(
        _mem_kernel,
        grid=(b, t // tq),
        in_specs=[
            pl.BlockSpec((1, tq, w), lambda bi, qi: (bi, qi, 0)),
            pl.BlockSpec((1, m, 2 * w), lambda bi, qi: (bi, 0, 0)),
        ],
        out_specs=pl.BlockSpec((1, tq, w), lambda bi, qi: (bi, qi, 0)),
        out_shape=jax.ShapeDtypeStruct((b, t, w), BF16),
        compiler_params=_params("parallel", "parallel"),
        name="mem_attn",
    )(q, mkv)


def _merge_kernel(sb_ref, nsa_ref, mem_ref, mg_ref, bm_ref, x_ref, wsb_ref, wnsa_ref, wmem_ref,
                  wout_ref, fg_ref, wq_ref, x1_ref, h2_ref, q_ref):
    d = x_ref.shape[1]
    branches = (_dot(sb_ref[...], wsb_ref[...]), _dot(nsa_ref[...], wnsa_ref[...]),
                _dot(mem_ref[...], wmem_ref[...]))
    merged = jnp.zeros_like(branches[0])
    for j in range(N_BRANCHES):
        gate = _sigmoid(mg_ref[:, j * d:(j + 1) * d] + bm_ref[:, j * d:(j + 1) * d])
        merged = merged + gate * branches[j]
    x1 = x_ref[...] + _dot(merged.astype(BF16), wout_ref[...])
    x1_ref[...] = x1
    h2 = _rms(x1, fg_ref[...])
    h2_ref[...] = h2
    q = _dot(h2.astype(BF16), wq_ref[...]).astype(q_ref.dtype)
    for c in range(q_ref.shape[0]):
        q_ref[c] = q[:, c * LANES:(c + 1) * LANES]


def _merge(sb, nsa, mem, gates_slab, b_merge, x2d, w_sb, w_nsa, w_mem, w_out, ffn_g, w_q, tm=256):
    n, d = x2d.shape
    nq = w_q.shape[1] // LANES
    row = lambda w: pl.BlockSpec((tm, w), lambda i: (i, 0))
    full = lambda a: pl.BlockSpec(a.shape, lambda i: (0,) * a.ndim)
    bm = b_merge.reshape(1, -1)
    fg = ffn_g.reshape(1, d)
    return pl.pallas_call(
        _merge_kernel,
        grid=(n // tm,),
        in_specs=[row(sb.shape[1]), row(nsa.shape[1]), row(mem.shape[1]), row(N_BRANCHES * d),
                  full(bm), row(d), full(w_sb), full(w_nsa), full(w_mem), full(w_out), full(fg),
                  full(w_q)],
        out_specs=[row(d), row(d), pl.BlockSpec((nq, tm, LANES), lambda i: (0, i, 0))],
        out_shape=[jax.ShapeDtypeStruct((n, d), F32), jax.ShapeDtypeStruct((n, d), F32),
                   jax.ShapeDtypeStruct((nq, n, LANES), BF16)],
        compiler_params=_params("parallel"),
        name="merge",
    )(sb, nsa, mem, gates_slab, bm, x2d, w_sb, w_nsa, w_mem, w_out, fg, w_q)


def _peer_topk_kernel(q_ref, sk_ref, idx_ref, gw_ref, tv_ref, ti_ref, bv_ref):
    tt = q_ref.shape[1]
    kk = PEER_TOPK
    nk = PEER_N_KEYS
    rid = lax.broadcasted_iota(I32, (nk, tt), 0).astype(F32)
    for c in range(2 * PEER_HEADS):
        scores = _dot_nt(sk_ref[c], q_ref[c])

        def pick(k, s, c=c):
            best = jnp.max(s, axis=0, keepdims=True)
            first = jnp.min(jnp.where(s == best, rid, float(nk)), axis=0, keepdims=True)
            tv_ref[c, pl.ds(k, 1), :] = best
            ti_ref[c, pl.ds(k, 1), :] = first
            return jnp.where(rid == first, -jnp.inf, s)

        lax.fori_loop(0, kk, pick, scores)

    sub = 8
    widths = [min(kk, -(-(kk // (i + 1)) // sub) * sub) for i in range(kk // 2)]
    n_cand = sum(widths) + kk // 2
    pos = lax.broadcasted_iota(I32, (n_cand, tt), 0).astype(F32)
    for h in range(PEER_HEADS):
        s0, s1 = tv_ref[2 * h], tv_ref[2 * h + 1]
        i0, i1 = ti_ref[2 * h], ti_ref[2 * h + 1]
        cand, cidx = [], []
        for i, wd in enumerate(widths):
            keep = lax.broadcasted_iota(I32, (wd, 1), 0) < kk // (i + 1)
            cand.append(jnp.where(keep, s0[i:i + 1, :] + s1[0:wd, :], -jnp.inf))
            cidx.append(i0[i:i + 1, :] * float(nk) + i1[0:wd, :])
        cand.append(s0[kk // 2:, :] + s1[0:1, :])
        cidx.append(i0[kk // 2:, :] * float(nk) + i1[0:1, :])
        cand = jnp.concatenate(cand, axis=0)
        cidx = jnp.concatenate(cidx, axis=0)

        def pick2(k, s, h=h, cidx=cidx):
            best = jnp.max(s, axis=0, keepdims=True)
            first = jnp.min(jnp.where(s == best, pos, float(n_cand)), axis=0, keepdims=True)
            hit = pos == first
            bv_ref[pl.ds(k, 1), :] = best
            expert = jnp.max(jnp.where(hit, cidx, -1.0), axis=0, keepdims=True)
            idx_ref[pl.ds(h * kk + k, 1), :] = expert.astype(I32)
            return jnp.where(hit, -jnp.inf, s)

        lax.fori_loop(0, kk, pick2, cand)
        best = bv_ref[...]
        e = jnp.exp(best - best[0:1, :])
        gw_ref[h * kk:(h + 1) * kk, :] = e / jnp.sum(e, axis=0, keepdims=True)


def _peer_topk(q_chunks, subkeys, tt=256):
    nchunk, n, half = q_chunks.shape
    slots = PEER_HEADS * PEER_TOPK
    return pl.pallas_call(
        _peer_topk_kernel,
        grid=(n // tt,),
        in_specs=[
            pl.BlockSpec((nchunk, tt, half), lambda i: (0, i, 0)),
            pl.BlockSpec(subkeys.shape, lambda i: (0, 0, 0)),
        ],
        out_specs=[pl.BlockSpec((slots, tt), lambda i: (0, i)),
                   pl.BlockSpec((slots, tt), lambda i: (0, i))],
        out_shape=[jax.ShapeDtypeStruct((slots, n), I32), jax.ShapeDtypeStruct((slots, n), F32)],
        scratch_shapes=[pltpu.VMEM((nchunk, PEER_TOPK, tt), F32),
                        pltpu.VMEM((nchunk, PEER_TOPK, tt), F32),
                        pltpu.VMEM((PEER_TOPK, tt), F32)],
        compiler_params=_params("parallel"),
        name="peer_topk",
    )(q_chunks, subkeys)


def _peer_ffn_kernel(*refs, first_tiles, staged):
    ns = len(staged)
    idx_hbm, tables = refs[0], refs[1:1 + ns]
    io = [refs[1 + ns + 3 * s:4 + ns + 3 * s] for s in range(ns)]
    fg_ref = refs[1 + 4 * ns]
    ys = refs[2 + 4 * ns:2 + 5 * ns]
    idx_smem, bufs = refs[2 + 5 * ns], refs[3 + 5 * ns:3 + 6 * ns]
    idx_sem, sems = refs[3 + 6 * ns], refs[4 + 6 * ns:4 + 7 * ns]
    tt, d = io[0][1].shape
    slots = io[0][0].shape[0]
    rows = d // (2 * LANES)
    tile = pl.program_id(0)
    nbuf = bufs[0].shape[0]
    tok_lane = lax.broadcasted_iota(I32, (1, tt), 1)

    def halves(words):
        return tuple(pltpu.unpack_elementwise(words, index=i, packed_dtype=BF16, unpacked_dtype=F32)
                     for i in range(2))

    def make_stream(s):
        table, buf, row_sem, y_ref = tables[s], bufs[s], sems[s], ys[s]
        gw_ref, h_ref, _ = io[s]

        if staged[s]:
            def start(tok, slot):
                for r in range(buf.shape[1]):
                    pltpu.make_async_copy(table.at[tile * tt + tok, :, r, :], buf.at[slot, r],
                                          row_sem.at[slot]).start(priority=r % 2)
        else:
            idx_copy = pltpu.make_async_copy(idx_hbm.at[tile + first_tiles[s]], idx_smem, idx_sem)
            idx_copy.start()
            idx_copy.wait()

            def start(tok, slot):
                for e in range(slots):
                    pltpu.make_async_copy(table.at[idx_smem[tok * slots + e]], buf.at[slot, :, e, :],
                                          row_sem.at[slot]).start(priority=e % 2)

        def wait(slot):
            pltpu.make_async_copy(buf.at[slot], buf.at[slot], row_sem.at[slot]).wait()

        def expert_weights(tok, slot):
            h = h_ref[pl.ds(tok, 1), :]
            prod = jnp.zeros((slots, LANES), F32)
            for r in range(rows):
                lo, hi = halves(buf[slot, r])
                prod = prod + lo * h[:, r * LANES:(r + 1) * LANES]
                prod = prod + hi * h[:, d // 2 + r * LANES:d // 2 + (r + 1) * LANES]
            a = jnp.sum(prod, axis=1, keepdims=True)
            act = 0.5 * a * (1.0 + lax.erf(a * (2.0 ** -0.5)))
            gate = jnp.sum(jnp.where(tok_lane == tok, gw_ref[...], 0.0), axis=1, keepdims=True)
            return gate * act

        def weighted_values(tok, slot, w):
            parts = [halves(buf[slot, rows + r]) for r in range(rows)]
            y_ref[pl.ds(tok, 1), :] = jnp.concatenate(
                [jnp.sum(parts[r][i] * w, axis=0, keepdims=True) for i in range(2) for r in range(rows)],
                axis=1)

        def step(tok, slot, w, refill):
            nxt = (slot + 1) % nbuf
            wait(nxt)
            w_next = expert_weights(tok + 1, nxt)
            weighted_values(tok, slot, w)
            if refill:
                start(tok + nbuf, slot)
            return w_next

        return start, wait, expert_weights, weighted_values, step

    streams = [make_stream(s) for s in range(ns)]

    ws = []
    for start, wait, expert_weights, _, _ in streams:
        for tok in range(nbuf):
            start(tok, tok)
    for start, wait, expert_weights, _, _ in streams:
        wait(0)
        ws.append(expert_weights(0, 0))

    def round_of_steps(i, ws):
        ws = list(ws)
        for slot in range(nbuf):
            for s in range(ns):
                ws[s] = streams[s][4](i * nbuf + slot, slot, ws[s], True)
        return tuple(ws)

    n_rounds = tt // nbuf - 1
    ws = list(lax.fori_loop(0, n_rounds, round_of_steps, tuple(ws)))
    for slot in range(nbuf - 1):
        for s in range(ns):
            ws[s] = streams[s][4](n_rounds * nbuf + slot, slot, ws[s], False)
    for s in range(ns):
        streams[s][3](tt - 1, nbuf - 1, ws[s])
        ys[s][...] = _rms(io[s][2][...] + ys[s][...], fg_ref[...])


def _pack_tables_kernel(u_ref, v_ref, o_ref):
    d = u_ref.shape[1]
    rows = d // (2 * LANES)
    for base, src in ((0, u_ref), (rows, v_ref)):
        for r in range(rows):
            lo = src[:, r * LANES:(r + 1) * LANES]
            hi = src[:, d // 2 + r * LANES:d // 2 + (r + 1) * LANES]
            o_ref[:, base + r, :] = pltpu.pack_elementwise([lo, hi], packed_dtype=BF16)


def _pack_tables(u, v, te=512):
    n_exp, d = u.shape
    rows = d // LANES
    return pl.pallas_call(
        _pack_tables_kernel,
        grid=(n_exp // te,),
        in_specs=[pl.BlockSpec((te, d), lambda i: (i, 0)), pl.BlockSpec((te, d), lambda i: (i, 0))],
        out_specs=pl.BlockSpec((te, rows, LANES), lambda i: (i, 0, 0)),
        out_shape=jax.ShapeDtypeStruct((n_exp, rows, LANES), jnp.uint32),
        compiler_params=_params("parallel"),
        name="pack_tables",
    )(u, v)


def _sc_gather_slabs(slabs, slab_idx):
    m = slab_idx.shape[0]
    mesh = plsc.VectorSubcoreMesh(core_axis_name="core", subcore_axis_name="subcore")
    idx_rows = jnp.pad(slab_idx.reshape(m // SC_GATHER_WINDOW, SC_GATHER_WINDOW),
                       ((0, 0), (0, LANES - SC_GATHER_WINDOW)))

    @pl.kernel(out_type=jax.ShapeDtypeStruct((m,) + slabs.shape[1:], slabs.dtype), mesh=mesh)
    def gather(slabs_hbm, idx_hbm, out_hbm):
        def window(idx_vmem, out_vmem):
            pltpu.sync_copy(slabs_hbm.at[idx_vmem.at[0, pl.ds(0, SC_GATHER_WINDOW)]], out_vmem)

        pltpu.emit_pipeline(
            window,
            grid=(m // SC_GATHER_WINDOW,),
            in_specs=[pl.BlockSpec((1, LANES), index_map=lambda i: (i, 0))],
            out_specs=[pl.BlockSpec((SC_GATHER_WINDOW,) + slabs.shape[1:], index_map=lambda i: (i, 0, 0))],
            core_axis_name=("core", "subcore"),
            dimension_semantics=(pltpu.PARALLEL,),
            trace_scopes=False,
        )(idx_hbm, out_hbm)

    return gather(slabs, idx_rows)


def _peer_ffn(idx_tiles, tables, gw, h2, x1, final_g, tt, first_tiles, n_tiles, staged):
    _, d = h2.shape
    slots = gw.shape[0]
    rows = d // LANES
    ns = len(staged)
    assert sum(not st for st in staged) <= 1, "one index buffer: at most one gathering stream"
    hbm = pl.BlockSpec(memory_space=pl.ANY)
    io_specs, io_args = [], []
    for s in range(ns):
        io_specs += [pl.BlockSpec((slots, tt), lambda i, f=first_tiles[s]: (0, i + f)),
                     pl.BlockSpec((tt, d), lambda i, f=first_tiles[s]: (i + f, 0)),
                     pl.BlockSpec((tt, d), lambda i, f=first_tiles[s]: (i + f, 0))]
        io_args += [gw, h2, x1]
    buf = pltpu.VMEM((PEER_GATHER_BUFFERS, rows, slots, LANES), tables[0].dtype)
    sem = pltpu.SemaphoreType.DMA((PEER_GATHER_BUFFERS,))
    outs = pl.pallas_call(
        functools.partial(_peer_ffn_kernel, first_tiles=tuple(first_tiles), staged=tuple(staged)),
        grid=(n_tiles,),
        in_specs=[hbm] + [hbm] * ns + io_specs + [pl.BlockSpec((1, d), lambda i: (0, 0))],
        out_specs=[pl.BlockSpec((tt, d), lambda i: (i, 0))] * ns,
        out_shape=[jax.ShapeDtypeStruct((n_tiles * tt, d), F32)] * ns,
        scratch_shapes=[pltpu.SMEM((tt * slots,), I32)] + [buf] * ns + [pltpu.SemaphoreType.DMA] + [sem] * ns,
        compiler_params=_params("arbitrary"),
        name="peer_ffn_" + "_".join("staged" if st else "gather" for st in staged),
    )(idx_tiles, *tables, *io_args, final_g.reshape(1, d))
    return list(outs)


def _mixers(x, mem, mix_g, mem_g, w_in, b_merge, pe_k, pe_v, cw_k, cw_v, w_mem_kv, w_sb_br, w_nsa_br,
            w_mem_br, w_out, ffn_g, peer_w_q, subkeys):
    b, t, d = x.shape
    m = mem.shape[1]
    g = NSA_KV_GROUPS
    n = b * t
    x2d = x.reshape(n, d)

    o_sbq, o_sbk, o_sbv = 0, SB_W, 2 * SB_W
    o_nq = 3 * SB_W
    o_nkv = o_nq + NSA_W
    o_ng = o_nkv + 6 * NSA_KV_W
    o_mq = o_ng + NSA_HEADS * 3
    o_mg = o_mq + MEM_W
    w_act = jnp.concatenate([w_in[:, :o_ng], w_in[:, o_mq:o_mg]], axis=1).astype(BF16)
    gate_pad = LANES - NSA_HEADS * 3
    w_gate = jnp.concatenate([w_in[:, o_mg:], w_in[:, o_ng:o_mq], jnp.zeros((d, gate_pad), w_in.dtype)],
                             axis=1).astype(BF16)
    act = _norm_matmul(x2d, mix_g, w_act, BF16, tm=256, tn=512)
    gates_slab = _norm_matmul(x2d, mix_g, w_gate, F32, tm=256, tn=640)

    def heads(lo, nh):
        return act[:, lo:lo + nh * HEAD_DIM].reshape(b, t, nh, HEAD_DIM).transpose(0, 2, 1, 3)

    sb_out = _sb_attention(heads(o_sbq, SB_HEADS), heads(o_sbk, SB_HEADS), heads(o_sbv, SB_HEADS))

    kv = [heads(o_nkv + j * NSA_KV_W, g) for j in range(6)]
    kc = _nsa_compress(kv[0], pe_k, cw_k)
    vc = _nsa_compress(kv[1], pe_v, cw_v)
    gl = gates_slab[:, N_BRANCHES * d:N_BRANCHES * d + NSA_HEADS * 3]
    gl = gl.reshape(b, t, g, NSA_GROUP * 3).transpose(0, 2, 1, 3)
    gl = jnp.pad(gl, ((0, 0), (0, 0), (0, 0), (0, LANES - NSA_GROUP * 3)))
    slopes = jnp.asarray([2.0 ** (-8.0 * (h + 1) / NSA_HEADS) for h in range(NSA_HEADS)], F32)
    nsa_out = _nsa_attention(heads(o_nq, NSA_HEADS), kc, vc, kv[2], kv[3], kv[4], kv[5], gl, slopes)
    nsa_out = nsa_out.transpose(0, 2, 1, 3).reshape(n, NSA_W)

    mkv = _norm_matmul(mem.reshape(b * m, d), mem_g, w_mem_kv.astype(BF16), BF16, tm=256, tn=512)
    mem_q = act[:, o_nkv + 6 * NSA_KV_W:].reshape(b, t, MEM_W)
    mem_out = _mem_attention(mem_q, mkv.reshape(b, m, 2 * MEM_W))

    x1, h2, q_chunks = _merge(
        sb_out.reshape(n, SB_W), nsa_out, mem_out.reshape(n, MEM_W), gates_slab, b_merge, x2d,
        w_sb_br.astype(BF16), w_nsa_br.astype(BF16), w_mem_br.astype(BF16), w_out.astype(BF16), ffn_g,
        peer_w_q.astype(BF16))

    half = PEER_QUERY_DIM // 2
    idx, gw = _peer_topk(q_chunks, subkeys.reshape(2 * PEER_HEADS, PEER_N_KEYS, half).astype(BF16))
    return x1, h2, idx.T, gw


def _layer(x, mem, *params_and_tables):
    uv, out_g = params_and_tables[-2:]
    b, t, d = x.shape
    tt = 128
    x1, h2, idx_tok, gw = _mixers(x, mem, *params_and_tables[:-2])
    n, slots = idx_tok.shape
    n_tiles = n // tt
    idx_tiles = idx_tok.reshape(n_tiles, tt * slots)
    chunk = SC_CHUNK_TILES
    sc_first = n_tiles - SC_CHUNKS * chunk
    lead = sc_first - SC_CHUNKS * chunk
    assert lead > 0

    def stage(c):
        lo = (sc_first + c * chunk) * tt
        slabs = _sc_gather_slabs(uv, idx_tok[lo:lo + chunk * tt].reshape(-1))
        return slabs.reshape((chunk * tt, slots) + uv.shape[1:])

    staged = [stage(c) for c in range(SC_CHUNKS)]
    ffn = functools.partial(_peer_ffn, idx_tiles, gw=gw, h2=h2, x1=x1, final_g=out_g, tt=tt)
    own = ffn(tables=[uv], first_tiles=[0], n_tiles=lead, staged=[False])
    theirs = []
    for c in range(SC_CHUNKS):
        mine, other = ffn(tables=[uv, staged[c]], first_tiles=[lead + c * chunk, sc_first + c * chunk],
                          n_tiles=chunk, staged=[False, True])
        own.append(mine)
        theirs.append(other)
    return jnp.concatenate(own + theirs, axis=0).reshape(b, t, d)


def kernel(x, mem, mix_norm_g, mem_norm_g, w_in, b_merge, cmp_pe_k, cmp_pe_v, cmp_w_k, cmp_w_v, w_mem_kv, w_sb_br, w_nsa_br, w_mem_br, w_out, ffn_norm_g, peer_w_q, peer_subkeys, peer_u, peer_v, final_norm_g):
    depth = w_in.shape[0]
    assert depth == 1, "the final rmsnorm is fused into the last layer's PEER kernel"
    l = 0
    uv = _pack_tables(peer_u[l], peer_v[l])
    return _layer(x, mem, mix_norm_g[l], mem_norm_g[l], w_in[l], b_merge[l], cmp_pe_k[l], cmp_pe_v[l],
                  cmp_w_k[l], cmp_w_v[l], w_mem_kv[l], w_sb_br[l], w_nsa_br[l], w_mem_br[l], w_out[l],
                  ffn_norm_g[l], peer_w_q[l], peer_subkeys[l], uv, final_norm_g)
```

```python
import functools
import math

import jax
import jax.numpy as jnp
from jax import lax
from jax.experimental import pallas as pl
from jax.experimental.pallas import tpu as pltpu
from jax.experimental.pallas import tpu_sc as plsc

F32 = jnp.float32
BF16 = jnp.bfloat16
I32 = jnp.int32

HEAD_DIM = 64
SB_HEADS = 6
NSA_HEADS = 6
NSA_KV_GROUPS = 2
NSA_GROUP = NSA_HEADS // NSA_KV_GROUPS
MEM_HEADS = 4
N_BRANCHES = 3
SB_W = SB_HEADS * HEAD_DIM
NSA_W = NSA_HEADS * HEAD_DIM
NSA_KV_W = NSA_KV_GROUPS * HEAD_DIM
MEM_W = MEM_HEADS * HEAD_DIM
CMP_LEN = 32
CMP_STRIDE = 16
SEL_BLOCK = 64
N_SELECT = 16
WINDOW = 512
FORCED_SCORE = 1e4
PEER_HEADS = 8
PEER_N_KEYS = 128
PEER_QUERY_DIM = 256
PEER_TOPK = 16
RMS_EPS = 1e-6
NEG_INF = -1e30
SCALE = HEAD_DIM ** -0.5
SB_DEAD_LOG = 104.0

LANES = 128
PEER_GATHER_BUFFERS = 8
SC_GATHER_WINDOW = 32
SC_CHUNKS = 4
SC_CHUNK_TILES = 28
VMEM_LIMIT_BYTES = 56 * 1024 * 1024

_NT = (((1,), (1,)), ((), ()))


def _params(*sem):
    return pltpu.CompilerParams(dimension_semantics=sem, vmem_limit_bytes=VMEM_LIMIT_BYTES)


def _dot(a, b):
    return jnp.dot(a, b, preferred_element_type=F32)


def _dot_nt(a, b):
    return lax.dot_general(a, b, _NT, preferred_element_type=F32)


def _sigmoid(x):
    return 1.0 / (1.0 + jnp.exp(-x))


def _rms(x, g):
    return x * lax.rsqrt(jnp.mean(x * x, axis=-1, keepdims=True) + RMS_EPS) * g


def _norm_matmul_kernel(x_ref, g_ref, w_ref, o_ref, *, tn):
    h = _rms(x_ref[...], g_ref[...]).astype(BF16)
    for c in range(0, o_ref.shape[1], tn):
        o_ref[:, c:c + tn] = _dot(h, w_ref[:, c:c + tn]).astype(o_ref.dtype)


def _norm_matmul(x2d, g, w, out_dtype, tm, tn):
    n, d = x2d.shape
    m = w.shape[1]
    return pl.pallas_call(
        functools.partial(_norm_matmul_kernel, tn=tn),
        grid=(n // tm,),
        in_specs=[
            pl.BlockSpec((tm, d), lambda i: (i, 0)),
            pl.BlockSpec((1, d), lambda i: (0, 0)),
            pl.BlockSpec((d, m), lambda i: (0, 0)),
        ],
        out_specs=pl.BlockSpec((tm, m), lambda i: (i, 0)),
        out_shape=jax.ShapeDtypeStruct((n, m), out_dtype),
        compiler_params=_params("parallel"),
        name="norm_matmul",
    )(x2d, g.reshape(1, d), w)


def _norm_matmul_heads_kernel(x_ref, g_ref, w_ref, o_ref, *, tn):
    h = _rms(x_ref[...], g_ref[...]).astype(BF16)
    per = tn // HEAD_DIM
    for c in range(0, w_ref.shape[1], tn):
        res = _dot(h, w_ref[:, c:c + tn]).astype(o_ref.dtype)
        for j in range(per):
            o_ref[c // HEAD_DIM + j] = res[:, j * HEAD_DIM:(j + 1) * HEAD_DIM]


def _norm_matmul_heads(x2d, g, w, tm, tn):
    n, d = x2d.shape
    m = w.shape[1]
    nh = m // HEAD_DIM
    return pl.pallas_call(
        functools.partial(_norm_matmul_heads_kernel, tn=tn),
        grid=(n // tm,),
        in_specs=[
            pl.BlockSpec((tm, d), lambda i: (i, 0)),
            pl.BlockSpec((1, d), lambda i: (0, 0)),
            pl.BlockSpec((d, m), lambda i: (0, 0)),
        ],
        out_specs=pl.BlockSpec((nh, tm, HEAD_DIM), lambda i: (0, i, 0)),
        out_shape=jax.ShapeDtypeStruct((nh, n, HEAD_DIM), BF16),
        compiler_params=_params("parallel"),
        name="norm_matmul_heads",
    )(x2d, g.reshape(1, d), w)


def _sb_kernel(q_ref, k_ref, v_ref, o_ref, *, tile):
    qi = pl.program_id(2)
    nh = q_ref.shape[0]
    row = lax.broadcasted_iota(I32, (tile, tile), 0)
    col = lax.broadcasted_iota(I32, (tile, tile), 1)
    lower = row > col
    later = lower.astype(BF16)
    qs = [q_ref[hh, 0] for hh in range(nh)]

    def visit(hh, ks, c, acc, diagonal):
        k = k_ref[hh, 0, pl.ds(ks, tile), :]
        v = v_ref[hh, 0, pl.ds(ks, tile), :]
        z = _dot_nt(qs[hh], k) * SCALE
        sp = jnp.maximum(z, 0.0) + jnp.log(1.0 + jnp.exp(-jnp.abs(z)))
        if diagonal:
            sp = jnp.where(lower, sp, 0.0)
        hi = sp.astype(BF16)
        lo = (sp - hi.astype(F32)).astype(BF16)
        after = _dot(hi, later) + _dot(lo, later)
        a = jnp.exp(z - sp - after - c)
        if diagonal:
            a = jnp.where(lower, a, 0.0)
        return c + jnp.sum(sp, axis=1, keepdims=True), acc + _dot(a.astype(BF16), v)

    q0 = pl.multiple_of(qi * tile, tile)
    state = []
    for hh in range(nh):
        state.extend(visit(hh, q0, jnp.zeros((tile, 1), F32), jnp.zeros((tile, HEAD_DIM), F32), True))

    def smallest_carry(state):
        c = state[0]
        for hh in range(1, nh):
            c = jnp.minimum(c, state[2 * hh])
        return jnp.min(c)

    def live(carry):
        i, c_min, _ = carry
        return (i < qi) & (c_min <= SB_DEAD_LOG)

    def body(carry):
        i, _, state = carry
        ks = pl.multiple_of((qi - 1 - i) * tile, tile)
        out = []
        for hh in range(nh):
            out.extend(visit(hh, ks, state[2 * hh], state[2 * hh + 1], False))
        return i + 1, smallest_carry(out), tuple(out)

    _, _, state = lax.while_loop(live, body, (0, smallest_carry(state), tuple(state)))
    o_ref[0] = jnp.concatenate([state[2 * hh + 1] for hh in range(nh)], axis=1).astype(o_ref.dtype)


def _sb_attention(hd, q0, k0, v0, h, tile=256):
    _, b, t, dh = hd.shape
    hp = 2
    tq = tile
    assert q0 % hp == 0 and k0 % hp == 0 and v0 % hp == 0
    return pl.pallas_call(
        functools.partial(_sb_kernel, tile=tile),
        grid=(b, h // hp, t // tq),
        in_specs=[
            pl.BlockSpec((hp, 1, tq, dh), lambda bi, hi, qi: (q0 // hp + hi, bi, qi, 0)),
            pl.BlockSpec((hp, 1, t, dh), lambda bi, hi, qi: (k0 // hp + hi, bi, 0, 0)),
            pl.BlockSpec((hp, 1, t, dh), lambda bi, hi, qi: (v0 // hp + hi, bi, 0, 0)),
        ],
        out_specs=pl.BlockSpec((1, tq, hp * dh), lambda bi, hi, qi: (bi, qi, hi)),
        out_shape=jax.ShapeDtypeStruct((b, t, h * dh), BF16),
        compiler_params=_params("parallel", "parallel", "arbitrary"),
        name="sb_attn",
    )(hd, hd, hd)


def _compress_kernel(x_ref, pe_ref, w_ref, o_ref):
    x = x_ref[0, 0]
    nc = x.shape[0]
    w_lo, w_hi = w_ref[0], w_ref[1]
    first = _dot(x, w_lo)
    second = _dot(x, w_hi)
    feat = pe_ref.shape[1]
    pe_lo = jnp.broadcast_to(pe_ref[0:1, :], (8, feat)).astype(BF16)
    pe_hi = jnp.broadcast_to(pe_ref[1:2, :], (8, feat)).astype(BF16)
    bias = _dot(pe_lo, w_lo)[0:1] + _dot(pe_hi, w_hi)[0:1]
    o_ref[0, 0] = (first + pltpu.roll(second, nc - 1, 0) + bias).astype(o_ref.dtype)


def _nsa_compress(hd, head0, pe, w):
    nh, b, t, dh = hd.shape
    g = NSA_KV_GROUPS
    nc = t // CMP_STRIDE
    feat = CMP_STRIDE * dh
    x = hd.reshape(nh, b, nc, feat)
    pe2 = pe.reshape(2, feat)
    w2 = w.reshape(2, feat, dh).astype(BF16)
    return pl.pallas_call(
        _compress_kernel,
        grid=(b, g),
        in_specs=[
            pl.BlockSpec((1, 1, nc, feat), lambda bi, gi: (head0 + gi, bi, 0, 0)),
            pl.BlockSpec((2, feat), lambda bi, gi: (0, 0)),
            pl.BlockSpec((2, feat, dh), lambda bi, gi: (0, 0, 0)),
        ],
        out_specs=pl.BlockSpec((1, 1, nc, dh), lambda bi, gi: (bi, gi, 0, 0)),
        out_shape=jax.ShapeDtypeStruct((b, g, nc, dh), BF16),
        compiler_params=_params("parallel", "parallel"),
        name="nsa_compress",
    )(x, pe2, w2)


def _nsa_kernel(slopes_ref, q_ref, kc_ref, vc_ref, ks_ref, vs_ref, kw_ref, vw_ref, gl_ref, pool_ref, ex_ref,
                o_ref, used_ref, *, tq, tk, n_sel, real_blocks):
    grp = pl.program_id(1)
    t0 = pl.program_id(2) * tq
    nc = kc_ref.shape[2]
    rr = NSA_GROUP
    t = t0 + lax.broadcasted_iota(I32, (tq, 1), 0)

    def stack(x):
        return jnp.concatenate([x] * rr, axis=0)

    q = q_ref[:, 0].reshape(rr * tq, HEAD_DIM)
    slope = jnp.concatenate([jnp.full((tq, 1), slopes_ref[grp * rr + r], F32) for r in range(rr)], axis=0)

    def heads_sum(x):
        out = x[0:tq]
        for r in range(1, rr):
            out = out + x[r * tq:(r + 1) * tq]
        return out

    def masked_softmax(s, valid):
        p = jnp.where(valid, jnp.exp(s - jnp.max(s, axis=1, keepdims=True)), 0.0)
        denom = jnp.sum(p, axis=1, keepdims=True)
        return p / jnp.where(denom > 0, denom, 1.0)

    lane = lax.broadcasted_iota(I32, (1, nc), 1)
    dist_c = (t - (lane * CMP_STRIDE + (CMP_LEN - 1))).astype(F32)
    valid_c = stack(dist_c >= 0)
    s = _dot_nt(q, kc_ref[0, 0]) * SCALE - slope * stack(dist_c)
    p = masked_softmax(jnp.where(valid_c, s, NEG_INF), valid_c)
    o_cmp = _dot(p.astype(BF16), vc_ref[0, 0])
    psum = heads_sum(p)

    n_blk = pool_ref.shape[0]
    hi = psum.astype(BF16)
    rest = psum - hi.astype(F32)
    mid = rest.astype(BF16)
    lo = (rest - mid.astype(F32)).astype(BF16)
    pool = pool_ref[...]
    imp = _dot_nt(pool, hi) + _dot_nt(pool, mid) + _dot_nt(pool, lo)
    blk = lax.broadcasted_iota(I32, (n_blk, 1), 0)
    cur = (t0 + lax.broadcasted_iota(I32, (1, tq), 1)) // SEL_BLOCK
    forced = (blk == 0) | (blk == cur) | (blk == cur - 1)
    imp = jnp.where(forced, FORCED_SCORE, jnp.where(blk <= cur, imp, -1.0))
    imp = jnp.where(blk < real_blocks, imp, -jnp.inf)
    blk_f = blk.astype(F32)

    def pick(_, carry):
        imp, sel = carry
        best = jnp.max(imp, axis=0, keepdims=True)
        first = jnp.min(jnp.where(imp == best, blk_f, float(n_blk)), axis=0, keepdims=True)
        hit = blk_f == first
        return jnp.where(hit, -jnp.inf, imp), jnp.where(hit, 1.0, sel)

    _, sel = lax.fori_loop(0, n_sel, pick, (imp, jnp.zeros((n_blk, tq), F32)))
    picked = jnp.max(sel, axis=1, keepdims=True)
    per_tile = tk // SEL_BLOCK
    n_causal = (t0 + tq + tk - 1) // tk
    n_used = jnp.int32(0)
    for j in range(used_ref.shape[0]):
        used_ref[n_used] = j
        hit = (jnp.max(picked[j * per_tile:(j + 1) * per_tile, :]) > 0) & (j < n_causal)
        n_used = n_used + hit.astype(I32)
    sel = sel.T.astype(BF16)

    span = WINDOW + tq
    w0 = pl.multiple_of(jnp.maximum(t0 - WINDOW, 0), tq)
    dist_w = t - (w0 + lax.broadcasted_iota(I32, (1, span), 1))
    valid_w = stack((dist_w >= 0) & (dist_w < WINDOW))
    s = _dot_nt(q, kw_ref[0, 0, pl.ds(w0, span), :]) * SCALE - slope * stack(dist_w.astype(F32))
    p = masked_softmax(jnp.where(valid_w, s, NEG_INF), valid_w)
    o_win = _dot(p.astype(BF16), vw_ref[0, 0, pl.ds(w0, span), :])

    kcol = lax.broadcasted_iota(I32, (1, tk), 1)

    def sel_body(i, carry):
        m, l, acc = carry
        kb = used_ref[i]
        ks0 = pl.multiple_of(kb * tk, tk)
        chosen = _dot(sel, ex_ref[kb]) > 0.5
        dist = t - (ks0 + kcol)
        valid = stack(chosen & (dist >= 0))
        s = _dot_nt(q, ks_ref[0, 0, pl.ds(ks0, tk), :]) * SCALE - slope * stack(dist.astype(F32))
        s = jnp.where(valid, s, NEG_INF)
        m_new = jnp.maximum(m, jnp.max(s, axis=1, keepdims=True))
        alpha = jnp.exp(m - m_new)
        p = jnp.where(valid, jnp.exp(s - m_new), 0.0)
        l = alpha * l + jnp.sum(p, axis=1, keepdims=True)
        acc = alpha * acc + _dot(p.astype(BF16), vs_ref[0, 0, pl.ds(ks0, tk), :])
        return m_new, l, acc

    init = (jnp.full((rr * tq, 1), NEG_INF, F32), jnp.zeros((rr * tq, 1), F32),
            jnp.zeros((rr * tq, HEAD_DIM), F32))
    _, l_sel, acc_sel = lax.fori_loop(0, n_used, sel_body, init)
    o_sel = acc_sel / l_sel

    gates = _sigmoid(gl_ref[0, 0])
    outs = []
    for r in range(rr):
        rows = slice(r * tq, (r + 1) * tq)
        outs.append(gates[:, 3 * r:3 * r + 1] * o_cmp[rows] + gates[:, 3 * r + 1:3 * r + 2] * o_sel[rows]
                    + gates[:, 3 * r + 2:3 * r + 3] * o_win[rows])
    o_ref[0, 0] = jnp.concatenate(outs, axis=1).astype(o_ref.dtype)


def _nsa_attention(hd, q0, sel0, kc, vc, gate_logits, slopes, tq=256, tk=256):
    _, b, t, dh = hd.shape
    g = NSA_KV_GROUPS
    assert q0 % NSA_GROUP == 0
    nc = kc.shape[2]
    n_sel = min(N_SELECT, t // SEL_BLOCK)
    real_blocks = t // SEL_BLOCK
    n_blk = LANES
    assert t % tk == 0 and t >= WINDOW + tq and real_blocks <= n_blk
    blk_ids = jnp.arange(n_blk, dtype=I32)
    pool = (blk_ids[:, None] == jnp.arange(nc, dtype=I32)[None, :] // (SEL_BLOCK // CMP_STRIDE)).astype(BF16)
    key_blk = (jnp.arange(t, dtype=I32) // SEL_BLOCK).reshape(t // tk, 1, tk)
    expand = (blk_ids[None, :, None] == key_blk).astype(BF16)
    def kv_spec(j):
        return pl.BlockSpec((1, 1, t, dh), lambda bi, gi, qi: (sel0 + j * g + gi, bi, 0, 0))

    c_spec = pl.BlockSpec((1, 1, nc, dh), lambda bi, gi, qi: (bi, gi, 0, 0))
    return pl.pallas_call(
        functools.partial(_nsa_kernel, tq=tq, tk=tk, n_sel=n_sel, real_blocks=real_blocks),
        grid=(b, g, t // tq),
        in_specs=[
            pl.BlockSpec(memory_space=pltpu.SMEM),
            pl.BlockSpec((NSA_GROUP, 1, tq, dh), lambda bi, gi, qi: (q0 // NSA_GROUP + gi, bi, qi, 0)),
            c_spec, c_spec, kv_spec(0), kv_spec(1), kv_spec(2), kv_spec(3),
            pl.BlockSpec((1, 1, tq, LANES), lambda bi, gi, qi: (bi, gi, qi, 0)),
            pl.BlockSpec(pool.shape, lambda bi, gi, qi: (0, 0)),
            pl.BlockSpec(expand.shape, lambda bi, gi, qi: (0, 0, 0)),
        ],
        out_specs=pl.BlockSpec((1, 1, tq, NSA_GROUP * dh), lambda bi, gi, qi: (bi, gi, qi, 0)),
        out_shape=jax.ShapeDtypeStruct((b, g, t, NSA_GROUP * dh), BF16),
        scratch_shapes=[pltpu.SMEM((t // tk,), I32)],
        compiler_params=_params("parallel", "parallel", "arbitrary"),
        name="nsa_attn",
    )(slopes, hd, kc, vc, hd, hd, hd, hd, gate_logits, pool, expand)


def _mem_kernel(q_ref, kv_ref, o_ref):
    kv = kv_ref[0]
    outs = []
    for h in range(MEM_HEADS):
        sl = slice(h * HEAD_DIM, (h + 1) * HEAD_DIM)
        s = _dot_nt(q_ref[h, 0], kv[:, sl]) * SCALE
        p = jnp.exp(s - jnp.max(s, axis=1, keepdims=True))
        p = p / jnp.sum(p, axis=1, keepdims=True)
        outs.append(_dot(p.astype(BF16), kv[:, MEM_W + h * HEAD_DIM:MEM_W + (h + 1) * HEAD_DIM]))
    o_ref[0] = jnp.concatenate(outs, axis=1).astype(o_ref.dtype)


def _mem_attention(hd, q0, mkv, tq=512):
    _, b, t, dh = hd.shape
    w = MEM_HEADS * dh
    m = mkv.shape[1]
    assert q0 % MEM_HEADS == 0
    return pl.pallas_call(
        _mem_kernel,
        grid=(b, t // tq),
        in_specs=[
            pl.BlockSpec((MEM_HEADS, 1, tq, dh), lambda bi, qi: (q0 // MEM_HEADS, bi, qi, 0)),
            pl.BlockSpec((1, m, 2 * w), lambda bi, qi: (bi, 0, 0)),
        ],
        out_specs=pl.BlockSpec((1, tq, w), lambda bi, qi: (bi, qi, 0)),
        out_shape=jax.ShapeDtypeStruct((b, t, w), BF16),
        compiler_params=_params("parallel", "parallel"),
        name="mem_attn",
    )(hd, mkv)


def _merge_kernel(sb_ref, nsa_ref, mem_ref, mg_ref, bm_ref, x_ref, wsb_ref, wnsa_ref, wmem_ref,
                  wout_ref, fg_ref, wq_ref, x1_ref, h2_ref, q_ref):
    d = x_ref.shape[1]
    branches = (_dot(sb_ref[...], wsb_ref[...]), _dot(nsa_ref[...], wnsa_ref[...]),
                _dot(mem_ref[...], wmem_ref[...]))
    merged = jnp.zeros_like(branches[0])
    for j in range(N_BRANCHES):
        gate = _sigmoid(mg_ref[:, j * d:(j + 1) * d] + bm_ref[:, j * d:(j + 1) * d])
        merged = merged + gate * branches[j]
    x1 = x_ref[...] + _dot(merged.astype(BF16), wout_ref[...])
    x1_ref[...] = x1
    h2 = _rms(x1, fg_ref[...])
    h2_ref[...] = h2
    q = _dot(h2.astype(BF16), wq_ref[...]).astype(q_ref.dtype)
    for c in range(q_ref.shape[0]):
        q_ref[c] = q[:, c * LANES:(c + 1) * LANES]


def _merge(sb, nsa, mem, gates_slab, b_merge, x2d, w_sb, w_nsa, w_mem, w_out, ffn_g, w_q, tm=256):
    n, d = x2d.shape
    nq = w_q.shape[1] // LANES
    row = lambda w: pl.BlockSpec((tm, w), lambda i: (i, 0))
    full = lambda a: pl.BlockSpec(a.shape, lambda i: (0,) * a.ndim)
    bm = b_merge.reshape(1, -1)
    fg = ffn_g.reshape(1, d)
    return pl.pallas_call(
        _merge_kernel,
        grid=(n // tm,),
        in_specs=[row(sb.shape[1]), row(nsa.shape[1]), row(mem.shape[1]), row(N_BRANCHES * d),
                  full(bm), row(d), full(w_sb), full(w_nsa), full(w_mem), full(w_out), full(fg),
                  full(w_q)],
        out_specs=[row(d), row(d), pl.BlockSpec((nq, tm, LANES), lambda i: (0, i, 0))],
        out_shape=[jax.ShapeDtypeStruct((n, d), F32), jax.ShapeDtypeStruct((n, d), F32),
                   jax.ShapeDtypeStruct((nq, n, LANES), BF16)],
        compiler_params=_params("parallel"),
        name="merge",
    )(sb, nsa, mem, gates_slab, bm, x2d, w_sb, w_nsa, w_mem, w_out, fg, w_q)


def _peer_topk_kernel(q_ref, sk_ref, idx_ref, gw_ref, tv_ref, ti_ref, bv_ref):
    tt = q_ref.shape[1]
    kk = PEER_TOPK
    nk = PEER_N_KEYS
    rid = lax.broadcasted_iota(I32, (nk, tt), 0).astype(F32)
    for c in range(2 * PEER_HEADS):
        scores = _dot_nt(sk_ref[c], q_ref[c])

        def pick(k, s, c=c):
            best = jnp.max(s, axis=0, keepdims=True)
            first = jnp.min(jnp.where(s == best, rid, float(nk)), axis=0, keepdims=True)
            tv_ref[c, pl.ds(k, 1), :] = best
            ti_ref[c, pl.ds(k, 1), :] = first
            return jnp.where(rid == first, -jnp.inf, s)

        lax.fori_loop(0, kk, pick, scores)

    sub = 8
    widths = [min(kk, -(-(kk // (i + 1)) // sub) * sub) for i in range(kk // 2)]
    n_cand = sum(widths) + kk // 2
    pos = lax.broadcasted_iota(I32, (n_cand, tt), 0).astype(F32)
    for h in range(PEER_HEADS):
        s0, s1 = tv_ref[2 * h], tv_ref[2 * h + 1]
        i0, i1 = ti_ref[2 * h], ti_ref[2 * h + 1]
        cand, cidx = [], []
        for i, wd in enumerate(widths):
            keep = lax.broadcasted_iota(I32, (wd, 1), 0) < kk // (i + 1)
            cand.append(jnp.where(keep, s0[i:i + 1, :] + s1[0:wd, :], -jnp.inf))
            cidx.append(i0[i:i + 1, :] * float(nk) + i1[0:wd, :])
        cand.append(s0[kk // 2:, :] + s1[0:1, :])
        cidx.append(i0[kk // 2:, :] * float(nk) + i1[0:1, :])
        cand = jnp.concatenate(cand, axis=0)
        cidx = jnp.concatenate(cidx, axis=0)

        def pick2(k, s, h=h, cidx=cidx):
            best = jnp.max(s, axis=0, keepdims=True)
            first = jnp.min(jnp.where(s == best, pos, float(n_cand)), axis=0, keepdims=True)
            hit = pos == first
            bv_ref[pl.ds(k, 1), :] = best
            expert = jnp.max(jnp.where(hit, cidx, -1.0), axis=0, keepdims=True)
            idx_ref[pl.ds(h * kk + k, 1), :] = expert.astype(I32)
            return jnp.where(hit, -jnp.inf, s)

        lax.fori_loop(0, kk, pick2, cand)
        best = bv_ref[...]
        e = jnp.exp(best - best[0:1, :])
        gw_ref[h * kk:(h + 1) * kk, :] = e / jnp.sum(e, axis=0, keepdims=True)


def _peer_topk(q_chunks, subkeys, tt=256):
    nchunk, n, half = q_chunks.shape
    slots = PEER_HEADS * PEER_TOPK
    return pl.pallas_call(
        _peer_topk_kernel,
        grid=(n // tt,),
        in_specs=[
            pl.BlockSpec((nchunk, tt, half), lambda i: (0, i, 0)),
            pl.BlockSpec(subkeys.shape, lambda i: (0, 0, 0)),
        ],
        out_specs=[pl.BlockSpec((slots, tt), lambda i: (0, i)),
                   pl.BlockSpec((slots, tt), lambda i: (0, i))],
        out_shape=[jax.ShapeDtypeStruct((slots, n), I32), jax.ShapeDtypeStruct((slots, n), F32)],
        scratch_shapes=[pltpu.VMEM((nchunk, PEER_TOPK, tt), F32),
                        pltpu.VMEM((nchunk, PEER_TOPK, tt), F32),
                        pltpu.VMEM((PEER_TOPK, tt), F32)],
        compiler_params=_params("parallel"),
        name="peer_topk",
    )(q_chunks, subkeys)


def _peer_ffn_kernel(*refs, first_tiles, staged):
    ns = len(staged)
    idx_hbm, tables = refs[0], refs[1:1 + ns]
    io = [refs[1 + ns + 3 * s:4 + ns + 3 * s] for s in range(ns)]
    fg_ref = refs[1 + 4 * ns]
    ys = refs[2 + 4 * ns:2 + 5 * ns]
    idx_smem, bufs = refs[2 + 5 * ns], refs[3 + 5 * ns:3 + 6 * ns]
    idx_sem, sems = refs[3 + 6 * ns], refs[4 + 6 * ns:4 + 7 * ns]
    tt, d = io[0][1].shape
    slots = io[0][0].shape[0]
    rows = d // (2 * LANES)
    tile = pl.program_id(0)
    nbuf = bufs[0].shape[0]
    tok_lane = lax.broadcasted_iota(I32, (1, tt), 1)

    def halves(words):
        return tuple(pltpu.unpack_elementwise(words, index=i, packed_dtype=BF16, unpacked_dtype=F32)
                     for i in range(2))

    def make_stream(s):
        table, buf, row_sem, y_ref = tables[s], bufs[s], sems[s], ys[s]
        gw_ref, h_ref, _ = io[s]

        if staged[s]:
            def start(tok, slot):
                for r in range(buf.shape[1]):
                    pltpu.make_async_copy(table.at[tile * tt + tok, :, r, :], buf.at[slot, r],
                                          row_sem.at[slot]).start(priority=r % 2)
        else:
            idx_copy = pltpu.make_async_copy(idx_hbm.at[tile + first_tiles[s]], idx_smem, idx_sem)
            idx_copy.start()
            idx_copy.wait()

            def start(tok, slot):
                for e in range(slots):
                    pltpu.make_async_copy(table.at[idx_smem[tok * slots + e]], buf.at[slot, :, e, :],
                                          row_sem.at[slot]).start(priority=e % 2)

        def wait(slot):
            pltpu.make_async_copy(buf.at[slot], buf.at[slot], row_sem.at[slot]).wait()

        def expert_weights(tok, slot):
            h = h_ref[pl.ds(tok, 1), :]
            prod = jnp.zeros((slots, LANES), F32)
            for r in range(rows):
                lo, hi = halves(buf[slot, r])
                prod = prod + lo * h[:, r * LANES:(r + 1) * LANES]
                prod = prod + hi * h[:, d // 2 + r * LANES:d // 2 + (r + 1) * LANES]
            a = jnp.sum(prod, axis=1, keepdims=True)
            act = 0.5 * a * (1.0 + lax.erf(a * (2.0 ** -0.5)))
            gate = jnp.sum(jnp.where(tok_lane == tok, gw_ref[...], 0.0), axis=1, keepdims=True)
            return gate * act

        def weighted_values(tok, slot, w):
            parts = [halves(buf[slot, rows + r]) for r in range(rows)]
            y_ref[pl.ds(tok, 1), :] = jnp.concatenate(
                [jnp.sum(parts[r][i] * w, axis=0, keepdims=True) for i in range(2) for r in range(rows)],
                axis=1)

        def step(tok, slot, w, refill):
            nxt = (slot + 1) % nbuf
            wait(nxt)
            w_next = expert_weights(tok + 1, nxt)
            weighted_values(tok, slot, w)
            if refill:
                start(tok + nbuf, slot)
            return w_next

        return start, wait, expert_weights, weighted_values, step

    streams = [make_stream(s) for s in range(ns)]

    ws = []
    for start, wait, expert_weights, _, _ in streams:
        for tok in range(nbuf):
            start(tok, tok)
    for start, wait, expert_weights, _, _ in streams:
        wait(0)
        ws.append(expert_weights(0, 0))

    def round_of_steps(i, ws):
        ws = list(ws)
        for slot in range(nbuf):
            for s in range(ns):
                ws[s] = streams[s][4](i * nbuf + slot, slot, ws[s], True)
        return tuple(ws)

    n_rounds = tt // nbuf - 1
    ws = list(lax.fori_loop(0, n_rounds, round_of_steps, tuple(ws)))
    for slot in range(nbuf - 1):
        for s in range(ns):
            ws[s] = streams[s][4](n_rounds * nbuf + slot, slot, ws[s], False)
    for s in range(ns):
        streams[s][3](tt - 1, nbuf - 1, ws[s])
        ys[s][...] = _rms(io[s][2][...] + ys[s][...], fg_ref[...])


def _pack_tables_kernel(u_ref, v_ref, o_ref):
    d = u_ref.shape[1]
    rows = d // (2 * LANES)
    for base, src in ((0, u_ref), (rows, v_ref)):
        for r in range(rows):
            lo = src[:, r * LANES:(r + 1) * LANES]
            hi = src[:, d // 2 + r * LANES:d // 2 + (r + 1) * LANES]
            o_ref[:, base + r, :] = pltpu.pack_elementwise([lo, hi], packed_dtype=BF16)


def _pack_tables(u, v, te=512):
    n_exp, d = u.shape
    rows = d // LANES
    return pl.pallas_call(
        _pack_tables_kernel,
        grid=(n_exp // te,),
        in_specs=[pl.BlockSpec((te, d), lambda i: (i, 0)), pl.BlockSpec((te, d), lambda i: (i, 0))],
        out_specs=pl.BlockSpec((te, rows, LANES), lambda i: (i, 0, 0)),
        out_shape=jax.ShapeDtypeStruct((n_exp, rows, LANES), jnp.uint32),
        compiler_params=_params("parallel"),
        name="pack_tables",
    )(u, v)


def _sc_gather_slabs(slabs, slab_idx):
    m = slab_idx.shape[0]
    mesh = plsc.VectorSubcoreMesh(core_axis_name="core", subcore_axis_name="subcore")
    idx_rows = jnp.pad(slab_idx.reshape(m // SC_GATHER_WINDOW, SC_GATHER_WINDOW),
                       ((0, 0), (0, LANES - SC_GATHER_WINDOW)))

    @pl.kernel(out_type=jax.ShapeDtypeStruct((m,) + slabs.shape[1:], slabs.dtype), mesh=mesh)
    def gather(slabs_hbm, idx_hbm, out_hbm):
        def window(idx_vmem, out_vmem):
            pltpu.sync_copy(slabs_hbm.at[idx_vmem.at[0, pl.ds(0, SC_GATHER_WINDOW)]], out_vmem)

        pltpu.emit_pipeline(
            window,
            grid=(m // SC_GATHER_WINDOW,),
            in_specs=[pl.BlockSpec((1, LANES), index_map=lambda i: (i, 0))],
            out_specs=[pl.BlockSpec((SC_GATHER_WINDOW,) + slabs.shape[1:], index_map=lambda i: (i, 0, 0))],
            core_axis_name=("core", "subcore"),
            dimension_semantics=(pltpu.PARALLEL,),
            trace_scopes=False,
        )(idx_hbm, out_hbm)

    return gather(slabs, idx_rows)


def _peer_ffn(idx_tiles, tables, gw, h2, x1, final_g, tt, first_tiles, n_tiles, staged):
    _, d = h2.shape
    slots = gw.shape[0]
    rows = d // LANES
    ns = len(staged)
    assert sum(not st for st in staged) <= 1, "one index buffer: at most one gathering stream"
    hbm = pl.BlockSpec(memory_space=pl.ANY)
    io_specs, io_args = [], []
    for s in range(ns):
        io_specs += [pl.BlockSpec((slots, tt), lambda i, f=first_tiles[s]: (0, i + f)),
                     pl.BlockSpec((tt, d), lambda i, f=first_tiles[s]: (i + f, 0)),
                     pl.BlockSpec((tt, d), lambda i, f=first_tiles[s]: (i + f, 0))]
        io_args += [gw, h2, x1]
    buf = pltpu.VMEM((PEER_GATHER_BUFFERS, rows, slots, LANES), tables[0].dtype)
    sem = pltpu.SemaphoreType.DMA((PEER_GATHER_BUFFERS,))
    outs = pl.pallas_call(
        functools.partial(_peer_ffn_kernel, first_tiles=tuple(first_tiles), staged=tuple(staged)),
        grid=(n_tiles,),
        in_specs=[hbm] + [hbm] * ns + io_specs + [pl.BlockSpec((1, d), lambda i: (0, 0))],
        out_specs=[pl.BlockSpec((tt, d), lambda i: (i, 0))] * ns,
        out_shape=[jax.ShapeDtypeStruct((n_tiles * tt, d), F32)] * ns,
        scratch_shapes=[pltpu.SMEM((tt * slots,), I32)] + [buf] * ns + [pltpu.SemaphoreType.DMA] + [sem] * ns,
        compiler_params=_params("arbitrary"),
        name="peer_ffn_" + "_".join("staged" if st else "gather" for st in staged),
    )(idx_tiles, *tables, *io_args, final_g.reshape(1, d))
    return list(outs)


def _mixers(x, mem, mix_g, mem_g, w_in, b_merge, pe_k, pe_v, cw_k, cw_v, w_mem_kv, w_sb_br, w_nsa_br,
            w_mem_br, w_out, ffn_g, peer_w_q, subkeys):
    b, t, d = x.shape
    m = mem.shape[1]
    g = NSA_KV_GROUPS
    n = b * t
    x2d = x.reshape(n, d)

    o_sbq, o_sbk, o_sbv = 0, SB_W, 2 * SB_W
    o_nq = 3 * SB_W
    o_nkv = o_nq + NSA_W
    o_ng = o_nkv + 6 * NSA_KV_W
    o_mq = o_ng + NSA_HEADS * 3
    o_mg = o_mq + MEM_W
    w_act = jnp.concatenate([w_in[:, :o_ng], w_in[:, o_mq:o_mg]], axis=1).astype(BF16)
    gate_pad = LANES - NSA_HEADS * 3
    w_gate = jnp.concatenate([w_in[:, o_mg:], w_in[:, o_ng:o_mq], jnp.zeros((d, gate_pad), w_in.dtype)],
                             axis=1).astype(BF16)
    hd = _norm_matmul_heads(x2d, mix_g, w_act, tm=256, tn=512).reshape(-1, b, t, HEAD_DIM)
    gates_slab = _norm_matmul(x2d, mix_g, w_gate, F32, tm=256, tn=640)
    head_of = lambda col: col // HEAD_DIM

    sb_out = _sb_attention(hd, head_of(o_sbq), head_of(o_sbk), head_of(o_sbv), SB_HEADS)

    kc = _nsa_compress(hd, head_of(o_nkv), pe_k, cw_k)
    vc = _nsa_compress(hd, head_of(o_nkv) + g, pe_v, cw_v)
    gl = gates_slab[:, N_BRANCHES * d:N_BRANCHES * d + NSA_HEADS * 3]
    gl = gl.reshape(b, t, g, NSA_GROUP * 3).transpose(0, 2, 1, 3)
    gl = jnp.pad(gl, ((0, 0), (0, 0), (0, 0), (0, LANES - NSA_GROUP * 3)))
    slopes = jnp.asarray([2.0 ** (-8.0 * (h + 1) / NSA_HEADS) for h in range(NSA_HEADS)], F32)
    nsa_out = _nsa_attention(hd, head_of(o_nq), head_of(o_nkv) + 2 * g, kc, vc, gl, slopes)
    nsa_out = nsa_out.transpose(0, 2, 1, 3).reshape(n, NSA_W)

    mkv = _norm_matmul(mem.reshape(b * m, d), mem_g, w_mem_kv.astype(BF16), BF16, tm=256, tn=512)
    mem_out = _mem_attention(hd, head_of(o_nkv + 6 * NSA_KV_W), mkv.reshape(b, m, 2 * MEM_W))

    x1, h2, q_chunks = _merge(
        sb_out.reshape(n, SB_W), nsa_out, mem_out.reshape(n, MEM_W), gates_slab, b_merge, x2d,
        w_sb_br.astype(BF16), w_nsa_br.astype(BF16), w_mem_br.astype(BF16), w_out.astype(BF16), ffn_g,
        peer_w_q.astype(BF16))

    half = PEER_QUERY_DIM // 2
    idx, gw = _peer_topk(q_chunks, subkeys.reshape(2 * PEER_HEADS, PEER_N_KEYS, half).astype(BF16))
    return x1, h2, idx.T, gw


def _layer(x, mem, *params_and_tables):
    uv, out_g = params_and_tables[-2:]
    b, t, d = x.shape
    tt = 128
    x1, h2, idx_tok, gw = _mixers(x, mem, *params_and_tables[:-2])
    n, slots = idx_tok.shape
    n_tiles = n // tt
    idx_tiles = idx_tok.reshape(n_tiles, tt * slots)
    chunk = SC_CHUNK_TILES
    sc_first = n_tiles - SC_CHUNKS * chunk
    lead = sc_first - SC_CHUNKS * chunk
    assert lead > 0

    def stage(c):
        lo = (sc_first + c * chunk) * tt
        slabs = _sc_gather_slabs(uv, idx_tok[lo:lo + chunk * tt].reshape(-1))
        return slabs.reshape((chunk * tt, slots) + uv.shape[1:])

    staged = [stage(c) for c in range(SC_CHUNKS)]
    ffn = functools.partial(_peer_ffn, idx_tiles, gw=gw, h2=h2, x1=x1, final_g=out_g, tt=tt)
    own = ffn(tables=[uv], first_tiles=[0], n_tiles=lead, staged=[False])
    theirs = []
    for c in range(SC_CHUNKS):
        mine, other = ffn(tables=[uv, staged[c]], first_tiles=[lead + c * chunk, sc_first + c * chunk],
                          n_tiles=chunk, staged=[False, True])
        own.append(mine)
        theirs.append(other)
    return jnp.concatenate(own + theirs, axis=0).reshape(b, t, d)


def kernel(x, mem, mix_norm_g, mem_norm_g, w_in, b_merge, cmp_pe_k, cmp_pe_v, cmp_w_k, cmp_w_v, w_mem_kv, w_sb_br, w_nsa_br, w_mem_br, w_out, ffn_norm_g, peer_w_q, peer_subkeys, peer_u, peer_v, final_norm_g):
    depth = w_in.shape[0]
    assert depth == 1, "the final rmsnorm is fused into the last layer's PEER kernel"
    l = 0
    uv = _pack_tables(peer_u[l], peer_v[l])
    return _layer(x, mem, mix_norm_g[l], mem_norm_g[l], w_in[l], b_merge[l], cmp_pe_k[l], cmp_pe_v[l],
                  cmp_w_k[l], cmp_w_v[l], w_mem_kv[l], w_sb_br[l], w_nsa_br[l], w_mem_br[l], w_out[l],
                  ffn_norm_g[l], peer_w_q[l], peer_subkeys[l], uv, final_norm_g)
```

```python
import functools
import math

import jax
import jax.numpy as jnp
from jax import lax
from jax.experimental import pallas as pl
from jax.experimental.pallas import tpu as pltpu
from jax.experimental.pallas import tpu_sc as plsc

F32 = jnp.float32
BF16 = jnp.bfloat16
I32 = jnp.int32

HEAD_DIM = 64
SB_HEADS = 6
NSA_HEADS = 6
NSA_KV_GROUPS = 2
NSA_GROUP = NSA_HEADS // NSA_KV_GROUPS
MEM_HEADS = 4
N_BRANCHES = 3
SB_W = SB_HEADS * HEAD_DIM
NSA_W = NSA_HEADS * HEAD_DIM
NSA_KV_W = NSA_KV_GROUPS * HEAD_DIM
MEM_W = MEM_HEADS * HEAD_DIM
CMP_LEN = 32
CMP_STRIDE = 16
SEL_BLOCK = 64
N_SELECT = 16
WINDOW = 512
FORCED_SCORE = 1e4
PEER_HEADS = 8
PEER_N_KEYS = 128
PEER_QUERY_DIM = 256
PEER_TOPK = 16
RMS_EPS = 1e-6
NEG_INF = -1e30
SCALE = HEAD_DIM ** -0.5
SB_DEAD_LOG = 104.0

LANES = 128
PEER_GATHER_BUFFERS = 8
SC_GATHER_WINDOW = 32
SC_CHUNKS = 4
SC_CHUNK_TILES = 28
VMEM_LIMIT_BYTES = 56 * 1024 * 1024

_NT = (((1,), (1,)), ((), ()))


def _params(*sem):
    return pltpu.CompilerParams(dimension_semantics=sem, vmem_limit_bytes=VMEM_LIMIT_BYTES)


def _dot(a, b):
    return jnp.dot(a, b, preferred_element_type=F32)


def _dot_nt(a, b):
    return lax.dot_general(a, b, _NT, preferred_element_type=F32)


def _sigmoid(x):
    return 1.0 / (1.0 + jnp.exp(-x))


def _rms(x, g):
    return x * lax.rsqrt(jnp.mean(x * x, axis=-1, keepdims=True) + RMS_EPS) * g


def _norm_matmul_kernel(x_ref, g_ref, w_ref, o_ref, *, tn):
    h = _rms(x_ref[...], g_ref[...]).astype(BF16)
    for c in range(0, o_ref.shape[1], tn):
        o_ref[:, c:c + tn] = _dot(h, w_ref[:, c:c + tn]).astype(o_ref.dtype)


def _norm_matmul(x2d, g, w, out_dtype, tm, tn):
    n, d = x2d.shape
    m = w.shape[1]
    return pl.pallas_call(
        functools.partial(_norm_matmul_kernel, tn=tn),
        grid=(n // tm,),
        in_specs=[
            pl.BlockSpec((tm, d), lambda i: (i, 0)),
            pl.BlockSpec((1, d), lambda i: (0, 0)),
            pl.BlockSpec((d, m), lambda i: (0, 0)),
        ],
        out_specs=pl.BlockSpec((tm, m), lambda i: (i, 0)),
        out_shape=jax.ShapeDtypeStruct((n, m), out_dtype),
        compiler_params=_params("parallel"),
        name="norm_matmul",
    )(x2d, g.reshape(1, d), w)


def _norm_matmul_heads_kernel(x_ref, g_ref, w_ref, o_ref, *, tn):
    h = _rms(x_ref[...], g_ref[...]).astype(BF16)
    per = tn // HEAD_DIM
    for c in range(0, w_ref.shape[1], tn):
        res = _dot(h, w_ref[:, c:c + tn]).astype(o_ref.dtype)
        for j in range(per):
            o_ref[c // HEAD_DIM + j] = res[:, j * HEAD_DIM:(j + 1) * HEAD_DIM]


def _norm_matmul_heads(x2d, g, w, tm, tn):
    n, d = x2d.shape
    m = w.shape[1]
    nh = m // HEAD_DIM
    return pl.pallas_call(
        functools.partial(_norm_matmul_heads_kernel, tn=tn),
        grid=(n // tm,),
        in_specs=[
            pl.BlockSpec((tm, d), lambda i: (i, 0)),
            pl.BlockSpec((1, d), lambda i: (0, 0)),
            pl.BlockSpec((d, m), lambda i: (0, 0)),
        ],
        out_specs=pl.BlockSpec((nh, tm, HEAD_DIM), lambda i: (0, i, 0)),
        out_shape=jax.ShapeDtypeStruct((nh, n, HEAD_DIM), BF16),
        compiler_params=_params("parallel"),
        name="norm_matmul_heads",
    )(x2d, g.reshape(1, d), w)


def _sb_kernel(q_ref, k_ref, v_ref, o_ref, *, tile):
    qi = pl.program_id(2)
    nh = q_ref.shape[0]
    row = lax.broadcasted_iota(I32, (tile, tile), 0)
    col = lax.broadcasted_iota(I32, (tile, tile), 1)
    lower = row > col
    later = lower.astype(BF16)
    qs = [q_ref[hh, 0] for hh in range(nh)]

    def visit(hh, ks, c, acc, diagonal):
        k = k_ref[hh, 0, pl.ds(ks, tile), :]
        v = v_ref[hh, 0, pl.ds(ks, tile), :]
        z = _dot_nt(qs[hh], k) * SCALE
        sp = jnp.maximum(z, 0.0) + jnp.log(1.0 + jnp.exp(-jnp.abs(z)))
        if diagonal:
            sp = jnp.where(lower, sp, 0.0)
        hi = sp.astype(BF16)
        lo = (sp - hi.astype(F32)).astype(BF16)
        after = _dot(hi, later) + _dot(lo, later)
        a = jnp.exp(z - sp - after - c)
        if diagonal:
            a = jnp.where(lower, a, 0.0)
        return c + jnp.sum(sp, axis=1, keepdims=True), acc + _dot(a.astype(BF16), v)

    q0 = pl.multiple_of(qi * tile, tile)
    state = []
    for hh in range(nh):
        state.extend(visit(hh, q0, jnp.zeros((tile, 1), F32), jnp.zeros((tile, HEAD_DIM), F32), True))

    def smallest_carry(state):
        c = state[0]
        for hh in range(1, nh):
            c = jnp.minimum(c, state[2 * hh])
        return jnp.min(c)

    def live(carry):
        i, c_min, _ = carry
        return (i < qi) & (c_min <= SB_DEAD_LOG)

    def body(carry):
        i, _, state = carry
        ks = pl.multiple_of((qi - 1 - i) * tile, tile)
        out = []
        for hh in range(nh):
            out.extend(visit(hh, ks, state[2 * hh], state[2 * hh + 1], False))
        return i + 1, smallest_carry(out), tuple(out)

    _, _, state = lax.while_loop(live, body, (0, smallest_carry(state), tuple(state)))
    o_ref[0] = jnp.concatenate([state[2 * hh + 1] for hh in range(nh)], axis=1).astype(o_ref.dtype)


def _sb_attention(hd, q0, k0, v0, h, tile=256):
    _, b, t, dh = hd.shape
    hp = 2
    tq = tile
    assert q0 % hp == 0 and k0 % hp == 0 and v0 % hp == 0
    return pl.pallas_call(
        functools.partial(_sb_kernel, tile=tile),
        grid=(b, h // hp, t // tq),
        in_specs=[
            pl.BlockSpec((hp, 1, tq, dh), lambda bi, hi, qi: (q0 // hp + hi, bi, qi, 0)),
            pl.BlockSpec((hp, 1, t, dh), lambda bi, hi, qi: (k0 // hp + hi, bi, 0, 0)),
            pl.BlockSpec((hp, 1, t, dh), lambda bi, hi, qi: (v0 // hp + hi, bi, 0, 0)),
        ],
        out_specs=pl.BlockSpec((1, tq, hp * dh), lambda bi, hi, qi: (bi, qi, hi)),
        out_shape=jax.ShapeDtypeStruct((b, t, h * dh), BF16),
        compiler_params=_params("parallel", "parallel", "arbitrary"),
        name="sb_attn",
    )(hd, hd, hd)


def _compress_kernel(x_ref, pe_ref, w_ref, o_ref):
    x = x_ref[0, 0]
    nc = x.shape[0]
    w_lo, w_hi = w_ref[0], w_ref[1]
    first = _dot(x, w_lo)
    second = _dot(x, w_hi)
    feat = pe_ref.shape[1]
    pe_lo = jnp.broadcast_to(pe_ref[0:1, :], (8, feat)).astype(BF16)
    pe_hi = jnp.broadcast_to(pe_ref[1:2, :], (8, feat)).astype(BF16)
    bias = _dot(pe_lo, w_lo)[0:1] + _dot(pe_hi, w_hi)[0:1]
    o_ref[0, 0] = (first + pltpu.roll(second, nc - 1, 0) + bias).astype(o_ref.dtype)


def _nsa_compress(hd, head0, pe, w):
    nh, b, t, dh = hd.shape
    g = NSA_KV_GROUPS
    nc = t // CMP_STRIDE
    feat = CMP_STRIDE * dh
    x = hd[head0:head0 + g].reshape(g, b, nc, feat)
    pe2 = pe.reshape(2, feat)
    w2 = w.reshape(2, feat, dh).astype(BF16)
    return pl.pallas_call(
        _compress_kernel,
        grid=(b, g),
        in_specs=[
            pl.BlockSpec((1, 1, nc, feat), lambda bi, gi: (gi, bi, 0, 0)),
            pl.BlockSpec((2, feat), lambda bi, gi: (0, 0)),
            pl.BlockSpec((2, feat, dh), lambda bi, gi: (0, 0, 0)),
        ],
        out_specs=pl.BlockSpec((1, 1, nc, dh), lambda bi, gi: (bi, gi, 0, 0)),
        out_shape=jax.ShapeDtypeStruct((b, g, nc, dh), BF16),
        compiler_params=_params("parallel", "parallel"),
        name="nsa_compress",
    )(x, pe2, w2)


def _nsa_kernel(slopes_ref, q_ref, kc_ref, vc_ref, ks_ref, vs_ref, kw_ref, vw_ref, gl_ref, pool_ref, ex_ref,
                o_ref, used_ref, *, tq, tk, n_sel, real_blocks):
    grp = pl.program_id(1)
    t0 = pl.program_id(2) * tq
    nc = kc_ref.shape[2]
    rr = NSA_GROUP
    t = t0 + lax.broadcasted_iota(I32, (tq, 1), 0)

    def stack(x):
        return jnp.concatenate([x] * rr, axis=0)

    q = q_ref[:, 0].reshape(rr * tq, HEAD_DIM)
    slope = jnp.concatenate([jnp.full((tq, 1), slopes_ref[grp * rr + r], F32) for r in range(rr)], axis=0)

    def heads_sum(x):
        out = x[0:tq]
        for r in range(1, rr):
            out = out + x[r * tq:(r + 1) * tq]
        return out

    def masked_softmax(s, valid):
        p = jnp.where(valid, jnp.exp(s - jnp.max(s, axis=1, keepdims=True)), 0.0)
        denom = jnp.sum(p, axis=1, keepdims=True)
        return p / jnp.where(denom > 0, denom, 1.0)

    lane = lax.broadcasted_iota(I32, (1, nc), 1)
    dist_c = (t - (lane * CMP_STRIDE + (CMP_LEN - 1))).astype(F32)
    valid_c = stack(dist_c >= 0)
    s = _dot_nt(q, kc_ref[0, 0]) * SCALE - slope * stack(dist_c)
    p = masked_softmax(jnp.where(valid_c, s, NEG_INF), valid_c)
    o_cmp = _dot(p.astype(BF16), vc_ref[0, 0])
    psum = heads_sum(p)

    n_blk = pool_ref.shape[0]
    hi = psum.astype(BF16)
    rest = psum - hi.astype(F32)
    mid = rest.astype(BF16)
    lo = (rest - mid.astype(F32)).astype(BF16)
    pool = pool_ref[...]
    imp = _dot_nt(pool, hi) + _dot_nt(pool, mid) + _dot_nt(pool, lo)
    blk = lax.broadcasted_iota(I32, (n_blk, 1), 0)
    cur = (t0 + lax.broadcasted_iota(I32, (1, tq), 1)) // SEL_BLOCK
    forced = (blk == 0) | (blk == cur) | (blk == cur - 1)
    imp = jnp.where(forced, FORCED_SCORE, jnp.where(blk <= cur, imp, -1.0))
    imp = jnp.where(blk < real_blocks, imp, -jnp.inf)
    blk_f = blk.astype(F32)

    def pick(_, carry):
        imp, sel = carry
        best = jnp.max(imp, axis=0, keepdims=True)
        first = jnp.min(jnp.where(imp == best, blk_f, float(n_blk)), axis=0, keepdims=True)
        hit = blk_f == first
        return jnp.where(hit, -jnp.inf, imp), jnp.where(hit, 1.0, sel)

    _, sel = lax.fori_loop(0, n_sel, pick, (imp, jnp.zeros((n_blk, tq), F32)))
    picked = jnp.max(sel, axis=1, keepdims=True)
    per_tile = tk // SEL_BLOCK
    n_causal = (t0 + tq + tk - 1) // tk
    n_used = jnp.int32(0)
    for j in range(used_ref.shape[0]):
        used_ref[n_used] = j
        hit = (jnp.max(picked[j * per_tile:(j + 1) * per_tile, :]) > 0) & (j < n_causal)
        n_used = n_used + hit.astype(I32)
    sel = sel.T.astype(BF16)

    span = WINDOW + tq
    w0 = pl.multiple_of(jnp.maximum(t0 - WINDOW, 0), tq)
    dist_w = t - (w0 + lax.broadcasted_iota(I32, (1, span), 1))
    valid_w = stack((dist_w >= 0) & (dist_w < WINDOW))
    s = _dot_nt(q, kw_ref[0, 0, pl.ds(w0, span), :]) * SCALE - slope * stack(dist_w.astype(F32))
    p = masked_softmax(jnp.where(valid_w, s, NEG_INF), valid_w)
    o_win = _dot(p.astype(BF16), vw_ref[0, 0, pl.ds(w0, span), :])

    kcol = lax.broadcasted_iota(I32, (1, tk), 1)

    def sel_body(i, carry):
        m, l, acc = carry
        kb = used_ref[i]
        ks0 = pl.multiple_of(kb * tk, tk)
        chosen = _dot(sel, ex_ref[kb]) > 0.5
        dist = t - (ks0 + kcol)
        valid = stack(chosen & (dist >= 0))
        s = _dot_nt(q, ks_ref[0, 0, pl.ds(ks0, tk), :]) * SCALE - slope * stack(dist.astype(F32))
        s = jnp.where(valid, s, NEG_INF)
        m_new = jnp.maximum(m, jnp.max(s, axis=1, keepdims=True))
        alpha = jnp.exp(m - m_new)
        p = jnp.where(valid, jnp.exp(s - m_new), 0.0)
        l = alpha * l + jnp.sum(p, axis=1, keepdims=True)
        acc = alpha * acc + _dot(p.astype(BF16), vs_ref[0, 0, pl.ds(ks0, tk), :])
        return m_new, l, acc

    init = (jnp.full((rr * tq, 1), NEG_INF, F32), jnp.zeros((rr * tq, 1), F32),
            jnp.zeros((rr * tq, HEAD_DIM), F32))
    _, l_sel, acc_sel = lax.fori_loop(0, n_used, sel_body, init)
    o_sel = acc_sel / l_sel

    gates = _sigmoid(gl_ref[0, 0])
    outs = []
    for r in range(rr):
        rows = slice(r * tq, (r + 1) * tq)
        outs.append(gates[:, 3 * r:3 * r + 1] * o_cmp[rows] + gates[:, 3 * r + 1:3 * r + 2] * o_sel[rows]
                    + gates[:, 3 * r + 2:3 * r + 3] * o_win[rows])
    o_ref[0, 0] = jnp.concatenate(outs, axis=1).astype(o_ref.dtype)


def _nsa_attention(hd, q0, sel0, kc, vc, gate_logits, slopes, tq=256, tk=256):
    _, b, t, dh = hd.shape
    g = NSA_KV_GROUPS
    assert q0 % NSA_GROUP == 0
    nc = kc.shape[2]
    n_sel = min(N_SELECT, t // SEL_BLOCK)
    real_blocks = t // SEL_BLOCK
    n_blk = LANES
    assert t % tk == 0 and t >= WINDOW + tq and real_blocks <= n_blk
    blk_ids = jnp.arange(n_blk, dtype=I32)
    pool = (blk_ids[:, None] == jnp.arange(nc, dtype=I32)[None, :] // (SEL_BLOCK // CMP_STRIDE)).astype(BF16)
    key_blk = (jnp.arange(t, dtype=I32) // SEL_BLOCK).reshape(t // tk, 1, tk)
    expand = (blk_ids[None, :, None] == key_blk).astype(BF16)
    def kv_spec(j):
        return pl.BlockSpec((1, 1, t, dh), lambda bi, gi, qi: (sel0 + j * g + gi, bi, 0, 0))

    c_spec = pl.BlockSpec((1, 1, nc, dh), lambda bi, gi, qi: (bi, gi, 0, 0))
    return pl.pallas_call(
        functools.partial(_nsa_kernel, tq=tq, tk=tk, n_sel=n_sel, real_blocks=real_blocks),
        grid=(b, g, t // tq),
        in_specs=[
            pl.BlockSpec(memory_space=pltpu.SMEM),
            pl.BlockSpec((NSA_GROUP, 1, tq, dh), lambda bi, gi, qi: (q0 // NSA_GROUP + gi, bi, qi, 0)),
            c_spec, c_spec, kv_spec(0), kv_spec(1), kv_spec(2), kv_spec(3),
            pl.BlockSpec((1, 1, tq, LANES), lambda bi, gi, qi: (bi, gi, qi, 0)),
            pl.BlockSpec(pool.shape, lambda bi, gi, qi: (0, 0)),
            pl.BlockSpec(expand.shape, lambda bi, gi, qi: (0, 0, 0)),
        ],
        out_specs=pl.BlockSpec((1, 1, tq, NSA_GROUP * dh), lambda bi, gi, qi: (bi, gi, qi, 0)),
        out_shape=jax.ShapeDtypeStruct((b, g, t, NSA_GROUP * dh), BF16),
        scratch_shapes=[pltpu.SMEM((t // tk,), I32)],
        compiler_params=_params("parallel", "parallel", "arbitrary"),
        name="nsa_attn",
    )(slopes, hd, kc, vc, hd, hd, hd, hd, gate_logits, pool, expand)


def _mem_kernel(q_ref, kv_ref, o_ref):
    kv = kv_ref[0]
    outs = []
    for h in range(MEM_HEADS):
        sl = slice(h * HEAD_DIM, (h + 1) * HEAD_DIM)
        s = _dot_nt(q_ref[h, 0], kv[:, sl]) * SCALE
        p = jnp.exp(s - jnp.max(s, axis=1, keepdims=True))
        p = p / jnp.sum(p, axis=1, keepdims=True)
        outs.append(_dot(p.astype(BF16), kv[:, MEM_W + h * HEAD_DIM:MEM_W + (h + 1) * HEAD_DIM]))
    o_ref[0] = jnp.concatenate(outs, axis=1).astype(o_ref.dtype)


def _mem_attention(hd, q0, mkv, tq=512):
    _, b, t, dh = hd.shape
    w = MEM_HEADS * dh
    m = mkv.shape[1]
    assert q0 % MEM_HEADS == 0
    return pl.pallas_call(
        _mem_kernel,
        grid=(b, t // tq),
        in_specs=[
            pl.BlockSpec((MEM_HEADS, 1, tq, dh), lambda bi, qi: (q0 // MEM_HEADS, bi, qi, 0)),
            pl.BlockSpec((1, m, 2 * w), lambda bi, qi: (bi, 0, 0)),
        ],
        out_specs=pl.BlockSpec((1, tq, w), lambda bi, qi: (bi, qi, 0)),
        out_shape=jax.ShapeDtypeStruct((b, t, w), BF16),
        compiler_params=_params("parallel", "parallel"),
        name="mem_attn",
    )(hd, mkv)


def _merge_kernel(sb_ref, nsa_ref, mem_ref, mg_ref, bm_ref, x_ref, wsb_ref, wnsa_ref, wmem_ref,
                  wout_ref, fg_ref, wq_ref, x1_ref, h2_ref, q_ref):
    d = x_ref.shape[1]
    branches = (_dot(sb_ref[...], wsb_ref[...]), _dot(nsa_ref[...], wnsa_ref[...]),
                _dot(mem_ref[...], wmem_ref[...]))
    merged = jnp.zeros_like(branches[0])
    for j in range(N_BRANCHES):
        gate = _sigmoid(mg_ref[:, j * d:(j + 1) * d] + bm_ref[:, j * d:(j + 1) * d])
        merged = merged + gate * branches[j]
    x1 = x_ref[...] + _dot(merged.astype(BF16), wout_ref[...])
    x1_ref[...] = x1
    h2 = _rms(x1, fg_ref[...])
    h2_ref[...] = h2
    q = _dot(h2.astype(BF16), wq_ref[...]).astype(q_ref.dtype)
    for c in range(q_ref.shape[0]):
        q_ref[c] = q[:, c * LANES:(c + 1) * LANES]


def _merge(sb, nsa, mem, gates_slab, b_merge, x2d, w_sb, w_nsa, w_mem, w_out, ffn_g, w_q, tm=256):
    n, d = x2d.shape
    nq = w_q.shape[1] // LANES
    row = lambda w: pl.BlockSpec((tm, w), lambda i: (i, 0))
    full = lambda a: pl.BlockSpec(a.shape, lambda i: (0,) * a.ndim)
    bm = b_merge.reshape(1, -1)
    fg = ffn_g.reshape(1, d)
    return pl.pallas_call(
        _merge_kernel,
        grid=(n // tm,),
        in_specs=[row(sb.shape[1]), row(nsa.shape[1]), row(mem.shape[1]), row(N_BRANCHES * d),
                  full(bm), row(d), full(w_sb), full(w_nsa), full(w_mem), full(w_out), full(fg),
                  full(w_q)],
        out_specs=[row(d), row(d), pl.BlockSpec((nq, tm, LANES), lambda i: (0, i, 0))],
        out_shape=[jax.ShapeDtypeStruct((n, d), F32), jax.ShapeDtypeStruct((n, d), F32),
                   jax.ShapeDtypeStruct((nq, n, LANES), BF16)],
        compiler_params=_params("parallel"),
        name="merge",
    )(sb, nsa, mem, gates_slab, bm, x2d, w_sb, w_nsa, w_mem, w_out, fg, w_q)


def _peer_topk_kernel(q_ref, sk_ref, idx_ref, gw_ref, tv_ref, ti_ref, bv_ref):
    tt = q_ref.shape[1]
    kk = PEER_TOPK
    nk = PEER_N_KEYS
    rid = lax.broadcasted_iota(I32, (nk, tt), 0).astype(F32)
    for c in range(2 * PEER_HEADS):
        scores = _dot_nt(sk_ref[c], q_ref[c])

        def pick(k, s, c=c):
            best = jnp.max(s, axis=0, keepdims=True)
            first = jnp.min(jnp.where(s == best, rid, float(nk)), axis=0, keepdims=True)
            tv_ref[c, pl.ds(k, 1), :] = best
            ti_ref[c, pl.ds(k, 1), :] = first
            return jnp.where(rid == first, -jnp.inf, s)

        lax.fori_loop(0, kk, pick, scores)

    sub = 8
    widths = [min(kk, -(-(kk // (i + 1)) // sub) * sub) for i in range(kk // 2)]
    n_cand = sum(widths) + kk // 2
    pos = lax.broadcasted_iota(I32, (n_cand, tt), 0).astype(F32)
    for h in range(PEER_HEADS):
        s0, s1 = tv_ref[2 * h], tv_ref[2 * h + 1]
        i0, i1 = ti_ref[2 * h], ti_ref[2 * h + 1]
        cand, cidx = [], []
        for i, wd in enumerate(widths):
            keep = lax.broadcasted_iota(I32, (wd, 1), 0) < kk // (i + 1)
            cand.append(jnp.where(keep, s0[i:i + 1, :] + s1[0:wd, :], -jnp.inf))
            cidx.append(i0[i:i + 1, :] * float(nk) + i1[0:wd, :])
        cand.append(s0[kk // 2:, :] + s1[0:1, :])
        cidx.append(i0[kk // 2:, :] * float(nk) + i1[0:1, :])
        cand = jnp.concatenate(cand, axis=0)
        cidx = jnp.concatenate(cidx, axis=0)

        def pick2(k, s, h=h, cidx=cidx):
            best = jnp.max(s, axis=0, keepdims=True)
            first = jnp.min(jnp.where(s == best, pos, float(n_cand)), axis=0, keepdims=True)
            hit = pos == first
            bv_ref[pl.ds(k, 1), :] = best
            expert = jnp.max(jnp.where(hit, cidx, -1.0), axis=0, keepdims=True)
            idx_ref[pl.ds(h * kk + k, 1), :] = expert.astype(I32)
            return jnp.where(hit, -jnp.inf, s)

        lax.fori_loop(0, kk, pick2, cand)
        best = bv_ref[...]
        e = jnp.exp(best - best[0:1, :])
        gw_ref[h * kk:(h + 1) * kk, :] = e / jnp.sum(e, axis=0, keepdims=True)


def _peer_topk(q_chunks, subkeys, tt=256):
    nchunk, n, half = q_chunks.shape
    slots = PEER_HEADS * PEER_TOPK
    return pl.pallas_call(
        _peer_topk_kernel,
        grid=(n // tt,),
        in_specs=[
            pl.BlockSpec((nchunk, tt, half), lambda i: (0, i, 0)),
            pl.BlockSpec(subkeys.shape, lambda i: (0, 0, 0)),
        ],
        out_specs=[pl.BlockSpec((slots, tt), lambda i: (0, i)),
                   pl.BlockSpec((slots, tt), lambda i: (0, i))],
        out_shape=[jax.ShapeDtypeStruct((slots, n), I32), jax.ShapeDtypeStruct((slots, n), F32)],
        scratch_shapes=[pltpu.VMEM((nchunk, PEER_TOPK, tt), F32),
                        pltpu.VMEM((nchunk, PEER_TOPK, tt), F32),
                        pltpu.VMEM((PEER_TOPK, tt), F32)],
        compiler_params=_params("parallel"),
        name="peer_topk",
    )(q_chunks, subkeys)


def _peer_ffn_kernel(*refs, first_tiles, staged):
    ns = len(staged)
    idx_hbm, tables = refs[0], refs[1:1 + ns]
    io = [refs[1 + ns + 3 * s:4 + ns + 3 * s] for s in range(ns)]
    fg_ref = refs[1 + 4 * ns]
    ys = refs[2 + 4 * ns:2 + 5 * ns]
    idx_smem, bufs = refs[2 + 5 * ns], refs[3 + 5 * ns:3 + 6 * ns]
    idx_sem, sems = refs[3 + 6 * ns], refs[4 + 6 * ns:4 + 7 * ns]
    tt, d = io[0][1].shape
    slots = io[0][0].shape[0]
    rows = d // (2 * LANES)
    tile = pl.program_id(0)
    nbuf = bufs[0].shape[0]
    tok_lane = lax.broadcasted_iota(I32, (1, tt), 1)

    def halves(words):
        return tuple(pltpu.unpack_elementwise(words, index=i, packed_dtype=BF16, unpacked_dtype=F32)
                     for i in range(2))

    def make_stream(s):
        table, buf, row_sem, y_ref = tables[s], bufs[s], sems[s], ys[s]
        gw_ref, h_ref, _ = io[s]

        if staged[s]:
            def start(tok, slot):
                for r in range(buf.shape[1]):
                    pltpu.make_async_copy(table.at[tile * tt + tok, :, r, :], buf.at[slot, r],
                                          row_sem.at[slot]).start(priority=r % 2)
        else:
            idx_copy = pltpu.make_async_copy(idx_hbm.at[tile + first_tiles[s]], idx_smem, idx_sem)
            idx_copy.start()
            idx_copy.wait()

            def start(tok, slot):
                for e in range(slots):
                    pltpu.make_async_copy(table.at[idx_smem[tok * slots + e]], buf.at[slot, :, e, :],
                                          row_sem.at[slot]).start(priority=e % 2)

        def wait(slot):
            pltpu.make_async_copy(buf.at[slot], buf.at[slot], row_sem.at[slot]).wait()

        def expert_weights(tok, slot):
            h = h_ref[pl.ds(tok, 1), :]
            prod = jnp.zeros((slots, LANES), F32)
            for r in range(rows):
                lo, hi = halves(buf[slot, r])
                prod = prod + lo * h[:, r * LANES:(r + 1) * LANES]
                prod = prod + hi * h[:, d // 2 + r * LANES:d // 2 + (r + 1) * LANES]
            a = jnp.sum(prod, axis=1, keepdims=True)
            act = 0.5 * a * (1.0 + lax.erf(a * (2.0 ** -0.5)))
            gate = jnp.sum(jnp.where(tok_lane == tok, gw_ref[...], 0.0), axis=1, keepdims=True)
            return gate * act

        def weighted_values(tok, slot, w):
            parts = [halves(buf[slot, rows + r]) for r in range(rows)]
            y_ref[pl.ds(tok, 1), :] = jnp.concatenate(
                [jnp.sum(parts[r][i] * w, axis=0, keepdims=True) for i in range(2) for r in range(rows)],
                axis=1)

        def step(tok, slot, w, refill):
            nxt = (slot + 1) % nbuf
            wait(nxt)
            w_next = expert_weights(tok + 1, nxt)
            weighted_values(tok, slot, w)
            if refill:
                start(tok + nbuf, slot)
            return w_next

        return start, wait, expert_weights, weighted_values, step

    streams = [make_stream(s) for s in range(ns)]

    ws = []
    for start, wait, expert_weights, _, _ in streams:
        for tok in range(nbuf):
            start(tok, tok)
    for start, wait, expert_weights, _, _ in streams:
        wait(0)
        ws.append(expert_weights(0, 0))

    def round_of_steps(i, ws):
        ws = list(ws)
        for slot in range(nbuf):
            for s in range(ns):
                ws[s] = streams[s][4](i * nbuf + slot, slot, ws[s], True)
        return tuple(ws)

    n_rounds = tt // nbuf - 1
    ws = list(lax.fori_loop(0, n_rounds, round_of_steps, tuple(ws)))
    for slot in range(nbuf - 1):
        for s in range(ns):
            ws[s] = streams[s][4](n_rounds * nbuf + slot, slot, ws[s], False)
    for s in range(ns):
        streams[s][3](tt - 1, nbuf - 1, ws[s])
        ys[s][...] = _rms(io[s][2][...] + ys[s][...], fg_ref[...])


def _pack_tables_kernel(u_ref, v_ref, o_ref):
    d = u_ref.shape[1]
    rows = d // (2 * LANES)
    for base, src in ((0, u_ref), (rows, v_ref)):
        for r in range(rows):
            lo = src[:, r * LANES:(r + 1) * LANES]
            hi = src[:, d // 2 + r * LANES:d // 2 + (r + 1) * LANES]
            o_ref[:, base + r, :] = pltpu.pack_elementwise([lo, hi], packed_dtype=BF16)


def _pack_tables(u, v, te=512):
    n_exp, d = u.shape
    rows = d // LANES
    return pl.pallas_call(
        _pack_tables_kernel,
        grid=(n_exp // te,),
        in_specs=[pl.BlockSpec((te, d), lambda i: (i, 0)), pl.BlockSpec((te, d), lambda i: (i, 0))],
        out_specs=pl.BlockSpec((te, rows, LANES), lambda i: (i, 0, 0)),
        out_shape=jax.ShapeDtypeStruct((n_exp, rows, LANES), jnp.uint32),
        compiler_params=_params("parallel"),
        name="pack_tables",
    )(u, v)


def _sc_gather_slabs(slabs, slab_idx):
    m = slab_idx.shape[0]
    mesh = plsc.VectorSubcoreMesh(core_axis_name="core", subcore_axis_name="subcore")
    idx_rows = jnp.pad(slab_idx.reshape(m // SC_GATHER_WINDOW, SC_GATHER_WINDOW),
                       ((0, 0), (0, LANES - SC_GATHER_WINDOW)))

    @pl.kernel(out_type=jax.ShapeDtypeStruct((m,) + slabs.shape[1:], slabs.dtype), mesh=mesh)
    def gather(slabs_hbm, idx_hbm, out_hbm):
        def window(idx_vmem, out_vmem):
            pltpu.sync_copy(slabs_hbm.at[idx_vmem.at[0, pl.ds(0, SC_GATHER_WINDOW)]], out_vmem)

        pltpu.emit_pipeline(
            window,
            grid=(m // SC_GATHER_WINDOW,),
            in_specs=[pl.BlockSpec((1, LANES), index_map=lambda i: (i, 0))],
            out_specs=[pl.BlockSpec((SC_GATHER_WINDOW,) + slabs.shape[1:], index_map=lambda i: (i, 0, 0))],
            core_axis_name=("core", "subcore"),
            dimension_semantics=(pltpu.PARALLEL,),
            trace_scopes=False,
        )(idx_hbm, out_hbm)

    return gather(slabs, idx_rows)


def _peer_ffn(idx_tiles, tables, gw, h2, x1, final_g, tt, first_tiles, n_tiles, staged):
    _, d = h2.shape
    slots = gw.shape[0]
    rows = d // LANES
    ns = len(staged)
    assert sum(not st for st in staged) <= 1, "one index buffer: at most one gathering stream"
    hbm = pl.BlockSpec(memory_space=pl.ANY)
    io_specs, io_args = [], []
    for s in range(ns):
        io_specs += [pl.BlockSpec((slots, tt), lambda i, f=first_tiles[s]: (0, i + f)),
                     pl.BlockSpec((tt, d), lambda i, f=first_tiles[s]: (i + f, 0)),
                     pl.BlockSpec((tt, d), lambda i, f=first_tiles[s]: (i + f, 0))]
        io_args += [gw, h2, x1]
    buf = pltpu.VMEM((PEER_GATHER_BUFFERS, rows, slots, LANES), tables[0].dtype)
    sem = pltpu.SemaphoreType.DMA((PEER_GATHER_BUFFERS,))
    outs = pl.pallas_call(
        functools.partial(_peer_ffn_kernel, first_tiles=tuple(first_tiles), staged=tuple(staged)),
        grid=(n_tiles,),
        in_specs=[hbm] + [hbm] * ns + io_specs + [pl.BlockSpec((1, d), lambda i: (0, 0))],
        out_specs=[pl.BlockSpec((tt, d), lambda i: (i, 0))] * ns,
        out_shape=[jax.ShapeDtypeStruct((n_tiles * tt, d), F32)] * ns,
        scratch_shapes=[pltpu.SMEM((tt * slots,), I32)] + [buf] * ns + [pltpu.SemaphoreType.DMA] + [sem] * ns,
        compiler_params=_params("arbitrary"),
        name="peer_ffn_" + "_".join("staged" if st else "gather" for st in staged),
    )(idx_tiles, *tables, *io_args, final_g.reshape(1, d))
    return list(outs)


def _mixers(x, mem, mix_g, mem_g, w_in, b_merge, pe_k, pe_v, cw_k, cw_v, w_mem_kv, w_sb_br, w_nsa_br,
            w_mem_br, w_out, ffn_g, peer_w_q, subkeys):
    b, t, d = x.shape
    m = mem.shape[1]
    g = NSA_KV_GROUPS
    n = b * t
    x2d = x.reshape(n, d)

    o_sbq, o_sbk, o_sbv = 0, SB_W, 2 * SB_W
    o_nq = 3 * SB_W
    o_nkv = o_nq + NSA_W
    o_ng = o_nkv + 6 * NSA_KV_W
    o_mq = o_ng + NSA_HEADS * 3
    o_mg = o_mq + MEM_W
    w_act = jnp.concatenate([w_in[:, :o_ng], w_in[:, o_mq:o_mg]], axis=1).astype(BF16)
    gate_pad = LANES - NSA_HEADS * 3
    w_gate = jnp.concatenate([w_in[:, o_mg:], w_in[:, o_ng:o_mq], jnp.zeros((d, gate_pad), w_in.dtype)],
                             axis=1).astype(BF16)
    hd = _norm_matmul_heads(x2d, mix_g, w_act, tm=256, tn=512).reshape(-1, b, t, HEAD_DIM)
    gates_slab = _norm_matmul(x2d, mix_g, w_gate, F32, tm=256, tn=640)
    head_of = lambda col: col // HEAD_DIM

    sb_out = _sb_attention(hd, head_of(o_sbq), head_of(o_sbk), head_of(o_sbv), SB_HEADS)

    kc = _nsa_compress(hd, head_of(o_nkv), pe_k, cw_k)
    vc = _nsa_compress(hd, head_of(o_nkv) + g, pe_v, cw_v)
    gl = gates_slab[:, N_BRANCHES * d:N_BRANCHES * d + NSA_HEADS * 3]
    gl = gl.reshape(b, t, g, NSA_GROUP * 3).transpose(0, 2, 1, 3)
    gl = jnp.pad(gl, ((0, 0), (0, 0), (0, 0), (0, LANES - NSA_GROUP * 3)))
    slopes = jnp.asarray([2.0 ** (-8.0 * (h + 1) / NSA_HEADS) for h in range(NSA_HEADS)], F32)
    nsa_out = _nsa_attention(hd, head_of(o_nq), head_of(o_nkv) + 2 * g, kc, vc, gl, slopes)
    nsa_out = nsa_out.transpose(0, 2, 1, 3).reshape(n, NSA_W)

    mkv = _norm_matmul(mem.reshape(b * m, d), mem_g, w_mem_kv.astype(BF16), BF16, tm=256, tn=512)
    mem_out = _mem_attention(hd, head_of(o_nkv + 6 * NSA_KV_W), mkv.reshape(b, m, 2 * MEM_W))

    x1, h2, q_chunks = _merge(
        sb_out.reshape(n, SB_W), nsa_out, mem_out.reshape(n, MEM_W), gates_slab, b_merge, x2d,
        w_sb_br.astype(BF16), w_nsa_br.astype(BF16), w_mem_br.astype(BF16), w_out.astype(BF16), ffn_g,
        peer_w_q.astype(BF16))

    half = PEER_QUERY_DIM // 2
    idx, gw = _peer_topk(q_chunks, subkeys.reshape(2 * PEER_HEADS, PEER_N_KEYS, half).astype(BF16))
    return x1, h2, idx.T, gw


def _layer(x, mem, *params_and_tables):
    uv, out_g = params_and_tables[-2:]
    b, t, d = x.shape
    tt = 128
    x1, h2, idx_tok, gw = _mixers(x, mem, *params_and_tables[:-2])
    n, slots = idx_tok.shape
    n_tiles = n // tt
    idx_tiles = idx_tok.reshape(n_tiles, tt * slots)
    chunk = SC_CHUNK_TILES
    sc_first = n_tiles - SC_CHUNKS * chunk
    lead = sc_first - SC_CHUNKS * chunk
    assert lead > 0

    def stage(c):
        lo = (sc_first + c * chunk) * tt
        slabs = _sc_gather_slabs(uv, idx_tok[lo:lo + chunk * tt].reshape(-1))
        return slabs.reshape((chunk * tt, slots) + uv.shape[1:])

    staged = [stage(c) for c in range(SC_CHUNKS)]
    ffn = functools.partial(_peer_ffn, idx_tiles, gw=gw, h2=h2, x1=x1, final_g=out_g, tt=tt)
    own = ffn(tables=[uv], first_tiles=[0], n_tiles=lead, staged=[False])
    theirs = []
    for c in range(SC_CHUNKS):
        mine, other = ffn(tables=[uv, staged[c]], first_tiles=[lead + c * chunk, sc_first + c * chunk],
                          n_tiles=chunk, staged=[False, True])
        own.append(mine)
        theirs.append(other)
    return jnp.concatenate(own + theirs, axis=0).reshape(b, t, d)


def kernel(x, mem, mix_norm_g, mem_norm_g, w_in, b_merge, cmp_pe_k, cmp_pe_v, cmp_w_k, cmp_w_v, w_mem_kv, w_sb_br, w_nsa_br, w_mem_br, w_out, ffn_norm_g, peer_w_q, peer_subkeys, peer_u, peer_v, final_norm_g):
    depth = w_in.shape[0]
    assert depth == 1, "the final rmsnorm is fused into the last layer's PEER kernel"
    l = 0
    uv = _pack_tables(peer_u[l], peer_v[l])
    return _layer(x, mem, mix_norm_g[l], mem_norm_g[l], w_in[l], b_merge[l], cmp_pe_k[l], cmp_pe_v[l],
                  cmp_w_k[l], cmp_w_v[l], w_mem_kv[l], w_sb_br[l], w_nsa_br[l], w_mem_br[l], w_out[l],
                  ffn_norm_g[l], peer_w_q[l], peer_subkeys[l], uv, final_norm_g)
```

```python
import functools
import math

import jax
import jax.numpy as jnp
from jax import lax
from jax.experimental import pallas as pl
from jax.experimental.pallas import tpu as pltpu
from jax.experimental.pallas import tpu_sc as plsc

F32 = jnp.float32
BF16 = jnp.bfloat16
I32 = jnp.int32

HEAD_DIM = 64
SB_HEADS = 6
NSA_HEADS = 6
NSA_KV_GROUPS = 2
NSA_GROUP = NSA_HEADS // NSA_KV_GROUPS
MEM_HEADS = 4
N_BRANCHES = 3
SB_W = SB_HEADS * HEAD_DIM
NSA_W = NSA_HEADS * HEAD_DIM
NSA_KV_W = NSA_KV_GROUPS * HEAD_DIM
MEM_W = MEM_HEADS * HEAD_DIM
CMP_LEN = 32
CMP_STRIDE = 16
SEL_BLOCK = 64
N_SELECT = 16
WINDOW = 512
FORCED_SCORE = 1e4
PEER_HEADS = 8
PEER_N_KEYS = 128
PEER_QUERY_DIM = 256
PEER_TOPK = 16
RMS_EPS = 1e-6
NEG_INF = -1e30
SCALE = HEAD_DIM ** -0.5
SB_DEAD_LOG = 104.0

LANES = 128
PEER_GATHER_BUFFERS = 8
SC_GATHER_WINDOW = 32
SC_CHUNKS = 4
SC_CHUNK_TILES = 28
VMEM_LIMIT_BYTES = 56 * 1024 * 1024

_NT = (((1,), (1,)), ((), ()))


def _params(*sem):
    return pltpu.CompilerParams(dimension_semantics=sem, vmem_limit_bytes=VMEM_LIMIT_BYTES)


def _dot(a, b):
    return jnp.dot(a, b, preferred_element_type=F32)


def _dot_nt(a, b):
    return lax.dot_general(a, b, _NT, preferred_element_type=F32)


def _sigmoid(x):
    return 1.0 / (1.0 + jnp.exp(-x))


def _rms(x, g):
    return x * lax.rsqrt(jnp.mean(x * x, axis=-1, keepdims=True) + RMS_EPS) * g


def _norm_matmul_kernel(x_ref, g_ref, w_ref, o_ref, *, tn):
    h = _rms(x_ref[...], g_ref[...]).astype(BF16)
    for c in range(0, o_ref.shape[1], tn):
        o_ref[:, c:c + tn] = _dot(h, w_ref[:, c:c + tn]).astype(o_ref.dtype)


def _norm_matmul(x2d, g, w, out_dtype, tm, tn):
    n, d = x2d.shape
    m = w.shape[1]
    return pl.pallas_call(
        functools.partial(_norm_matmul_kernel, tn=tn),
        grid=(n // tm,),
        in_specs=[
            pl.BlockSpec((tm, d), lambda i: (i, 0)),
            pl.BlockSpec((1, d), lambda i: (0, 0)),
            pl.BlockSpec((d, m), lambda i: (0, 0)),
        ],
        out_specs=pl.BlockSpec((tm, m), lambda i: (i, 0)),
        out_shape=jax.ShapeDtypeStruct((n, m), out_dtype),
        compiler_params=_params("parallel"),
        name="norm_matmul",
    )(x2d, g.reshape(1, d), w)


def _norm_matmul_heads_kernel(x_ref, g_ref, w_ref, o_ref, *, tn):
    h = _rms(x_ref[...], g_ref[...]).astype(BF16)
    per = tn // HEAD_DIM
    for c in range(0, w_ref.shape[1], tn):
        res = _dot(h, w_ref[:, c:c + tn]).astype(o_ref.dtype)
        for j in range(per):
            o_ref[c // HEAD_DIM + j] = res[:, j * HEAD_DIM:(j + 1) * HEAD_DIM]


def _norm_matmul_heads(x2d, g, w, tm, tn):
    n, d = x2d.shape
    m = w.shape[1]
    nh = m // HEAD_DIM
    return pl.pallas_call(
        functools.partial(_norm_matmul_heads_kernel, tn=tn),
        grid=(n // tm,),
        in_specs=[
            pl.BlockSpec((tm, d), lambda i: (i, 0)),
            pl.BlockSpec((1, d), lambda i: (0, 0)),
            pl.BlockSpec((d, m), lambda i: (0, 0)),
        ],
        out_specs=pl.BlockSpec((nh, tm, HEAD_DIM), lambda i: (0, i, 0)),
        out_shape=jax.ShapeDtypeStruct((nh, n, HEAD_DIM), BF16),
        compiler_params=_params("parallel"),
        name="norm_matmul_heads",
    )(x2d, g.reshape(1, d), w)


def _sb_kernel(q_ref, k_ref, v_ref, o_ref, *, tile):
    qi = pl.program_id(2)
    nh = q_ref.shape[0]
    row = lax.broadcasted_iota(I32, (tile, tile), 0)
    col = lax.broadcasted_iota(I32, (tile, tile), 1)
    lower = row > col
    later = lower.astype(BF16)
    qs = [q_ref[hh, 0] for hh in range(nh)]

    def visit(hh, ks, c, acc, diagonal):
        k = k_ref[hh, 0, pl.ds(ks, tile), :]
        v = v_ref[hh, 0, pl.ds(ks, tile), :]
        z = _dot_nt(qs[hh], k) * SCALE
        sp = jnp.maximum(z, 0.0) + jnp.log(1.0 + jnp.exp(-jnp.abs(z)))
        if diagonal:
            sp = jnp.where(lower, sp, 0.0)
        hi = sp.astype(BF16)
        lo = (sp - hi.astype(F32)).astype(BF16)
        after = _dot(hi, later) + _dot(lo, later)
        a = jnp.exp(z - sp - after - c)
        if diagonal:
            a = jnp.where(lower, a, 0.0)
        return c + jnp.sum(sp, axis=1, keepdims=True), acc + _dot(a.astype(BF16), v)

    q0 = pl.multiple_of(qi * tile, tile)
    state = []
    for hh in range(nh):
        state.extend(visit(hh, q0, jnp.zeros((tile, 1), F32), jnp.zeros((tile, HEAD_DIM), F32), True))

    def smallest_carry(state):
        c = state[0]
        for hh in range(1, nh):
            c = jnp.minimum(c, state[2 * hh])
        return jnp.min(c)

    def live(carry):
        i, c_min, _ = carry
        return (i < qi) & (c_min <= SB_DEAD_LOG)

    def body(carry):
        i, _, state = carry
        ks = pl.multiple_of((qi - 1 - i) * tile, tile)
        out = []
        for hh in range(nh):
            out.extend(visit(hh, ks, state[2 * hh], state[2 * hh + 1], False))
        return i + 1, smallest_carry(out), tuple(out)

    _, _, state = lax.while_loop(live, body, (0, smallest_carry(state), tuple(state)))
    o_ref[0] = jnp.concatenate([state[2 * hh + 1] for hh in range(nh)], axis=1).astype(o_ref.dtype)


def _sb_attention(hd, q0, k0, v0, h, tile=256):
    _, b, t, dh = hd.shape
    hp = 2
    tq = tile
    assert q0 % hp == 0 and k0 % hp == 0 and v0 % hp == 0
    return pl.pallas_call(
        functools.partial(_sb_kernel, tile=tile),
        grid=(b, h // hp, t // tq),
        in_specs=[
            pl.BlockSpec((hp, 1, tq, dh), lambda bi, hi, qi: (q0 // hp + hi, bi, qi, 0)),
            pl.BlockSpec((hp, 1, t, dh), lambda bi, hi, qi: (k0 // hp + hi, bi, 0, 0)),
            pl.BlockSpec((hp, 1, t, dh), lambda bi, hi, qi: (v0 // hp + hi, bi, 0, 0)),
        ],
        out_specs=pl.BlockSpec((1, tq, hp * dh), lambda bi, hi, qi: (bi, qi, hi)),
        out_shape=jax.ShapeDtypeStruct((b, t, h * dh), BF16),
        compiler_params=_params("parallel", "parallel", "arbitrary"),
        name="sb_attn",
    )(hd, hd, hd)


def _compress_kernel(x_ref, pe_ref, w_ref, o_ref):
    x = x_ref[0, 0]
    nc = x.shape[0]
    w_lo, w_hi = w_ref[0], w_ref[1]
    first = _dot(x, w_lo)
    second = _dot(x, w_hi)
    feat = pe_ref.shape[1]
    pe_lo = jnp.broadcast_to(pe_ref[0:1, :], (8, feat)).astype(BF16)
    pe_hi = jnp.broadcast_to(pe_ref[1:2, :], (8, feat)).astype(BF16)
    bias = _dot(pe_lo, w_lo)[0:1] + _dot(pe_hi, w_hi)[0:1]
    o_ref[0, 0] = (first + pltpu.roll(second, nc - 1, 0) + bias).astype(o_ref.dtype)


def _nsa_compress(hd, head0, pe, w):
    nh, b, t, dh = hd.shape
    g = NSA_KV_GROUPS
    nc = t // CMP_STRIDE
    feat = CMP_STRIDE * dh
    x = hd[head0:head0 + g].reshape(g, b, nc, feat)
    pe2 = pe.reshape(2, feat)
    w2 = w.reshape(2, feat, dh).astype(BF16)
    return pl.pallas_call(
        _compress_kernel,
        grid=(b, g),
        in_specs=[
            pl.BlockSpec((1, 1, nc, feat), lambda bi, gi: (gi, bi, 0, 0)),
            pl.BlockSpec((2, feat), lambda bi, gi: (0, 0)),
            pl.BlockSpec((2, feat, dh), lambda bi, gi: (0, 0, 0)),
        ],
        out_specs=pl.BlockSpec((1, 1, nc, dh), lambda bi, gi: (bi, gi, 0, 0)),
        out_shape=jax.ShapeDtypeStruct((b, g, nc, dh), BF16),
        compiler_params=_params("parallel", "parallel"),
        name="nsa_compress",
    )(x, pe2, w2)


def _nsa_kernel(slopes_ref, q_ref, kc_ref, vc_ref, ks_ref, vs_ref, kw_ref, vw_ref, gl_ref, pool_ref, ex_ref,
                o_ref, used_ref, *, tq, tk, n_sel, real_blocks):
    grp = pl.program_id(1)
    t0 = pl.program_id(2) * tq
    nc = kc_ref.shape[2]
    rr = NSA_GROUP
    t = t0 + lax.broadcasted_iota(I32, (tq, 1), 0)

    def stack(x):
        return jnp.concatenate([x] * rr, axis=0)

    q = q_ref[:, 0].reshape(rr * tq, HEAD_DIM)
    slope = jnp.concatenate([jnp.full((tq, 1), slopes_ref[grp * rr + r], F32) for r in range(rr)], axis=0)

    def heads_sum(x):
        out = x[0:tq]
        for r in range(1, rr):
            out = out + x[r * tq:(r + 1) * tq]
        return out

    def masked_softmax(s, valid):
        p = jnp.where(valid, jnp.exp(s - jnp.max(s, axis=1, keepdims=True)), 0.0)
        denom = jnp.sum(p, axis=1, keepdims=True)
        return p / jnp.where(denom > 0, denom, 1.0)

    lane = lax.broadcasted_iota(I32, (1, nc), 1)
    dist_c = (t - (lane * CMP_STRIDE + (CMP_LEN - 1))).astype(F32)
    valid_c = stack(dist_c >= 0)
    s = _dot_nt(q, kc_ref[0, 0]) * SCALE - slope * stack(dist_c)
    p = masked_softmax(jnp.where(valid_c, s, NEG_INF), valid_c)
    o_cmp = _dot(p.astype(BF16), vc_ref[0, 0])
    psum = heads_sum(p)

    n_blk = pool_ref.shape[0]
    hi = psum.astype(BF16)
    rest = psum - hi.astype(F32)
    mid = rest.astype(BF16)
    lo = (rest - mid.astype(F32)).astype(BF16)
    pool = pool_ref[...]
    imp = _dot_nt(pool, hi) + _dot_nt(pool, mid) + _dot_nt(pool, lo)
    blk = lax.broadcasted_iota(I32, (n_blk, 1), 0)
    cur = (t0 + lax.broadcasted_iota(I32, (1, tq), 1)) // SEL_BLOCK
    forced = (blk == 0) | (blk == cur) | (blk == cur - 1)
    imp = jnp.where(forced, FORCED_SCORE, jnp.where(blk <= cur, imp, -1.0))
    imp = jnp.where(blk < real_blocks, imp, -jnp.inf)
    blk_f = blk.astype(F32)

    def pick(_, carry):
        imp, sel = carry
        best = jnp.max(imp, axis=0, keepdims=True)
        first = jnp.min(jnp.where(imp == best, blk_f, float(n_blk)), axis=0, keepdims=True)
        hit = blk_f == first
        return jnp.where(hit, -jnp.inf, imp), jnp.where(hit, 1.0, sel)

    _, sel = lax.fori_loop(0, n_sel, pick, (imp, jnp.zeros((n_blk, tq), F32)))
    picked = jnp.max(sel, axis=1, keepdims=True)
    per_tile = tk // SEL_BLOCK
    n_causal = (t0 + tq + tk - 1) // tk
    n_used = jnp.int32(0)
    for j in range(used_ref.shape[0]):
        used_ref[n_used] = j
        hit = (jnp.max(picked[j * per_tile:(j + 1) * per_tile, :]) > 0) & (j < n_causal)
        n_used = n_used + hit.astype(I32)
    sel = sel.T.astype(BF16)

    span = WINDOW + tq
    w0 = pl.multiple_of(jnp.maximum(t0 - WINDOW, 0), tq)
    dist_w = t - (w0 + lax.broadcasted_iota(I32, (1, span), 1))
    valid_w = stack((dist_w >= 0) & (dist_w < WINDOW))
    s = _dot_nt(q, kw_ref[0, 0, pl.ds(w0, span), :]) * SCALE - slope * stack(dist_w.astype(F32))
    p = masked_softmax(jnp.where(valid_w, s, NEG_INF), valid_w)
    o_win = _dot(p.astype(BF16), vw_ref[0, 0, pl.ds(w0, span), :])

    kcol = lax.broadcasted_iota(I32, (1, tk), 1)

    def sel_body(i, carry):
        m, l, acc = carry
        kb = used_ref[i]
        ks0 = pl.multiple_of(kb * tk, tk)
        chosen = _dot(sel, ex_ref[kb]) > 0.5
        dist = t - (ks0 + kcol)
        valid = stack(chosen & (dist >= 0))
        s = _dot_nt(q, ks_ref[0, 0, pl.ds(ks0, tk), :]) * SCALE - slope * stack(dist.astype(F32))
        s = jnp.where(valid, s, NEG_INF)
        m_new = jnp.maximum(m, jnp.max(s, axis=1, keepdims=True))
        alpha = jnp.exp(m - m_new)
        p = jnp.where(valid, jnp.exp(s - m_new), 0.0)
        l = alpha * l + jnp.sum(p, axis=1, keepdims=True)
        acc = alpha * acc + _dot(p.astype(BF16), vs_ref[0, 0, pl.ds(ks0, tk), :])
        return m_new, l, acc

    init = (jnp.full((rr * tq, 1), NEG_INF, F32), jnp.zeros((rr * tq, 1), F32),
            jnp.zeros((rr * tq, HEAD_DIM), F32))
    _, l_sel, acc_sel = lax.fori_loop(0, n_used, sel_body, init)
    o_sel = acc_sel / l_sel

    gates = _sigmoid(gl_ref[0, 0])
    outs = []
    for r in range(rr):
        rows = slice(r * tq, (r + 1) * tq)
        outs.append(gates[:, 3 * r:3 * r + 1] * o_cmp[rows] + gates[:, 3 * r + 1:3 * r + 2] * o_sel[rows]
                    + gates[:, 3 * r + 2:3 * r + 3] * o_win[rows])
    o_ref[0, 0] = jnp.concatenate(outs, axis=1).astype(o_ref.dtype)


def _nsa_attention(hd, q0, sel0, kc, vc, gate_logits, slopes, tq=256, tk=256):
    _, b, t, dh = hd.shape
    g = NSA_KV_GROUPS
    assert q0 % NSA_GROUP == 0
    nc = kc.shape[2]
    n_sel = min(N_SELECT, t // SEL_BLOCK)
    real_blocks = t // SEL_BLOCK
    n_blk = LANES
    assert t % tk == 0 and t >= WINDOW + tq and real_blocks <= n_blk
    blk_ids = jnp.arange(n_blk, dtype=I32)
    pool = (blk_ids[:, None] == jnp.arange(nc, dtype=I32)[None, :] // (SEL_BLOCK // CMP_STRIDE)).astype(BF16)
    key_blk = (jnp.arange(t, dtype=I32) // SEL_BLOCK).reshape(t // tk, 1, tk)
    expand = (blk_ids[None, :, None] == key_blk).astype(BF16)
    def kv_spec(j):
        return pl.BlockSpec((1, 1, t, dh), lambda bi, gi, qi: (sel0 + j * g + gi, bi, 0, 0))

    c_spec = pl.BlockSpec((1, 1, nc, dh), lambda bi, gi, qi: (bi, gi, 0, 0))
    return pl.pallas_call(
        functools.partial(_nsa_kernel, tq=tq, tk=tk, n_sel=n_sel, real_blocks=real_blocks),
        grid=(b, g, t // tq),
        in_specs=[
            pl.BlockSpec(memory_space=pltpu.SMEM),
            pl.BlockSpec((NSA_GROUP, 1, tq, dh), lambda bi, gi, qi: (q0 // NSA_GROUP + gi, bi, qi, 0)),
            c_spec, c_spec, kv_spec(0), kv_spec(1), kv_spec(2), kv_spec(3),
            pl.BlockSpec((1, 1, tq, LANES), lambda bi, gi, qi: (bi, gi, qi, 0)),
            pl.BlockSpec(pool.shape, lambda bi, gi, qi: (0, 0)),
            pl.BlockSpec(expand.shape, lambda bi, gi, qi: (0, 0, 0)),
        ],
        out_specs=pl.BlockSpec((1, 1, tq, NSA_GROUP * dh), lambda bi, gi, qi: (bi, gi, qi, 0)),
        out_shape=jax.ShapeDtypeStruct((b, g, t, NSA_GROUP * dh), BF16),
        scratch_shapes=[pltpu.SMEM((t // tk,), I32)],
        compiler_params=_params("parallel", "parallel", "arbitrary"),
        name="nsa_attn",
    )(slopes, hd, kc, vc, hd, hd, hd, hd, gate_logits, pool, expand)


def _mem_kernel(q_ref, kv_ref, o_ref):
    kv = kv_ref[0]
    outs = []
    for h in range(MEM_HEADS):
        sl = slice(h * HEAD_DIM, (h + 1) * HEAD_DIM)
        s = _dot_nt(q_ref[h, 0], kv[:, sl]) * SCALE
        p = jnp.exp(s - jnp.max(s, axis=1, keepdims=True))
        p = p / jnp.sum(p, axis=1, keepdims=True)
        outs.append(_dot(p.astype(BF16), kv[:, MEM_W + h * HEAD_DIM:MEM_W + (h + 1) * HEAD_DIM]))
    o_ref[0] = jnp.concatenate(outs, axis=1).astype(o_ref.dtype)


def _mem_attention(hd, q0, mkv, tq=512):
    _, b, t, dh = hd.shape
    w = MEM_HEADS * dh
    m = mkv.shape[1]
    assert q0 % MEM_HEADS == 0
    return pl.pallas_call(
        _mem_kernel,
        grid=(b, t // tq),
        in_specs=[
            pl.BlockSpec((MEM_HEADS, 1, tq, dh), lambda bi, qi: (q0 // MEM_HEADS, bi, qi, 0)),
            pl.BlockSpec((1, m, 2 * w), lambda bi, qi: (bi, 0, 0)),
        ],
        out_specs=pl.BlockSpec((1, tq, w), lambda bi, qi: (bi, qi, 0)),
        out_shape=jax.ShapeDtypeStruct((b, t, w), BF16),
        compiler_params=_params("parallel", "parallel"),
        name="mem_attn",
    )(hd, mkv)


def _merge_kernel(sb_ref, nsa0_ref, nsa1_ref, mem_ref, mg_ref, bm_ref, x_ref, wsb_ref, wnsa_ref, wmem_ref,
                  wout_ref, fg_ref, wq_ref, x1_ref, h2_ref, q_ref):
    d = x_ref.shape[1]
    gw = nsa0_ref.shape[-1]
    nsa = _dot(nsa0_ref[0, 0], wnsa_ref[0:gw, :]) + _dot(nsa1_ref[0, 0], wnsa_ref[gw:2 * gw, :])
    branches = (_dot(sb_ref[...], wsb_ref[...]), nsa, _dot(mem_ref[...], wmem_ref[...]))
    merged = jnp.zeros_like(branches[0])
    for j in range(N_BRANCHES):
        gate = _sigmoid(mg_ref[:, j * d:(j + 1) * d] + bm_ref[:, j * d:(j + 1) * d])
        merged = merged + gate * branches[j]
    x1 = x_ref[...] + _dot(merged.astype(BF16), wout_ref[...])
    x1_ref[...] = x1
    h2 = _rms(x1, fg_ref[...])
    h2_ref[...] = h2
    q = _dot(h2.astype(BF16), wq_ref[...]).astype(q_ref.dtype)
    for c in range(q_ref.shape[0]):
        q_ref[c] = q[:, c * LANES:(c + 1) * LANES]


def _merge(sb, nsa, mem, gates_slab, b_merge, x2d, w_sb, w_nsa, w_mem, w_out, ffn_g, w_q, tm=256):
    n, d = x2d.shape
    nq = w_q.shape[1] // LANES
    _, g, t, gw = nsa.shape
    assert g == 2 and t % tm == 0
    per_row = t // tm
    row = lambda w: pl.BlockSpec((tm, w), lambda i: (i, 0))
    nsa_group = lambda gi: pl.BlockSpec((1, 1, tm, gw), lambda i: (i // per_row, gi, i % per_row, 0))
    full = lambda a: pl.BlockSpec(a.shape, lambda i: (0,) * a.ndim)
    bm = b_merge.reshape(1, -1)
    fg = ffn_g.reshape(1, d)
    return pl.pallas_call(
        _merge_kernel,
        grid=(n // tm,),
        in_specs=[row(sb.shape[1]), nsa_group(0), nsa_group(1), row(mem.shape[1]), row(N_BRANCHES * d),
                  full(bm), row(d), full(w_sb), full(w_nsa), full(w_mem), full(w_out), full(fg),
                  full(w_q)],
        out_specs=[row(d), row(d), pl.BlockSpec((nq, tm, LANES), lambda i: (0, i, 0))],
        out_shape=[jax.ShapeDtypeStruct((n, d), F32), jax.ShapeDtypeStruct((n, d), F32),
                   jax.ShapeDtypeStruct((nq, n, LANES), BF16)],
        compiler_params=_params("parallel"),
        name="merge",
    )(sb, nsa, nsa, mem, gates_slab, bm, x2d, w_sb, w_nsa, w_mem, w_out, fg, w_q)


def _peer_topk_kernel(q_ref, sk_ref, idx_ref, gw_ref, tv_ref, ti_ref, bv_ref):
    tt = q_ref.shape[1]
    kk = PEER_TOPK
    nk = PEER_N_KEYS
    rid = lax.broadcasted_iota(I32, (nk, tt), 0).astype(F32)
    for c in range(2 * PEER_HEADS):
        scores = _dot_nt(sk_ref[c], q_ref[c])

        def pick(k, s, c=c):
            best = jnp.max(s, axis=0, keepdims=True)
            first = jnp.min(jnp.where(s == best, rid, float(nk)), axis=0, keepdims=True)
            tv_ref[c, pl.ds(k, 1), :] = best
            ti_ref[c, pl.ds(k, 1), :] = first
            return jnp.where(rid == first, -jnp.inf, s)

        lax.fori_loop(0, kk, pick, scores)

    sub = 8
    widths = [min(kk, -(-(kk // (i + 1)) // sub) * sub) for i in range(kk // 2)]
    n_cand = sum(widths) + kk // 2
    pos = lax.broadcasted_iota(I32, (n_cand, tt), 0).astype(F32)
    for h in range(PEER_HEADS):
        s0, s1 = tv_ref[2 * h], tv_ref[2 * h + 1]
        i0, i1 = ti_ref[2 * h], ti_ref[2 * h + 1]
        cand, cidx = [], []
        for i, wd in enumerate(widths):
            keep = lax.broadcasted_iota(I32, (wd, 1), 0) < kk // (i + 1)
            cand.append(jnp.where(keep, s0[i:i + 1, :] + s1[0:wd, :], -jnp.inf))
            cidx.append(i0[i:i + 1, :] * float(nk) + i1[0:wd, :])
        cand.append(s0[kk // 2:, :] + s1[0:1, :])
        cidx.append(i0[kk // 2:, :] * float(nk) + i1[0:1, :])
        cand = jnp.concatenate(cand, axis=0)
        cidx = jnp.concatenate(cidx, axis=0)

        def pick2(k, s, h=h, cidx=cidx):
            best = jnp.max(s, axis=0, keepdims=True)
            first = jnp.min(jnp.where(s == best, pos, float(n_cand)), axis=0, keepdims=True)
            hit = pos == first
            bv_ref[pl.ds(k, 1), :] = best
            expert = jnp.max(jnp.where(hit, cidx, -1.0), axis=0, keepdims=True)
            idx_ref[pl.ds(h * kk + k, 1), :] = expert.astype(I32)
            return jnp.where(hit, -jnp.inf, s)

        lax.fori_loop(0, kk, pick2, cand)
        best = bv_ref[...]
        e = jnp.exp(best - best[0:1, :])
        gw_ref[h * kk:(h + 1) * kk, :] = e / jnp.sum(e, axis=0, keepdims=True)


def _peer_topk(q_chunks, subkeys, tt=256):
    nchunk, n, half = q_chunks.shape
    slots = PEER_HEADS * PEER_TOPK
    return pl.pallas_call(
        _peer_topk_kernel,
        grid=(n // tt,),
        in_specs=[
            pl.BlockSpec((nchunk, tt, half), lambda i: (0, i, 0)),
            pl.BlockSpec(subkeys.shape, lambda i: (0, 0, 0)),
        ],
        out_specs=[pl.BlockSpec((slots, tt), lambda i: (0, i)),
                   pl.BlockSpec((slots, tt), lambda i: (0, i))],
        out_shape=[jax.ShapeDtypeStruct((slots, n), I32), jax.ShapeDtypeStruct((slots, n), F32)],
        scratch_shapes=[pltpu.VMEM((nchunk, PEER_TOPK, tt), F32),
                        pltpu.VMEM((nchunk, PEER_TOPK, tt), F32),
                        pltpu.VMEM((PEER_TOPK, tt), F32)],
        compiler_params=_params("parallel"),
        name="peer_topk",
    )(q_chunks, subkeys)


def _peer_ffn_kernel(*refs, first_tiles, staged):
    ns = len(staged)
    idx_hbm, tables = refs[0], refs[1:1 + ns]
    io = [refs[1 + ns + 3 * s:4 + ns + 3 * s] for s in range(ns)]
    fg_ref = refs[1 + 4 * ns]
    ys = refs[2 + 4 * ns:2 + 5 * ns]
    idx_smem, bufs = refs[2 + 5 * ns], refs[3 + 5 * ns:3 + 6 * ns]
    idx_sem, sems = refs[3 + 6 * ns], refs[4 + 6 * ns:4 + 7 * ns]
    tt, d = io[0][1].shape
    slots = io[0][0].shape[0]
    rows = d // (2 * LANES)
    tile = pl.program_id(0)
    nbuf = bufs[0].shape[0]
    tok_lane = lax.broadcasted_iota(I32, (1, tt), 1)

    def halves(words):
        return tuple(pltpu.unpack_elementwise(words, index=i, packed_dtype=BF16, unpacked_dtype=F32)
                     for i in range(2))

    def make_stream(s):
        table, buf, row_sem, y_ref = tables[s], bufs[s], sems[s], ys[s]
        gw_ref, h_ref, _ = io[s]

        if staged[s]:
            def start(tok, slot):
                for r in range(buf.shape[1]):
                    pltpu.make_async_copy(table.at[tile * tt + tok, :, r, :], buf.at[slot, r],
                                          row_sem.at[slot]).start(priority=r % 2)
        else:
            idx_copy = pltpu.make_async_copy(idx_hbm.at[tile + first_tiles[s]], idx_smem, idx_sem)
            idx_copy.start()
            idx_copy.wait()

            def start(tok, slot):
                for e in range(slots):
                    pltpu.make_async_copy(table.at[idx_smem[tok * slots + e]], buf.at[slot, :, e, :],
                                          row_sem.at[slot]).start(priority=e % 2)

        def wait(slot):
            pltpu.make_async_copy(buf.at[slot], buf.at[slot], row_sem.at[slot]).wait()

        def expert_weights(tok, slot):
            h = h_ref[pl.ds(tok, 1), :]
            prod = jnp.zeros((slots, LANES), F32)
            for r in range(rows):
                lo, hi = halves(buf[slot, r])
                prod = prod + lo * h[:, r * LANES:(r + 1) * LANES]
                prod = prod + hi * h[:, d // 2 + r * LANES:d // 2 + (r + 1) * LANES]
            a = jnp.sum(prod, axis=1, keepdims=True)
            act = 0.5 * a * (1.0 + lax.erf(a * (2.0 ** -0.5)))
            gate = jnp.sum(jnp.where(tok_lane == tok, gw_ref[...], 0.0), axis=1, keepdims=True)
            return gate * act

        def weighted_values(tok, slot, w):
            parts = [halves(buf[slot, rows + r]) for r in range(rows)]
            y_ref[pl.ds(tok, 1), :] = jnp.concatenate(
                [jnp.sum(parts[r][i] * w, axis=0, keepdims=True) for i in range(2) for r in range(rows)],
                axis=1)

        def step(tok, slot, w, refill):
            nxt = (slot + 1) % nbuf
            wait(nxt)
            w_next = expert_weights(tok + 1, nxt)
            weighted_values(tok, slot, w)
            if refill:
                start(tok + nbuf, slot)
            return w_next

        return start, wait, expert_weights, weighted_values, step

    streams = [make_stream(s) for s in range(ns)]

    ws = []
    for start, wait, expert_weights, _, _ in streams:
        for tok in range(nbuf):
            start(tok, tok)
    for start, wait, expert_weights, _, _ in streams:
        wait(0)
        ws.append(expert_weights(0, 0))

    def round_of_steps(i, ws):
        ws = list(ws)
        for slot in range(nbuf):
            for s in range(ns):
                ws[s] = streams[s][4](i * nbuf + slot, slot, ws[s], True)
        return tuple(ws)

    n_rounds = tt // nbuf - 1
    ws = list(lax.fori_loop(0, n_rounds, round_of_steps, tuple(ws)))
    for slot in range(nbuf - 1):
        for s in range(ns):
            ws[s] = streams[s][4](n_rounds * nbuf + slot, slot, ws[s], False)
    for s in range(ns):
        streams[s][3](tt - 1, nbuf - 1, ws[s])
        ys[s][...] = _rms(io[s][2][...] + ys[s][...], fg_ref[...])


def _pack_tables_kernel(u_ref, v_ref, o_ref):
    d = u_ref.shape[1]
    rows = d // (2 * LANES)
    for base, src in ((0, u_ref), (rows, v_ref)):
        for r in range(rows):
            lo = src[:, r * LANES:(r + 1) * LANES]
            hi = src[:, d // 2 + r * LANES:d // 2 + (r + 1) * LANES]
            o_ref[:, base + r, :] = pltpu.pack_elementwise([lo, hi], packed_dtype=BF16)


def _pack_tables(u, v, te=512):
    n_exp, d = u.shape
    rows = d // LANES
    return pl.pallas_call(
        _pack_tables_kernel,
        grid=(n_exp // te,),
        in_specs=[pl.BlockSpec((te, d), lambda i: (i, 0)), pl.BlockSpec((te, d), lambda i: (i, 0))],
        out_specs=pl.BlockSpec((te, rows, LANES), lambda i: (i, 0, 0)),
        out_shape=jax.ShapeDtypeStruct((n_exp, rows, LANES), jnp.uint32),
        compiler_params=_params("parallel"),
        name="pack_tables",
    )(u, v)


def _sc_gather_slabs(slabs, slab_idx):
    m = slab_idx.shape[0]
    mesh = plsc.VectorSubcoreMesh(core_axis_name="core", subcore_axis_name="subcore")
    idx_rows = jnp.pad(slab_idx.reshape(m // SC_GATHER_WINDOW, SC_GATHER_WINDOW),
                       ((0, 0), (0, LANES - SC_GATHER_WINDOW)))

    @pl.kernel(out_type=jax.ShapeDtypeStruct((m,) + slabs.shape[1:], slabs.dtype), mesh=mesh)
    def gather(slabs_hbm, idx_hbm, out_hbm):
        def window(idx_vmem, out_vmem):
            pltpu.sync_copy(slabs_hbm.at[idx_vmem.at[0, pl.ds(0, SC_GATHER_WINDOW)]], out_vmem)

        pltpu.emit_pipeline(
            window,
            grid=(m // SC_GATHER_WINDOW,),
            in_specs=[pl.BlockSpec((1, LANES), index_map=lambda i: (i, 0))],
            out_specs=[pl.BlockSpec((SC_GATHER_WINDOW,) + slabs.shape[1:], index_map=lambda i: (i, 0, 0))],
            core_axis_name=("core", "subcore"),
            dimension_semantics=(pltpu.PARALLEL,),
            trace_scopes=False,
        )(idx_hbm, out_hbm)

    return gather(slabs, idx_rows)


def _peer_ffn(idx_tiles, tables, gw, h2, x1, final_g, tt, first_tiles, n_tiles, staged):
    _, d = h2.shape
    slots = gw.shape[0]
    rows = d // LANES
    ns = len(staged)
    assert sum(not st for st in staged) <= 1, "one index buffer: at most one gathering stream"
    hbm = pl.BlockSpec(memory_space=pl.ANY)
    io_specs, io_args = [], []
    for s in range(ns):
        io_specs += [pl.BlockSpec((slots, tt), lambda i, f=first_tiles[s]: (0, i + f)),
                     pl.BlockSpec((tt, d), lambda i, f=first_tiles[s]: (i + f, 0)),
                     pl.BlockSpec((tt, d), lambda i, f=first_tiles[s]: (i + f, 0))]
        io_args += [gw, h2, x1]
    buf = pltpu.VMEM((PEER_GATHER_BUFFERS, rows, slots, LANES), tables[0].dtype)
    sem = pltpu.SemaphoreType.DMA((PEER_GATHER_BUFFERS,))
    outs = pl.pallas_call(
        functools.partial(_peer_ffn_kernel, first_tiles=tuple(first_tiles), staged=tuple(staged)),
        grid=(n_tiles,),
        in_specs=[hbm] + [hbm] * ns + io_specs + [pl.BlockSpec((1, d), lambda i: (0, 0))],
        out_specs=[pl.BlockSpec((tt, d), lambda i: (i, 0))] * ns,
        out_shape=[jax.ShapeDtypeStruct((n_tiles * tt, d), F32)] * ns,
        scratch_shapes=[pltpu.SMEM((tt * slots,), I32)] + [buf] * ns + [pltpu.SemaphoreType.DMA] + [sem] * ns,
        compiler_params=_params("arbitrary"),
        name="peer_ffn_" + "_".join("staged" if st else "gather" for st in staged),
    )(idx_tiles, *tables, *io_args, final_g.reshape(1, d))
    return list(outs)


def _mixers(x, mem, mix_g, mem_g, w_in, b_merge, pe_k, pe_v, cw_k, cw_v, w_mem_kv, w_sb_br, w_nsa_br,
            w_mem_br, w_out, ffn_g, peer_w_q, subkeys):
    b, t, d = x.shape
    m = mem.shape[1]
    g = NSA_KV_GROUPS
    n = b * t
    x2d = x.reshape(n, d)

    o_sbq, o_sbk, o_sbv = 0, SB_W, 2 * SB_W
    o_nq = 3 * SB_W
    o_nkv = o_nq + NSA_W
    o_ng = o_nkv + 6 * NSA_KV_W
    o_mq = o_ng + NSA_HEADS * 3
    o_mg = o_mq + MEM_W
    w_act = jnp.concatenate([w_in[:, :o_ng], w_in[:, o_mq:o_mg]], axis=1).astype(BF16)
    gate_pad = LANES - NSA_HEADS * 3
    w_gate = jnp.concatenate([w_in[:, o_mg:], w_in[:, o_ng:o_mq], jnp.zeros((d, gate_pad), w_in.dtype)],
                             axis=1).astype(BF16)
    hd = _norm_matmul_heads(x2d, mix_g, w_act, tm=256, tn=512).reshape(-1, b, t, HEAD_DIM)
    gates_slab = _norm_matmul(x2d, mix_g, w_gate, F32, tm=256, tn=640)
    head_of = lambda col: col // HEAD_DIM

    sb_out = _sb_attention(hd, head_of(o_sbq), head_of(o_sbk), head_of(o_sbv), SB_HEADS)

    kc = _nsa_compress(hd, head_of(o_nkv), pe_k, cw_k)
    vc = _nsa_compress(hd, head_of(o_nkv) + g, pe_v, cw_v)
    gl = gates_slab[:, N_BRANCHES * d:N_BRANCHES * d + NSA_HEADS * 3]
    gl = gl.reshape(b, t, g, NSA_GROUP * 3).transpose(0, 2, 1, 3)
    gl = jnp.pad(gl, ((0, 0), (0, 0), (0, 0), (0, LANES - NSA_GROUP * 3)))
    slopes = jnp.asarray([2.0 ** (-8.0 * (h + 1) / NSA_HEADS) for h in range(NSA_HEADS)], F32)
    nsa_out = _nsa_attention(hd, head_of(o_nq), head_of(o_nkv) + 2 * g, kc, vc, gl, slopes)

    mkv = _norm_matmul(mem.reshape(b * m, d), mem_g, w_mem_kv.astype(BF16), BF16, tm=256, tn=512)
    mem_out = _mem_attention(hd, head_of(o_nkv + 6 * NSA_KV_W), mkv.reshape(b, m, 2 * MEM_W))

    x1, h2, q_chunks = _merge(
        sb_out.reshape(n, SB_W), nsa_out, mem_out.reshape(n, MEM_W), gates_slab, b_merge, x2d,
        w_sb_br.astype(BF16), w_nsa_br.astype(BF16), w_mem_br.astype(BF16), w_out.astype(BF16), ffn_g,
        peer_w_q.astype(BF16))

    half = PEER_QUERY_DIM // 2
    idx, gw = _peer_topk(q_chunks, subkeys.reshape(2 * PEER_HEADS, PEER_N_KEYS, half).astype(BF16))
    return x1, h2, idx.T, gw


def _layer(x, mem, *params_and_tables):
    uv, out_g = params_and_tables[-2:]
    b, t, d = x.shape
    tt = 128
    x1, h2, idx_tok, gw = _mixers(x, mem, *params_and_tables[:-2])
    n, slots = idx_tok.shape
    n_tiles = n // tt
    idx_tiles = idx_tok.reshape(n_tiles, tt * slots)
    chunk = SC_CHUNK_TILES
    sc_first = n_tiles - SC_CHUNKS * chunk
    lead = sc_first - SC_CHUNKS * chunk
    assert lead > 0

    def stage(c):
        lo = (sc_first + c * chunk) * tt
        slabs = _sc_gather_slabs(uv, idx_tok[lo:lo + chunk * tt].reshape(-1))
        return slabs.reshape((chunk * tt, slots) + uv.shape[1:])

    staged = [stage(c) for c in range(SC_CHUNKS)]
    ffn = functools.partial(_peer_ffn, idx_tiles, gw=gw, h2=h2, x1=x1, final_g=out_g, tt=tt)
    own = ffn(tables=[uv], first_tiles=[0], n_tiles=lead, staged=[False])
    theirs = []
    for c in range(SC_CHUNKS):
        mine, other = ffn(tables=[uv, staged[c]], first_tiles=[lead + c * chunk, sc_first + c * chunk],
                          n_tiles=chunk, staged=[False, True])
        own.append(mine)
        theirs.append(other)
    return jnp.concatenate(own + theirs, axis=0).reshape(b, t, d)


def kernel(x, mem, mix_norm_g, mem_norm_g, w_in, b_merge, cmp_pe_k, cmp_pe_v, cmp_w_k, cmp_w_v, w_mem_kv, w_sb_br, w_nsa_br, w_mem_br, w_out, ffn_norm_g, peer_w_q, peer_subkeys, peer_u, peer_v, final_norm_g):
    depth = w_in.shape[0]
    assert depth == 1, "the final rmsnorm is fused into the last layer's PEER kernel"
    l = 0
    uv = _pack_tables(peer_u[l], peer_v[l])
    return _layer(x, mem, mix_norm_g[l], mem_norm_g[l], w_in[l], b_merge[l], cmp_pe_k[l], cmp_pe_v[l],
                  cmp_w_k[l], cmp_w_v[l], w_mem_kv[l], w_sb_br[l], w_nsa_br[l], w_mem_br[l], w_out[l],
                  ffn_norm_g[l], peer_w_q[l], peer_subkeys[l], uv, final_norm_g)
```

```python
import functools
import math

import jax
import jax.numpy as jnp
from jax import lax
from jax.experimental import pallas as pl
from jax.experimental.pallas import tpu as pltpu
from jax.experimental.pallas import tpu_sc as plsc

F32 = jnp.float32
BF16 = jnp.bfloat16
I32 = jnp.int32

HEAD_DIM = 64
SB_HEADS = 6
NSA_HEADS = 6
NSA_KV_GROUPS = 2
NSA_GROUP = NSA_HEADS // NSA_KV_GROUPS
MEM_HEADS = 4
N_BRANCHES = 3
SB_W = SB_HEADS * HEAD_DIM
NSA_W = NSA_HEADS * HEAD_DIM
NSA_KV_W = NSA_KV_GROUPS * HEAD_DIM
MEM_W = MEM_HEADS * HEAD_DIM
CMP_LEN = 32
CMP_STRIDE = 16
SEL_BLOCK = 64
N_SELECT = 16
WINDOW = 512
FORCED_SCORE = 1e4
PEER_HEADS = 8
PEER_N_KEYS = 128
PEER_QUERY_DIM = 256
PEER_TOPK = 16
RMS_EPS = 1e-6
NEG_INF = -1e30
SCALE = HEAD_DIM ** -0.5
SB_DEAD_LOG = 104.0

LANES = 128
PEER_GATHER_BUFFERS = 8
SC_GATHER_WINDOW = 32
SC_CHUNKS = 0
SC_CHUNK_TILES = 28
VMEM_LIMIT_BYTES = 56 * 1024 * 1024

_NT = (((1,), (1,)), ((), ()))


def _params(*sem):
    return pltpu.CompilerParams(dimension_semantics=sem, vmem_limit_bytes=VMEM_LIMIT_BYTES)


def _dot(a, b):
    return jnp.dot(a, b, preferred_element_type=F32)


def _dot_nt(a, b):
    return lax.dot_general(a, b, _NT, preferred_element_type=F32)


def _sigmoid(x):
    return 1.0 / (1.0 + jnp.exp(-x))


def _rms(x, g):
    return x * lax.rsqrt(jnp.mean(x * x, axis=-1, keepdims=True) + RMS_EPS) * g


def _norm_matmul_kernel(x_ref, g_ref, w_ref, o_ref, *, tn):
    h = _rms(x_ref[...], g_ref[...]).astype(BF16)
    for c in range(0, o_ref.shape[1], tn):
        o_ref[:, c:c + tn] = _dot(h, w_ref[:, c:c + tn]).astype(o_ref.dtype)


def _norm_matmul(x2d, g, w, out_dtype, tm, tn):
    n, d = x2d.shape
    m = w.shape[1]
    return pl.pallas_call(
        functools.partial(_norm_matmul_kernel, tn=tn),
        grid=(n // tm,),
        in_specs=[
            pl.BlockSpec((tm, d), lambda i: (i, 0)),
            pl.BlockSpec((1, d), lambda i: (0, 0)),
            pl.BlockSpec((d, m), lambda i: (0, 0)),
        ],
        out_specs=pl.BlockSpec((tm, m), lambda i: (i, 0)),
        out_shape=jax.ShapeDtypeStruct((n, m), out_dtype),
        compiler_params=_params("parallel"),
        name="norm_matmul",
    )(x2d, g.reshape(1, d), w)


def _norm_matmul_heads_kernel(x_ref, g_ref, w_ref, o_ref, *, tn):
    h = _rms(x_ref[...], g_ref[...]).astype(BF16)
    per = tn // HEAD_DIM
    for c in range(0, w_ref.shape[1], tn):
        res = _dot(h, w_ref[:, c:c + tn]).astype(o_ref.dtype)
        for j in range(per):
            o_ref[c // HEAD_DIM + j] = res[:, j * HEAD_DIM:(j + 1) * HEAD_DIM]


def _norm_matmul_heads(x2d, g, w, tm, tn):
    n, d = x2d.shape
    m = w.shape[1]
    nh = m // HEAD_DIM
    return pl.pallas_call(
        functools.partial(_norm_matmul_heads_kernel, tn=tn),
        grid=(n // tm,),
        in_specs=[
            pl.BlockSpec((tm, d), lambda i: (i, 0)),
            pl.BlockSpec((1, d), lambda i: (0, 0)),
            pl.BlockSpec((d, m), lambda i: (0, 0)),
        ],
        out_specs=pl.BlockSpec((nh, tm, HEAD_DIM), lambda i: (0, i, 0)),
        out_shape=jax.ShapeDtypeStruct((nh, n, HEAD_DIM), BF16),
        compiler_params=_params("parallel"),
        name="norm_matmul_heads",
    )(x2d, g.reshape(1, d), w)


def _sb_kernel(q_ref, k_ref, v_ref, o_ref, *, tile):
    qi = pl.program_id(2)
    nh = q_ref.shape[0]
    row = lax.broadcasted_iota(I32, (tile, tile), 0)
    col = lax.broadcasted_iota(I32, (tile, tile), 1)
    lower = row > col
    later = lower.astype(BF16)
    qs = [q_ref[hh, 0] for hh in range(nh)]

    def visit(hh, ks, c, acc, diagonal):
        k = k_ref[hh, 0, pl.ds(ks, tile), :]
        v = v_ref[hh, 0, pl.ds(ks, tile), :]
        z = _dot_nt(qs[hh], k) * SCALE
        sp = jnp.maximum(z, 0.0) + jnp.log(1.0 + jnp.exp(-jnp.abs(z)))
        if diagonal:
            sp = jnp.where(lower, sp, 0.0)
        hi = sp.astype(BF16)
        lo = (sp - hi.astype(F32)).astype(BF16)
        after = _dot(hi, later) + _dot(lo, later)
        a = jnp.exp(z - sp - after - c)
        if diagonal:
            a = jnp.where(lower, a, 0.0)
        return c + jnp.sum(sp, axis=1, keepdims=True), acc + _dot(a.astype(BF16), v)

    q0 = pl.multiple_of(qi * tile, tile)
    state = []
    for hh in range(nh):
        state.extend(visit(hh, q0, jnp.zeros((tile, 1), F32), jnp.zeros((tile, HEAD_DIM), F32), True))

    def smallest_carry(state):
        c = state[0]
        for hh in range(1, nh):
            c = jnp.minimum(c, state[2 * hh])
        return jnp.min(c)

    def live(carry):
        i, c_min, _ = carry
        return (i < qi) & (c_min <= SB_DEAD_LOG)

    def body(carry):
        i, _, state = carry
        ks = pl.multiple_of((qi - 1 - i) * tile, tile)
        out = []
        for hh in range(nh):
            out.extend(visit(hh, ks, state[2 * hh], state[2 * hh + 1], False))
        return i + 1, smallest_carry(out), tuple(out)

    _, _, state = lax.while_loop(live, body, (0, smallest_carry(state), tuple(state)))
    o_ref[0] = jnp.concatenate([state[2 * hh + 1] for hh in range(nh)], axis=1).astype(o_ref.dtype)


def _sb_attention(hd, q0, k0, v0, h, tile=256):
    _, b, t, dh = hd.shape
    hp = 2
    tq = tile
    assert q0 % hp == 0 and k0 % hp == 0 and v0 % hp == 0
    return pl.pallas_call(
        functools.partial(_sb_kernel, tile=tile),
        grid=(b, h // hp, t // tq),
        in_specs=[
            pl.BlockSpec((hp, 1, tq, dh), lambda bi, hi, qi: (q0 // hp + hi, bi, qi, 0)),
            pl.BlockSpec((hp, 1, t, dh), lambda bi, hi, qi: (k0 // hp + hi, bi, 0, 0)),
            pl.BlockSpec((hp, 1, t, dh), lambda bi, hi, qi: (v0 // hp + hi, bi, 0, 0)),
        ],
        out_specs=pl.BlockSpec((1, tq, hp * dh), lambda bi, hi, qi: (bi, qi, hi)),
        out_shape=jax.ShapeDtypeStruct((b, t, h * dh), BF16),
        compiler_params=_params("parallel", "parallel", "arbitrary"),
        name="sb_attn",
    )(hd, hd, hd)


def _compress_kernel(x_ref, pe_ref, w_ref, o_ref):
    x = x_ref[0, 0]
    nc = x.shape[0]
    w_lo, w_hi = w_ref[0], w_ref[1]
    first = _dot(x, w_lo)
    second = _dot(x, w_hi)
    feat = pe_ref.shape[1]
    pe_lo = jnp.broadcast_to(pe_ref[0:1, :], (8, feat)).astype(BF16)
    pe_hi = jnp.broadcast_to(pe_ref[1:2, :], (8, feat)).astype(BF16)
    bias = _dot(pe_lo, w_lo)[0:1] + _dot(pe_hi, w_hi)[0:1]
    o_ref[0, 0] = (first + pltpu.roll(second, nc - 1, 0) + bias).astype(o_ref.dtype)


def _nsa_compress(hd, head0, pe, w):
    nh, b, t, dh = hd.shape
    g = NSA_KV_GROUPS
    nc = t // CMP_STRIDE
    feat = CMP_STRIDE * dh
    x = hd[head0:head0 + g].reshape(g, b, nc, feat)
    pe2 = pe.reshape(2, feat)
    w2 = w.reshape(2, feat, dh).astype(BF16)
    return pl.pallas_call(
        _compress_kernel,
        grid=(b, g),
        in_specs=[
            pl.BlockSpec((1, 1, nc, feat), lambda bi, gi: (gi, bi, 0, 0)),
            pl.BlockSpec((2, feat), lambda bi, gi: (0, 0)),
            pl.BlockSpec((2, feat, dh), lambda bi, gi: (0, 0, 0)),
        ],
        out_specs=pl.BlockSpec((1, 1, nc, dh), lambda bi, gi: (bi, gi, 0, 0)),
        out_shape=jax.ShapeDtypeStruct((b, g, nc, dh), BF16),
        compiler_params=_params("parallel", "parallel"),
        name="nsa_compress",
    )(x, pe2, w2)


def _nsa_kernel(slopes_ref, q_ref, kc_ref, vc_ref, ks_ref, vs_ref, kw_ref, vw_ref, gl_ref, pool_ref, ex_ref,
                o_ref, used_ref, *, tq, tk, n_sel, real_blocks):
    grp = pl.program_id(1)
    t0 = pl.program_id(2) * tq
    nc = kc_ref.shape[2]
    rr = NSA_GROUP
    t = t0 + lax.broadcasted_iota(I32, (tq, 1), 0)

    def stack(x):
        return jnp.concatenate([x] * rr, axis=0)

    q = q_ref[:, 0].reshape(rr * tq, HEAD_DIM)
    slope = jnp.concatenate([jnp.full((tq, 1), slopes_ref[grp * rr + r], F32) for r in range(rr)], axis=0)

    def heads_sum(x):
        out = x[0:tq]
        for r in range(1, rr):
            out = out + x[r * tq:(r + 1) * tq]
        return out

    def masked_softmax(s, valid):
        p = jnp.where(valid, jnp.exp(s - jnp.max(s, axis=1, keepdims=True)), 0.0)
        denom = jnp.sum(p, axis=1, keepdims=True)
        return p / jnp.where(denom > 0, denom, 1.0)

    lane = lax.broadcasted_iota(I32, (1, nc), 1)
    dist_c = (t - (lane * CMP_STRIDE + (CMP_LEN - 1))).astype(F32)
    valid_c = stack(dist_c >= 0)
    s = _dot_nt(q, kc_ref[0, 0]) * SCALE - slope * stack(dist_c)
    p = masked_softmax(jnp.where(valid_c, s, NEG_INF), valid_c)
    o_cmp = _dot(p.astype(BF16), vc_ref[0, 0])
    psum = heads_sum(p)

    n_blk = pool_ref.shape[0]
    hi = psum.astype(BF16)
    rest = psum - hi.astype(F32)
    mid = rest.astype(BF16)
    lo = (rest - mid.astype(F32)).astype(BF16)
    pool = pool_ref[...]
    imp = _dot_nt(pool, hi) + _dot_nt(pool, mid) + _dot_nt(pool, lo)
    blk = lax.broadcasted_iota(I32, (n_blk, 1), 0)
    cur = (t0 + lax.broadcasted_iota(I32, (1, tq), 1)) // SEL_BLOCK
    forced = (blk == 0) | (blk == cur) | (blk == cur - 1)
    imp = jnp.where(forced, FORCED_SCORE, jnp.where(blk <= cur, imp, -1.0))
    imp = jnp.where(blk < real_blocks, imp, -jnp.inf)
    blk_f = blk.astype(F32)

    def pick(_, carry):
        imp, sel = carry
        best = jnp.max(imp, axis=0, keepdims=True)
        first = jnp.min(jnp.where(imp == best, blk_f, float(n_blk)), axis=0, keepdims=True)
        hit = blk_f == first
        return jnp.where(hit, -jnp.inf, imp), jnp.where(hit, 1.0, sel)

    _, sel = lax.fori_loop(0, n_sel, pick, (imp, jnp.zeros((n_blk, tq), F32)))
    picked = jnp.max(sel, axis=1, keepdims=True)
    per_tile = tk // SEL_BLOCK
    n_causal = (t0 + tq + tk - 1) // tk
    n_used = jnp.int32(0)
    for j in range(used_ref.shape[0]):
        used_ref[n_used] = j
        hit = (jnp.max(picked[j * per_tile:(j + 1) * per_tile, :]) > 0) & (j < n_causal)
        n_used = n_used + hit.astype(I32)
    sel = sel.T.astype(BF16)

    span = WINDOW + tq
    w0 = pl.multiple_of(jnp.maximum(t0 - WINDOW, 0), tq)
    dist_w = t - (w0 + lax.broadcasted_iota(I32, (1, span), 1))
    valid_w = stack((dist_w >= 0) & (dist_w < WINDOW))
    s = _dot_nt(q, kw_ref[0, 0, pl.ds(w0, span), :]) * SCALE - slope * stack(dist_w.astype(F32))
    p = masked_softmax(jnp.where(valid_w, s, NEG_INF), valid_w)
    o_win = _dot(p.astype(BF16), vw_ref[0, 0, pl.ds(w0, span), :])

    kcol = lax.broadcasted_iota(I32, (1, tk), 1)

    def sel_body(i, carry):
        m, l, acc = carry
        kb = used_ref[i]
        ks0 = pl.multiple_of(kb * tk, tk)
        chosen = _dot(sel, ex_ref[kb]) > 0.5
        dist = t - (ks0 + kcol)
        valid = stack(chosen & (dist >= 0))
        s = _dot_nt(q, ks_ref[0, 0, pl.ds(ks0, tk), :]) * SCALE - slope * stack(dist.astype(F32))
        s = jnp.where(valid, s, NEG_INF)
        m_new = jnp.maximum(m, jnp.max(s, axis=1, keepdims=True))
        alpha = jnp.exp(m - m_new)
        p = jnp.where(valid, jnp.exp(s - m_new), 0.0)
        l = alpha * l + jnp.sum(p, axis=1, keepdims=True)
        acc = alpha * acc + _dot(p.astype(BF16), vs_ref[0, 0, pl.ds(ks0, tk), :])
        return m_new, l, acc

    init = (jnp.full((rr * tq, 1), NEG_INF, F32), jnp.zeros((rr * tq, 1), F32),
            jnp.zeros((rr * tq, HEAD_DIM), F32))
    _, l_sel, acc_sel = lax.fori_loop(0, n_used, sel_body, init)
    o_sel = acc_sel / l_sel

    gates = _sigmoid(gl_ref[0, 0])
    outs = []
    for r in range(rr):
        rows = slice(r * tq, (r + 1) * tq)
        outs.append(gates[:, 3 * r:3 * r + 1] * o_cmp[rows] + gates[:, 3 * r + 1:3 * r + 2] * o_sel[rows]
                    + gates[:, 3 * r + 2:3 * r + 3] * o_win[rows])
    o_ref[0, 0] = jnp.concatenate(outs, axis=1).astype(o_ref.dtype)


def _nsa_attention(hd, q0, sel0, kc, vc, gate_logits, slopes, tq=256, tk=256):
    _, b, t, dh = hd.shape
    g = NSA_KV_GROUPS
    assert q0 % NSA_GROUP == 0
    nc = kc.shape[2]
    n_sel = min(N_SELECT, t // SEL_BLOCK)
    real_blocks = t // SEL_BLOCK
    n_blk = LANES
    assert t % tk == 0 and t >= WINDOW + tq and real_blocks <= n_blk
    blk_ids = jnp.arange(n_blk, dtype=I32)
    pool = (blk_ids[:, None] == jnp.arange(nc, dtype=I32)[None, :] // (SEL_BLOCK // CMP_STRIDE)).astype(BF16)
    key_blk = (jnp.arange(t, dtype=I32) // SEL_BLOCK).reshape(t // tk, 1, tk)
    expand = (blk_ids[None, :, None] == key_blk).astype(BF16)
    def kv_spec(j):
        return pl.BlockSpec((1, 1, t, dh), lambda bi, gi, qi: (sel0 + j * g + gi, bi, 0, 0))

    c_spec = pl.BlockSpec((1, 1, nc, dh), lambda bi, gi, qi: (bi, gi, 0, 0))
    return pl.pallas_call(
        functools.partial(_nsa_kernel, tq=tq, tk=tk, n_sel=n_sel, real_blocks=real_blocks),
        grid=(b, g, t // tq),
        in_specs=[
            pl.BlockSpec(memory_space=pltpu.SMEM),
            pl.BlockSpec((NSA_GROUP, 1, tq, dh), lambda bi, gi, qi: (q0 // NSA_GROUP + gi, bi, qi, 0)),
            c_spec, c_spec, kv_spec(0), kv_spec(1), kv_spec(2), kv_spec(3),
            pl.BlockSpec((1, 1, tq, LANES), lambda bi, gi, qi: (bi, gi, qi, 0)),
            pl.BlockSpec(pool.shape, lambda bi, gi, qi: (0, 0)),
            pl.BlockSpec(expand.shape, lambda bi, gi, qi: (0, 0, 0)),
        ],
        out_specs=pl.BlockSpec((1, 1, tq, NSA_GROUP * dh), lambda bi, gi, qi: (bi, gi, qi, 0)),
        out_shape=jax.ShapeDtypeStruct((b, g, t, NSA_GROUP * dh), BF16),
        scratch_shapes=[pltpu.SMEM((t // tk,), I32)],
        compiler_params=_params("parallel", "parallel", "arbitrary"),
        name="nsa_attn",
    )(slopes, hd, kc, vc, hd, hd, hd, hd, gate_logits, pool, expand)


def _mem_kernel(q_ref, kv_ref, o_ref):
    kv = kv_ref[0]
    outs = []
    for h in range(MEM_HEADS):
        sl = slice(h * HEAD_DIM, (h + 1) * HEAD_DIM)
        s = _dot_nt(q_ref[h, 0], kv[:, sl]) * SCALE
        p = jnp.exp(s - jnp.max(s, axis=1, keepdims=True))
        p = p / jnp.sum(p, axis=1, keepdims=True)
        outs.append(_dot(p.astype(BF16), kv[:, MEM_W + h * HEAD_DIM:MEM_W + (h + 1) * HEAD_DIM]))
    o_ref[0] = jnp.concatenate(outs, axis=1).astype(o_ref.dtype)


def _mem_attention(hd, q0, mkv, tq=512):
    _, b, t, dh = hd.shape
    w = MEM_HEADS * dh
    m = mkv.shape[1]
    assert q0 % MEM_HEADS == 0
    return pl.pallas_call(
        _mem_kernel,
        grid=(b, t // tq),
        in_specs=[
            pl.BlockSpec((MEM_HEADS, 1, tq, dh), lambda bi, qi: (q0 // MEM_HEADS, bi, qi, 0)),
            pl.BlockSpec((1, m, 2 * w), lambda bi, qi: (bi, 0, 0)),
        ],
        out_specs=pl.BlockSpec((1, tq, w), lambda bi, qi: (bi, qi, 0)),
        out_shape=jax.ShapeDtypeStruct((b, t, w), BF16),
        compiler_params=_params("parallel", "parallel"),
        name="mem_attn",
    )(hd, mkv)


def _merge_kernel(sb_ref, nsa0_ref, nsa1_ref, mem_ref, mg_ref, bm_ref, x_ref, wsb_ref, wnsa_ref, wmem_ref,
                  wout_ref, fg_ref, wq_ref, x1_ref, h2_ref, q_ref):
    d = x_ref.shape[1]
    gw = nsa0_ref.shape[-1]
    nsa = _dot(nsa0_ref[0, 0], wnsa_ref[0:gw, :]) + _dot(nsa1_ref[0, 0], wnsa_ref[gw:2 * gw, :])
    branches = (_dot(sb_ref[...], wsb_ref[...]), nsa, _dot(mem_ref[...], wmem_ref[...]))
    merged = jnp.zeros_like(branches[0])
    for j in range(N_BRANCHES):
        gate = _sigmoid(mg_ref[:, j * d:(j + 1) * d] + bm_ref[:, j * d:(j + 1) * d])
        merged = merged + gate * branches[j]
    x1 = x_ref[...] + _dot(merged.astype(BF16), wout_ref[...])
    x1_ref[...] = x1
    h2 = _rms(x1, fg_ref[...])
    h2_ref[...] = h2
    q = _dot(h2.astype(BF16), wq_ref[...]).astype(q_ref.dtype)
    for c in range(q_ref.shape[0]):
        q_ref[c] = q[:, c * LANES:(c + 1) * LANES]


def _merge(sb, nsa, mem, gates_slab, b_merge, x2d, w_sb, w_nsa, w_mem, w_out, ffn_g, w_q, tm=256):
    n, d = x2d.shape
    nq = w_q.shape[1] // LANES
    _, g, t, gw = nsa.shape
    assert g == 2 and t % tm == 0
    per_row = t // tm
    row = lambda w: pl.BlockSpec((tm, w), lambda i: (i, 0))
    nsa_group = lambda gi: pl.BlockSpec((1, 1, tm, gw), lambda i: (i // per_row, gi, i % per_row, 0))
    full = lambda a: pl.BlockSpec(a.shape, lambda i: (0,) * a.ndim)
    bm = b_merge.reshape(1, -1)
    fg = ffn_g.reshape(1, d)
    return pl.pallas_call(
        _merge_kernel,
        grid=(n // tm,),
        in_specs=[row(sb.shape[1]), nsa_group(0), nsa_group(1), row(mem.shape[1]), row(N_BRANCHES * d),
                  full(bm), row(d), full(w_sb), full(w_nsa), full(w_mem), full(w_out), full(fg),
                  full(w_q)],
        out_specs=[row(d), row(d), pl.BlockSpec((nq, tm, LANES), lambda i: (0, i, 0))],
        out_shape=[jax.ShapeDtypeStruct((n, d), F32), jax.ShapeDtypeStruct((n, d), F32),
                   jax.ShapeDtypeStruct((nq, n, LANES), BF16)],
        compiler_params=_params("parallel"),
        name="merge",
    )(sb, nsa, nsa, mem, gates_slab, bm, x2d, w_sb, w_nsa, w_mem, w_out, fg, w_q)


def _peer_topk_kernel(q_ref, sk_ref, idx_ref, gw_ref, tv_ref, ti_ref, bv_ref):
    tt = q_ref.shape[1]
    kk = PEER_TOPK
    nk = PEER_N_KEYS
    rid = lax.broadcasted_iota(I32, (nk, tt), 0).astype(F32)
    for c in range(2 * PEER_HEADS):
        scores = _dot_nt(sk_ref[c], q_ref[c])

        def pick(k, s, c=c):
            best = jnp.max(s, axis=0, keepdims=True)
            first = jnp.min(jnp.where(s == best, rid, float(nk)), axis=0, keepdims=True)
            tv_ref[c, pl.ds(k, 1), :] = best
            ti_ref[c, pl.ds(k, 1), :] = first
            return jnp.where(rid == first, -jnp.inf, s)

        lax.fori_loop(0, kk, pick, scores)

    sub = 8
    widths = [min(kk, -(-(kk // (i + 1)) // sub) * sub) for i in range(kk // 2)]
    n_cand = sum(widths) + kk // 2
    pos = lax.broadcasted_iota(I32, (n_cand, tt), 0).astype(F32)
    for h in range(PEER_HEADS):
        s0, s1 = tv_ref[2 * h], tv_ref[2 * h + 1]
        i0, i1 = ti_ref[2 * h], ti_ref[2 * h + 1]
        cand, cidx = [], []
        for i, wd in enumerate(widths):
            keep = lax.broadcasted_iota(I32, (wd, 1), 0) < kk // (i + 1)
            cand.append(jnp.where(keep, s0[i:i + 1, :] + s1[0:wd, :], -jnp.inf))
            cidx.append(i0[i:i + 1, :] * float(nk) + i1[0:wd, :])
        cand.append(s0[kk // 2:, :] + s1[0:1, :])
        cidx.append(i0[kk // 2:, :] * float(nk) + i1[0:1, :])
        cand = jnp.concatenate(cand, axis=0)
        cidx = jnp.concatenate(cidx, axis=0)

        def pick2(k, s, h=h, cidx=cidx):
            best = jnp.max(s, axis=0, keepdims=True)
            first = jnp.min(jnp.where(s == best, pos, float(n_cand)), axis=0, keepdims=True)
            hit = pos == first
            bv_ref[pl.ds(k, 1), :] = best
            expert = jnp.max(jnp.where(hit, cidx, -1.0), axis=0, keepdims=True)
            idx_ref[pl.ds(h * kk + k, 1), :] = expert.astype(I32)
            return jnp.where(hit, -jnp.inf, s)

        lax.fori_loop(0, kk, pick2, cand)
        best = bv_ref[...]
        e = jnp.exp(best - best[0:1, :])
        gw_ref[h * kk:(h + 1) * kk, :] = e / jnp.sum(e, axis=0, keepdims=True)


def _peer_topk(q_chunks, subkeys, tt=256):
    nchunk, n, half = q_chunks.shape
    slots = PEER_HEADS * PEER_TOPK
    return pl.pallas_call(
        _peer_topk_kernel,
        grid=(n // tt,),
        in_specs=[
            pl.BlockSpec((nchunk, tt, half), lambda i: (0, i, 0)),
            pl.BlockSpec(subkeys.shape, lambda i: (0, 0, 0)),
        ],
        out_specs=[pl.BlockSpec((slots, tt), lambda i: (0, i)),
                   pl.BlockSpec((slots, tt), lambda i: (0, i))],
        out_shape=[jax.ShapeDtypeStruct((slots, n), I32), jax.ShapeDtypeStruct((slots, n), F32)],
        scratch_shapes=[pltpu.VMEM((nchunk, PEER_TOPK, tt), F32),
                        pltpu.VMEM((nchunk, PEER_TOPK, tt), F32),
                        pltpu.VMEM((PEER_TOPK, tt), F32)],
        compiler_params=_params("parallel"),
        name="peer_topk",
    )(q_chunks, subkeys)


def _peer_ffn_kernel(*refs, first_tiles, staged):
    ns = len(staged)
    idx_hbm, tables = refs[0], refs[1:1 + ns]
    io = [refs[1 + ns + 3 * s:4 + ns + 3 * s] for s in range(ns)]
    fg_ref = refs[1 + 4 * ns]
    ys = refs[2 + 4 * ns:2 + 5 * ns]
    idx_smem, bufs = refs[2 + 5 * ns], refs[3 + 5 * ns:3 + 6 * ns]
    idx_sem, sems = refs[3 + 6 * ns], refs[4 + 6 * ns:4 + 7 * ns]
    tt, d = io[0][1].shape
    slots = io[0][0].shape[0]
    rows = d // (2 * LANES)
    tile = pl.program_id(0)
    nbuf = bufs[0].shape[0]
    tok_lane = lax.broadcasted_iota(I32, (1, tt), 1)

    def halves(words):
        return tuple(pltpu.unpack_elementwise(words, index=i, packed_dtype=BF16, unpacked_dtype=F32)
                     for i in range(2))

    def make_stream(s):
        table, buf, row_sem, y_ref = tables[s], bufs[s], sems[s], ys[s]
        gw_ref, h_ref, _ = io[s]

        if staged[s]:
            def start(tok, slot):
                for r in range(buf.shape[1]):
                    pltpu.make_async_copy(table.at[tile * tt + tok, :, r, :], buf.at[slot, r],
                                          row_sem.at[slot]).start(priority=r % 2)
        else:
            idx_copy = pltpu.make_async_copy(idx_hbm.at[tile + first_tiles[s]], idx_smem, idx_sem)
            idx_copy.start()
            idx_copy.wait()

            def start(tok, slot):
                for e in range(slots):
                    pltpu.make_async_copy(table.at[idx_smem[tok * slots + e]], buf.at[slot, :, e, :],
                                          row_sem.at[slot]).start(priority=e % 2)

        def wait(slot):
            pltpu.make_async_copy(buf.at[slot], buf.at[slot], row_sem.at[slot]).wait()

        def expert_weights(tok, slot):
            h = h_ref[pl.ds(tok, 1), :]
            prod = jnp.zeros((slots, LANES), F32)
            for r in range(rows):
                lo, hi = halves(buf[slot, r])
                prod = prod + lo * h[:, r * LANES:(r + 1) * LANES]
                prod = prod + hi * h[:, d // 2 + r * LANES:d // 2 + (r + 1) * LANES]
            a = jnp.sum(prod, axis=1, keepdims=True)
            act = 0.5 * a * (1.0 + lax.erf(a * (2.0 ** -0.5)))
            gate = jnp.sum(jnp.where(tok_lane == tok, gw_ref[...], 0.0), axis=1, keepdims=True)
            return gate * act

        def weighted_values(tok, slot, w):
            parts = [halves(buf[slot, rows + r]) for r in range(rows)]
            y_ref[pl.ds(tok, 1), :] = jnp.concatenate(
                [jnp.sum(parts[r][i] * w, axis=0, keepdims=True) for i in range(2) for r in range(rows)],
                axis=1)

        def step(tok, slot, w, refill):
            nxt = (slot + 1) % nbuf
            wait(nxt)
            w_next = expert_weights(tok + 1, nxt)
            weighted_values(tok, slot, w)
            if refill:
                start(tok + nbuf, slot)
            return w_next

        return start, wait, expert_weights, weighted_values, step

    streams = [make_stream(s) for s in range(ns)]

    ws = []
    for start, wait, expert_weights, _, _ in streams:
        for tok in range(nbuf):
            start(tok, tok)
    for start, wait, expert_weights, _, _ in streams:
        wait(0)
        ws.append(expert_weights(0, 0))

    def round_of_steps(i, ws):
        ws = list(ws)
        for slot in range(nbuf):
            for s in range(ns):
                ws[s] = streams[s][4](i * nbuf + slot, slot, ws[s], True)
        return tuple(ws)

    n_rounds = tt // nbuf - 1
    ws = list(lax.fori_loop(0, n_rounds, round_of_steps, tuple(ws)))
    for slot in range(nbuf - 1):
        for s in range(ns):
            ws[s] = streams[s][4](n_rounds * nbuf + slot, slot, ws[s], False)
    for s in range(ns):
        streams[s][3](tt - 1, nbuf - 1, ws[s])
        ys[s][...] = _rms(io[s][2][...] + ys[s][...], fg_ref[...])


def _pack_tables_kernel(u_ref, v_ref, o_ref):
    d = u_ref.shape[1]
    rows = d // (2 * LANES)
    for base, src in ((0, u_ref), (rows, v_ref)):
        for r in range(rows):
            lo = src[:, r * LANES:(r + 1) * LANES]
            hi = src[:, d // 2 + r * LANES:d // 2 + (r + 1) * LANES]
            o_ref[:, base + r, :] = pltpu.pack_elementwise([lo, hi], packed_dtype=BF16)


def _pack_tables(u, v, te=512):
    n_exp, d = u.shape
    rows = d // LANES
    return pl.pallas_call(
        _pack_tables_kernel,
        grid=(n_exp // te,),
        in_specs=[pl.BlockSpec((te, d), lambda i: (i, 0)), pl.BlockSpec((te, d), lambda i: (i, 0))],
        out_specs=pl.BlockSpec((te, rows, LANES), lambda i: (i, 0, 0)),
        out_shape=jax.ShapeDtypeStruct((n_exp, rows, LANES), jnp.uint32),
        compiler_params=_params("parallel"),
        name="pack_tables",
    )(u, v)


def _sc_gather_slabs(slabs, slab_idx):
    m = slab_idx.shape[0]
    mesh = plsc.VectorSubcoreMesh(core_axis_name="core", subcore_axis_name="subcore")
    idx_rows = jnp.pad(slab_idx.reshape(m // SC_GATHER_WINDOW, SC_GATHER_WINDOW),
                       ((0, 0), (0, LANES - SC_GATHER_WINDOW)))

    @pl.kernel(out_type=jax.ShapeDtypeStruct((m,) + slabs.shape[1:], slabs.dtype), mesh=mesh)
    def gather(slabs_hbm, idx_hbm, out_hbm):
        def window(idx_vmem, out_vmem):
            pltpu.sync_copy(slabs_hbm.at[idx_vmem.at[0, pl.ds(0, SC_GATHER_WINDOW)]], out_vmem)

        pltpu.emit_pipeline(
            window,
            grid=(m // SC_GATHER_WINDOW,),
            in_specs=[pl.BlockSpec((1, LANES), index_map=lambda i: (i, 0))],
            out_specs=[pl.BlockSpec((SC_GATHER_WINDOW,) + slabs.shape[1:], index_map=lambda i: (i, 0, 0))],
            core_axis_name=("core", "subcore"),
            dimension_semantics=(pltpu.PARALLEL,),
            trace_scopes=False,
        )(idx_hbm, out_hbm)

    return gather(slabs, idx_rows)


def _peer_ffn(idx_tiles, tables, gw, h2, x1, final_g, tt, first_tiles, n_tiles, staged):
    _, d = h2.shape
    slots = gw.shape[0]
    rows = d // LANES
    ns = len(staged)
    assert sum(not st for st in staged) <= 1, "one index buffer: at most one gathering stream"
    hbm = pl.BlockSpec(memory_space=pl.ANY)
    io_specs, io_args = [], []
    for s in range(ns):
        io_specs += [pl.BlockSpec((slots, tt), lambda i, f=first_tiles[s]: (0, i + f)),
                     pl.BlockSpec((tt, d), lambda i, f=first_tiles[s]: (i + f, 0)),
                     pl.BlockSpec((tt, d), lambda i, f=first_tiles[s]: (i + f, 0))]
        io_args += [gw, h2, x1]
    buf = pltpu.VMEM((PEER_GATHER_BUFFERS, rows, slots, LANES), tables[0].dtype)
    sem = pltpu.SemaphoreType.DMA((PEER_GATHER_BUFFERS,))
    outs = pl.pallas_call(
        functools.partial(_peer_ffn_kernel, first_tiles=tuple(first_tiles), staged=tuple(staged)),
        grid=(n_tiles,),
        in_specs=[hbm] + [hbm] * ns + io_specs + [pl.BlockSpec((1, d), lambda i: (0, 0))],
        out_specs=[pl.BlockSpec((tt, d), lambda i: (i, 0))] * ns,
        out_shape=[jax.ShapeDtypeStruct((n_tiles * tt, d), F32)] * ns,
        scratch_shapes=[pltpu.SMEM((tt * slots,), I32)] + [buf] * ns + [pltpu.SemaphoreType.DMA] + [sem] * ns,
        compiler_params=_params("arbitrary"),
        name="peer_ffn_" + "_".join("staged" if st else "gather" for st in staged),
    )(idx_tiles, *tables, *io_args, final_g.reshape(1, d))
    return list(outs)


def _mixers(x, mem, mix_g, mem_g, w_in, b_merge, pe_k, pe_v, cw_k, cw_v, w_mem_kv, w_sb_br, w_nsa_br,
            w_mem_br, w_out, ffn_g, peer_w_q, subkeys):
    b, t, d = x.shape
    m = mem.shape[1]
    g = NSA_KV_GROUPS
    n = b * t
    x2d = x.reshape(n, d)

    o_sbq, o_sbk, o_sbv = 0, SB_W, 2 * SB_W
    o_nq = 3 * SB_W
    o_nkv = o_nq + NSA_W
    o_ng = o_nkv + 6 * NSA_KV_W
    o_mq = o_ng + NSA_HEADS * 3
    o_mg = o_mq + MEM_W
    w_act = jnp.concatenate([w_in[:, :o_ng], w_in[:, o_mq:o_mg]], axis=1).astype(BF16)
    gate_pad = LANES - NSA_HEADS * 3
    w_gate = jnp.concatenate([w_in[:, o_mg:], w_in[:, o_ng:o_mq], jnp.zeros((d, gate_pad), w_in.dtype)],
                             axis=1).astype(BF16)
    hd = _norm_matmul_heads(x2d, mix_g, w_act, tm=256, tn=512).reshape(-1, b, t, HEAD_DIM)
    gates_slab = _norm_matmul(x2d, mix_g, w_gate, F32, tm=256, tn=640)
    head_of = lambda col: col // HEAD_DIM

    sb_out = _sb_attention(hd, head_of(o_sbq), head_of(o_sbk), head_of(o_sbv), SB_HEADS)

    kc = _nsa_compress(hd, head_of(o_nkv), pe_k, cw_k)
    vc = _nsa_compress(hd, head_of(o_nkv) + g, pe_v, cw_v)
    gl = gates_slab[:, N_BRANCHES * d:N_BRANCHES * d + NSA_HEADS * 3]
    gl = gl.reshape(b, t, g, NSA_GROUP * 3).transpose(0, 2, 1, 3)
    gl = jnp.pad(gl, ((0, 0), (0, 0), (0, 0), (0, LANES - NSA_GROUP * 3)))
    slopes = jnp.asarray([2.0 ** (-8.0 * (h + 1) / NSA_HEADS) for h in range(NSA_HEADS)], F32)
    nsa_out = _nsa_attention(hd, head_of(o_nq), head_of(o_nkv) + 2 * g, kc, vc, gl, slopes)

    mkv = _norm_matmul(mem.reshape(b * m, d), mem_g, w_mem_kv.astype(BF16), BF16, tm=256, tn=512)
    mem_out = _mem_attention(hd, head_of(o_nkv + 6 * NSA_KV_W), mkv.reshape(b, m, 2 * MEM_W))

    x1, h2, q_chunks = _merge(
        sb_out.reshape(n, SB_W), nsa_out, mem_out.reshape(n, MEM_W), gates_slab, b_merge, x2d,
        w_sb_br.astype(BF16), w_nsa_br.astype(BF16), w_mem_br.astype(BF16), w_out.astype(BF16), ffn_g,
        peer_w_q.astype(BF16))

    half = PEER_QUERY_DIM // 2
    idx, gw = _peer_topk(q_chunks, subkeys.reshape(2 * PEER_HEADS, PEER_N_KEYS, half).astype(BF16))
    return x1, h2, idx.T, gw


def _layer(x, mem, *params_and_tables):
    uv, out_g = params_and_tables[-2:]
    b, t, d = x.shape
    tt = 128
    x1, h2, idx_tok, gw = _mixers(x, mem, *params_and_tables[:-2])
    n, slots = idx_tok.shape
    n_tiles = n // tt
    idx_tiles = idx_tok.reshape(n_tiles, tt * slots)
    chunk = SC_CHUNK_TILES
    sc_first = n_tiles - SC_CHUNKS * chunk
    lead = sc_first - SC_CHUNKS * chunk
    assert lead > 0

    def stage(c):
        lo = (sc_first + c * chunk) * tt
        slabs = _sc_gather_slabs(uv, idx_tok[lo:lo + chunk * tt].reshape(-1))
        return slabs.reshape((chunk * tt, slots) + uv.shape[1:])

    staged = [stage(c) for c in range(SC_CHUNKS)]
    ffn = functools.partial(_peer_ffn, idx_tiles, gw=gw, h2=h2, x1=x1, final_g=out_g, tt=tt)
    own = ffn(tables=[uv], first_tiles=[0], n_tiles=lead, staged=[False])
    theirs = []
    for c in range(SC_CHUNKS):
        mine, other = ffn(tables=[uv, staged[c]], first_tiles=[lead + c * chunk, sc_first + c * chunk],
                          n_tiles=chunk, staged=[False, True])
        own.append(mine)
        theirs.append(other)
    return jnp.concatenate(own + theirs, axis=0).reshape(b, t, d)


def kernel(x, mem, mix_norm_g, mem_norm_g, w_in, b_merge, cmp_pe_k, cmp_pe_v, cmp_w_k, cmp_w_v, w_mem_kv, w_sb_br, w_nsa_br, w_mem_br, w_out, ffn_norm_g, peer_w_q, peer_subkeys, peer_u, peer_v, final_norm_g):
    depth = w_in.shape[0]
    assert depth == 1, "the final rmsnorm is fused into the last layer's PEER kernel"
    l = 0
    uv = _pack_tables(peer_u[l], peer_v[l])
    return _layer(x, mem, mix_norm_g[l], mem_norm_g[l], w_in[l], b_merge[l], cmp_pe_k[l], cmp_pe_v[l],
                  cmp_w_k[l], cmp_w_v[l], w_mem_kv[l], w_sb_br[l], w_nsa_br[l], w_mem_br[l], w_out[l],
                  ffn_norm_g[l], peer_w_q[l], peer_subkeys[l], uv, final_norm_g)
```

```python
import functools
import math

import jax
import jax.numpy as jnp
from jax import lax
from jax.experimental import pallas as pl
from jax.experimental.pallas import tpu as pltpu
from jax.experimental.pallas import tpu_sc as plsc

F32 = jnp.float32
BF16 = jnp.bfloat16
I32 = jnp.int32

HEAD_DIM = 64
SB_HEADS = 6
NSA_HEADS = 6
NSA_KV_GROUPS = 2
NSA_GROUP = NSA_HEADS // NSA_KV_GROUPS
MEM_HEADS = 4
N_BRANCHES = 3
SB_W = SB_HEADS * HEAD_DIM
NSA_W = NSA_HEADS * HEAD_DIM
NSA_KV_W = NSA_KV_GROUPS * HEAD_DIM
MEM_W = MEM_HEADS * HEAD_DIM
CMP_LEN = 32
CMP_STRIDE = 16
SEL_BLOCK = 64
N_SELECT = 16
WINDOW = 512
FORCED_SCORE = 1e4
PEER_HEADS = 8
PEER_N_KEYS = 128
PEER_QUERY_DIM = 256
PEER_TOPK = 16
RMS_EPS = 1e-6
NEG_INF = -1e30
SCALE = HEAD_DIM ** -0.5
SB_DEAD_LOG = 104.0

LANES = 128
PEER_GATHER_BUFFERS = 8
SC_GATHER_WINDOW = 32
SC_CHUNKS = 4
SC_CHUNK_TILES = 28
VMEM_LIMIT_BYTES = 56 * 1024 * 1024

_NT = (((1,), (1,)), ((), ()))


def _params(*sem):
    return pltpu.CompilerParams(dimension_semantics=sem, vmem_limit_bytes=VMEM_LIMIT_BYTES)


def _dot(a, b):
    return jnp.dot(a, b, preferred_element_type=F32)


def _dot_nt(a, b):
    return lax.dot_general(a, b, _NT, preferred_element_type=F32)


def _sigmoid(x):
    return 1.0 / (1.0 + jnp.exp(-x))


def _rms(x, g):
    return x * lax.rsqrt(jnp.mean(x * x, axis=-1, keepdims=True) + RMS_EPS) * g


def _norm_matmul_kernel(x_ref, g_ref, w_ref, o_ref, *, tn):
    h = _rms(x_ref[...], g_ref[...]).astype(BF16)
    for c in range(0, o_ref.shape[1], tn):
        o_ref[:, c:c + tn] = _dot(h, w_ref[:, c:c + tn]).astype(o_ref.dtype)


def _norm_matmul(x2d, g, w, out_dtype, tm, tn):
    n, d = x2d.shape
    m = w.shape[1]
    return pl.pallas_call(
        functools.partial(_norm_matmul_kernel, tn=tn),
        grid=(n // tm,),
        in_specs=[
            pl.BlockSpec((tm, d), lambda i: (i, 0)),
            pl.BlockSpec((1, d), lambda i: (0, 0)),
            pl.BlockSpec((d, m), lambda i: (0, 0)),
        ],
        out_specs=pl.BlockSpec((tm, m), lambda i: (i, 0)),
        out_shape=jax.ShapeDtypeStruct((n, m), out_dtype),
        compiler_params=_params("parallel"),
        name="norm_matmul",
    )(x2d, g.reshape(1, d), w)


def _norm_matmul_heads_kernel(x_ref, g_ref, w_ref, o_ref, *, tn):
    h = _rms(x_ref[...], g_ref[...]).astype(BF16)
    per = tn // HEAD_DIM
    for c in range(0, w_ref.shape[1], tn):
        res = _dot(h, w_ref[:, c:c + tn]).astype(o_ref.dtype)
        for j in range(per):
            o_ref[c // HEAD_DIM + j] = res[:, j * HEAD_DIM:(j + 1) * HEAD_DIM]


def _norm_matmul_heads(x2d, g, w, tm, tn):
    n, d = x2d.shape
    m = w.shape[1]
    nh = m // HEAD_DIM
    return pl.pallas_call(
        functools.partial(_norm_matmul_heads_kernel, tn=tn),
        grid=(n // tm,),
        in_specs=[
            pl.BlockSpec((tm, d), lambda i: (i, 0)),
            pl.BlockSpec((1, d), lambda i: (0, 0)),
            pl.BlockSpec((d, m), lambda i: (0, 0)),
        ],
        out_specs=pl.BlockSpec((nh, tm, HEAD_DIM), lambda i: (0, i, 0)),
        out_shape=jax.ShapeDtypeStruct((nh, n, HEAD_DIM), BF16),
        compiler_params=_params("parallel"),
        name="norm_matmul_heads",
    )(x2d, g.reshape(1, d), w)


def _sb_kernel(q_ref, k_ref, v_ref, o_ref, *, tile):
    qi = pl.program_id(2)
    nh = q_ref.shape[0]
    row = lax.broadcasted_iota(I32, (tile, tile), 0)
    col = lax.broadcasted_iota(I32, (tile, tile), 1)
    lower = row > col
    later = lower.astype(BF16)
    qs = [q_ref[hh, 0] for hh in range(nh)]

    def visit(hh, ks, c, acc, diagonal):
        k = k_ref[hh, 0, pl.ds(ks, tile), :]
        v = v_ref[hh, 0, pl.ds(ks, tile), :]
        z = _dot_nt(qs[hh], k) * SCALE
        sp = jnp.maximum(z, 0.0) + jnp.log(1.0 + jnp.exp(-jnp.abs(z)))
        if diagonal:
            sp = jnp.where(lower, sp, 0.0)
        hi = sp.astype(BF16)
        lo = (sp - hi.astype(F32)).astype(BF16)
        after = _dot(hi, later) + _dot(lo, later)
        a = jnp.exp(z - sp - after - c)
        if diagonal:
            a = jnp.where(lower, a, 0.0)
        return c + jnp.sum(sp, axis=1, keepdims=True), acc + _dot(a.astype(BF16), v)

    q0 = pl.multiple_of(qi * tile, tile)
    state = []
    for hh in range(nh):
        state.extend(visit(hh, q0, jnp.zeros((tile, 1), F32), jnp.zeros((tile, HEAD_DIM), F32), True))

    def smallest_carry(state):
        c = state[0]
        for hh in range(1, nh):
            c = jnp.minimum(c, state[2 * hh])
        return jnp.min(c)

    def live(carry):
        i, c_min, _ = carry
        return (i < qi) & (c_min <= SB_DEAD_LOG)

    def body(carry):
        i, _, state = carry
        ks = pl.multiple_of((qi - 1 - i) * tile, tile)
        out = []
        for hh in range(nh):
            out.extend(visit(hh, ks, state[2 * hh], state[2 * hh + 1], False))
        return i + 1, smallest_carry(out), tuple(out)

    _, _, state = lax.while_loop(live, body, (0, smallest_carry(state), tuple(state)))
    o_ref[0] = jnp.concatenate([state[2 * hh + 1] for hh in range(nh)], axis=1).astype(o_ref.dtype)


def _sb_attention(hd, q0, k0, v0, h, tile=256):
    _, b, t, dh = hd.shape
    hp = 2
    tq = tile
    assert q0 % hp == 0 and k0 % hp == 0 and v0 % hp == 0
    return pl.pallas_call(
        functools.partial(_sb_kernel, tile=tile),
        grid=(b, h // hp, t // tq),
        in_specs=[
            pl.BlockSpec((hp, 1, tq, dh), lambda bi, hi, qi: (q0 // hp + hi, bi, qi, 0)),
            pl.BlockSpec((hp, 1, t, dh), lambda bi, hi, qi: (k0 // hp + hi, bi, 0, 0)),
            pl.BlockSpec((hp, 1, t, dh), lambda bi, hi, qi: (v0 // hp + hi, bi, 0, 0)),
        ],
        out_specs=pl.BlockSpec((1, tq, hp * dh), lambda bi, hi, qi: (bi, qi, hi)),
        out_shape=jax.ShapeDtypeStruct((b, t, h * dh), BF16),
        compiler_params=_params("parallel", "parallel", "arbitrary"),
        name="sb_attn",
    )(hd, hd, hd)


def _compress_kernel(x_ref, pe_ref, w_ref, o_ref):
    x = x_ref[0, 0]
    nc = x.shape[0]
    w_lo, w_hi = w_ref[0], w_ref[1]
    first = _dot(x, w_lo)
    second = _dot(x, w_hi)
    feat = pe_ref.shape[1]
    pe_lo = jnp.broadcast_to(pe_ref[0:1, :], (8, feat)).astype(BF16)
    pe_hi = jnp.broadcast_to(pe_ref[1:2, :], (8, feat)).astype(BF16)
    bias = _dot(pe_lo, w_lo)[0:1] + _dot(pe_hi, w_hi)[0:1]
    o_ref[0, 0] = (first + pltpu.roll(second, nc - 1, 0) + bias).astype(o_ref.dtype)


def _nsa_compress(hd, head0, pe, w):
    nh, b, t, dh = hd.shape
    g = NSA_KV_GROUPS
    nc = t // CMP_STRIDE
    feat = CMP_STRIDE * dh
    x = hd[head0:head0 + g].reshape(g, b, nc, feat)
    pe2 = pe.reshape(2, feat)
    w2 = w.reshape(2, feat, dh).astype(BF16)
    return pl.pallas_call(
        _compress_kernel,
        grid=(b, g),
        in_specs=[
            pl.BlockSpec((1, 1, nc, feat), lambda bi, gi: (gi, bi, 0, 0)),
            pl.BlockSpec((2, feat), lambda bi, gi: (0, 0)),
            pl.BlockSpec((2, feat, dh), lambda bi, gi: (0, 0, 0)),
        ],
        out_specs=pl.BlockSpec((1, 1, nc, dh), lambda bi, gi: (bi, gi, 0, 0)),
        out_shape=jax.ShapeDtypeStruct((b, g, nc, dh), BF16),
        compiler_params=_params("parallel", "parallel"),
        name="nsa_compress",
    )(x, pe2, w2)


def _nsa_kernel(slopes_ref, q_ref, kc_ref, vc_ref, ks_ref, vs_ref, kw_ref, vw_ref, gl_ref, pool_ref, ex_ref,
                o_ref, used_ref, *, tq, tk, n_sel, real_blocks):
    grp = pl.program_id(1)
    t0 = pl.program_id(2) * tq
    nc = kc_ref.shape[2]
    rr = NSA_GROUP
    t = t0 + lax.broadcasted_iota(I32, (tq, 1), 0)

    def stack(x):
        return jnp.concatenate([x] * rr, axis=0)

    q = q_ref[:, 0].reshape(rr * tq, HEAD_DIM)
    slope = jnp.concatenate([jnp.full((tq, 1), slopes_ref[grp * rr + r], F32) for r in range(rr)], axis=0)

    def heads_sum(x):
        out = x[0:tq]
        for r in range(1, rr):
            out = out + x[r * tq:(r + 1) * tq]
        return out

    def masked_softmax(s, valid):
        p = jnp.where(valid, jnp.exp(s - jnp.max(s, axis=1, keepdims=True)), 0.0)
        denom = jnp.sum(p, axis=1, keepdims=True)
        return p / jnp.where(denom > 0, denom, 1.0)

    lane = lax.broadcasted_iota(I32, (1, nc), 1)
    dist_c = (t - (lane * CMP_STRIDE + (CMP_LEN - 1))).astype(F32)
    valid_c = stack(dist_c >= 0)
    s = _dot_nt(q, kc_ref[0, 0]) * SCALE - slope * stack(dist_c)
    p = masked_softmax(jnp.where(valid_c, s, NEG_INF), valid_c)
    o_cmp = _dot(p.astype(BF16), vc_ref[0, 0])
    psum = heads_sum(p)

    n_blk = pool_ref.shape[0]
    hi = psum.astype(BF16)
    rest = psum - hi.astype(F32)
    mid = rest.astype(BF16)
    lo = (rest - mid.astype(F32)).astype(BF16)
    pool = pool_ref[...]
    imp = _dot_nt(pool, hi) + _dot_nt(pool, mid) + _dot_nt(pool, lo)
    blk = lax.broadcasted_iota(I32, (n_blk, 1), 0)
    cur = (t0 + lax.broadcasted_iota(I32, (1, tq), 1)) // SEL_BLOCK
    forced = (blk == 0) | (blk == cur) | (blk == cur - 1)
    imp = jnp.where(forced, FORCED_SCORE, jnp.where(blk <= cur, imp, -1.0))
    imp = jnp.where(blk < real_blocks, imp, -jnp.inf)
    blk_f = blk.astype(F32)

    def pick(_, carry):
        imp, sel = carry
        best = jnp.max(imp, axis=0, keepdims=True)
        first = jnp.min(jnp.where(imp == best, blk_f, float(n_blk)), axis=0, keepdims=True)
        hit = blk_f == first
        return jnp.where(hit, -jnp.inf, imp), jnp.where(hit, 1.0, sel)

    _, sel = lax.fori_loop(0, n_sel, pick, (imp, jnp.zeros((n_blk, tq), F32)))
    picked = jnp.max(sel, axis=1, keepdims=True)
    per_tile = tk // SEL_BLOCK
    n_causal = (t0 + tq + tk - 1) // tk
    n_used = jnp.int32(0)
    for j in range(used_ref.shape[0]):
        used_ref[n_used] = j
        hit = (jnp.max(picked[j * per_tile:(j + 1) * per_tile, :]) > 0) & (j < n_causal)
        n_used = n_used + hit.astype(I32)
    sel = sel.T.astype(BF16)

    span = WINDOW + tq
    w0 = pl.multiple_of(jnp.maximum(t0 - WINDOW, 0), tq)
    dist_w = t - (w0 + lax.broadcasted_iota(I32, (1, span), 1))
    valid_w = stack((dist_w >= 0) & (dist_w < WINDOW))
    s = _dot_nt(q, kw_ref[0, 0, pl.ds(w0, span), :]) * SCALE - slope * stack(dist_w.astype(F32))
    p = masked_softmax(jnp.where(valid_w, s, NEG_INF), valid_w)
    o_win = _dot(p.astype(BF16), vw_ref[0, 0, pl.ds(w0, span), :])

    kcol = lax.broadcasted_iota(I32, (1, tk), 1)

    def sel_body(i, carry):
        m, l, acc = carry
        kb = used_ref[i]
        ks0 = pl.multiple_of(kb * tk, tk)
        chosen = _dot(sel, ex_ref[kb]) > 0.5
        dist = t - (ks0 + kcol)
        valid = stack(chosen & (dist >= 0))
        s = _dot_nt(q, ks_ref[0, 0, pl.ds(ks0, tk), :]) * SCALE - slope * stack(dist.astype(F32))
        s = jnp.where(valid, s, NEG_INF)
        m_new = jnp.maximum(m, jnp.max(s, axis=1, keepdims=True))
        alpha = jnp.exp(m - m_new)
        p = jnp.where(valid, jnp.exp(s - m_new), 0.0)
        l = alpha * l + jnp.sum(p, axis=1, keepdims=True)
        acc = alpha * acc + _dot(p.astype(BF16), vs_ref[0, 0, pl.ds(ks0, tk), :])
        return m_new, l, acc

    init = (jnp.full((rr * tq, 1), NEG_INF, F32), jnp.zeros((rr * tq, 1), F32),
            jnp.zeros((rr * tq, HEAD_DIM), F32))
    _, l_sel, acc_sel = lax.fori_loop(0, n_used, sel_body, init)
    o_sel = acc_sel / l_sel

    gates = _sigmoid(gl_ref[0, 0])
    outs = []
    for r in range(rr):
        rows = slice(r * tq, (r + 1) * tq)
        outs.append(gates[:, 3 * r:3 * r + 1] * o_cmp[rows] + gates[:, 3 * r + 1:3 * r + 2] * o_sel[rows]
                    + gates[:, 3 * r + 2:3 * r + 3] * o_win[rows])
    o_ref[0, 0] = jnp.concatenate(outs, axis=1).astype(o_ref.dtype)


def _nsa_attention(hd, q0, sel0, kc, vc, gate_logits, slopes, tq=256, tk=256):
    _, b, t, dh = hd.shape
    g = NSA_KV_GROUPS
    assert q0 % NSA_GROUP == 0
    nc = kc.shape[2]
    n_sel = min(N_SELECT, t // SEL_BLOCK)
    real_blocks = t // SEL_BLOCK
    n_blk = LANES
    assert t % tk == 0 and t >= WINDOW + tq and real_blocks <= n_blk
    blk_ids = jnp.arange(n_blk, dtype=I32)
    pool = (blk_ids[:, None] == jnp.arange(nc, dtype=I32)[None, :] // (SEL_BLOCK // CMP_STRIDE)).astype(BF16)
    key_blk = (jnp.arange(t, dtype=I32) // SEL_BLOCK).reshape(t // tk, 1, tk)
    expand = (blk_ids[None, :, None] == key_blk).astype(BF16)
    def kv_spec(j):
        return pl.BlockSpec((1, 1, t, dh), lambda bi, gi, qi: (sel0 + j * g + gi, bi, 0, 0))

    c_spec = pl.BlockSpec((1, 1, nc, dh), lambda bi, gi, qi: (bi, gi, 0, 0))
    return pl.pallas_call(
        functools.partial(_nsa_kernel, tq=tq, tk=tk, n_sel=n_sel, real_blocks=real_blocks),
        grid=(b, g, t // tq),
        in_specs=[
            pl.BlockSpec(memory_space=pltpu.SMEM),
            pl.BlockSpec((NSA_GROUP, 1, tq, dh), lambda bi, gi, qi: (q0 // NSA_GROUP + gi, bi, qi, 0)),
            c_spec, c_spec, kv_spec(0), kv_spec(1), kv_spec(2), kv_spec(3),
            pl.BlockSpec((1, 1, tq, LANES), lambda bi, gi, qi: (bi, gi, qi, 0)),
            pl.BlockSpec(pool.shape, lambda bi, gi, qi: (0, 0)),
            pl.BlockSpec(expand.shape, lambda bi, gi, qi: (0, 0, 0)),
        ],
        out_specs=pl.BlockSpec((1, 1, tq, NSA_GROUP * dh), lambda bi, gi, qi: (bi, gi, qi, 0)),
        out_shape=jax.ShapeDtypeStruct((b, g, t, NSA_GROUP * dh), BF16),
        scratch_shapes=[pltpu.SMEM((t // tk,), I32)],
        compiler_params=_params("parallel", "parallel", "arbitrary"),
        name="nsa_attn",
    )(slopes, hd, kc, vc, hd, hd, hd, hd, gate_logits, pool, expand)


def _mem_kernel(q_ref, kv_ref, o_ref):
    kv = kv_ref[0]
    outs = []
    for h in range(MEM_HEADS):
        sl = slice(h * HEAD_DIM, (h + 1) * HEAD_DIM)
        s = _dot_nt(q_ref[h, 0], kv[:, sl]) * SCALE
        p = jnp.exp(s - jnp.max(s, axis=1, keepdims=True))
        p = p / jnp.sum(p, axis=1, keepdims=True)
        outs.append(_dot(p.astype(BF16), kv[:, MEM_W + h * HEAD_DIM:MEM_W + (h + 1) * HEAD_DIM]))
    o_ref[0] = jnp.concatenate(outs, axis=1).astype(o_ref.dtype)


def _mem_attention(hd, q0, mkv, tq=512):
    _, b, t, dh = hd.shape
    w = MEM_HEADS * dh
    m = mkv.shape[1]
    assert q0 % MEM_HEADS == 0
    return pl.pallas_call(
        _mem_kernel,
        grid=(b, t // tq),
        in_specs=[
            pl.BlockSpec((MEM_HEADS, 1, tq, dh), lambda bi, qi: (q0 // MEM_HEADS, bi, qi, 0)),
            pl.BlockSpec((1, m, 2 * w), lambda bi, qi: (bi, 0, 0)),
        ],
        out_specs=pl.BlockSpec((1, tq, w), lambda bi, qi: (bi, qi, 0)),
        out_shape=jax.ShapeDtypeStruct((b, t, w), BF16),
        compiler_params=_params("parallel", "parallel"),
        name="mem_attn",
    )(hd, mkv)


def _merge_kernel(sb_ref, nsa0_ref, nsa1_ref, mem_ref, mg_ref, bm_ref, x_ref, wsb_ref, wnsa_ref, wmem_ref,
                  wout_ref, fg_ref, wq_ref, x1_ref, h2_ref, q_ref):
    d = x_ref.shape[1]
    gw = nsa0_ref.shape[-1]
    nsa = _dot(nsa0_ref[0, 0], wnsa_ref[0:gw, :]) + _dot(nsa1_ref[0, 0], wnsa_ref[gw:2 * gw, :])
    branches = (_dot(sb_ref[...], wsb_ref[...]), nsa, _dot(mem_ref[...], wmem_ref[...]))
    merged = jnp.zeros_like(branches[0])
    for j in range(N_BRANCHES):
        gate = _sigmoid(mg_ref[:, j * d:(j + 1) * d] + bm_ref[:, j * d:(j + 1) * d])
        merged = merged + gate * branches[j]
    x1 = x_ref[...] + _dot(merged.astype(BF16), wout_ref[...])
    x1_ref[...] = x1
    h2 = _rms(x1, fg_ref[...])
    h2_ref[...] = h2
    q = _dot(h2.astype(BF16), wq_ref[...]).astype(q_ref.dtype)
    for c in range(q_ref.shape[0]):
        q_ref[c] = q[:, c * LANES:(c + 1) * LANES]


def _merge(sb, nsa, mem, gates_slab, b_merge, x2d, w_sb, w_nsa, w_mem, w_out, ffn_g, w_q, tm=256):
    n, d = x2d.shape
    nq = w_q.shape[1] // LANES
    _, g, t, gw = nsa.shape
    assert g == 2 and t % tm == 0
    per_row = t // tm
    row = lambda w: pl.BlockSpec((tm, w), lambda i: (i, 0))
    nsa_group = lambda gi: pl.BlockSpec((1, 1, tm, gw), lambda i: (i // per_row, gi, i % per_row, 0))
    full = lambda a: pl.BlockSpec(a.shape, lambda i: (0,) * a.ndim)
    bm = b_merge.reshape(1, -1)
    fg = ffn_g.reshape(1, d)
    return pl.pallas_call(
        _merge_kernel,
        grid=(n // tm,),
        in_specs=[row(sb.shape[1]), nsa_group(0), nsa_group(1), row(mem.shape[1]), row(N_BRANCHES * d),
                  full(bm), row(d), full(w_sb), full(w_nsa), full(w_mem), full(w_out), full(fg),
                  full(w_q)],
        out_specs=[row(d), row(d), pl.BlockSpec((nq, tm, LANES), lambda i: (0, i, 0))],
        out_shape=[jax.ShapeDtypeStruct((n, d), F32), jax.ShapeDtypeStruct((n, d), F32),
                   jax.ShapeDtypeStruct((nq, n, LANES), BF16)],
        compiler_params=_params("parallel"),
        name="merge",
    )(sb, nsa, nsa, mem, gates_slab, bm, x2d, w_sb, w_nsa, w_mem, w_out, fg, w_q)


def _peer_topk_kernel(q_ref, sk_ref, idx_ref, gw_ref, tv_ref, ti_ref, bv_ref):
    tt = q_ref.shape[1]
    kk = PEER_TOPK
    nk = PEER_N_KEYS
    rid = lax.broadcasted_iota(I32, (nk, tt), 0).astype(F32)
    for c in range(2 * PEER_HEADS):
        scores = _dot_nt(sk_ref[c], q_ref[c])

        def pick(k, s, c=c):
            best = jnp.max(s, axis=0, keepdims=True)
            first = jnp.min(jnp.where(s == best, rid, float(nk)), axis=0, keepdims=True)
            tv_ref[c, pl.ds(k, 1), :] = best
            ti_ref[c, pl.ds(k, 1), :] = first
            return jnp.where(rid == first, -jnp.inf, s)

        lax.fori_loop(0, kk, pick, scores)

    sub = 8
    widths = [min(kk, -(-(kk // (i + 1)) // sub) * sub) for i in range(kk // 2)]
    n_cand = sum(widths) + kk // 2
    pos = lax.broadcasted_iota(I32, (n_cand, tt), 0).astype(F32)
    for h in range(PEER_HEADS):
        s0, s1 = tv_ref[2 * h], tv_ref[2 * h + 1]
        i0, i1 = ti_ref[2 * h], ti_ref[2 * h + 1]
        cand, cidx = [], []
        for i, wd in enumerate(widths):
            keep = lax.broadcasted_iota(I32, (wd, 1), 0) < kk // (i + 1)
            cand.append(jnp.where(keep, s0[i:i + 1, :] + s1[0:wd, :], -jnp.inf))
            cidx.append(i0[i:i + 1, :] * float(nk) + i1[0:wd, :])
        cand.append(s0[kk // 2:, :] + s1[0:1, :])
        cidx.append(i0[kk // 2:, :] * float(nk) + i1[0:1, :])
        cand = jnp.concatenate(cand, axis=0)
        cidx = jnp.concatenate(cidx, axis=0)

        def pick2(k, s, h=h, cidx=cidx):
            best = jnp.max(s, axis=0, keepdims=True)
            first = jnp.min(jnp.where(s == best, pos, float(n_cand)), axis=0, keepdims=True)
            hit = pos == first
            bv_ref[pl.ds(k, 1), :] = best
            expert = jnp.max(jnp.where(hit, cidx, -1.0), axis=0, keepdims=True)
            idx_ref[pl.ds(h * kk + k, 1), :] = expert.astype(I32)
            return jnp.where(hit, -jnp.inf, s)

        lax.fori_loop(0, kk, pick2, cand)
        best = bv_ref[...]
        e = jnp.exp(best - best[0:1, :])
        gw_ref[h * kk:(h + 1) * kk, :] = e / jnp.sum(e, axis=0, keepdims=True)


def _peer_topk(q_chunks, subkeys, tt=256):
    nchunk, n, half = q_chunks.shape
    slots = PEER_HEADS * PEER_TOPK
    return pl.pallas_call(
        _peer_topk_kernel,
        grid=(n // tt,),
        in_specs=[
            pl.BlockSpec((nchunk, tt, half), lambda i: (0, i, 0)),
            pl.BlockSpec(subkeys.shape, lambda i: (0, 0, 0)),
        ],
        out_specs=[pl.BlockSpec((slots, tt), lambda i: (0, i)),
                   pl.BlockSpec((slots, tt), lambda i: (0, i))],
        out_shape=[jax.ShapeDtypeStruct((slots, n), I32), jax.ShapeDtypeStruct((slots, n), F32)],
        scratch_shapes=[pltpu.VMEM((nchunk, PEER_TOPK, tt), F32),
                        pltpu.VMEM((nchunk, PEER_TOPK, tt), F32),
                        pltpu.VMEM((PEER_TOPK, tt), F32)],
        compiler_params=_params("parallel"),
        name="peer_topk",
    )(q_chunks, subkeys)


def _peer_ffn_kernel(*refs, first_tiles, staged):
    ns = len(staged)
    idx_hbm, tables = refs[0], refs[1:1 + ns]
    io = [refs[1 + ns + 3 * s:4 + ns + 3 * s] for s in range(ns)]
    fg_ref = refs[1 + 4 * ns]
    ys = refs[2 + 4 * ns:2 + 5 * ns]
    idx_smem, bufs = refs[2 + 5 * ns], refs[3 + 5 * ns:3 + 6 * ns]
    idx_sem, sems = refs[3 + 6 * ns], refs[4 + 6 * ns:4 + 7 * ns]
    tt, d = io[0][1].shape
    slots = io[0][0].shape[0]
    rows = d // (2 * LANES)
    tile = pl.program_id(0)
    nbuf = bufs[0].shape[0]
    tok_lane = lax.broadcasted_iota(I32, (1, tt), 1)

    def halves(words):
        return tuple(pltpu.unpack_elementwise(words, index=i, packed_dtype=BF16, unpacked_dtype=F32)
                     for i in range(2))

    def make_stream(s):
        table, buf, row_sem, y_ref = tables[s], bufs[s], sems[s], ys[s]
        gw_ref, h_ref, _ = io[s]

        if staged[s]:
            def start(tok, slot):
                for r in range(buf.shape[1]):
                    pltpu.make_async_copy(table.at[tile * tt + tok, :, r, :], buf.at[slot, r],
                                          row_sem.at[slot]).start(priority=r % 2)
        else:
            idx_copy = pltpu.make_async_copy(idx_hbm.at[tile + first_tiles[s]], idx_smem, idx_sem)
            idx_copy.start()
            idx_copy.wait()

            def start(tok, slot):
                for e in range(slots):
                    pltpu.make_async_copy(table.at[idx_smem[tok * slots + e]], buf.at[slot, :, e, :],
                                          row_sem.at[slot]).start(priority=e % 2)

        def wait(slot):
            pltpu.make_async_copy(buf.at[slot], buf.at[slot], row_sem.at[slot]).wait()

        def expert_weights(tok, slot):
            h = h_ref[pl.ds(tok, 1), :]
            prod = jnp.zeros((slots, LANES), F32)
            for r in range(rows):
                lo, hi = halves(buf[slot, r])
                prod = prod + lo * h[:, r * LANES:(r + 1) * LANES]
                prod = prod + hi * h[:, d // 2 + r * LANES:d // 2 + (r + 1) * LANES]
            a = jnp.sum(prod, axis=1, keepdims=True)
            act = 0.5 * a * (1.0 + lax.erf(a * (2.0 ** -0.5)))
            gate = jnp.sum(jnp.where(tok_lane == tok, gw_ref[...], 0.0), axis=1, keepdims=True)
            return gate * act

        def weighted_values(tok, slot, w):
            parts = [halves(buf[slot, rows + r]) for r in range(rows)]
            y_ref[pl.ds(tok, 1), :] = jnp.concatenate(
                [jnp.sum(parts[r][i] * w, axis=0, keepdims=True) for i in range(2) for r in range(rows)],
                axis=1)

        def step(tok, slot, w, refill):
            nxt = (slot + 1) % nbuf
            wait(nxt)
            w_next = expert_weights(tok + 1, nxt)
            weighted_values(tok, slot, w)
            if refill:
                start(tok + nbuf, slot)
            return w_next

        return start, wait, expert_weights, weighted_values, step

    streams = [make_stream(s) for s in range(ns)]

    ws = []
    for start, wait, expert_weights, _, _ in streams:
        for tok in range(nbuf):
            start(tok, tok)
    for start, wait, expert_weights, _, _ in streams:
        wait(0)
        ws.append(expert_weights(0, 0))

    def round_of_steps(i, ws):
        ws = list(ws)
        for slot in range(nbuf):
            for s in range(ns):
                ws[s] = streams[s][4](i * nbuf + slot, slot, ws[s], True)
        return tuple(ws)

    n_rounds = tt // nbuf - 1
    ws = list(lax.fori_loop(0, n_rounds, round_of_steps, tuple(ws)))
    for slot in range(nbuf - 1):
        for s in range(ns):
            ws[s] = streams[s][4](n_rounds * nbuf + slot, slot, ws[s], False)
    for s in range(ns):
        streams[s][3](tt - 1, nbuf - 1, ws[s])
        ys[s][...] = _rms(io[s][2][...] + ys[s][...], fg_ref[...])


def _pack_tables_kernel(u_ref, v_ref, o_ref):
    d = u_ref.shape[1]
    rows = d // (2 * LANES)
    for base, src in ((0, u_ref), (rows, v_ref)):
        for r in range(rows):
            lo = src[:, r * LANES:(r + 1) * LANES]
            hi = src[:, d // 2 + r * LANES:d // 2 + (r + 1) * LANES]
            o_ref[:, base + r, :] = pltpu.pack_elementwise([lo, hi], packed_dtype=BF16)


def _pack_tables(u, v, te=512):
    n_exp, d = u.shape
    rows = d // LANES
    return pl.pallas_call(
        _pack_tables_kernel,
        grid=(n_exp // te,),
        in_specs=[pl.BlockSpec((te, d), lambda i: (i, 0)), pl.BlockSpec((te, d), lambda i: (i, 0))],
        out_specs=pl.BlockSpec((te, rows, LANES), lambda i: (i, 0, 0)),
        out_shape=jax.ShapeDtypeStruct((n_exp, rows, LANES), jnp.uint32),
        compiler_params=_params("parallel"),
        name="pack_tables",
    )(u, v)


def _sc_gather_slabs(slabs, slab_idx):
    m = slab_idx.shape[0]
    mesh = plsc.VectorSubcoreMesh(core_axis_name="core", subcore_axis_name="subcore")
    idx_rows = jnp.pad(slab_idx.reshape(m // SC_GATHER_WINDOW, SC_GATHER_WINDOW),
                       ((0, 0), (0, LANES - SC_GATHER_WINDOW)))

    @pl.kernel(out_type=jax.ShapeDtypeStruct((m,) + slabs.shape[1:], slabs.dtype), mesh=mesh)
    def gather(slabs_hbm, idx_hbm, out_hbm):
        def window(idx_vmem, out_vmem):
            pltpu.sync_copy(slabs_hbm.at[idx_vmem.at[0, pl.ds(0, SC_GATHER_WINDOW)]], out_vmem)

        pltpu.emit_pipeline(
            window,
            grid=(m // SC_GATHER_WINDOW,),
            in_specs=[pl.BlockSpec((1, LANES), index_map=lambda i: (i, 0))],
            out_specs=[pl.BlockSpec((SC_GATHER_WINDOW,) + slabs.shape[1:], index_map=lambda i: (i, 0, 0))],
            core_axis_name=("core", "subcore"),
            dimension_semantics=(pltpu.PARALLEL,),
            trace_scopes=False,
        )(idx_hbm, out_hbm)

    return gather(slabs, idx_rows)


def _peer_ffn(idx_tiles, tables, gw, h2, x1, final_g, tt, first_tiles, n_tiles, staged):
    _, d = h2.shape
    slots = gw.shape[0]
    rows = d // LANES
    ns = len(staged)
    assert sum(not st for st in staged) <= 1, "one index buffer: at most one gathering stream"
    hbm = pl.BlockSpec(memory_space=pl.ANY)
    io_specs, io_args = [], []
    for s in range(ns):
        io_specs += [pl.BlockSpec((slots, tt), lambda i, f=first_tiles[s]: (0, i + f)),
                     pl.BlockSpec((tt, d), lambda i, f=first_tiles[s]: (i + f, 0)),
                     pl.BlockSpec((tt, d), lambda i, f=first_tiles[s]: (i + f, 0))]
        io_args += [gw, h2, x1]
    buf = pltpu.VMEM((PEER_GATHER_BUFFERS, rows, slots, LANES), tables[0].dtype)
    sem = pltpu.SemaphoreType.DMA((PEER_GATHER_BUFFERS,))
    outs = pl.pallas_call(
        functools.partial(_peer_ffn_kernel, first_tiles=tuple(first_tiles), staged=tuple(staged)),
        grid=(n_tiles,),
        in_specs=[hbm] + [hbm] * ns + io_specs + [pl.BlockSpec((1, d), lambda i: (0, 0))],
        out_specs=[pl.BlockSpec((tt, d), lambda i: (i, 0))] * ns,
        out_shape=[jax.ShapeDtypeStruct((n_tiles * tt, d), F32)] * ns,
        scratch_shapes=[pltpu.SMEM((tt * slots,), I32)] + [buf] * ns + [pltpu.SemaphoreType.DMA] + [sem] * ns,
        compiler_params=_params("arbitrary"),
        name="peer_ffn_" + "_".join("staged" if st else "gather" for st in staged),
    )(idx_tiles, *tables, *io_args, final_g.reshape(1, d))
    return list(outs)


def _mixers(x, mem, mix_g, mem_g, w_in, b_merge, pe_k, pe_v, cw_k, cw_v, w_mem_kv, w_sb_br, w_nsa_br,
            w_mem_br, w_out, ffn_g, peer_w_q, subkeys):
    b, t, d = x.shape
    m = mem.shape[1]
    g = NSA_KV_GROUPS
    n = b * t
    x2d = x.reshape(n, d)

    o_sbq, o_sbk, o_sbv = 0, SB_W, 2 * SB_W
    o_nq = 3 * SB_W
    o_nkv = o_nq + NSA_W
    o_ng = o_nkv + 6 * NSA_KV_W
    o_mq = o_ng + NSA_HEADS * 3
    o_mg = o_mq + MEM_W
    w_act = jnp.concatenate([w_in[:, :o_ng], w_in[:, o_mq:o_mg]], axis=1).astype(BF16)
    gate_pad = LANES - NSA_HEADS * 3
    w_gate = jnp.concatenate([w_in[:, o_mg:], w_in[:, o_ng:o_mq], jnp.zeros((d, gate_pad), w_in.dtype)],
                             axis=1).astype(BF16)
    hd = _norm_matmul_heads(x2d, mix_g, w_act, tm=256, tn=512).reshape(-1, b, t, HEAD_DIM)
    gates_slab = _norm_matmul(x2d, mix_g, w_gate, F32, tm=256, tn=640)
    head_of = lambda col: col // HEAD_DIM

    sb_out = _sb_attention(hd, head_of(o_sbq), head_of(o_sbk), head_of(o_sbv), SB_HEADS)

    kc = _nsa_compress(hd, head_of(o_nkv), pe_k, cw_k)
    vc = _nsa_compress(hd, head_of(o_nkv) + g, pe_v, cw_v)
    gl = gates_slab[:, N_BRANCHES * d:N_BRANCHES * d + NSA_HEADS * 3]
    gl = gl.reshape(b, t, g, NSA_GROUP * 3).transpose(0, 2, 1, 3)
    gl = jnp.pad(gl, ((0, 0), (0, 0), (0, 0), (0, LANES - NSA_GROUP * 3)))
    slopes = jnp.asarray([2.0 ** (-8.0 * (h + 1) / NSA_HEADS) for h in range(NSA_HEADS)], F32)
    nsa_out = _nsa_attention(hd, head_of(o_nq), head_of(o_nkv) + 2 * g, kc, vc, gl, slopes)

    mkv = _norm_matmul(mem.reshape(b * m, d), mem_g, w_mem_kv.astype(BF16), BF16, tm=256, tn=512)
    mem_out = _mem_attention(hd, head_of(o_nkv + 6 * NSA_KV_W), mkv.reshape(b, m, 2 * MEM_W))

    x1, h2, q_chunks = _merge(
        sb_out.reshape(n, SB_W), nsa_out, mem_out.reshape(n, MEM_W), gates_slab, b_merge, x2d,
        w_sb_br.astype(BF16), w_nsa_br.astype(BF16), w_mem_br.astype(BF16), w_out.astype(BF16), ffn_g,
        peer_w_q.astype(BF16))

    sk = subkeys.reshape(2 * PEER_HEADS, PEER_N_KEYS, PEER_QUERY_DIM // 2).astype(BF16)
    picks = [_peer_topk(q_chunks[:, lo:lo + n // 2], sk) for lo in (0, n // 2)]
    return x1, h2, [idx.T for idx, _ in picks], jnp.concatenate([gw for _, gw in picks], axis=1)


def _layer(x, mem, *params_and_tables):
    uv, out_g = params_and_tables[-2:]
    b, t, d = x.shape
    tt = 128
    x1, h2, idx_halves, gw = _mixers(x, mem, *params_and_tables[:-2])
    idx_tok = jnp.concatenate(idx_halves, axis=0)
    n, slots = idx_tok.shape
    n_tiles = n // tt
    idx_tiles = idx_tok.reshape(n_tiles, tt * slots)
    chunk = SC_CHUNK_TILES
    own_first = SC_CHUNKS * chunk
    lead = n_tiles - 2 * own_first
    assert lead > 0 and own_first * tt <= idx_halves[0].shape[0]

    def stage(c):
        lo = c * chunk * tt
        slabs = _sc_gather_slabs(uv, idx_halves[0][lo:lo + chunk * tt].reshape(-1))
        return slabs.reshape((chunk * tt, slots) + uv.shape[1:])

    staged = [stage(c) for c in range(SC_CHUNKS)]
    ffn = functools.partial(_peer_ffn, idx_tiles, gw=gw, h2=h2, x1=x1, final_g=out_g, tt=tt)
    own = ffn(tables=[uv], first_tiles=[own_first], n_tiles=lead, staged=[False])
    theirs = []
    for c in range(SC_CHUNKS):
        mine, other = ffn(tables=[uv, staged[c]], first_tiles=[own_first + lead + c * chunk, c * chunk],
                          n_tiles=chunk, staged=[False, True])
        own.append(mine)
        theirs.append(other)
    return jnp.concatenate(theirs + own, axis=0).reshape(b, t, d)


def kernel(x, mem, mix_norm_g, mem_norm_g, w_in, b_merge, cmp_pe_k, cmp_pe_v, cmp_w_k, cmp_w_v, w_mem_kv, w_sb_br, w_nsa_br, w_mem_br, w_out, ffn_norm_g, peer_w_q, peer_subkeys, peer_u, peer_v, final_norm_g):
    depth = w_in.shape[0]
    assert depth == 1, "the final rmsnorm is fused into the last layer's PEER kernel"
    l = 0
    uv = _pack_tables(peer_u[l], peer_v[l])
    return _layer(x, mem, mix_norm_g[l], mem_norm_g[l], w_in[l], b_merge[l], cmp_pe_k[l], cmp_pe_v[l],
                  cmp_w_k[l], cmp_w_v[l], w_mem_kv[l], w_sb_br[l], w_nsa_br[l], w_mem_br[l], w_out[l],
                  ffn_norm_g[l], peer_w_q[l], peer_subkeys[l], uv, final_norm_g)
```

```python
import functools

import jax
import jax.numpy as jnp
from jax import lax
from jax.experimental import pallas as pl
from jax.experimental.pallas import tpu as pltpu
from jax.experimental.pallas import tpu_sc as plsc

F32 = jnp.float32
BF16 = jnp.bfloat16
I32 = jnp.int32

HEAD_DIM = 64
SB_HEADS = 6
NSA_HEADS = 6
NSA_KV_GROUPS = 2
NSA_GROUP = NSA_HEADS // NSA_KV_GROUPS
MEM_HEADS = 4
N_BRANCHES = 3
SB_W = SB_HEADS * HEAD_DIM
NSA_W = NSA_HEADS * HEAD_DIM
NSA_KV_W = NSA_KV_GROUPS * HEAD_DIM
MEM_W = MEM_HEADS * HEAD_DIM
CMP_LEN = 32
CMP_STRIDE = 16
SEL_BLOCK = 64
N_SELECT = 16
WINDOW = 512
FORCED_SCORE = 1e4
PEER_HEADS = 8
PEER_N_KEYS = 128
PEER_QUERY_DIM = 256
PEER_TOPK = 16
RMS_EPS = 1e-6
NEG_INF = -1e30
SCALE = HEAD_DIM ** -0.5
SB_DEAD_LOG = 104.0

LANES = 128
PEER_GATHER_BUFFERS = 8
SC_GATHER_WINDOW = 32
SC_CHUNKS = 4
SC_CHUNK_TILES = 28
VMEM_LIMIT_BYTES = 56 * 1024 * 1024

_NT = (((1,), (1,)), ((), ()))


def _params(*sem):
    return pltpu.CompilerParams(dimension_semantics=sem, vmem_limit_bytes=VMEM_LIMIT_BYTES)


def _dot(a, b):
    return jnp.dot(a, b, preferred_element_type=F32)


def _dot_nt(a, b):
    return lax.dot_general(a, b, _NT, preferred_element_type=F32)


def _sigmoid(x):
    return 1.0 / (1.0 + jnp.exp(-x))


def _rms(x, g):
    return x * lax.rsqrt(jnp.mean(x * x, axis=-1, keepdims=True) + RMS_EPS) * g


def _norm_matmul_kernel(x_ref, g_ref, w_ref, o_ref, *, tn):
    h = _rms(x_ref[...], g_ref[...]).astype(BF16)
    for c in range(0, o_ref.shape[1], tn):
        o_ref[:, c:c + tn] = _dot(h, w_ref[:, c:c + tn]).astype(o_ref.dtype)


def _norm_matmul(x2d, g, w, out_dtype, tm, tn):
    n, d = x2d.shape
    m = w.shape[1]
    return pl.pallas_call(
        functools.partial(_norm_matmul_kernel, tn=tn),
        grid=(n // tm,),
        in_specs=[
            pl.BlockSpec((tm, d), lambda i: (i, 0)),
            pl.BlockSpec((1, d), lambda i: (0, 0)),
            pl.BlockSpec((d, m), lambda i: (0, 0)),
        ],
        out_specs=pl.BlockSpec((tm, m), lambda i: (i, 0)),
        out_shape=jax.ShapeDtypeStruct((n, m), out_dtype),
        compiler_params=_params("parallel"),
        name="norm_matmul",
    )(x2d, g.reshape(1, d), w)


def _norm_matmul_heads_kernel(x_ref, g_ref, w_ref, o_ref, *, tn):
    h = _rms(x_ref[...], g_ref[...]).astype(BF16)
    per = tn // HEAD_DIM
    for c in range(0, w_ref.shape[1], tn):
        res = _dot(h, w_ref[:, c:c + tn]).astype(o_ref.dtype)
        for j in range(per):
            o_ref[c // HEAD_DIM + j] = res[:, j * HEAD_DIM:(j + 1) * HEAD_DIM]


def _norm_matmul_heads(x2d, g, w, tm, tn):
    n, d = x2d.shape
    m = w.shape[1]
    nh = m // HEAD_DIM
    return pl.pallas_call(
        functools.partial(_norm_matmul_heads_kernel, tn=tn),
        grid=(n // tm,),
        in_specs=[
            pl.BlockSpec((tm, d), lambda i: (i, 0)),
            pl.BlockSpec((1, d), lambda i: (0, 0)),
            pl.BlockSpec((d, m), lambda i: (0, 0)),
        ],
        out_specs=pl.BlockSpec((nh, tm, HEAD_DIM), lambda i: (0, i, 0)),
        out_shape=jax.ShapeDtypeStruct((nh, n, HEAD_DIM), BF16),
        compiler_params=_params("parallel"),
        name="norm_matmul_heads",
    )(x2d, g.reshape(1, d), w)


def _sb_kernel(q_ref, k_ref, v_ref, o_ref, *, tile):
    qi = pl.program_id(2)
    nh = q_ref.shape[0]
    row = lax.broadcasted_iota(I32, (tile, tile), 0)
    col = lax.broadcasted_iota(I32, (tile, tile), 1)
    lower = row > col
    later = lower.astype(BF16)
    qs = [q_ref[hh, 0] for hh in range(nh)]

    def visit(hh, ks, c, acc, diagonal):
        k = k_ref[hh, 0, pl.ds(ks, tile), :]
        v = v_ref[hh, 0, pl.ds(ks, tile), :]
        z = _dot_nt(qs[hh], k) * SCALE
        sp = jnp.maximum(z, 0.0) + jnp.log(1.0 + jnp.exp(-jnp.abs(z)))
        if diagonal:
            sp = jnp.where(lower, sp, 0.0)
        hi = sp.astype(BF16)
        lo = (sp - hi.astype(F32)).astype(BF16)
        after = _dot(hi, later) + _dot(lo, later)
        a = jnp.exp(z - sp - after - c)
        if diagonal:
            a = jnp.where(lower, a, 0.0)
        return c + jnp.sum(sp, axis=1, keepdims=True), acc + _dot(a.astype(BF16), v)

    q0 = pl.multiple_of(qi * tile, tile)
    state = []
    for hh in range(nh):
        state.extend(visit(hh, q0, jnp.zeros((tile, 1), F32), jnp.zeros((tile, HEAD_DIM), F32), True))

    def smallest_carry(state):
        c = state[0]
        for hh in range(1, nh):
            c = jnp.minimum(c, state[2 * hh])
        return jnp.min(c)

    def live(carry):
        i, c_min, _ = carry
        return (i < qi) & (c_min <= SB_DEAD_LOG)

    def body(carry):
        i, _, state = carry
        ks = pl.multiple_of((qi - 1 - i) * tile, tile)
        out = []
        for hh in range(nh):
            out.extend(visit(hh, ks, state[2 * hh], state[2 * hh + 1], False))
        return i + 1, smallest_carry(out), tuple(out)

    _, _, state = lax.while_loop(live, body, (0, smallest_carry(state), tuple(state)))
    o_ref[0] = jnp.concatenate([state[2 * hh + 1] for hh in range(nh)], axis=1).astype(o_ref.dtype)


def _sb_attention(hd, q0, k0, v0, h, tile=256):
    _, b, t, dh = hd.shape
    hp = 2
    tq = tile
    assert q0 % hp == 0 and k0 % hp == 0 and v0 % hp == 0
    return pl.pallas_call(
        functools.partial(_sb_kernel, tile=tile),
        grid=(b, h // hp, t // tq),
        in_specs=[
            pl.BlockSpec((hp, 1, tq, dh), lambda bi, hi, qi: (q0 // hp + hi, bi, qi, 0)),
            pl.BlockSpec((hp, 1, t, dh), lambda bi, hi, qi: (k0 // hp + hi, bi, 0, 0)),
            pl.BlockSpec((hp, 1, t, dh), lambda bi, hi, qi: (v0 // hp + hi, bi, 0, 0)),
        ],
        out_specs=pl.BlockSpec((1, tq, hp * dh), lambda bi, hi, qi: (bi, qi, hi)),
        out_shape=jax.ShapeDtypeStruct((b, t, h * dh), BF16),
        compiler_params=_params("parallel", "parallel", "arbitrary"),
        name="sb_attn",
    )(hd, hd, hd)


def _compress_kernel(x_ref, pe_ref, w_ref, o_ref):
    x = x_ref[0, 0]
    nc = x.shape[0]
    w_lo, w_hi = w_ref[0], w_ref[1]
    first = _dot(x, w_lo)
    second = _dot(x, w_hi)
    feat = pe_ref.shape[1]
    pe_lo = jnp.broadcast_to(pe_ref[0:1, :], (8, feat)).astype(BF16)
    pe_hi = jnp.broadcast_to(pe_ref[1:2, :], (8, feat)).astype(BF16)
    bias = _dot(pe_lo, w_lo)[0:1] + _dot(pe_hi, w_hi)[0:1]
    o_ref[0, 0] = (first + pltpu.roll(second, nc - 1, 0) + bias).astype(o_ref.dtype)


def _nsa_compress(hd, head0, pe, w):
    nh, b, t, dh = hd.shape
    g = NSA_KV_GROUPS
    nc = t // CMP_STRIDE
    feat = CMP_STRIDE * dh
    x = hd[head0:head0 + g].reshape(g, b, nc, feat)
    pe2 = pe.reshape(2, feat)
    w2 = w.reshape(2, feat, dh).astype(BF16)
    return pl.pallas_call(
        _compress_kernel,
        grid=(b, g),
        in_specs=[
            pl.BlockSpec((1, 1, nc, feat), lambda bi, gi: (gi, bi, 0, 0)),
            pl.BlockSpec((2, feat), lambda bi, gi: (0, 0)),
            pl.BlockSpec((2, feat, dh), lambda bi, gi: (0, 0, 0)),
        ],
        out_specs=pl.BlockSpec((1, 1, nc, dh), lambda bi, gi: (bi, gi, 0, 0)),
        out_shape=jax.ShapeDtypeStruct((b, g, nc, dh), BF16),
        compiler_params=_params("parallel", "parallel"),
        name="nsa_compress",
    )(x, pe2, w2)


def _nsa_kernel(slopes_ref, q_ref, kc_ref, vc_ref, ks_ref, vs_ref, kw_ref, vw_ref, gl_ref, pool_ref, ex_ref,
                o_ref, used_ref, *, tq, tk, n_sel, real_blocks):
    grp = pl.program_id(1)
    t0 = pl.program_id(2) * tq
    nc = kc_ref.shape[2]
    rr = NSA_GROUP
    t = t0 + lax.broadcasted_iota(I32, (tq, 1), 0)

    def stack(x):
        return jnp.concatenate([x] * rr, axis=0)

    q = q_ref[:, 0].reshape(rr * tq, HEAD_DIM)
    slope = jnp.concatenate([jnp.full((tq, 1), slopes_ref[grp * rr + r], F32) for r in range(rr)], axis=0)

    def heads_sum(x):
        out = x[0:tq]
        for r in range(1, rr):
            out = out + x[r * tq:(r + 1) * tq]
        return out

    def masked_softmax(s, valid):
        p = jnp.where(valid, jnp.exp(s - jnp.max(s, axis=1, keepdims=True)), 0.0)
        denom = jnp.sum(p, axis=1, keepdims=True)
        return p / jnp.where(denom > 0, denom, 1.0)

    lane = lax.broadcasted_iota(I32, (1, nc), 1)
    dist_c = (t - (lane * CMP_STRIDE + (CMP_LEN - 1))).astype(F32)
    valid_c = stack(dist_c >= 0)
    s = _dot_nt(q, kc_ref[0, 0]) * SCALE - slope * stack(dist_c)
    p = masked_softmax(jnp.where(valid_c, s, NEG_INF), valid_c)
    o_cmp = _dot(p.astype(BF16), vc_ref[0, 0])
    psum = heads_sum(p)

    n_blk = pool_ref.shape[0]
    hi = psum.astype(BF16)
    rest = psum - hi.astype(F32)
    mid = rest.astype(BF16)
    lo = (rest - mid.astype(F32)).astype(BF16)
    pool = pool_ref[...]
    imp = _dot_nt(pool, hi) + _dot_nt(pool, mid) + _dot_nt(pool, lo)
    blk = lax.broadcasted_iota(I32, (n_blk, 1), 0)
    cur = (t0 + lax.broadcasted_iota(I32, (1, tq), 1)) // SEL_BLOCK
    forced = (blk == 0) | (blk == cur) | (blk == cur - 1)
    imp = jnp.where(forced, FORCED_SCORE, jnp.where(blk <= cur, imp, -1.0))
    imp = jnp.where(blk < real_blocks, imp, -jnp.inf)
    blk_f = blk.astype(F32)

    def pick(_, carry):
        imp, sel = carry
        best = jnp.max(imp, axis=0, keepdims=True)
        first = jnp.min(jnp.where(imp == best, blk_f, float(n_blk)), axis=0, keepdims=True)
        hit = blk_f == first
        return jnp.where(hit, -jnp.inf, imp), jnp.where(hit, 1.0, sel)

    _, sel = lax.fori_loop(0, n_sel, pick, (imp, jnp.zeros((n_blk, tq), F32)))
    picked = jnp.max(sel, axis=1, keepdims=True)
    per_tile = tk // SEL_BLOCK
    n_causal = (t0 + tq + tk - 1) // tk
    n_used = jnp.int32(0)
    for j in range(used_ref.shape[0]):
        used_ref[n_used] = j
        hit = (jnp.max(picked[j * per_tile:(j + 1) * per_tile, :]) > 0) & (j < n_causal)
        n_used = n_used + hit.astype(I32)
    sel = sel.T.astype(BF16)

    span = WINDOW + tq
    w0 = pl.multiple_of(jnp.maximum(t0 - WINDOW, 0), tq)
    dist_w = t - (w0 + lax.broadcasted_iota(I32, (1, span), 1))
    valid_w = stack((dist_w >= 0) & (dist_w < WINDOW))
    s = _dot_nt(q, kw_ref[0, 0, pl.ds(w0, span), :]) * SCALE - slope * stack(dist_w.astype(F32))
    p = masked_softmax(jnp.where(valid_w, s, NEG_INF), valid_w)
    o_win = _dot(p.astype(BF16), vw_ref[0, 0, pl.ds(w0, span), :])

    kcol = lax.broadcasted_iota(I32, (1, tk), 1)

    def sel_body(i, carry):
        m, l, acc = carry
        kb = used_ref[i]
        ks0 = pl.multiple_of(kb * tk, tk)
        chosen = _dot(sel, ex_ref[kb]) > 0.5
        dist = t - (ks0 + kcol)
        valid = stack(chosen & (dist >= 0))
        s = _dot_nt(q, ks_ref[0, 0, pl.ds(ks0, tk), :]) * SCALE - slope * stack(dist.astype(F32))
        s = jnp.where(valid, s, NEG_INF)
        m_new = jnp.maximum(m, jnp.max(s, axis=1, keepdims=True))
        alpha = jnp.exp(m - m_new)
        p = jnp.where(valid, jnp.exp(s - m_new), 0.0)
        l = alpha * l + jnp.sum(p, axis=1, keepdims=True)
        acc = alpha * acc + _dot(p.astype(BF16), vs_ref[0, 0, pl.ds(ks0, tk), :])
        return m_new, l, acc

    init = (jnp.full((rr * tq, 1), NEG_INF, F32), jnp.zeros((rr * tq, 1), F32),
            jnp.zeros((rr * tq, HEAD_DIM), F32))
    _, l_sel, acc_sel = lax.fori_loop(0, n_used, sel_body, init)
    o_sel = acc_sel / l_sel

    gates = _sigmoid(gl_ref[0, 0])
    outs = []
    for r in range(rr):
        rows = slice(r * tq, (r + 1) * tq)
        outs.append(gates[:, 3 * r:3 * r + 1] * o_cmp[rows] + gates[:, 3 * r + 1:3 * r + 2] * o_sel[rows]
                    + gates[:, 3 * r + 2:3 * r + 3] * o_win[rows])
    o_ref[0, 0] = jnp.concatenate(outs, axis=1).astype(o_ref.dtype)


def _nsa_attention(hd, q0, sel0, kc, vc, gate_logits, slopes, tq=256, tk=256):
    _, b, t, dh = hd.shape
    g = NSA_KV_GROUPS
    assert q0 % NSA_GROUP == 0
    nc = kc.shape[2]
    n_sel = min(N_SELECT, t // SEL_BLOCK)
    real_blocks = t // SEL_BLOCK
    n_blk = LANES
    assert t % tk == 0 and t >= WINDOW + tq and real_blocks <= n_blk
    blk_ids = jnp.arange(n_blk, dtype=I32)
    pool = (blk_ids[:, None] == jnp.arange(nc, dtype=I32)[None, :] // (SEL_BLOCK // CMP_STRIDE)).astype(BF16)
    key_blk = (jnp.arange(t, dtype=I32) // SEL_BLOCK).reshape(t // tk, 1, tk)
    expand = (blk_ids[None, :, None] == key_blk).astype(BF16)
    def kv_spec(j):
        return pl.BlockSpec((1, 1, t, dh), lambda bi, gi, qi: (sel0 + j * g + gi, bi, 0, 0))

    c_spec = pl.BlockSpec((1, 1, nc, dh), lambda bi, gi, qi: (bi, gi, 0, 0))
    return pl.pallas_call(
        functools.partial(_nsa_kernel, tq=tq, tk=tk, n_sel=n_sel, real_blocks=real_blocks),
        grid=(b, g, t // tq),
        in_specs=[
            pl.BlockSpec(memory_space=pltpu.SMEM),
            pl.BlockSpec((NSA_GROUP, 1, tq, dh), lambda bi, gi, qi: (q0 // NSA_GROUP + gi, bi, qi, 0)),
            c_spec, c_spec, kv_spec(0), kv_spec(1), kv_spec(2), kv_spec(3),
            pl.BlockSpec((1, 1, tq, LANES), lambda bi, gi, qi: (bi, gi, qi, 0)),
            pl.BlockSpec(pool.shape, lambda bi, gi, qi: (0, 0)),
            pl.BlockSpec(expand.shape, lambda bi, gi, qi: (0, 0, 0)),
        ],
        out_specs=pl.BlockSpec((1, 1, tq, NSA_GROUP * dh), lambda bi, gi, qi: (bi, gi, qi, 0)),
        out_shape=jax.ShapeDtypeStruct((b, g, t, NSA_GROUP * dh), BF16),
        scratch_shapes=[pltpu.SMEM((t // tk,), I32)],
        compiler_params=_params("parallel", "parallel", "arbitrary"),
        name="nsa_attn",
    )(slopes, hd, kc, vc, hd, hd, hd, hd, gate_logits, pool, expand)


def _mem_kernel(q_ref, kv_ref, o_ref):
    kv = kv_ref[0]
    outs = []
    for h in range(MEM_HEADS):
        sl = slice(h * HEAD_DIM, (h + 1) * HEAD_DIM)
        s = _dot_nt(q_ref[h, 0], kv[:, sl]) * SCALE
        p = jnp.exp(s - jnp.max(s, axis=1, keepdims=True))
        p = p / jnp.sum(p, axis=1, keepdims=True)
        outs.append(_dot(p.astype(BF16), kv[:, MEM_W + h * HEAD_DIM:MEM_W + (h + 1) * HEAD_DIM]))
    o_ref[0] = jnp.concatenate(outs, axis=1).astype(o_ref.dtype)


def _mem_attention(hd, q0, mkv, tq=512):
    _, b, t, dh = hd.shape
    w = MEM_HEADS * dh
    m = mkv.shape[1]
    assert q0 % MEM_HEADS == 0
    return pl.pallas_call(
        _mem_kernel,
        grid=(b, t // tq),
        in_specs=[
            pl.BlockSpec((MEM_HEADS, 1, tq, dh), lambda bi, qi: (q0 // MEM_HEADS, bi, qi, 0)),
            pl.BlockSpec((1, m, 2 * w), lambda bi, qi: (bi, 0, 0)),
        ],
        out_specs=pl.BlockSpec((1, tq, w), lambda bi, qi: (bi, qi, 0)),
        out_shape=jax.ShapeDtypeStruct((b, t, w), BF16),
        compiler_params=_params("parallel", "parallel"),
        name="mem_attn",
    )(hd, mkv)


def _merge_kernel(sb_ref, nsa0_ref, nsa1_ref, mem_ref, mg_ref, bm_ref, x_ref, wsb_ref, wnsa_ref, wmem_ref,
                  wout_ref, fg_ref, wq_ref, x1_ref, h2_ref, q_ref):
    d = x_ref.shape[1]
    gw = nsa0_ref.shape[-1]
    nsa = _dot(nsa0_ref[0, 0], wnsa_ref[0:gw, :]) + _dot(nsa1_ref[0, 0], wnsa_ref[gw:2 * gw, :])
    branches = (_dot(sb_ref[...], wsb_ref[...]), nsa, _dot(mem_ref[...], wmem_ref[...]))
    merged = jnp.zeros_like(branches[0])
    for j in range(N_BRANCHES):
        gate = _sigmoid(mg_ref[:, j * d:(j + 1) * d] + bm_ref[:, j * d:(j + 1) * d])
        merged = merged + gate * branches[j]
    x1 = x_ref[...] + _dot(merged.astype(BF16), wout_ref[...])
    x1_ref[...] = x1
    h2 = _rms(x1, fg_ref[...])
    h2_ref[...] = h2
    q = _dot(h2.astype(BF16), wq_ref[...]).astype(q_ref.dtype)
    for c in range(q_ref.shape[0]):
        q_ref[c] = q[:, c * LANES:(c + 1) * LANES]


def _merge(sb, nsa, mem, gates_slab, b_merge, x2d, w_sb, w_nsa, w_mem, w_out, ffn_g, w_q, tm=256):
    n, d = x2d.shape
    nq = w_q.shape[1] // LANES
    _, g, t, gw = nsa.shape
    assert g == 2 and t % tm == 0
    per_row = t // tm
    row = lambda w: pl.BlockSpec((tm, w), lambda i: (i, 0))
    nsa_group = lambda gi: pl.BlockSpec((1, 1, tm, gw), lambda i: (i // per_row, gi, i % per_row, 0))
    full = lambda a: pl.BlockSpec(a.shape, lambda i: (0,) * a.ndim)
    bm = b_merge.reshape(1, -1)
    fg = ffn_g.reshape(1, d)
    return pl.pallas_call(
        _merge_kernel,
        grid=(n // tm,),
        in_specs=[row(sb.shape[1]), nsa_group(0), nsa_group(1), row(mem.shape[1]), row(N_BRANCHES * d),
                  full(bm), row(d), full(w_sb), full(w_nsa), full(w_mem), full(w_out), full(fg),
                  full(w_q)],
        out_specs=[row(d), row(d), pl.BlockSpec((nq, tm, LANES), lambda i: (0, i, 0))],
        out_shape=[jax.ShapeDtypeStruct((n, d), F32), jax.ShapeDtypeStruct((n, d), F32),
                   jax.ShapeDtypeStruct((nq, n, LANES), BF16)],
        compiler_params=_params("parallel"),
        name="merge",
    )(sb, nsa, nsa, mem, gates_slab, bm, x2d, w_sb, w_nsa, w_mem, w_out, fg, w_q)


def _peer_topk_kernel(q_ref, sk_ref, idx_ref, gw_ref, tv_ref, ti_ref, bv_ref):
    tt = q_ref.shape[1]
    kk = PEER_TOPK
    nk = PEER_N_KEYS
    rid = lax.broadcasted_iota(I32, (nk, tt), 0).astype(F32)
    for c in range(2 * PEER_HEADS):
        scores = _dot_nt(sk_ref[c], q_ref[c])

        def pick(k, s, c=c):
            best = jnp.max(s, axis=0, keepdims=True)
            first = jnp.min(jnp.where(s == best, rid, float(nk)), axis=0, keepdims=True)
            tv_ref[c, pl.ds(k, 1), :] = best
            ti_ref[c, pl.ds(k, 1), :] = first
            return jnp.where(rid == first, -jnp.inf, s)

        lax.fori_loop(0, kk, pick, scores)

    sub = 8
    widths = [min(kk, -(-(kk // (i + 1)) // sub) * sub) for i in range(kk // 2)]
    n_cand = sum(widths) + kk // 2
    pos = lax.broadcasted_iota(I32, (n_cand, tt), 0).astype(F32)
    for h in range(PEER_HEADS):
        s0, s1 = tv_ref[2 * h], tv_ref[2 * h + 1]
        i0, i1 = ti_ref[2 * h], ti_ref[2 * h + 1]
        cand, cidx = [], []
        for i, wd in enumerate(widths):
            keep = lax.broadcasted_iota(I32, (wd, 1), 0) < kk // (i + 1)
            cand.append(jnp.where(keep, s0[i:i + 1, :] + s1[0:wd, :], -jnp.inf))
            cidx.append(i0[i:i + 1, :] * float(nk) + i1[0:wd, :])
        cand.append(s0[kk // 2:, :] + s1[0:1, :])
        cidx.append(i0[kk // 2:, :] * float(nk) + i1[0:1, :])
        cand = jnp.concatenate(cand, axis=0)
        cidx = jnp.concatenate(cidx, axis=0)

        def pick2(k, s, h=h, cidx=cidx):
            best = jnp.max(s, axis=0, keepdims=True)
            first = jnp.min(jnp.where(s == best, pos, float(n_cand)), axis=0, keepdims=True)
            hit = pos == first
            bv_ref[pl.ds(k, 1), :] = best
            expert = jnp.max(jnp.where(hit, cidx, -1.0), axis=0, keepdims=True)
            idx_ref[pl.ds(h * kk + k, 1), :] = expert.astype(I32)
            return jnp.where(hit, -jnp.inf, s)

        lax.fori_loop(0, kk, pick2, cand)
        best = bv_ref[...]
        e = jnp.exp(best - best[0:1, :])
        gw_ref[h * kk:(h + 1) * kk, :] = e / jnp.sum(e, axis=0, keepdims=True)


def _peer_topk(q_chunks, subkeys, tt=256):
    nchunk, n, half = q_chunks.shape
    slots = PEER_HEADS * PEER_TOPK
    return pl.pallas_call(
        _peer_topk_kernel,
        grid=(n // tt,),
        in_specs=[
            pl.BlockSpec((nchunk, tt, half), lambda i: (0, i, 0)),
            pl.BlockSpec(subkeys.shape, lambda i: (0, 0, 0)),
        ],
        out_specs=[pl.BlockSpec((slots, tt), lambda i: (0, i)),
                   pl.BlockSpec((slots, tt), lambda i: (0, i))],
        out_shape=[jax.ShapeDtypeStruct((slots, n), I32), jax.ShapeDtypeStruct((slots, n), F32)],
        scratch_shapes=[pltpu.VMEM((nchunk, PEER_TOPK, tt), F32),
                        pltpu.VMEM((nchunk, PEER_TOPK, tt), F32),
                        pltpu.VMEM((PEER_TOPK, tt), F32)],
        compiler_params=_params("parallel"),
        name="peer_topk",
    )(q_chunks, subkeys)


def _peer_ffn_kernel(*refs, first_tiles, staged):
    ns = len(staged)
    idx_hbm, tables = refs[0], refs[1:1 + ns]
    io = [refs[1 + ns + 3 * s:4 + ns + 3 * s] for s in range(ns)]
    fg_ref = refs[1 + 4 * ns]
    ys = refs[2 + 4 * ns:2 + 5 * ns]
    idx_smem, bufs = refs[2 + 5 * ns], refs[3 + 5 * ns:3 + 6 * ns]
    idx_sem, sems = refs[3 + 6 * ns], refs[4 + 6 * ns:4 + 7 * ns]
    tt, d = io[0][1].shape
    slots = io[0][0].shape[0]
    rows = d // (2 * LANES)
    tile = pl.program_id(0)
    nbuf = bufs[0].shape[0]
    tok_lane = lax.broadcasted_iota(I32, (1, tt), 1)

    def halves(words):
        return tuple(pltpu.unpack_elementwise(words, index=i, packed_dtype=BF16, unpacked_dtype=F32)
                     for i in range(2))

    def make_stream(s):
        table, buf, row_sem, y_ref = tables[s], bufs[s], sems[s], ys[s]
        gw_ref, h_ref, _ = io[s]

        if staged[s]:
            def start(tok, slot):
                for r in range(buf.shape[1]):
                    pltpu.make_async_copy(table.at[tile * tt + tok, :, r, :], buf.at[slot, r],
                                          row_sem.at[slot]).start(priority=r % 2)
        else:
            idx_copy = pltpu.make_async_copy(idx_hbm.at[tile + first_tiles[s]], idx_smem, idx_sem)
            idx_copy.start()
            idx_copy.wait()

            def start(tok, slot):
                for e in range(slots):
                    pltpu.make_async_copy(table.at[idx_smem[tok * slots + e]], buf.at[slot, :, e, :],
                                          row_sem.at[slot]).start(priority=e % 2)

        def wait(slot):
            pltpu.make_async_copy(buf.at[slot], buf.at[slot], row_sem.at[slot]).wait()

        def expert_weights(tok, slot):
            h = h_ref[pl.ds(tok, 1), :]
            prod = jnp.zeros((slots, LANES), F32)
            for r in range(rows):
                lo, hi = halves(buf[slot, r])
                prod = prod + lo * h[:, r * LANES:(r + 1) * LANES]
                prod = prod + hi * h[:, d // 2 + r * LANES:d // 2 + (r + 1) * LANES]
            a = jnp.sum(prod, axis=1, keepdims=True)
            act = 0.5 * a * (1.0 + lax.erf(a * (2.0 ** -0.5)))
            gate = jnp.sum(jnp.where(tok_lane == tok, gw_ref[...], 0.0), axis=1, keepdims=True)
            return gate * act

        def weighted_values(tok, slot, w):
            parts = [halves(buf[slot, rows + r]) for r in range(rows)]
            y_ref[pl.ds(tok, 1), :] = jnp.concatenate(
                [jnp.sum(parts[r][i] * w, axis=0, keepdims=True) for i in range(2) for r in range(rows)],
                axis=1)

        def step(tok, slot, w, refill):
            nxt = (slot + 1) % nbuf
            wait(nxt)
            w_next = expert_weights(tok + 1, nxt)
            weighted_values(tok, slot, w)
            if refill:
                start(tok + nbuf, slot)
            return w_next

        return start, wait, expert_weights, weighted_values, step

    streams = [make_stream(s) for s in range(ns)]

    ws = []
    for start, wait, expert_weights, _, _ in streams:
        for tok in range(nbuf):
            start(tok, tok)
    for start, wait, expert_weights, _, _ in streams:
        wait(0)
        ws.append(expert_weights(0, 0))

    def round_of_steps(i, ws):
        ws = list(ws)
        for slot in range(nbuf):
            for s in range(ns):
                ws[s] = streams[s][4](i * nbuf + slot, slot, ws[s], True)
        return tuple(ws)

    n_rounds = tt // nbuf - 1
    ws = list(lax.fori_loop(0, n_rounds, round_of_steps, tuple(ws)))
    for slot in range(nbuf - 1):
        for s in range(ns):
            ws[s] = streams[s][4](n_rounds * nbuf + slot, slot, ws[s], False)
    for s in range(ns):
        streams[s][3](tt - 1, nbuf - 1, ws[s])
        ys[s][...] = _rms(io[s][2][...] + ys[s][...], fg_ref[...])


def _pack_tables_kernel(u_ref, v_ref, o_ref):
    d = u_ref.shape[1]
    rows = d // (2 * LANES)
    for base, src in ((0, u_ref), (rows, v_ref)):
        for r in range(rows):
            lo = src[:, r * LANES:(r + 1) * LANES]
            hi = src[:, d // 2 + r * LANES:d // 2 + (r + 1) * LANES]
            o_ref[:, base + r, :] = pltpu.pack_elementwise([lo, hi], packed_dtype=BF16)


def _pack_tables(u, v, te=512):
    n_exp, d = u.shape
    rows = d // LANES
    return pl.pallas_call(
        _pack_tables_kernel,
        grid=(n_exp // te,),
        in_specs=[pl.BlockSpec((te, d), lambda i: (i, 0)), pl.BlockSpec((te, d), lambda i: (i, 0))],
        out_specs=pl.BlockSpec((te, rows, LANES), lambda i: (i, 0, 0)),
        out_shape=jax.ShapeDtypeStruct((n_exp, rows, LANES), jnp.uint32),
        compiler_params=_params("parallel"),
        name="pack_tables",
    )(u, v)


def _sc_gather_slabs(slabs, slab_idx):
    m = slab_idx.shape[0]
    mesh = plsc.VectorSubcoreMesh(core_axis_name="core", subcore_axis_name="subcore")
    idx_rows = jnp.pad(slab_idx.reshape(m // SC_GATHER_WINDOW, SC_GATHER_WINDOW),
                       ((0, 0), (0, LANES - SC_GATHER_WINDOW)))

    @pl.kernel(out_type=jax.ShapeDtypeStruct((m,) + slabs.shape[1:], slabs.dtype), mesh=mesh)
    def gather(slabs_hbm, idx_hbm, out_hbm):
        def window(idx_vmem, out_vmem):
            pltpu.sync_copy(slabs_hbm.at[idx_vmem.at[0, pl.ds(0, SC_GATHER_WINDOW)]], out_vmem)

        pltpu.emit_pipeline(
            window,
            grid=(m // SC_GATHER_WINDOW,),
            in_specs=[pl.BlockSpec((1, LANES), index_map=lambda i: (i, 0))],
            out_specs=[pl.BlockSpec((SC_GATHER_WINDOW,) + slabs.shape[1:], index_map=lambda i: (i, 0, 0))],
            core_axis_name=("core", "subcore"),
            dimension_semantics=(pltpu.PARALLEL,),
            trace_scopes=False,
        )(idx_hbm, out_hbm)

    return gather(slabs, idx_rows)


def _peer_ffn(idx_tiles, tables, gw, h2, x1, final_g, tt, first_tiles, n_tiles, staged):
    _, d = h2.shape
    slots = gw.shape[0]
    rows = d // LANES
    ns = len(staged)
    assert sum(not st for st in staged) <= 1, "one index buffer: at most one gathering stream"
    hbm = pl.BlockSpec(memory_space=pl.ANY)
    io_specs, io_args = [], []
    for s in range(ns):
        io_specs += [pl.BlockSpec((slots, tt), lambda i, f=first_tiles[s]: (0, i + f)),
                     pl.BlockSpec((tt, d), lambda i, f=first_tiles[s]: (i + f, 0)),
                     pl.BlockSpec((tt, d), lambda i, f=first_tiles[s]: (i + f, 0))]
        io_args += [gw, h2, x1]
    buf = pltpu.VMEM((PEER_GATHER_BUFFERS, rows, slots, LANES), tables[0].dtype)
    sem = pltpu.SemaphoreType.DMA((PEER_GATHER_BUFFERS,))
    outs = pl.pallas_call(
        functools.partial(_peer_ffn_kernel, first_tiles=tuple(first_tiles), staged=tuple(staged)),
        grid=(n_tiles,),
        in_specs=[hbm] + [hbm] * ns + io_specs + [pl.BlockSpec((1, d), lambda i: (0, 0))],
        out_specs=[pl.BlockSpec((tt, d), lambda i: (i, 0))] * ns,
        out_shape=[jax.ShapeDtypeStruct((n_tiles * tt, d), F32)] * ns,
        scratch_shapes=[pltpu.SMEM((tt * slots,), I32)] + [buf] * ns + [pltpu.SemaphoreType.DMA] + [sem] * ns,
        compiler_params=_params("arbitrary"),
        name="peer_ffn_" + "_".join("staged" if st else "gather" for st in staged),
    )(idx_tiles, *tables, *io_args, final_g.reshape(1, d))
    return list(outs)


def _mixers(x, mem, mix_g, mem_g, w_in, b_merge, pe_k, pe_v, cw_k, cw_v, w_mem_kv, w_sb_br, w_nsa_br,
            w_mem_br, w_out, ffn_g, peer_w_q, subkeys):
    b, t, d = x.shape
    m = mem.shape[1]
    g = NSA_KV_GROUPS
    n = b * t
    x2d = x.reshape(n, d)

    o_sbq, o_sbk, o_sbv = 0, SB_W, 2 * SB_W
    o_nq = 3 * SB_W
    o_nkv = o_nq + NSA_W
    o_ng = o_nkv + 6 * NSA_KV_W
    o_mq = o_ng + NSA_HEADS * 3
    o_mg = o_mq + MEM_W
    w_act = jnp.concatenate([w_in[:, :o_ng], w_in[:, o_mq:o_mg]], axis=1).astype(BF16)
    gate_pad = LANES - NSA_HEADS * 3
    w_gate = jnp.concatenate([w_in[:, o_mg:], w_in[:, o_ng:o_mq], jnp.zeros((d, gate_pad), w_in.dtype)],
                             axis=1).astype(BF16)
    hd = _norm_matmul_heads(x2d, mix_g, w_act, tm=256, tn=512).reshape(-1, b, t, HEAD_DIM)
    gates_slab = _norm_matmul(x2d, mix_g, w_gate, F32, tm=256, tn=640)
    head_of = lambda col: col // HEAD_DIM

    sb_out = _sb_attention(hd, head_of(o_sbq), head_of(o_sbk), head_of(o_sbv), SB_HEADS)

    kc = _nsa_compress(hd, head_of(o_nkv), pe_k, cw_k)
    vc = _nsa_compress(hd, head_of(o_nkv) + g, pe_v, cw_v)
    gl = gates_slab[:, N_BRANCHES * d:N_BRANCHES * d + NSA_HEADS * 3]
    gl = gl.reshape(b, t, g, NSA_GROUP * 3).transpose(0, 2, 1, 3)
    gl = jnp.pad(gl, ((0, 0), (0, 0), (0, 0), (0, LANES - NSA_GROUP * 3)))
    slopes = jnp.asarray([2.0 ** (-8.0 * (h + 1) / NSA_HEADS) for h in range(NSA_HEADS)], F32)
    nsa_out = _nsa_attention(hd, head_of(o_nq), head_of(o_nkv) + 2 * g, kc, vc, gl, slopes)

    mkv = _norm_matmul(mem.reshape(b * m, d), mem_g, w_mem_kv.astype(BF16), BF16, tm=256, tn=512)
    mem_out = _mem_attention(hd, head_of(o_nkv + 6 * NSA_KV_W), mkv.reshape(b, m, 2 * MEM_W))

    x1, h2, q_chunks = _merge(
        sb_out.reshape(n, SB_W), nsa_out, mem_out.reshape(n, MEM_W), gates_slab, b_merge, x2d,
        w_sb_br.astype(BF16), w_nsa_br.astype(BF16), w_mem_br.astype(BF16), w_out.astype(BF16), ffn_g,
        peer_w_q.astype(BF16))

    half = PEER_QUERY_DIM // 2
    idx, gw = _peer_topk(q_chunks, subkeys.reshape(2 * PEER_HEADS, PEER_N_KEYS, half).astype(BF16))
    return x1, h2, idx.T, gw


def _layer(x, mem, *params_and_tables):
    uv, out_g = params_and_tables[-2:]
    b, t, d = x.shape
    tt = 128
    x1, h2, idx_tok, gw = _mixers(x, mem, *params_and_tables[:-2])
    n, slots = idx_tok.shape
    n_tiles = n // tt
    idx_tiles = idx_tok.reshape(n_tiles, tt * slots)
    chunk = SC_CHUNK_TILES
    sc_first = n_tiles - SC_CHUNKS * chunk
    lead = sc_first - SC_CHUNKS * chunk
    assert lead > 0

    def stage(c):
        lo = (sc_first + c * chunk) * tt
        slabs = _sc_gather_slabs(uv, idx_tok[lo:lo + chunk * tt].reshape(-1))
        return slabs.reshape((chunk * tt, slots) + uv.shape[1:])

    staged = [stage(c) for c in range(SC_CHUNKS)]
    ffn = functools.partial(_peer_ffn, idx_tiles, gw=gw, h2=h2, x1=x1, final_g=out_g, tt=tt)
    own = ffn(tables=[uv], first_tiles=[0], n_tiles=lead, staged=[False])
    theirs = []
    for c in range(SC_CHUNKS):
        mine, other = ffn(tables=[uv, staged[c]], first_tiles=[lead + c * chunk, sc_first + c * chunk],
                          n_tiles=chunk, staged=[False, True])
        own.append(mine)
        theirs.append(other)
    return jnp.concatenate(own + theirs, axis=0).reshape(b, t, d)


def kernel(x, mem, mix_norm_g, mem_norm_g, w_in, b_merge, cmp_pe_k, cmp_pe_v, cmp_w_k, cmp_w_v, w_mem_kv, w_sb_br, w_nsa_br, w_mem_br, w_out, ffn_norm_g, peer_w_q, peer_subkeys, peer_u, peer_v, final_norm_g):
    depth = w_in.shape[0]
    assert depth == 1, "the final rmsnorm is fused into the last layer's PEER kernel"
    l = 0
    uv = _pack_tables(peer_u[l], peer_v[l])
    return _layer(x, mem, mix_norm_g[l], mem_norm_g[l], w_in[l], b_merge[l], cmp_pe_k[l], cmp_pe_v[l],
                  cmp_w_k[l], cmp_w_v[l], w_mem_kv[l], w_sb_br[l], w_nsa_br[l], w_mem_br[l], w_out[l],
                  ffn_norm_g[l], peer_w_q[l], peer_subkeys[l], uv, final_norm_g)
```

```python
import functools

import jax
import jax.numpy as jnp
from jax import lax
from jax.experimental import pallas as pl
from jax.experimental.pallas import tpu as pltpu
from jax.experimental.pallas import tpu_sc as plsc

F32 = jnp.float32
BF16 = jnp.bfloat16
I32 = jnp.int32

HEAD_DIM = 64
SB_HEADS = 6
NSA_HEADS = 6
NSA_KV_GROUPS = 2
NSA_GROUP = NSA_HEADS // NSA_KV_GROUPS
MEM_HEADS = 4
N_BRANCHES = 3
SB_W = SB_HEADS * HEAD_DIM
NSA_W = NSA_HEADS * HEAD_DIM
NSA_KV_W = NSA_KV_GROUPS * HEAD_DIM
MEM_W = MEM_HEADS * HEAD_DIM
CMP_LEN = 32
CMP_STRIDE = 16
SEL_BLOCK = 64
N_SELECT = 16
WINDOW = 512
FORCED_SCORE = 1e4
PEER_HEADS = 8
PEER_N_KEYS = 128
PEER_QUERY_DIM = 256
PEER_TOPK = 16
RMS_EPS = 1e-6
NEG_INF = -1e30
SCALE = HEAD_DIM ** -0.5
SB_DEAD_LOG = 104.0

LANES = 128
PEER_GATHER_BUFFERS = 8
SC_GATHER_WINDOW = 32
SC_CHUNKS = 4
SC_CHUNK_TILES = 28
VMEM_LIMIT_BYTES = 56 * 1024 * 1024

_NT = (((1,), (1,)), ((), ()))


def _params(*sem):
    return pltpu.CompilerParams(dimension_semantics=sem, vmem_limit_bytes=VMEM_LIMIT_BYTES)


def _dot(a, b):
    return jnp.dot(a, b, preferred_element_type=F32)


def _dot_nt(a, b):
    return lax.dot_general(a, b, _NT, preferred_element_type=F32)


def _sigmoid(x):
    return 1.0 / (1.0 + jnp.exp(-x))


def _rms(x, g):
    return x * lax.rsqrt(jnp.mean(x * x, axis=-1, keepdims=True) + RMS_EPS) * g


def _norm_matmul_kernel(x_ref, g_ref, w_ref, o_ref, *, tn):
    h = _rms(x_ref[...], g_ref[...]).astype(BF16)
    for c in range(0, o_ref.shape[1], tn):
        o_ref[:, c:c + tn] = _dot(h, w_ref[:, c:c + tn]).astype(o_ref.dtype)


def _norm_matmul(x2d, g, w, out_dtype, tm, tn):
    n, d = x2d.shape
    m = w.shape[1]
    return pl.pallas_call(
        functools.partial(_norm_matmul_kernel, tn=tn),
        grid=(n // tm,),
        in_specs=[
            pl.BlockSpec((tm, d), lambda i: (i, 0)),
            pl.BlockSpec((1, d), lambda i: (0, 0)),
            pl.BlockSpec((d, m), lambda i: (0, 0)),
        ],
        out_specs=pl.BlockSpec((tm, m), lambda i: (i, 0)),
        out_shape=jax.ShapeDtypeStruct((n, m), out_dtype),
        compiler_params=_params("parallel"),
        name="norm_matmul",
    )(x2d, g.reshape(1, d), w)


def _norm_matmul_heads_kernel(x_ref, g_ref, w_ref, o_ref, *, tn):
    h = _rms(x_ref[...], g_ref[...]).astype(BF16)
    per = tn // HEAD_DIM
    for c in range(0, w_ref.shape[1], tn):
        res = _dot(h, w_ref[:, c:c + tn]).astype(o_ref.dtype)
        for j in range(per):
            o_ref[c // HEAD_DIM + j] = res[:, j * HEAD_DIM:(j + 1) * HEAD_DIM]


def _norm_matmul_heads(x2d, g, w, tm, tn):
    n, d = x2d.shape
    m = w.shape[1]
    nh = m // HEAD_DIM
    return pl.pallas_call(
        functools.partial(_norm_matmul_heads_kernel, tn=tn),
        grid=(n // tm,),
        in_specs=[
            pl.BlockSpec((tm, d), lambda i: (i, 0)),
            pl.BlockSpec((1, d), lambda i: (0, 0)),
            pl.BlockSpec((d, m), lambda i: (0, 0)),
        ],
        out_specs=pl.BlockSpec((nh, tm, HEAD_DIM), lambda i: (0, i, 0)),
        out_shape=jax.ShapeDtypeStruct((nh, n, HEAD_DIM), BF16),
        compiler_params=_params("parallel"),
        name="norm_matmul_heads",
    )(x2d, g.reshape(1, d), w)


def _sb_kernel(q_ref, k_ref, v_ref, o_ref, *, tile):
    qi = pl.program_id(2)
    nh = q_ref.shape[0]
    row = lax.broadcasted_iota(I32, (tile, tile), 0)
    col = lax.broadcasted_iota(I32, (tile, tile), 1)
    lower = row > col
    later = lower.astype(BF16)
    qs = [q_ref[hh, 0] for hh in range(nh)]

    def visit(hh, ks, c, acc, diagonal):
        k = k_ref[hh, 0, pl.ds(ks, tile), :]
        v = v_ref[hh, 0, pl.ds(ks, tile), :]
        z = _dot_nt(qs[hh], k) * SCALE
        sp = jnp.maximum(z, 0.0) + jnp.log(1.0 + jnp.exp(-jnp.abs(z)))
        if diagonal:
            sp = jnp.where(lower, sp, 0.0)
        hi = sp.astype(BF16)
        lo = (sp - hi.astype(F32)).astype(BF16)
        after = _dot(hi, later) + _dot(lo, later)
        a = jnp.exp(z - sp - after - c)
        if diagonal:
            a = jnp.where(lower, a, 0.0)
        return c + jnp.sum(sp, axis=1, keepdims=True), acc + _dot(a.astype(BF16), v)

    q0 = pl.multiple_of(qi * tile, tile)
    state = []
    for hh in range(nh):
        state.extend(visit(hh, q0, jnp.zeros((tile, 1), F32), jnp.zeros((tile, HEAD_DIM), F32), True))

    def smallest_carry(state):
        c = state[0]
        for hh in range(1, nh):
            c = jnp.minimum(c, state[2 * hh])
        return jnp.min(c)

    def live(carry):
        i, c_min, _ = carry
        return (i < qi) & (c_min <= SB_DEAD_LOG)

    def body(carry):
        i, _, state = carry
        ks = pl.multiple_of((qi - 1 - i) * tile, tile)
        out = []
        for hh in range(nh):
            out.extend(visit(hh, ks, state[2 * hh], state[2 * hh + 1], False))
        return i + 1, smallest_carry(out), tuple(out)

    _, _, state = lax.while_loop(live, body, (0, smallest_carry(state), tuple(state)))
    o_ref[0] = jnp.concatenate([state[2 * hh + 1] for hh in range(nh)], axis=1).astype(o_ref.dtype)


def _sb_attention(hd, q0, k0, v0, h, tile=256):
    _, b, t, dh = hd.shape
    hp = 2
    tq = tile
    assert q0 % hp == 0 and k0 % hp == 0 and v0 % hp == 0
    return pl.pallas_call(
        functools.partial(_sb_kernel, tile=tile),
        grid=(b, h // hp, t // tq),
        in_specs=[
            pl.BlockSpec((hp, 1, tq, dh), lambda bi, hi, qi: (q0 // hp + hi, bi, qi, 0)),
            pl.BlockSpec((hp, 1, t, dh), lambda bi, hi, qi: (k0 // hp + hi, bi, 0, 0)),
            pl.BlockSpec((hp, 1, t, dh), lambda bi, hi, qi: (v0 // hp + hi, bi, 0, 0)),
        ],
        out_specs=pl.BlockSpec((1, tq, hp * dh), lambda bi, hi, qi: (bi, qi, hi)),
        out_shape=jax.ShapeDtypeStruct((b, t, h * dh), BF16),
        compiler_params=_params("parallel", "parallel", "arbitrary"),
        name="sb_attn",
    )(hd, hd, hd)


def _compress_kernel(x_ref, pe_ref, w_ref, o_ref):
    x = x_ref[0, 0]
    nc = x.shape[0]
    w_lo, w_hi = w_ref[0], w_ref[1]
    first = _dot(x, w_lo)
    second = _dot(x, w_hi)
    feat = pe_ref.shape[1]
    pe_lo = jnp.broadcast_to(pe_ref[0:1, :], (8, feat)).astype(BF16)
    pe_hi = jnp.broadcast_to(pe_ref[1:2, :], (8, feat)).astype(BF16)
    bias = _dot(pe_lo, w_lo)[0:1] + _dot(pe_hi, w_hi)[0:1]
    o_ref[0, 0] = (first + pltpu.roll(second, nc - 1, 0) + bias).astype(o_ref.dtype)


def _nsa_compress(hd, head0, pe, w):
    nh, b, t, dh = hd.shape
    g = NSA_KV_GROUPS
    nc = t // CMP_STRIDE
    feat = CMP_STRIDE * dh
    x = hd[head0:head0 + g].reshape(g, b, nc, feat)
    pe2 = pe.reshape(2, feat)
    w2 = w.reshape(2, feat, dh).astype(BF16)
    return pl.pallas_call(
        _compress_kernel,
        grid=(b, g),
        in_specs=[
            pl.BlockSpec((1, 1, nc, feat), lambda bi, gi: (gi, bi, 0, 0)),
            pl.BlockSpec((2, feat), lambda bi, gi: (0, 0)),
            pl.BlockSpec((2, feat, dh), lambda bi, gi: (0, 0, 0)),
        ],
        out_specs=pl.BlockSpec((1, 1, nc, dh), lambda bi, gi: (bi, gi, 0, 0)),
        out_shape=jax.ShapeDtypeStruct((b, g, nc, dh), BF16),
        compiler_params=_params("parallel", "parallel"),
        name="nsa_compress",
    )(x, pe2, w2)


def _nsa_kernel(slopes_ref, q_ref, kc_ref, vc_ref, ks_ref, vs_ref, kw_ref, vw_ref, gl_ref, pool_ref, ex_ref,
                o_ref, used_ref, *, tq, tk, n_sel, real_blocks):
    grp = pl.program_id(1)
    t0 = pl.program_id(2) * tq
    nc = kc_ref.shape[2]
    rr = NSA_GROUP
    t = t0 + lax.broadcasted_iota(I32, (tq, 1), 0)

    def stack(x):
        return jnp.concatenate([x] * rr, axis=0)

    q = q_ref[:, 0].reshape(rr * tq, HEAD_DIM)
    slope = jnp.concatenate([jnp.full((tq, 1), slopes_ref[grp * rr + r], F32) for r in range(rr)], axis=0)

    def heads_sum(x):
        out = x[0:tq]
        for r in range(1, rr):
            out = out + x[r * tq:(r + 1) * tq]
        return out

    def masked_softmax(s, valid):
        p = jnp.where(valid, jnp.exp(s - jnp.max(s, axis=1, keepdims=True)), 0.0)
        denom = jnp.sum(p, axis=1, keepdims=True)
        return p / jnp.where(denom > 0, denom, 1.0)

    lane = lax.broadcasted_iota(I32, (1, nc), 1)
    dist_c = (t - (lane * CMP_STRIDE + (CMP_LEN - 1))).astype(F32)
    valid_c = stack(dist_c >= 0)
    s = _dot_nt(q, kc_ref[0, 0]) * SCALE - slope * stack(dist_c)
    p = masked_softmax(jnp.where(valid_c, s, NEG_INF), valid_c)
    o_cmp = _dot(p.astype(BF16), vc_ref[0, 0])
    psum = heads_sum(p)

    n_blk = pool_ref.shape[0]
    hi = psum.astype(BF16)
    rest = psum - hi.astype(F32)
    mid = rest.astype(BF16)
    lo = (rest - mid.astype(F32)).astype(BF16)
    pool = pool_ref[...]
    imp = _dot_nt(pool, hi) + _dot_nt(pool, mid) + _dot_nt(pool, lo)
    blk = lax.broadcasted_iota(I32, (n_blk, 1), 0)
    cur = (t0 + lax.broadcasted_iota(I32, (1, tq), 1)) // SEL_BLOCK
    forced = (blk == 0) | (blk == cur) | (blk == cur - 1)
    imp = jnp.where(forced, FORCED_SCORE, jnp.where(blk <= cur, imp, -1.0))
    imp = jnp.where(blk < real_blocks, imp, -jnp.inf)
    blk_f = blk.astype(F32)

    def pick(_, carry):
        imp, sel = carry
        best = jnp.max(imp, axis=0, keepdims=True)
        first = jnp.min(jnp.where(imp == best, blk_f, float(n_blk)), axis=0, keepdims=True)
        hit = blk_f == first
        return jnp.where(hit, -jnp.inf, imp), jnp.where(hit, 1.0, sel)

    _, sel = lax.fori_loop(0, n_sel, pick, (imp, jnp.zeros((n_blk, tq), F32)))
    picked = jnp.max(sel, axis=1, keepdims=True)
    per_tile = tk // SEL_BLOCK
    n_causal = (t0 + tq + tk - 1) // tk
    n_used = jnp.int32(0)
    for j in range(used_ref.shape[0]):
        used_ref[n_used] = j
        hit = (jnp.max(picked[j * per_tile:(j + 1) * per_tile, :]) > 0) & (j < n_causal)
        n_used = n_used + hit.astype(I32)
    sel = sel.T.astype(BF16)

    span = WINDOW + tq
    w0 = pl.multiple_of(jnp.maximum(t0 - WINDOW, 0), tq)
    dist_w = t - (w0 + lax.broadcasted_iota(I32, (1, span), 1))
    valid_w = stack((dist_w >= 0) & (dist_w < WINDOW))
    s = _dot_nt(q, kw_ref[0, 0, pl.ds(w0, span), :]) * SCALE - slope * stack(dist_w.astype(F32))
    p = masked_softmax(jnp.where(valid_w, s, NEG_INF), valid_w)
    o_win = _dot(p.astype(BF16), vw_ref[0, 0, pl.ds(w0, span), :])

    kcol = lax.broadcasted_iota(I32, (1, tk), 1)

    def sel_body(i, carry):
        m, l, acc = carry
        kb = used_ref[i]
        ks0 = pl.multiple_of(kb * tk, tk)
        chosen = _dot(sel, ex_ref[kb]) > 0.5
        dist = t - (ks0 + kcol)
        valid = stack(chosen & (dist >= 0))
        s = _dot_nt(q, ks_ref[0, 0, pl.ds(ks0, tk), :]) * SCALE - slope * stack(dist.astype(F32))
        s = jnp.where(valid, s, NEG_INF)
        m_new = jnp.maximum(m, jnp.max(s, axis=1, keepdims=True))
        alpha = jnp.exp(m - m_new)
        p = jnp.where(valid, jnp.exp(s - m_new), 0.0)
        l = alpha * l + jnp.sum(p, axis=1, keepdims=True)
        acc = alpha * acc + _dot(p.astype(BF16), vs_ref[0, 0, pl.ds(ks0, tk), :])
        return m_new, l, acc

    init = (jnp.full((rr * tq, 1), NEG_INF, F32), jnp.zeros((rr * tq, 1), F32),
            jnp.zeros((rr * tq, HEAD_DIM), F32))
    _, l_sel, acc_sel = lax.fori_loop(0, n_used, sel_body, init)
    o_sel = acc_sel / l_sel

    gates = _sigmoid(gl_ref[0, 0])
    outs = []
    for r in range(rr):
        rows = slice(r * tq, (r + 1) * tq)
        outs.append(gates[:, 3 * r:3 * r + 1] * o_cmp[rows] + gates[:, 3 * r + 1:3 * r + 2] * o_sel[rows]
                    + gates[:, 3 * r + 2:3 * r + 3] * o_win[rows])
    o_ref[0, 0] = jnp.concatenate(outs, axis=1).astype(o_ref.dtype)


def _nsa_attention(hd, q0, sel0, kc, vc, gate_logits, slopes, tq=256, tk=256):
    _, b, t, dh = hd.shape
    g = NSA_KV_GROUPS
    assert q0 % NSA_GROUP == 0
    nc = kc.shape[2]
    n_sel = min(N_SELECT, t // SEL_BLOCK)
    real_blocks = t // SEL_BLOCK
    n_blk = LANES
    assert t % tk == 0 and t >= WINDOW + tq and real_blocks <= n_blk
    blk_ids = jnp.arange(n_blk, dtype=I32)
    pool = (blk_ids[:, None] == jnp.arange(nc, dtype=I32)[None, :] // (SEL_BLOCK // CMP_STRIDE)).astype(BF16)
    key_blk = (jnp.arange(t, dtype=I32) // SEL_BLOCK).reshape(t // tk, 1, tk)
    expand = (blk_ids[None, :, None] == key_blk).astype(BF16)
    def kv_spec(j):
        return pl.BlockSpec((1, 1, t, dh), lambda bi, gi, qi: (sel0 + j * g + gi, bi, 0, 0))

    c_spec = pl.BlockSpec((1, 1, nc, dh), lambda bi, gi, qi: (bi, gi, 0, 0))
    return pl.pallas_call(
        functools.partial(_nsa_kernel, tq=tq, tk=tk, n_sel=n_sel, real_blocks=real_blocks),
        grid=(b, g, t // tq),
        in_specs=[
            pl.BlockSpec(memory_space=pltpu.SMEM),
            pl.BlockSpec((NSA_GROUP, 1, tq, dh), lambda bi, gi, qi: (q0 // NSA_GROUP + gi, bi, qi, 0)),
            c_spec, c_spec, kv_spec(0), kv_spec(1), kv_spec(2), kv_spec(3),
            pl.BlockSpec((1, 1, tq, LANES), lambda bi, gi, qi: (bi, gi, qi, 0)),
            pl.BlockSpec(pool.shape, lambda bi, gi, qi: (0, 0)),
            pl.BlockSpec(expand.shape, lambda bi, gi, qi: (0, 0, 0)),
        ],
        out_specs=pl.BlockSpec((1, 1, tq, NSA_GROUP * dh), lambda bi, gi, qi: (bi, gi, qi, 0)),
        out_shape=jax.ShapeDtypeStruct((b, g, t, NSA_GROUP * dh), BF16),
        scratch_shapes=[pltpu.SMEM((t // tk,), I32)],
        compiler_params=_params("parallel", "parallel", "arbitrary"),
        name="nsa_attn",
    )(slopes, hd, kc, vc, hd, hd, hd, hd, gate_logits, pool, expand)


def _mem_kernel(q_ref, kv_ref, o_ref):
    kv = kv_ref[0]
    outs = []
    for h in range(MEM_HEADS):
        sl = slice(h * HEAD_DIM, (h + 1) * HEAD_DIM)
        s = _dot_nt(q_ref[h, 0], kv[:, sl]) * SCALE
        p = jnp.exp(s - jnp.max(s, axis=1, keepdims=True))
        p = p / jnp.sum(p, axis=1, keepdims=True)
        outs.append(_dot(p.astype(BF16), kv[:, MEM_W + h * HEAD_DIM:MEM_W + (h + 1) * HEAD_DIM]))
    o_ref[0] = jnp.concatenate(outs, axis=1).astype(o_ref.dtype)


def _mem_attention(hd, q0, mkv, tq=512):
    _, b, t, dh = hd.shape
    w = MEM_HEADS * dh
    m = mkv.shape[1]
    assert q0 % MEM_HEADS == 0
    return pl.pallas_call(
        _mem_kernel,
        grid=(b, t // tq),
        in_specs=[
            pl.BlockSpec((MEM_HEADS, 1, tq, dh), lambda bi, qi: (q0 // MEM_HEADS, bi, qi, 0)),
            pl.BlockSpec((1, m, 2 * w), lambda bi, qi: (bi, 0, 0)),
        ],
        out_specs=pl.BlockSpec((1, tq, w), lambda bi, qi: (bi, qi, 0)),
        out_shape=jax.ShapeDtypeStruct((b, t, w), BF16),
        compiler_params=_params("parallel", "parallel"),
        name="mem_attn",
    )(hd, mkv)


def _merge_kernel(sb_ref, nsa0_ref, nsa1_ref, mem_ref, mg_ref, bm_ref, x_ref, wsb_ref, wnsa_ref, wmem_ref,
                  wout_ref, fg_ref, wq_ref, x1_ref, h2_ref, q_ref):
    d = x_ref.shape[1]
    gw = nsa0_ref.shape[-1]
    nsa = _dot(nsa0_ref[0, 0], wnsa_ref[0:gw, :]) + _dot(nsa1_ref[0, 0], wnsa_ref[gw:2 * gw, :])
    branches = (_dot(sb_ref[...], wsb_ref[...]), nsa, _dot(mem_ref[...], wmem_ref[...]))
    merged = jnp.zeros_like(branches[0])
    for j in range(N_BRANCHES):
        gate = _sigmoid(mg_ref[:, j * d:(j + 1) * d] + bm_ref[:, j * d:(j + 1) * d])
        merged = merged + gate * branches[j]
    x1 = x_ref[...] + _dot(merged.astype(BF16), wout_ref[...])
    x1_ref[...] = x1
    h2 = _rms(x1, fg_ref[...])
    h2_ref[...] = h2
    q = _dot(h2.astype(BF16), wq_ref[...]).astype(q_ref.dtype)
    for c in range(q_ref.shape[0]):
        q_ref[c] = q[:, c * LANES:(c + 1) * LANES]


def _merge(sb, nsa, mem, gates_slab, b_merge, x2d, w_sb, w_nsa, w_mem, w_out, ffn_g, w_q, tm=256):
    n, d = x2d.shape
    nq = w_q.shape[1] // LANES
    _, g, t, gw = nsa.shape
    assert g == 2 and t % tm == 0
    per_row = t // tm
    row = lambda w: pl.BlockSpec((tm, w), lambda i: (i, 0))
    nsa_group = lambda gi: pl.BlockSpec((1, 1, tm, gw), lambda i: (i // per_row, gi, i % per_row, 0))
    full = lambda a: pl.BlockSpec(a.shape, lambda i: (0,) * a.ndim)
    bm = b_merge.reshape(1, -1)
    fg = ffn_g.reshape(1, d)
    return pl.pallas_call(
        _merge_kernel,
        grid=(n // tm,),
        in_specs=[row(sb.shape[1]), nsa_group(0), nsa_group(1), row(mem.shape[1]), row(N_BRANCHES * d),
                  full(bm), row(d), full(w_sb), full(w_nsa), full(w_mem), full(w_out), full(fg),
                  full(w_q)],
        out_specs=[row(d), row(d), pl.BlockSpec((nq, tm, LANES), lambda i: (0, i, 0))],
        out_shape=[jax.ShapeDtypeStruct((n, d), F32), jax.ShapeDtypeStruct((n, d), F32),
                   jax.ShapeDtypeStruct((nq, n, LANES), BF16)],
        compiler_params=_params("parallel"),
        name="merge",
    )(sb, nsa, nsa, mem, gates_slab, bm, x2d, w_sb, w_nsa, w_mem, w_out, fg, w_q)


def _peer_topk_kernel(q_ref, sk_ref, idx_ref, gw_ref, tv_ref, ti_ref, bv_ref):
    tt = q_ref.shape[1]
    kk = PEER_TOPK
    nk = PEER_N_KEYS
    rid = lax.broadcasted_iota(I32, (nk, tt), 0).astype(F32)
    for h in range(PEER_HEADS):
        scores = tuple(_dot_nt(sk_ref[2 * h + p], q_ref[2 * h + p]) for p in range(2))

        def pick(k, ss, h=h):
            out = []
            for p, s in enumerate(ss):
                best = jnp.max(s, axis=0, keepdims=True)
                first = jnp.min(jnp.where(s == best, rid, float(nk)), axis=0, keepdims=True)
                tv_ref[2 * h + p, pl.ds(k, 1), :] = best
                ti_ref[2 * h + p, pl.ds(k, 1), :] = first
                out.append(jnp.where(rid == first, -jnp.inf, s))
            return tuple(out)

        lax.fori_loop(0, kk, pick, scores)

    sub = 8
    widths = [min(kk, -(-(kk // (i + 1)) // sub) * sub) for i in range(kk // 2)]
    n_cand = sum(widths) + kk // 2
    pos = lax.broadcasted_iota(I32, (n_cand, tt), 0).astype(F32)
    for h in range(PEER_HEADS):
        s0, s1 = tv_ref[2 * h], tv_ref[2 * h + 1]
        i0, i1 = ti_ref[2 * h], ti_ref[2 * h + 1]
        cand, cidx = [], []
        for i, wd in enumerate(widths):
            keep = lax.broadcasted_iota(I32, (wd, 1), 0) < kk // (i + 1)
            cand.append(jnp.where(keep, s0[i:i + 1, :] + s1[0:wd, :], -jnp.inf))
            cidx.append(i0[i:i + 1, :] * float(nk) + i1[0:wd, :])
        cand.append(s0[kk // 2:, :] + s1[0:1, :])
        cidx.append(i0[kk // 2:, :] * float(nk) + i1[0:1, :])
        cand = jnp.concatenate(cand, axis=0)
        cidx = jnp.concatenate(cidx, axis=0)

        def pick2(k, s, h=h, cidx=cidx):
            best = jnp.max(s, axis=0, keepdims=True)
            first = jnp.min(jnp.where(s == best, pos, float(n_cand)), axis=0, keepdims=True)
            hit = pos == first
            bv_ref[pl.ds(k, 1), :] = best
            expert = jnp.max(jnp.where(hit, cidx, -1.0), axis=0, keepdims=True)
            idx_ref[pl.ds(h * kk + k, 1), :] = expert.astype(I32)
            return jnp.where(hit, -jnp.inf, s)

        lax.fori_loop(0, kk, pick2, cand)
        best = bv_ref[...]
        e = jnp.exp(best - best[0:1, :])
        gw_ref[h * kk:(h + 1) * kk, :] = e / jnp.sum(e, axis=0, keepdims=True)


def _peer_topk(q_chunks, subkeys, tt=256):
    nchunk, n, half = q_chunks.shape
    slots = PEER_HEADS * PEER_TOPK
    return pl.pallas_call(
        _peer_topk_kernel,
        grid=(n // tt,),
        in_specs=[
            pl.BlockSpec((nchunk, tt, half), lambda i: (0, i, 0)),
            pl.BlockSpec(subkeys.shape, lambda i: (0, 0, 0)),
        ],
        out_specs=[pl.BlockSpec((slots, tt), lambda i: (0, i)),
                   pl.BlockSpec((slots, tt), lambda i: (0, i))],
        out_shape=[jax.ShapeDtypeStruct((slots, n), I32), jax.ShapeDtypeStruct((slots, n), F32)],
        scratch_shapes=[pltpu.VMEM((nchunk, PEER_TOPK, tt), F32),
                        pltpu.VMEM((nchunk, PEER_TOPK, tt), F32),
                        pltpu.VMEM((PEER_TOPK, tt), F32)],
        compiler_params=_params("parallel"),
        name="peer_topk",
    )(q_chunks, subkeys)


def _peer_ffn_kernel(*refs, first_tiles, staged):
    ns = len(staged)
    idx_hbm, tables = refs[0], refs[1:1 + ns]
    io = [refs[1 + ns + 3 * s:4 + ns + 3 * s] for s in range(ns)]
    fg_ref = refs[1 + 4 * ns]
    ys = refs[2 + 4 * ns:2 + 5 * ns]
    idx_smem, bufs = refs[2 + 5 * ns], refs[3 + 5 * ns:3 + 6 * ns]
    idx_sem, sems = refs[3 + 6 * ns], refs[4 + 6 * ns:4 + 7 * ns]
    tt, d = io[0][1].shape
    slots = io[0][0].shape[0]
    rows = d // (2 * LANES)
    tile = pl.program_id(0)
    nbuf = bufs[0].shape[0]
    tok_lane = lax.broadcasted_iota(I32, (1, tt), 1)

    def halves(words):
        return tuple(pltpu.unpack_elementwise(words, index=i, packed_dtype=BF16, unpacked_dtype=F32)
                     for i in range(2))

    def make_stream(s):
        table, buf, row_sem, y_ref = tables[s], bufs[s], sems[s], ys[s]
        gw_ref, h_ref, _ = io[s]

        if staged[s]:
            def start(tok, slot):
                for r in range(buf.shape[1]):
                    pltpu.make_async_copy(table.at[tile * tt + tok, :, r, :], buf.at[slot, r],
                                          row_sem.at[slot]).start(priority=r % 2)
        else:
            idx_copy = pltpu.make_async_copy(idx_hbm.at[tile + first_tiles[s]], idx_smem, idx_sem)
            idx_copy.start()
            idx_copy.wait()

            def start(tok, slot):
                for e in range(slots):
                    pltpu.make_async_copy(table.at[idx_smem[tok * slots + e]], buf.at[slot, :, e, :],
                                          row_sem.at[slot]).start(priority=e % 2)

        def wait(slot):
            pltpu.make_async_copy(buf.at[slot], buf.at[slot], row_sem.at[slot]).wait()

        def expert_weights(tok, slot):
            h = h_ref[pl.ds(tok, 1), :]
            prod = jnp.zeros((slots, LANES), F32)
            for r in range(rows):
                lo, hi = halves(buf[slot, r])
                prod = prod + lo * h[:, r * LANES:(r + 1) * LANES]
                prod = prod + hi * h[:, d // 2 + r * LANES:d // 2 + (r + 1) * LANES]
            a = jnp.sum(prod, axis=1, keepdims=True)
            act = 0.5 * a * (1.0 + lax.erf(a * (2.0 ** -0.5)))
            gate = jnp.sum(jnp.where(tok_lane == tok, gw_ref[...], 0.0), axis=1, keepdims=True)
            return gate * act

        def weighted_values(tok, slot, w):
            parts = [halves(buf[slot, rows + r]) for r in range(rows)]
            y_ref[pl.ds(tok, 1), :] = jnp.concatenate(
                [jnp.sum(parts[r][i] * w, axis=0, keepdims=True) for i in range(2) for r in range(rows)],
                axis=1)

        def step(tok, slot, w, refill):
            nxt = (slot + 1) % nbuf
            wait(nxt)
            w_next = expert_weights(tok + 1, nxt)
            weighted_values(tok, slot, w)
            if refill:
                start(tok + nbuf, slot)
            return w_next

        return start, wait, expert_weights, weighted_values, step

    streams = [make_stream(s) for s in range(ns)]

    ws = []
    for start, wait, expert_weights, _, _ in streams:
        for tok in range(nbuf):
            start(tok, tok)
    for start, wait, expert_weights, _, _ in streams:
        wait(0)
        ws.append(expert_weights(0, 0))

    def round_of_steps(i, ws):
        ws = list(ws)
        for slot in range(nbuf):
            for s in range(ns):
                ws[s] = streams[s][4](i * nbuf + slot, slot, ws[s], True)
        return tuple(ws)

    n_rounds = tt // nbuf - 1
    ws = list(lax.fori_loop(0, n_rounds, round_of_steps, tuple(ws)))
    for slot in range(nbuf - 1):
        for s in range(ns):
            ws[s] = streams[s][4](n_rounds * nbuf + slot, slot, ws[s], False)
    for s in range(ns):
        streams[s][3](tt - 1, nbuf - 1, ws[s])
        ys[s][...] = _rms(io[s][2][...] + ys[s][...], fg_ref[...])


def _pack_tables_kernel(u_ref, v_ref, o_ref):
    d = u_ref.shape[1]
    rows = d // (2 * LANES)
    for base, src in ((0, u_ref), (rows, v_ref)):
        for r in range(rows):
            lo = src[:, r * LANES:(r + 1) * LANES]
            hi = src[:, d // 2 + r * LANES:d // 2 + (r + 1) * LANES]
            o_ref[:, base + r, :] = pltpu.pack_elementwise([lo, hi], packed_dtype=BF16)


def _pack_tables(u, v, te=512):
    n_exp, d = u.shape
    rows = d // LANES
    return pl.pallas_call(
        _pack_tables_kernel,
        grid=(n_exp // te,),
        in_specs=[pl.BlockSpec((te, d), lambda i: (i, 0)), pl.BlockSpec((te, d), lambda i: (i, 0))],
        out_specs=pl.BlockSpec((te, rows, LANES), lambda i: (i, 0, 0)),
        out_shape=jax.ShapeDtypeStruct((n_exp, rows, LANES), jnp.uint32),
        compiler_params=_params("parallel"),
        name="pack_tables",
    )(u, v)


def _sc_gather_slabs(slabs, slab_idx):
    m = slab_idx.shape[0]
    mesh = plsc.VectorSubcoreMesh(core_axis_name="core", subcore_axis_name="subcore")
    idx_rows = jnp.pad(slab_idx.reshape(m // SC_GATHER_WINDOW, SC_GATHER_WINDOW),
                       ((0, 0), (0, LANES - SC_GATHER_WINDOW)))

    @pl.kernel(out_type=jax.ShapeDtypeStruct((m,) + slabs.shape[1:], slabs.dtype), mesh=mesh)
    def gather(slabs_hbm, idx_hbm, out_hbm):
        def window(idx_vmem, out_vmem):
            pltpu.sync_copy(slabs_hbm.at[idx_vmem.at[0, pl.ds(0, SC_GATHER_WINDOW)]], out_vmem)

        pltpu.emit_pipeline(
            window,
            grid=(m // SC_GATHER_WINDOW,),
            in_specs=[pl.BlockSpec((1, LANES), index_map=lambda i: (i, 0))],
            out_specs=[pl.BlockSpec((SC_GATHER_WINDOW,) + slabs.shape[1:], index_map=lambda i: (i, 0, 0))],
            core_axis_name=("core", "subcore"),
            dimension_semantics=(pltpu.PARALLEL,),
            trace_scopes=False,
        )(idx_hbm, out_hbm)

    return gather(slabs, idx_rows)


def _peer_ffn(idx_tiles, tables, gw, h2, x1, final_g, tt, first_tiles, n_tiles, staged):
    _, d = h2.shape
    slots = gw.shape[0]
    rows = d // LANES
    ns = len(staged)
    assert sum(not st for st in staged) <= 1, "one index buffer: at most one gathering stream"
    hbm = pl.BlockSpec(memory_space=pl.ANY)
    io_specs, io_args = [], []
    for s in range(ns):
        io_specs += [pl.BlockSpec((slots, tt), lambda i, f=first_tiles[s]: (0, i + f)),
                     pl.BlockSpec((tt, d), lambda i, f=first_tiles[s]: (i + f, 0)),
                     pl.BlockSpec((tt, d), lambda i, f=first_tiles[s]: (i + f, 0))]
        io_args += [gw, h2, x1]
    buf = pltpu.VMEM((PEER_GATHER_BUFFERS, rows, slots, LANES), tables[0].dtype)
    sem = pltpu.SemaphoreType.DMA((PEER_GATHER_BUFFERS,))
    outs = pl.pallas_call(
        functools.partial(_peer_ffn_kernel, first_tiles=tuple(first_tiles), staged=tuple(staged)),
        grid=(n_tiles,),
        in_specs=[hbm] + [hbm] * ns + io_specs + [pl.BlockSpec((1, d), lambda i: (0, 0))],
        out_specs=[pl.BlockSpec((tt, d), lambda i: (i, 0))] * ns,
        out_shape=[jax.ShapeDtypeStruct((n_tiles * tt, d), F32)] * ns,
        scratch_shapes=[pltpu.SMEM((tt * slots,), I32)] + [buf] * ns + [pltpu.SemaphoreType.DMA] + [sem] * ns,
        compiler_params=_params("arbitrary"),
        name="peer_ffn_" + "_".join("staged" if st else "gather" for st in staged),
    )(idx_tiles, *tables, *io_args, final_g.reshape(1, d))
    return list(outs)


def _mixers(x, mem, mix_g, mem_g, w_in, b_merge, pe_k, pe_v, cw_k, cw_v, w_mem_kv, w_sb_br, w_nsa_br,
            w_mem_br, w_out, ffn_g, peer_w_q, subkeys):
    b, t, d = x.shape
    m = mem.shape[1]
    g = NSA_KV_GROUPS
    n = b * t
    x2d = x.reshape(n, d)

    o_sbq, o_sbk, o_sbv = 0, SB_W, 2 * SB_W
    o_nq = 3 * SB_W
    o_nkv = o_nq + NSA_W
    o_ng = o_nkv + 6 * NSA_KV_W
    o_mq = o_ng + NSA_HEADS * 3
    o_mg = o_mq + MEM_W
    w_act = jnp.concatenate([w_in[:, :o_ng], w_in[:, o_mq:o_mg]], axis=1).astype(BF16)
    gate_pad = LANES - NSA_HEADS * 3
    w_gate = jnp.concatenate([w_in[:, o_mg:], w_in[:, o_ng:o_mq], jnp.zeros((d, gate_pad), w_in.dtype)],
                             axis=1).astype(BF16)
    hd = _norm_matmul_heads(x2d, mix_g, w_act, tm=256, tn=512).reshape(-1, b, t, HEAD_DIM)
    gates_slab = _norm_matmul(x2d, mix_g, w_gate, F32, tm=256, tn=640)
    head_of = lambda col: col // HEAD_DIM

    sb_out = _sb_attention(hd, head_of(o_sbq), head_of(o_sbk), head_of(o_sbv), SB_HEADS)

    kc = _nsa_compress(hd, head_of(o_nkv), pe_k, cw_k)
    vc = _nsa_compress(hd, head_of(o_nkv) + g, pe_v, cw_v)
    gl = gates_slab[:, N_BRANCHES * d:N_BRANCHES * d + NSA_HEADS * 3]
    gl = gl.reshape(b, t, g, NSA_GROUP * 3).transpose(0, 2, 1, 3)
    gl = jnp.pad(gl, ((0, 0), (0, 0), (0, 0), (0, LANES - NSA_GROUP * 3)))
    slopes = jnp.asarray([2.0 ** (-8.0 * (h + 1) / NSA_HEADS) for h in range(NSA_HEADS)], F32)
    nsa_out = _nsa_attention(hd, head_of(o_nq), head_of(o_nkv) + 2 * g, kc, vc, gl, slopes)

    mkv = _norm_matmul(mem.reshape(b * m, d), mem_g, w_mem_kv.astype(BF16), BF16, tm=256, tn=512)
    mem_out = _mem_attention(hd, head_of(o_nkv + 6 * NSA_KV_W), mkv.reshape(b, m, 2 * MEM_W))

    x1, h2, q_chunks = _merge(
        sb_out.reshape(n, SB_W), nsa_out, mem_out.reshape(n, MEM_W), gates_slab, b_merge, x2d,
        w_sb_br.astype(BF16), w_nsa_br.astype(BF16), w_mem_br.astype(BF16), w_out.astype(BF16), ffn_g,
        peer_w_q.astype(BF16))

    half = PEER_QUERY_DIM // 2
    idx, gw = _peer_topk(q_chunks, subkeys.reshape(2 * PEER_HEADS, PEER_N_KEYS, half).astype(BF16))
    return x1, h2, idx.T, gw


def _layer(x, mem, *params_and_tables):
    uv, out_g = params_and_tables[-2:]
    b, t, d = x.shape
    tt = 128
    x1, h2, idx_tok, gw = _mixers(x, mem, *params_and_tables[:-2])
    n, slots = idx_tok.shape
    n_tiles = n // tt
    idx_tiles = idx_tok.reshape(n_tiles, tt * slots)
    chunk = SC_CHUNK_TILES
    sc_first = n_tiles - SC_CHUNKS * chunk
    lead = sc_first - SC_CHUNKS * chunk
    assert lead > 0

    def stage(c):
        lo = (sc_first + c * chunk) * tt
        slabs = _sc_gather_slabs(uv, idx_tok[lo:lo + chunk * tt].reshape(-1))
        return slabs.reshape((chunk * tt, slots) + uv.shape[1:])

    staged = [stage(c) for c in range(SC_CHUNKS)]
    ffn = functools.partial(_peer_ffn, idx_tiles, gw=gw, h2=h2, x1=x1, final_g=out_g, tt=tt)
    own = ffn(tables=[uv], first_tiles=[0], n_tiles=lead, staged=[False])
    theirs = []
    for c in range(SC_CHUNKS):
        mine, other = ffn(tables=[uv, staged[c]], first_tiles=[lead + c * chunk, sc_first + c * chunk],
                          n_tiles=chunk, staged=[False, True])
        own.append(mine)
        theirs.append(other)
    return jnp.concatenate(own + theirs, axis=0).reshape(b, t, d)


def kernel(x, mem, mix_norm_g, mem_norm_g, w_in, b_merge, cmp_pe_k, cmp_pe_v, cmp_w_k, cmp_w_v, w_mem_kv, w_sb_br, w_nsa_br, w_mem_br, w_out, ffn_norm_g, peer_w_q, peer_subkeys, peer_u, peer_v, final_norm_g):
    depth = w_in.shape[0]
    assert depth == 1, "the final rmsnorm is fused into the last layer's PEER kernel"
    l = 0
    uv = _pack_tables(peer_u[l], peer_v[l])
    return _layer(x, mem, mix_norm_g[l], mem_norm_g[l], w_in[l], b_merge[l], cmp_pe_k[l], cmp_pe_v[l],
                  cmp_w_k[l], cmp_w_v[l], w_mem_kv[l], w_sb_br[l], w_nsa_br[l], w_mem_br[l], w_out[l],
                  ffn_norm_g[l], peer_w_q[l], peer_subkeys[l], uv, final_norm_g)
```

```python
import functools

import jax
import jax.numpy as jnp
from jax import lax
from jax.experimental import pallas as pl
from jax.experimental.pallas import tpu as pltpu
from jax.experimental.pallas import tpu_sc as plsc

F32 = jnp.float32
BF16 = jnp.bfloat16
I32 = jnp.int32

HEAD_DIM = 64
SB_HEADS = 6
NSA_HEADS = 6
NSA_KV_GROUPS = 2
NSA_GROUP = NSA_HEADS // NSA_KV_GROUPS
MEM_HEADS = 4
N_BRANCHES = 3
SB_W = SB_HEADS * HEAD_DIM
NSA_W = NSA_HEADS * HEAD_DIM
NSA_KV_W = NSA_KV_GROUPS * HEAD_DIM
MEM_W = MEM_HEADS * HEAD_DIM
CMP_LEN = 32
CMP_STRIDE = 16
SEL_BLOCK = 64
N_SELECT = 16
WINDOW = 512
FORCED_SCORE = 1e4
PEER_HEADS = 8
PEER_N_KEYS = 128
PEER_QUERY_DIM = 256
PEER_TOPK = 16
RMS_EPS = 1e-6
NEG_INF = -1e30
SCALE = HEAD_DIM ** -0.5
SB_DEAD_LOG = 104.0

LANES = 128
PEER_GATHER_BUFFERS = 8
SC_GATHER_WINDOW = 56
SC_CHUNKS = 4
SC_CHUNK_TILES = 28
VMEM_LIMIT_BYTES = 56 * 1024 * 1024

_NT = (((1,), (1,)), ((), ()))


def _params(*sem):
    return pltpu.CompilerParams(dimension_semantics=sem, vmem_limit_bytes=VMEM_LIMIT_BYTES)


def _dot(a, b):
    return jnp.dot(a, b, preferred_element_type=F32)


def _dot_nt(a, b):
    return lax.dot_general(a, b, _NT, preferred_element_type=F32)


def _sigmoid(x):
    return 1.0 / (1.0 + jnp.exp(-x))


def _rms(x, g):
    return x * lax.rsqrt(jnp.mean(x * x, axis=-1, keepdims=True) + RMS_EPS) * g


def _norm_matmul_kernel(x_ref, g_ref, w_ref, o_ref, *, tn):
    h = _rms(x_ref[...], g_ref[...]).astype(BF16)
    for c in range(0, o_ref.shape[1], tn):
        o_ref[:, c:c + tn] = _dot(h, w_ref[:, c:c + tn]).astype(o_ref.dtype)


def _norm_matmul(x2d, g, w, out_dtype, tm, tn):
    n, d = x2d.shape
    m = w.shape[1]
    return pl.pallas_call(
        functools.partial(_norm_matmul_kernel, tn=tn),
        grid=(n // tm,),
        in_specs=[
            pl.BlockSpec((tm, d), lambda i: (i, 0)),
            pl.BlockSpec((1, d), lambda i: (0, 0)),
            pl.BlockSpec((d, m), lambda i: (0, 0)),
        ],
        out_specs=pl.BlockSpec((tm, m), lambda i: (i, 0)),
        out_shape=jax.ShapeDtypeStruct((n, m), out_dtype),
        compiler_params=_params("parallel"),
        name="norm_matmul",
    )(x2d, g.reshape(1, d), w)


def _norm_matmul_heads_kernel(x_ref, g_ref, w_ref, o_ref, *, tn):
    h = _rms(x_ref[...], g_ref[...]).astype(BF16)
    per = tn // HEAD_DIM
    for c in range(0, w_ref.shape[1], tn):
        res = _dot(h, w_ref[:, c:c + tn]).astype(o_ref.dtype)
        for j in range(per):
            o_ref[c // HEAD_DIM + j] = res[:, j * HEAD_DIM:(j + 1) * HEAD_DIM]


def _norm_matmul_heads(x2d, g, w, tm, tn):
    n, d = x2d.shape
    m = w.shape[1]
    nh = m // HEAD_DIM
    return pl.pallas_call(
        functools.partial(_norm_matmul_heads_kernel, tn=tn),
        grid=(n // tm,),
        in_specs=[
            pl.BlockSpec((tm, d), lambda i: (i, 0)),
            pl.BlockSpec((1, d), lambda i: (0, 0)),
            pl.BlockSpec((d, m), lambda i: (0, 0)),
        ],
        out_specs=pl.BlockSpec((nh, tm, HEAD_DIM), lambda i: (0, i, 0)),
        out_shape=jax.ShapeDtypeStruct((nh, n, HEAD_DIM), BF16),
        compiler_params=_params("parallel"),
        name="norm_matmul_heads",
    )(x2d, g.reshape(1, d), w)


def _sb_kernel(q_ref, k_ref, v_ref, o_ref, *, tile):
    qi = pl.program_id(2)
    nh = q_ref.shape[0]
    row = lax.broadcasted_iota(I32, (tile, tile), 0)
    col = lax.broadcasted_iota(I32, (tile, tile), 1)
    lower = row > col
    later = lower.astype(BF16)
    qs = [q_ref[hh, 0] for hh in range(nh)]

    def visit(hh, ks, c, acc, diagonal):
        k = k_ref[hh, 0, pl.ds(ks, tile), :]
        v = v_ref[hh, 0, pl.ds(ks, tile), :]
        z = _dot_nt(qs[hh], k) * SCALE
        sp = jnp.maximum(z, 0.0) + jnp.log(1.0 + jnp.exp(-jnp.abs(z)))
        if diagonal:
            sp = jnp.where(lower, sp, 0.0)
        hi = sp.astype(BF16)
        lo = (sp - hi.astype(F32)).astype(BF16)
        after = _dot(hi, later) + _dot(lo, later)
        a = jnp.exp(z - sp - after - c)
        if diagonal:
            a = jnp.where(lower, a, 0.0)
        return c + jnp.sum(sp, axis=1, keepdims=True), acc + _dot(a.astype(BF16), v)

    q0 = pl.multiple_of(qi * tile, tile)
    state = []
    for hh in range(nh):
        state.extend(visit(hh, q0, jnp.zeros((tile, 1), F32), jnp.zeros((tile, HEAD_DIM), F32), True))

    def smallest_carry(state):
        c = state[0]
        for hh in range(1, nh):
            c = jnp.minimum(c, state[2 * hh])
        return jnp.min(c)

    def live(carry):
        i, c_min, _ = carry
        return (i < qi) & (c_min <= SB_DEAD_LOG)

    def body(carry):
        i, _, state = carry
        ks = pl.multiple_of((qi - 1 - i) * tile, tile)
        out = []
        for hh in range(nh):
            out.extend(visit(hh, ks, state[2 * hh], state[2 * hh + 1], False))
        return i + 1, smallest_carry(out), tuple(out)

    _, _, state = lax.while_loop(live, body, (0, smallest_carry(state), tuple(state)))
    o_ref[0] = jnp.concatenate([state[2 * hh + 1] for hh in range(nh)], axis=1).astype(o_ref.dtype)


def _sb_attention(hd, q0, k0, v0, h, tile=256):
    _, b, t, dh = hd.shape
    hp = 2
    tq = tile
    assert q0 % hp == 0 and k0 % hp == 0 and v0 % hp == 0
    return pl.pallas_call(
        functools.partial(_sb_kernel, tile=tile),
        grid=(b, h // hp, t // tq),
        in_specs=[
            pl.BlockSpec((hp, 1, tq, dh), lambda bi, hi, qi: (q0 // hp + hi, bi, qi, 0)),
            pl.BlockSpec((hp, 1, t, dh), lambda bi, hi, qi: (k0 // hp + hi, bi, 0, 0)),
            pl.BlockSpec((hp, 1, t, dh), lambda bi, hi, qi: (v0 // hp + hi, bi, 0, 0)),
        ],
        out_specs=pl.BlockSpec((1, tq, hp * dh), lambda bi, hi, qi: (bi, qi, hi)),
        out_shape=jax.ShapeDtypeStruct((b, t, h * dh), BF16),
        compiler_params=_params("parallel", "parallel", "arbitrary"),
        name="sb_attn",
    )(hd, hd, hd)


def _compress_kernel(x_ref, pe_ref, w_ref, o_ref):
    x = x_ref[0, 0]
    nc = x.shape[0]
    w_lo, w_hi = w_ref[0], w_ref[1]
    first = _dot(x, w_lo)
    second = _dot(x, w_hi)
    feat = pe_ref.shape[1]
    pe_lo = jnp.broadcast_to(pe_ref[0:1, :], (8, feat)).astype(BF16)
    pe_hi = jnp.broadcast_to(pe_ref[1:2, :], (8, feat)).astype(BF16)
    bias = _dot(pe_lo, w_lo)[0:1] + _dot(pe_hi, w_hi)[0:1]
    o_ref[0, 0] = (first + pltpu.roll(second, nc - 1, 0) + bias).astype(o_ref.dtype)


def _nsa_compress(hd, head0, pe, w):
    nh, b, t, dh = hd.shape
    g = NSA_KV_GROUPS
    nc = t // CMP_STRIDE
    feat = CMP_STRIDE * dh
    x = hd[head0:head0 + g].reshape(g, b, nc, feat)
    pe2 = pe.reshape(2, feat)
    w2 = w.reshape(2, feat, dh).astype(BF16)
    return pl.pallas_call(
        _compress_kernel,
        grid=(b, g),
        in_specs=[
            pl.BlockSpec((1, 1, nc, feat), lambda bi, gi: (gi, bi, 0, 0)),
            pl.BlockSpec((2, feat), lambda bi, gi: (0, 0)),
            pl.BlockSpec((2, feat, dh), lambda bi, gi: (0, 0, 0)),
        ],
        out_specs=pl.BlockSpec((1, 1, nc, dh), lambda bi, gi: (bi, gi, 0, 0)),
        out_shape=jax.ShapeDtypeStruct((b, g, nc, dh), BF16),
        compiler_params=_params("parallel", "parallel"),
        name="nsa_compress",
    )(x, pe2, w2)


def _nsa_kernel(slopes_ref, q_ref, kc_ref, vc_ref, ks_ref, vs_ref, kw_ref, vw_ref, gl_ref, pool_ref, ex_ref,
                o_ref, used_ref, *, tq, tk, n_sel, real_blocks):
    grp = pl.program_id(1)
    t0 = pl.program_id(2) * tq
    nc = kc_ref.shape[2]
    rr = NSA_GROUP
    t = t0 + lax.broadcasted_iota(I32, (tq, 1), 0)

    def stack(x):
        return jnp.concatenate([x] * rr, axis=0)

    q = q_ref[:, 0].reshape(rr * tq, HEAD_DIM)
    slope = jnp.concatenate([jnp.full((tq, 1), slopes_ref[grp * rr + r], F32) for r in range(rr)], axis=0)

    def heads_sum(x):
        out = x[0:tq]
        for r in range(1, rr):
            out = out + x[r * tq:(r + 1) * tq]
        return out

    def masked_softmax(s, valid):
        p = jnp.where(valid, jnp.exp(s - jnp.max(s, axis=1, keepdims=True)), 0.0)
        denom = jnp.sum(p, axis=1, keepdims=True)
        return p / jnp.where(denom > 0, denom, 1.0)

    lane = lax.broadcasted_iota(I32, (1, nc), 1)
    dist_c = (t - (lane * CMP_STRIDE + (CMP_LEN - 1))).astype(F32)
    valid_c = stack(dist_c >= 0)
    s = _dot_nt(q, kc_ref[0, 0]) * SCALE - slope * stack(dist_c)
    p = masked_softmax(jnp.where(valid_c, s, NEG_INF), valid_c)
    o_cmp = _dot(p.astype(BF16), vc_ref[0, 0])
    psum = heads_sum(p)

    n_blk = pool_ref.shape[0]
    hi = psum.astype(BF16)
    rest = psum - hi.astype(F32)
    mid = rest.astype(BF16)
    lo = (rest - mid.astype(F32)).astype(BF16)
    pool = pool_ref[...]
    imp = _dot_nt(pool, hi) + _dot_nt(pool, mid) + _dot_nt(pool, lo)
    blk = lax.broadcasted_iota(I32, (n_blk, 1), 0)
    cur = (t0 + lax.broadcasted_iota(I32, (1, tq), 1)) // SEL_BLOCK
    forced = (blk == 0) | (blk == cur) | (blk == cur - 1)
    imp = jnp.where(forced, FORCED_SCORE, jnp.where(blk <= cur, imp, -1.0))
    imp = jnp.where(blk < real_blocks, imp, -jnp.inf)
    blk_f = blk.astype(F32)

    def pick(_, carry):
        imp, sel = carry
        best = jnp.max(imp, axis=0, keepdims=True)
        first = jnp.min(jnp.where(imp == best, blk_f, float(n_blk)), axis=0, keepdims=True)
        hit = blk_f == first
        return jnp.where(hit, -jnp.inf, imp), jnp.where(hit, 1.0, sel)

    _, sel = lax.fori_loop(0, n_sel, pick, (imp, jnp.zeros((n_blk, tq), F32)))
    picked = jnp.max(sel, axis=1, keepdims=True)
    per_tile = tk // SEL_BLOCK
    n_causal = (t0 + tq + tk - 1) // tk
    n_used = jnp.int32(0)
    for j in range(used_ref.shape[0]):
        used_ref[n_used] = j
        hit = (jnp.max(picked[j * per_tile:(j + 1) * per_tile, :]) > 0) & (j < n_causal)
        n_used = n_used + hit.astype(I32)
    sel = sel.T.astype(BF16)

    span = WINDOW + tq
    w0 = pl.multiple_of(jnp.maximum(t0 - WINDOW, 0), tq)
    dist_w = t - (w0 + lax.broadcasted_iota(I32, (1, span), 1))
    valid_w = stack((dist_w >= 0) & (dist_w < WINDOW))
    s = _dot_nt(q, kw_ref[0, 0, pl.ds(w0, span), :]) * SCALE - slope * stack(dist_w.astype(F32))
    p = masked_softmax(jnp.where(valid_w, s, NEG_INF), valid_w)
    o_win = _dot(p.astype(BF16), vw_ref[0, 0, pl.ds(w0, span), :])

    kcol = lax.broadcasted_iota(I32, (1, tk), 1)

    def sel_body(i, carry):
        m, l, acc = carry
        kb = used_ref[i]
        ks0 = pl.multiple_of(kb * tk, tk)
        chosen = _dot(sel, ex_ref[kb]) > 0.5
        dist = t - (ks0 + kcol)
        valid = stack(chosen & (dist >= 0))
        s = _dot_nt(q, ks_ref[0, 0, pl.ds(ks0, tk), :]) * SCALE - slope * stack(dist.astype(F32))
        s = jnp.where(valid, s, NEG_INF)
        m_new = jnp.maximum(m, jnp.max(s, axis=1, keepdims=True))
        alpha = jnp.exp(m - m_new)
        p = jnp.where(valid, jnp.exp(s - m_new), 0.0)
        l = alpha * l + jnp.sum(p, axis=1, keepdims=True)
        acc = alpha * acc + _dot(p.astype(BF16), vs_ref[0, 0, pl.ds(ks0, tk), :])
        return m_new, l, acc

    init = (jnp.full((rr * tq, 1), NEG_INF, F32), jnp.zeros((rr * tq, 1), F32),
            jnp.zeros((rr * tq, HEAD_DIM), F32))
    _, l_sel, acc_sel = lax.fori_loop(0, n_used, sel_body, init)
    o_sel = acc_sel / l_sel

    gates = _sigmoid(gl_ref[0, 0])
    outs = []
    for r in range(rr):
        rows = slice(r * tq, (r + 1) * tq)
        outs.append(gates[:, 3 * r:3 * r + 1] * o_cmp[rows] + gates[:, 3 * r + 1:3 * r + 2] * o_sel[rows]
                    + gates[:, 3 * r + 2:3 * r + 3] * o_win[rows])
    o_ref[0, 0] = jnp.concatenate(outs, axis=1).astype(o_ref.dtype)


def _nsa_attention(hd, q0, sel0, kc, vc, gate_logits, slopes, tq=256, tk=256):
    _, b, t, dh = hd.shape
    g = NSA_KV_GROUPS
    assert q0 % NSA_GROUP == 0
    nc = kc.shape[2]
    n_sel = min(N_SELECT, t // SEL_BLOCK)
    real_blocks = t // SEL_BLOCK
    n_blk = LANES
    assert t % tk == 0 and t >= WINDOW + tq and real_blocks <= n_blk
    blk_ids = jnp.arange(n_blk, dtype=I32)
    pool = (blk_ids[:, None] == jnp.arange(nc, dtype=I32)[None, :] // (SEL_BLOCK // CMP_STRIDE)).astype(BF16)
    key_blk = (jnp.arange(t, dtype=I32) // SEL_BLOCK).reshape(t // tk, 1, tk)
    expand = (blk_ids[None, :, None] == key_blk).astype(BF16)
    def kv_spec(j):
        return pl.BlockSpec((1, 1, t, dh), lambda bi, gi, qi: (sel0 + j * g + gi, bi, 0, 0))

    c_spec = pl.BlockSpec((1, 1, nc, dh), lambda bi, gi, qi: (bi, gi, 0, 0))
    return pl.pallas_call(
        functools.partial(_nsa_kernel, tq=tq, tk=tk, n_sel=n_sel, real_blocks=real_blocks),
        grid=(b, g, t // tq),
        in_specs=[
            pl.BlockSpec(memory_space=pltpu.SMEM),
            pl.BlockSpec((NSA_GROUP, 1, tq, dh), lambda bi, gi, qi: (q0 // NSA_GROUP + gi, bi, qi, 0)),
            c_spec, c_spec, kv_spec(0), kv_spec(1), kv_spec(2), kv_spec(3),
            pl.BlockSpec((1, 1, tq, LANES), lambda bi, gi, qi: (bi, gi, qi, 0)),
            pl.BlockSpec(pool.shape, lambda bi, gi, qi: (0, 0)),
            pl.BlockSpec(expand.shape, lambda bi, gi, qi: (0, 0, 0)),
        ],
        out_specs=pl.BlockSpec((1, 1, tq, NSA_GROUP * dh), lambda bi, gi, qi: (bi, gi, qi, 0)),
        out_shape=jax.ShapeDtypeStruct((b, g, t, NSA_GROUP * dh), BF16),
        scratch_shapes=[pltpu.SMEM((t // tk,), I32)],
        compiler_params=_params("parallel", "parallel", "arbitrary"),
        name="nsa_attn",
    )(slopes, hd, kc, vc, hd, hd, hd, hd, gate_logits, pool, expand)


def _mem_kernel(q_ref, kv_ref, o_ref):
    kv = kv_ref[0]
    outs = []
    for h in range(MEM_HEADS):
        sl = slice(h * HEAD_DIM, (h + 1) * HEAD_DIM)
        s = _dot_nt(q_ref[h, 0], kv[:, sl]) * SCALE
        p = jnp.exp(s - jnp.max(s, axis=1, keepdims=True))
        p = p / jnp.sum(p, axis=1, keepdims=True)
        outs.append(_dot(p.astype(BF16), kv[:, MEM_W + h * HEAD_DIM:MEM_W + (h + 1) * HEAD_DIM]))
    o_ref[0] = jnp.concatenate(outs, axis=1).astype(o_ref.dtype)


def _mem_attention(hd, q0, mkv, tq=512):
    _, b, t, dh = hd.shape
    w = MEM_HEADS * dh
    m = mkv.shape[1]
    assert q0 % MEM_HEADS == 0
    return pl.pallas_call(
        _mem_kernel,
        grid=(b, t // tq),
        in_specs=[
            pl.BlockSpec((MEM_HEADS, 1, tq, dh), lambda bi, qi: (q0 // MEM_HEADS, bi, qi, 0)),
            pl.BlockSpec((1, m, 2 * w), lambda bi, qi: (bi, 0, 0)),
        ],
        out_specs=pl.BlockSpec((1, tq, w), lambda bi, qi: (bi, qi, 0)),
        out_shape=jax.ShapeDtypeStruct((b, t, w), BF16),
        compiler_params=_params("parallel", "parallel"),
        name="mem_attn",
    )(hd, mkv)


def _merge_kernel(sb_ref, nsa0_ref, nsa1_ref, mem_ref, mg_ref, bm_ref, x_ref, wsb_ref, wnsa_ref, wmem_ref,
                  wout_ref, fg_ref, wq_ref, x1_ref, h2_ref, q_ref):
    d = x_ref.shape[1]
    gw = nsa0_ref.shape[-1]
    nsa = _dot(nsa0_ref[0, 0], wnsa_ref[0:gw, :]) + _dot(nsa1_ref[0, 0], wnsa_ref[gw:2 * gw, :])
    branches = (_dot(sb_ref[...], wsb_ref[...]), nsa, _dot(mem_ref[...], wmem_ref[...]))
    merged = jnp.zeros_like(branches[0])
    for j in range(N_BRANCHES):
        gate = _sigmoid(mg_ref[:, j * d:(j + 1) * d] + bm_ref[:, j * d:(j + 1) * d])
        merged = merged + gate * branches[j]
    x1 = x_ref[...] + _dot(merged.astype(BF16), wout_ref[...])
    x1_ref[...] = x1
    h2 = _rms(x1, fg_ref[...])
    h2_ref[...] = h2
    q = _dot(h2.astype(BF16), wq_ref[...]).astype(q_ref.dtype)
    for c in range(q_ref.shape[0]):
        q_ref[c] = q[:, c * LANES:(c + 1) * LANES]


def _merge(sb, nsa, mem, gates_slab, b_merge, x2d, w_sb, w_nsa, w_mem, w_out, ffn_g, w_q, tm=256):
    n, d = x2d.shape
    nq = w_q.shape[1] // LANES
    _, g, t, gw = nsa.shape
    assert g == 2 and t % tm == 0
    per_row = t // tm
    row = lambda w: pl.BlockSpec((tm, w), lambda i: (i, 0))
    nsa_group = lambda gi: pl.BlockSpec((1, 1, tm, gw), lambda i: (i // per_row, gi, i % per_row, 0))
    full = lambda a: pl.BlockSpec(a.shape, lambda i: (0,) * a.ndim)
    bm = b_merge.reshape(1, -1)
    fg = ffn_g.reshape(1, d)
    return pl.pallas_call(
        _merge_kernel,
        grid=(n // tm,),
        in_specs=[row(sb.shape[1]), nsa_group(0), nsa_group(1), row(mem.shape[1]), row(N_BRANCHES * d),
                  full(bm), row(d), full(w_sb), full(w_nsa), full(w_mem), full(w_out), full(fg),
                  full(w_q)],
        out_specs=[row(d), row(d), pl.BlockSpec((nq, tm, LANES), lambda i: (0, i, 0))],
        out_shape=[jax.ShapeDtypeStruct((n, d), F32), jax.ShapeDtypeStruct((n, d), F32),
                   jax.ShapeDtypeStruct((nq, n, LANES), BF16)],
        compiler_params=_params("parallel"),
        name="merge",
    )(sb, nsa, nsa, mem, gates_slab, bm, x2d, w_sb, w_nsa, w_mem, w_out, fg, w_q)


def _peer_topk_kernel(q_ref, sk_ref, idx_ref, gw_ref, tv_ref, ti_ref, bv_ref):
    tt = q_ref.shape[1]
    kk = PEER_TOPK
    nk = PEER_N_KEYS
    rid = lax.broadcasted_iota(I32, (nk, tt), 0).astype(F32)
    for h in range(PEER_HEADS):
        scores = tuple(_dot_nt(sk_ref[2 * h + p], q_ref[2 * h + p]) for p in range(2))

        def pick(k, ss, h=h):
            out = []
            for p, s in enumerate(ss):
                best = jnp.max(s, axis=0, keepdims=True)
                first = jnp.min(jnp.where(s == best, rid, float(nk)), axis=0, keepdims=True)
                tv_ref[2 * h + p, pl.ds(k, 1), :] = best
                ti_ref[2 * h + p, pl.ds(k, 1), :] = first
                out.append(jnp.where(rid == first, -jnp.inf, s))
            return tuple(out)

        lax.fori_loop(0, kk, pick, scores)

    sub = 8
    widths = [min(kk, -(-(kk // (i + 1)) // sub) * sub) for i in range(kk // 2)]
    n_cand = sum(widths) + kk // 2
    pos = lax.broadcasted_iota(I32, (n_cand, tt), 0).astype(F32)
    for h in range(PEER_HEADS):
        s0, s1 = tv_ref[2 * h], tv_ref[2 * h + 1]
        i0, i1 = ti_ref[2 * h], ti_ref[2 * h + 1]
        cand, cidx = [], []
        for i, wd in enumerate(widths):
            keep = lax.broadcasted_iota(I32, (wd, 1), 0) < kk // (i + 1)
            cand.append(jnp.where(keep, s0[i:i + 1, :] + s1[0:wd, :], -jnp.inf))
            cidx.append(i0[i:i + 1, :] * float(nk) + i1[0:wd, :])
        cand.append(s0[kk // 2:, :] + s1[0:1, :])
        cidx.append(i0[kk // 2:, :] * float(nk) + i1[0:1, :])
        cand = jnp.concatenate(cand, axis=0)
        cidx = jnp.concatenate(cidx, axis=0)

        def pick2(k, s, h=h, cidx=cidx):
            best = jnp.max(s, axis=0, keepdims=True)
            first = jnp.min(jnp.where(s == best, pos, float(n_cand)), axis=0, keepdims=True)
            hit = pos == first
            bv_ref[pl.ds(k, 1), :] = best
            expert = jnp.max(jnp.where(hit, cidx, -1.0), axis=0, keepdims=True)
            idx_ref[pl.ds(h * kk + k, 1), :] = expert.astype(I32)
            return jnp.where(hit, -jnp.inf, s)

        lax.fori_loop(0, kk, pick2, cand)
        best = bv_ref[...]
        e = jnp.exp(best - best[0:1, :])
        gw_ref[h * kk:(h + 1) * kk, :] = e / jnp.sum(e, axis=0, keepdims=True)


def _peer_topk(q_chunks, subkeys, tt=256):
    nchunk, n, half = q_chunks.shape
    slots = PEER_HEADS * PEER_TOPK
    return pl.pallas_call(
        _peer_topk_kernel,
        grid=(n // tt,),
        in_specs=[
            pl.BlockSpec((nchunk, tt, half), lambda i: (0, i, 0)),
            pl.BlockSpec(subkeys.shape, lambda i: (0, 0, 0)),
        ],
        out_specs=[pl.BlockSpec((slots, tt), lambda i: (0, i)),
                   pl.BlockSpec((slots, tt), lambda i: (0, i))],
        out_shape=[jax.ShapeDtypeStruct((slots, n), I32), jax.ShapeDtypeStruct((slots, n), F32)],
        scratch_shapes=[pltpu.VMEM((nchunk, PEER_TOPK, tt), F32),
                        pltpu.VMEM((nchunk, PEER_TOPK, tt), F32),
                        pltpu.VMEM((PEER_TOPK, tt), F32)],
        compiler_params=_params("parallel"),
        name="peer_topk",
    )(q_chunks, subkeys)


def _peer_ffn_kernel(*refs, first_tiles, staged):
    ns = len(staged)
    idx_hbm, tables = refs[0], refs[1:1 + ns]
    io = [refs[1 + ns + 3 * s:4 + ns + 3 * s] for s in range(ns)]
    fg_ref = refs[1 + 4 * ns]
    ys = refs[2 + 4 * ns:2 + 5 * ns]
    idx_smem, bufs = refs[2 + 5 * ns], refs[3 + 5 * ns:3 + 6 * ns]
    idx_sem, sems = refs[3 + 6 * ns], refs[4 + 6 * ns:4 + 7 * ns]
    tt, d = io[0][1].shape
    slots = io[0][0].shape[0]
    rows = d // (2 * LANES)
    tile = pl.program_id(0)
    nbuf = bufs[0].shape[0]
    tok_lane = lax.broadcasted_iota(I32, (1, tt), 1)

    def halves(words):
        return tuple(pltpu.unpack_elementwise(words, index=i, packed_dtype=BF16, unpacked_dtype=F32)
                     for i in range(2))

    def make_stream(s):
        table, buf, row_sem, y_ref = tables[s], bufs[s], sems[s], ys[s]
        gw_ref, h_ref, _ = io[s]

        if staged[s]:
            def start(tok, slot):
                for r in range(buf.shape[1]):
                    pltpu.make_async_copy(table.at[tile * tt + tok, :, r, :], buf.at[slot, r],
                                          row_sem.at[slot]).start(priority=r % 2)
        else:
            idx_copy = pltpu.make_async_copy(idx_hbm.at[tile + first_tiles[s]], idx_smem, idx_sem)
            idx_copy.start()
            idx_copy.wait()

            def start(tok, slot):
                for e in range(slots):
                    pltpu.make_async_copy(table.at[idx_smem[tok * slots + e]], buf.at[slot, :, e, :],
                                          row_sem.at[slot]).start(priority=e % 2)

        def wait(slot):
            pltpu.make_async_copy(buf.at[slot], buf.at[slot], row_sem.at[slot]).wait()

        def expert_weights(tok, slot):
            h = h_ref[pl.ds(tok, 1), :]
            prod = jnp.zeros((slots, LANES), F32)
            for r in range(rows):
                lo, hi = halves(buf[slot, r])
                prod = prod + lo * h[:, r * LANES:(r + 1) * LANES]
                prod = prod + hi * h[:, d // 2 + r * LANES:d // 2 + (r + 1) * LANES]
            a = jnp.sum(prod, axis=1, keepdims=True)
            act = 0.5 * a * (1.0 + lax.erf(a * (2.0 ** -0.5)))
            gate = jnp.sum(jnp.where(tok_lane == tok, gw_ref[...], 0.0), axis=1, keepdims=True)
            return gate * act

        def weighted_values(tok, slot, w):
            parts = [halves(buf[slot, rows + r]) for r in range(rows)]
            y_ref[pl.ds(tok, 1), :] = jnp.concatenate(
                [jnp.sum(parts[r][i] * w, axis=0, keepdims=True) for i in range(2) for r in range(rows)],
                axis=1)

        def step(tok, slot, w, refill):
            nxt = (slot + 1) % nbuf
            wait(nxt)
            w_next = expert_weights(tok + 1, nxt)
            weighted_values(tok, slot, w)
            if refill:
                start(tok + nbuf, slot)
            return w_next

        return start, wait, expert_weights, weighted_values, step

    streams = [make_stream(s) for s in range(ns)]

    ws = []
    for start, wait, expert_weights, _, _ in streams:
        for tok in range(nbuf):
            start(tok, tok)
    for start, wait, expert_weights, _, _ in streams:
        wait(0)
        ws.append(expert_weights(0, 0))

    def round_of_steps(i, ws):
        ws = list(ws)
        for slot in range(nbuf):
            for s in range(ns):
                ws[s] = streams[s][4](i * nbuf + slot, slot, ws[s], True)
        return tuple(ws)

    n_rounds = tt // nbuf - 1
    ws = list(lax.fori_loop(0, n_rounds, round_of_steps, tuple(ws)))
    for slot in range(nbuf - 1):
        for s in range(ns):
            ws[s] = streams[s][4](n_rounds * nbuf + slot, slot, ws[s], False)
    for s in range(ns):
        streams[s][3](tt - 1, nbuf - 1, ws[s])
        ys[s][...] = _rms(io[s][2][...] + ys[s][...], fg_ref[...])


def _pack_tables_kernel(u_ref, v_ref, o_ref):
    d = u_ref.shape[1]
    rows = d // (2 * LANES)
    for base, src in ((0, u_ref), (rows, v_ref)):
        for r in range(rows):
            lo = src[:, r * LANES:(r + 1) * LANES]
            hi = src[:, d // 2 + r * LANES:d // 2 + (r + 1) * LANES]
            o_ref[:, base + r, :] = pltpu.pack_elementwise([lo, hi], packed_dtype=BF16)


def _pack_tables(u, v, te=512):
    n_exp, d = u.shape
    rows = d // LANES
    return pl.pallas_call(
        _pack_tables_kernel,
        grid=(n_exp // te,),
        in_specs=[pl.BlockSpec((te, d), lambda i: (i, 0)), pl.BlockSpec((te, d), lambda i: (i, 0))],
        out_specs=pl.BlockSpec((te, rows, LANES), lambda i: (i, 0, 0)),
        out_shape=jax.ShapeDtypeStruct((n_exp, rows, LANES), jnp.uint32),
        compiler_params=_params("parallel"),
        name="pack_tables",
    )(u, v)


def _sc_gather_slabs(slabs, slab_idx):
    m = slab_idx.shape[0]
    mesh = plsc.VectorSubcoreMesh(core_axis_name="core", subcore_axis_name="subcore")
    idx_rows = jnp.pad(slab_idx.reshape(m // SC_GATHER_WINDOW, SC_GATHER_WINDOW),
                       ((0, 0), (0, LANES - SC_GATHER_WINDOW)))

    @pl.kernel(out_type=jax.ShapeDtypeStruct((m,) + slabs.shape[1:], slabs.dtype), mesh=mesh)
    def gather(slabs_hbm, idx_hbm, out_hbm):
        def window(idx_vmem, out_vmem):
            pltpu.sync_copy(slabs_hbm.at[idx_vmem.at[0, pl.ds(0, SC_GATHER_WINDOW)]], out_vmem)

        pltpu.emit_pipeline(
            window,
            grid=(m // SC_GATHER_WINDOW,),
            in_specs=[pl.BlockSpec((1, LANES), index_map=lambda i: (i, 0))],
            out_specs=[pl.BlockSpec((SC_GATHER_WINDOW,) + slabs.shape[1:], index_map=lambda i: (i, 0, 0))],
            core_axis_name=("core", "subcore"),
            dimension_semantics=(pltpu.PARALLEL,),
            trace_scopes=False,
        )(idx_hbm, out_hbm)

    return gather(slabs, idx_rows)


def _peer_ffn(idx_tiles, tables, gw, h2, x1, final_g, tt, first_tiles, n_tiles, staged):
    _, d = h2.shape
    slots = gw.shape[0]
    rows = d // LANES
    ns = len(staged)
    assert sum(not st for st in staged) <= 1, "one index buffer: at most one gathering stream"
    hbm = pl.BlockSpec(memory_space=pl.ANY)
    io_specs, io_args = [], []
    for s in range(ns):
        io_specs += [pl.BlockSpec((slots, tt), lambda i, f=first_tiles[s]: (0, i + f)),
                     pl.BlockSpec((tt, d), lambda i, f=first_tiles[s]: (i + f, 0)),
                     pl.BlockSpec((tt, d), lambda i, f=first_tiles[s]: (i + f, 0))]
        io_args += [gw, h2, x1]
    buf = pltpu.VMEM((PEER_GATHER_BUFFERS, rows, slots, LANES), tables[0].dtype)
    sem = pltpu.SemaphoreType.DMA((PEER_GATHER_BUFFERS,))
    outs = pl.pallas_call(
        functools.partial(_peer_ffn_kernel, first_tiles=tuple(first_tiles), staged=tuple(staged)),
        grid=(n_tiles,),
        in_specs=[hbm] + [hbm] * ns + io_specs + [pl.BlockSpec((1, d), lambda i: (0, 0))],
        out_specs=[pl.BlockSpec((tt, d), lambda i: (i, 0))] * ns,
        out_shape=[jax.ShapeDtypeStruct((n_tiles * tt, d), F32)] * ns,
        scratch_shapes=[pltpu.SMEM((tt * slots,), I32)] + [buf] * ns + [pltpu.SemaphoreType.DMA] + [sem] * ns,
        compiler_params=_params("arbitrary"),
        name="peer_ffn_" + "_".join("staged" if st else "gather" for st in staged),
    )(idx_tiles, *tables, *io_args, final_g.reshape(1, d))
    return list(outs)


def _mixers(x, mem, mix_g, mem_g, w_in, b_merge, pe_k, pe_v, cw_k, cw_v, w_mem_kv, w_sb_br, w_nsa_br,
            w_mem_br, w_out, ffn_g, peer_w_q, subkeys):
    b, t, d = x.shape
    m = mem.shape[1]
    g = NSA_KV_GROUPS
    n = b * t
    x2d = x.reshape(n, d)

    o_sbq, o_sbk, o_sbv = 0, SB_W, 2 * SB_W
    o_nq = 3 * SB_W
    o_nkv = o_nq + NSA_W
    o_ng = o_nkv + 6 * NSA_KV_W
    o_mq = o_ng + NSA_HEADS * 3
    o_mg = o_mq + MEM_W
    w_act = jnp.concatenate([w_in[:, :o_ng], w_in[:, o_mq:o_mg]], axis=1).astype(BF16)
    gate_pad = LANES - NSA_HEADS * 3
    w_gate = jnp.concatenate([w_in[:, o_mg:], w_in[:, o_ng:o_mq], jnp.zeros((d, gate_pad), w_in.dtype)],
                             axis=1).astype(BF16)
    hd = _norm_matmul_heads(x2d, mix_g, w_act, tm=256, tn=512).reshape(-1, b, t, HEAD_DIM)
    gates_slab = _norm_matmul(x2d, mix_g, w_gate, F32, tm=256, tn=640)
    head_of = lambda col: col // HEAD_DIM

    sb_out = _sb_attention(hd, head_of(o_sbq), head_of(o_sbk), head_of(o_sbv), SB_HEADS)

    kc = _nsa_compress(hd, head_of(o_nkv), pe_k, cw_k)
    vc = _nsa_compress(hd, head_of(o_nkv) + g, pe_v, cw_v)
    gl = gates_slab[:, N_BRANCHES * d:N_BRANCHES * d + NSA_HEADS * 3]
    gl = gl.reshape(b, t, g, NSA_GROUP * 3).transpose(0, 2, 1, 3)
    gl = jnp.pad(gl, ((0, 0), (0, 0), (0, 0), (0, LANES - NSA_GROUP * 3)))
    slopes = jnp.asarray([2.0 ** (-8.0 * (h + 1) / NSA_HEADS) for h in range(NSA_HEADS)], F32)
    nsa_out = _nsa_attention(hd, head_of(o_nq), head_of(o_nkv) + 2 * g, kc, vc, gl, slopes)

    mkv = _norm_matmul(mem.reshape(b * m, d), mem_g, w_mem_kv.astype(BF16), BF16, tm=256, tn=512)
    mem_out = _mem_attention(hd, head_of(o_nkv + 6 * NSA_KV_W), mkv.reshape(b, m, 2 * MEM_W))

    x1, h2, q_chunks = _merge(
        sb_out.reshape(n, SB_W), nsa_out, mem_out.reshape(n, MEM_W), gates_slab, b_merge, x2d,
        w_sb_br.astype(BF16), w_nsa_br.astype(BF16), w_mem_br.astype(BF16), w_out.astype(BF16), ffn_g,
        peer_w_q.astype(BF16))

    half = PEER_QUERY_DIM // 2
    idx, gw = _peer_topk(q_chunks, subkeys.reshape(2 * PEER_HEADS, PEER_N_KEYS, half).astype(BF16))
    return x1, h2, idx.T, gw


def _layer(x, mem, *params_and_tables):
    uv, out_g = params_and_tables[-2:]
    b, t, d = x.shape
    tt = 128
    x1, h2, idx_tok, gw = _mixers(x, mem, *params_and_tables[:-2])
    n, slots = idx_tok.shape
    n_tiles = n // tt
    idx_tiles = idx_tok.reshape(n_tiles, tt * slots)
    chunk = SC_CHUNK_TILES
    sc_first = n_tiles - SC_CHUNKS * chunk
    lead = sc_first - SC_CHUNKS * chunk
    assert lead > 0

    def stage(c):
        lo = (sc_first + c * chunk) * tt
        slabs = _sc_gather_slabs(uv, idx_tok[lo:lo + chunk * tt].reshape(-1))
        return slabs.reshape((chunk * tt, slots) + uv.shape[1:])

    staged = [stage(c) for c in range(SC_CHUNKS)]
    ffn = functools.partial(_peer_ffn, idx_tiles, gw=gw, h2=h2, x1=x1, final_g=out_g, tt=tt)
    own = ffn(tables=[uv], first_tiles=[0], n_tiles=lead, staged=[False])
    theirs = []
    for c in range(SC_CHUNKS):
        mine, other = ffn(tables=[uv, staged[c]], first_tiles=[lead + c * chunk, sc_first + c * chunk],
                          n_tiles=chunk, staged=[False, True])
        own.append(mine)
        theirs.append(other)
    return jnp.concatenate(own + theirs, axis=0).reshape(b, t, d)


def kernel(x, mem, mix_norm_g, mem_norm_g, w_in, b_merge, cmp_pe_k, cmp_pe_v, cmp_w_k, cmp_w_v, w_mem_kv, w_sb_br, w_nsa_br, w_mem_br, w_out, ffn_norm_g, peer_w_q, peer_subkeys, peer_u, peer_v, final_norm_g):
    depth = w_in.shape[0]
    assert depth == 1, "the final rmsnorm is fused into the last layer's PEER kernel"
    l = 0
    uv = _pack_tables(peer_u[l], peer_v[l])
    return _layer(x, mem, mix_norm_g[l], mem_norm_g[l], w_in[l], b_merge[l], cmp_pe_k[l], cmp_pe_v[l],
                  cmp_w_k[l], cmp_w_v[l], w_mem_kv[l], w_sb_br[l], w_nsa_br[l], w_mem_br[l], w_out[l],
                  ffn_norm_g[l], peer_w_q[l], peer_subkeys[l], uv, final_norm_g)
```

```python
import functools

import jax
import jax.numpy as jnp
from jax import lax
from jax.experimental import pallas as pl
from jax.experimental.pallas import tpu as pltpu
from jax.experimental.pallas import tpu_sc as plsc

F32 = jnp.float32
BF16 = jnp.bfloat16
I32 = jnp.int32

HEAD_DIM = 64
SB_HEADS = 6
NSA_HEADS = 6
NSA_KV_GROUPS = 2
NSA_GROUP = NSA_HEADS // NSA_KV_GROUPS
MEM_HEADS = 4
N_BRANCHES = 3
SB_W = SB_HEADS * HEAD_DIM
NSA_W = NSA_HEADS * HEAD_DIM
NSA_KV_W = NSA_KV_GROUPS * HEAD_DIM
MEM_W = MEM_HEADS * HEAD_DIM
CMP_LEN = 32
CMP_STRIDE = 16
SEL_BLOCK = 64
N_SELECT = 16
WINDOW = 512
FORCED_SCORE = 1e4
PEER_HEADS = 8
PEER_N_KEYS = 128
PEER_QUERY_DIM = 256
PEER_TOPK = 16
RMS_EPS = 1e-6
NEG_INF = -1e30
SCALE = HEAD_DIM ** -0.5
SB_DEAD_LOG = 104.0

LANES = 128
PEER_GATHER_BUFFERS = 8
SC_GATHER_WINDOW = 32
SC_CHUNKS = 4
SC_CHUNK_TILES = 28
VMEM_LIMIT_BYTES = 56 * 1024 * 1024

_NT = (((1,), (1,)), ((), ()))


def _params(*sem):
    return pltpu.CompilerParams(dimension_semantics=sem, vmem_limit_bytes=VMEM_LIMIT_BYTES)


def _dot(a, b):
    return jnp.dot(a, b, preferred_element_type=F32)


def _dot_nt(a, b):
    return lax.dot_general(a, b, _NT, preferred_element_type=F32)


def _sigmoid(x):
    return 1.0 / (1.0 + jnp.exp(-x))


def _rms(x, g):
    return x * lax.rsqrt(jnp.mean(x * x, axis=-1, keepdims=True) + RMS_EPS) * g


def _norm_matmul_kernel(x_ref, g_ref, w_ref, o_ref, *, tn):
    h = _rms(x_ref[...], g_ref[...]).astype(BF16)
    for c in range(0, o_ref.shape[1], tn):
        o_ref[:, c:c + tn] = _dot(h, w_ref[:, c:c + tn]).astype(o_ref.dtype)


def _norm_matmul(x2d, g, w, out_dtype, tm, tn):
    n, d = x2d.shape
    m = w.shape[1]
    return pl.pallas_call(
        functools.partial(_norm_matmul_kernel, tn=tn),
        grid=(n // tm,),
        in_specs=[
            pl.BlockSpec((tm, d), lambda i: (i, 0)),
            pl.BlockSpec((1, d), lambda i: (0, 0)),
            pl.BlockSpec((d, m), lambda i: (0, 0)),
        ],
        out_specs=pl.BlockSpec((tm, m), lambda i: (i, 0)),
        out_shape=jax.ShapeDtypeStruct((n, m), out_dtype),
        compiler_params=_params("parallel"),
        name="norm_matmul",
    )(x2d, g.reshape(1, d), w)


def _norm_matmul_heads_kernel(x_ref, g_ref, w_ref, o_ref, *, tn):
    h = _rms(x_ref[...], g_ref[...]).astype(BF16)
    per = tn // HEAD_DIM
    for c in range(0, w_ref.shape[1], tn):
        res = _dot(h, w_ref[:, c:c + tn]).astype(o_ref.dtype)
        for j in range(per):
            o_ref[c // HEAD_DIM + j] = res[:, j * HEAD_DIM:(j + 1) * HEAD_DIM]


def _norm_matmul_heads(x2d, g, w, tm, tn):
    n, d = x2d.shape
    m = w.shape[1]
    nh = m // HEAD_DIM
    return pl.pallas_call(
        functools.partial(_norm_matmul_heads_kernel, tn=tn),
        grid=(n // tm,),
        in_specs=[
            pl.BlockSpec((tm, d), lambda i: (i, 0)),
            pl.BlockSpec((1, d), lambda i: (0, 0)),
            pl.BlockSpec((d, m), lambda i: (0, 0)),
        ],
        out_specs=pl.BlockSpec((nh, tm, HEAD_DIM), lambda i: (0, i, 0)),
        out_shape=jax.ShapeDtypeStruct((nh, n, HEAD_DIM), BF16),
        compiler_params=_params("parallel"),
        name="norm_matmul_heads",
    )(x2d, g.reshape(1, d), w)


def _sb_kernel(q_ref, k_ref, v_ref, o_ref, *, tile):
    qi = pl.program_id(2)
    nh = q_ref.shape[0]
    row = lax.broadcasted_iota(I32, (tile, tile), 0)
    col = lax.broadcasted_iota(I32, (tile, tile), 1)
    lower = row > col
    later = lower.astype(BF16)
    qs = [q_ref[hh, 0] for hh in range(nh)]

    def visit(hh, ks, c, acc, diagonal):
        k = k_ref[hh, 0, pl.ds(ks, tile), :]
        v = v_ref[hh, 0, pl.ds(ks, tile), :]
        z = _dot_nt(qs[hh], k) * SCALE
        sp = jnp.maximum(z, 0.0) + jnp.log(1.0 + jnp.exp(-jnp.abs(z)))
        if diagonal:
            sp = jnp.where(lower, sp, 0.0)
        hi = sp.astype(BF16)
        lo = (sp - hi.astype(F32)).astype(BF16)
        after = _dot(hi, later) + _dot(lo, later)
        a = jnp.exp(z - sp - after - c)
        if diagonal:
            a = jnp.where(lower, a, 0.0)
        return c + jnp.sum(sp, axis=1, keepdims=True), acc + _dot(a.astype(BF16), v)

    q0 = pl.multiple_of(qi * tile, tile)
    state = []
    for hh in range(nh):
        state.extend(visit(hh, q0, jnp.zeros((tile, 1), F32), jnp.zeros((tile, HEAD_DIM), F32), True))

    def smallest_carry(state):
        c = state[0]
        for hh in range(1, nh):
            c = jnp.minimum(c, state[2 * hh])
        return jnp.min(c)

    def live(carry):
        i, c_min, _ = carry
        return (i < qi) & (c_min <= SB_DEAD_LOG)

    def body(carry):
        i, _, state = carry
        ks = pl.multiple_of((qi - 1 - i) * tile, tile)
        out = []
        for hh in range(nh):
            out.extend(visit(hh, ks, state[2 * hh], state[2 * hh + 1], False))
        return i + 1, smallest_carry(out), tuple(out)

    _, _, state = lax.while_loop(live, body, (0, smallest_carry(state), tuple(state)))
    o_ref[0] = jnp.concatenate([state[2 * hh + 1] for hh in range(nh)], axis=1).astype(o_ref.dtype)


def _sb_attention(hd, q0, k0, v0, h, tile=256):
    _, b, t, dh = hd.shape
    hp = 2
    tq = tile
    assert q0 % hp == 0 and k0 % hp == 0 and v0 % hp == 0
    return pl.pallas_call(
        functools.partial(_sb_kernel, tile=tile),
        grid=(b, h // hp, t // tq),
        in_specs=[
            pl.BlockSpec((hp, 1, tq, dh), lambda bi, hi, qi: (q0 // hp + hi, bi, qi, 0)),
            pl.BlockSpec((hp, 1, t, dh), lambda bi, hi, qi: (k0 // hp + hi, bi, 0, 0)),
            pl.BlockSpec((hp, 1, t, dh), lambda bi, hi, qi: (v0 // hp + hi, bi, 0, 0)),
        ],
        out_specs=pl.BlockSpec((1, tq, hp * dh), lambda bi, hi, qi: (bi, qi, hi)),
        out_shape=jax.ShapeDtypeStruct((b, t, h * dh), BF16),
        compiler_params=_params("parallel", "parallel", "arbitrary"),
        name="sb_attn",
    )(hd, hd, hd)


def _compress_kernel(x_ref, pe_ref, w_ref, o_ref):
    x = x_ref[0, 0]
    nc = x.shape[0]
    w_lo, w_hi = w_ref[0], w_ref[1]
    first = _dot(x, w_lo)
    second = _dot(x, w_hi)
    feat = pe_ref.shape[1]
    pe_lo = jnp.broadcast_to(pe_ref[0:1, :], (8, feat)).astype(BF16)
    pe_hi = jnp.broadcast_to(pe_ref[1:2, :], (8, feat)).astype(BF16)
    bias = _dot(pe_lo, w_lo)[0:1] + _dot(pe_hi, w_hi)[0:1]
    o_ref[0, 0] = (first + pltpu.roll(second, nc - 1, 0) + bias).astype(o_ref.dtype)


def _nsa_compress(hd, head0, pe, w):
    nh, b, t, dh = hd.shape
    g = NSA_KV_GROUPS
    nc = t // CMP_STRIDE
    feat = CMP_STRIDE * dh
    x = hd[head0:head0 + g].reshape(g, b, nc, feat)
    pe2 = pe.reshape(2, feat)
    w2 = w.reshape(2, feat, dh).astype(BF16)
    return pl.pallas_call(
        _compress_kernel,
        grid=(b, g),
        in_specs=[
            pl.BlockSpec((1, 1, nc, feat), lambda bi, gi: (gi, bi, 0, 0)),
            pl.BlockSpec((2, feat), lambda bi, gi: (0, 0)),
            pl.BlockSpec((2, feat, dh), lambda bi, gi: (0, 0, 0)),
        ],
        out_specs=pl.BlockSpec((1, 1, nc, dh), lambda bi, gi: (bi, gi, 0, 0)),
        out_shape=jax.ShapeDtypeStruct((b, g, nc, dh), BF16),
        compiler_params=_params("parallel", "parallel"),
        name="nsa_compress",
    )(x, pe2, w2)


def _nsa_kernel(slopes_ref, q_ref, kc_ref, vc_ref, ks_ref, vs_ref, kw_ref, vw_ref, gl_ref, pool_ref, ex_ref,
                o_ref, used_ref, *, tq, tk, n_sel, real_blocks):
    grp = pl.program_id(1)
    t0 = pl.program_id(2) * tq
    nc = kc_ref.shape[2]
    rr = NSA_GROUP
    t = t0 + lax.broadcasted_iota(I32, (tq, 1), 0)

    def stack(x):
        return jnp.concatenate([x] * rr, axis=0)

    q = q_ref[:, 0].reshape(rr * tq, HEAD_DIM)
    slope = jnp.concatenate([jnp.full((tq, 1), slopes_ref[grp * rr + r], F32) for r in range(rr)], axis=0)

    def heads_sum(x):
        out = x[0:tq]
        for r in range(1, rr):
            out = out + x[r * tq:(r + 1) * tq]
        return out

    def masked_softmax(s, valid):
        p = jnp.where(valid, jnp.exp(s - jnp.max(s, axis=1, keepdims=True)), 0.0)
        denom = jnp.sum(p, axis=1, keepdims=True)
        return p / jnp.where(denom > 0, denom, 1.0)

    lane = lax.broadcasted_iota(I32, (1, nc), 1)
    dist_c = (t - (lane * CMP_STRIDE + (CMP_LEN - 1))).astype(F32)
    valid_c = stack(dist_c >= 0)
    s = _dot_nt(q, kc_ref[0, 0]) * SCALE - slope * stack(dist_c)
    p = masked_softmax(jnp.where(valid_c, s, NEG_INF), valid_c)
    o_cmp = _dot(p.astype(BF16), vc_ref[0, 0])
    psum = heads_sum(p)

    n_blk = pool_ref.shape[0]
    hi = psum.astype(BF16)
    rest = psum - hi.astype(F32)
    mid = rest.astype(BF16)
    lo = (rest - mid.astype(F32)).astype(BF16)
    pool = pool_ref[...]
    imp = _dot_nt(pool, hi) + _dot_nt(pool, mid) + _dot_nt(pool, lo)
    blk = lax.broadcasted_iota(I32, (n_blk, 1), 0)
    cur = (t0 + lax.broadcasted_iota(I32, (1, tq), 1)) // SEL_BLOCK
    forced = (blk == 0) | (blk == cur) | (blk == cur - 1)
    imp = jnp.where(forced, FORCED_SCORE, jnp.where(blk <= cur, imp, -1.0))
    imp = jnp.where(blk < real_blocks, imp, -jnp.inf)
    blk_f = blk.astype(F32)

    def pick(_, carry):
        imp, sel = carry
        best = jnp.max(imp, axis=0, keepdims=True)
        first = jnp.min(jnp.where(imp == best, blk_f, float(n_blk)), axis=0, keepdims=True)
        hit = blk_f == first
        return jnp.where(hit, -jnp.inf, imp), jnp.where(hit, 1.0, sel)

    _, sel = lax.fori_loop(0, n_sel, pick, (imp, jnp.zeros((n_blk, tq), F32)))
    picked = jnp.max(sel, axis=1, keepdims=True)
    per_tile = tk // SEL_BLOCK
    n_causal = (t0 + tq + tk - 1) // tk
    n_used = jnp.int32(0)
    for j in range(used_ref.shape[0]):
        used_ref[n_used] = j
        hit = (jnp.max(picked[j * per_tile:(j + 1) * per_tile, :]) > 0) & (j < n_causal)
        n_used = n_used + hit.astype(I32)
    sel = sel.T.astype(BF16)

    span = WINDOW + tq
    w0 = pl.multiple_of(jnp.maximum(t0 - WINDOW, 0), tq)
    dist_w = t - (w0 + lax.broadcasted_iota(I32, (1, span), 1))
    valid_w = stack((dist_w >= 0) & (dist_w < WINDOW))
    s = _dot_nt(q, kw_ref[0, 0, pl.ds(w0, span), :]) * SCALE - slope * stack(dist_w.astype(F32))
    p = masked_softmax(jnp.where(valid_w, s, NEG_INF), valid_w)
    o_win = _dot(p.astype(BF16), vw_ref[0, 0, pl.ds(w0, span), :])

    kcol = lax.broadcasted_iota(I32, (1, tk), 1)

    def sel_body(i, carry):
        m, l, acc = carry
        kb = used_ref[i]
        ks0 = pl.multiple_of(kb * tk, tk)
        chosen = _dot(sel, ex_ref[kb]) > 0.5
        dist = t - (ks0 + kcol)
        valid = stack(chosen & (dist >= 0))
        s = _dot_nt(q, ks_ref[0, 0, pl.ds(ks0, tk), :]) * SCALE - slope * stack(dist.astype(F32))
        s = jnp.where(valid, s, NEG_INF)
        m_new = jnp.maximum(m, jnp.max(s, axis=1, keepdims=True))
        alpha = jnp.exp(m - m_new)
        p = jnp.where(valid, jnp.exp(s - m_new), 0.0)
        l = alpha * l + jnp.sum(p, axis=1, keepdims=True)
        acc = alpha * acc + _dot(p.astype(BF16), vs_ref[0, 0, pl.ds(ks0, tk), :])
        return m_new, l, acc

    init = (jnp.full((rr * tq, 1), NEG_INF, F32), jnp.zeros((rr * tq, 1), F32),
            jnp.zeros((rr * tq, HEAD_DIM), F32))
    _, l_sel, acc_sel = lax.fori_loop(0, n_used, sel_body, init)
    o_sel = acc_sel / l_sel

    gates = _sigmoid(gl_ref[0, 0])
    outs = []
    for r in range(rr):
        rows = slice(r * tq, (r + 1) * tq)
        outs.append(gates[:, 3 * r:3 * r + 1] * o_cmp[rows] + gates[:, 3 * r + 1:3 * r + 2] * o_sel[rows]
                    + gates[:, 3 * r + 2:3 * r + 3] * o_win[rows])
    o_ref[0, 0] = jnp.concatenate(outs, axis=1).astype(o_ref.dtype)


def _nsa_attention(hd, q0, sel0, kc, vc, gate_logits, slopes, tq=256, tk=256):
    _, b, t, dh = hd.shape
    g = NSA_KV_GROUPS
    assert q0 % NSA_GROUP == 0
    nc = kc.shape[2]
    n_sel = min(N_SELECT, t // SEL_BLOCK)
    real_blocks = t // SEL_BLOCK
    n_blk = LANES
    assert t % tk == 0 and t >= WINDOW + tq and real_blocks <= n_blk
    blk_ids = jnp.arange(n_blk, dtype=I32)
    pool = (blk_ids[:, None] == jnp.arange(nc, dtype=I32)[None, :] // (SEL_BLOCK // CMP_STRIDE)).astype(BF16)
    key_blk = (jnp.arange(t, dtype=I32) // SEL_BLOCK).reshape(t // tk, 1, tk)
    expand = (blk_ids[None, :, None] == key_blk).astype(BF16)
    def kv_spec(j):
        return pl.BlockSpec((1, 1, t, dh), lambda bi, gi, qi: (sel0 + j * g + gi, bi, 0, 0))

    c_spec = pl.BlockSpec((1, 1, nc, dh), lambda bi, gi, qi: (bi, gi, 0, 0))
    return pl.pallas_call(
        functools.partial(_nsa_kernel, tq=tq, tk=tk, n_sel=n_sel, real_blocks=real_blocks),
        grid=(b, g, t // tq),
        in_specs=[
            pl.BlockSpec(memory_space=pltpu.SMEM),
            pl.BlockSpec((NSA_GROUP, 1, tq, dh), lambda bi, gi, qi: (q0 // NSA_GROUP + gi, bi, qi, 0)),
            c_spec, c_spec, kv_spec(0), kv_spec(1), kv_spec(2), kv_spec(3),
            pl.BlockSpec((1, 1, tq, LANES), lambda bi, gi, qi: (bi, gi, qi, 0)),
            pl.BlockSpec(pool.shape, lambda bi, gi, qi: (0, 0)),
            pl.BlockSpec(expand.shape, lambda bi, gi, qi: (0, 0, 0)),
        ],
        out_specs=pl.BlockSpec((1, 1, tq, NSA_GROUP * dh), lambda bi, gi, qi: (bi, gi, qi, 0)),
        out_shape=jax.ShapeDtypeStruct((b, g, t, NSA_GROUP * dh), BF16),
        scratch_shapes=[pltpu.SMEM((t // tk,), I32)],
        compiler_params=_params("parallel", "parallel", "arbitrary"),
        name="nsa_attn",
    )(slopes, hd, kc, vc, hd, hd, hd, hd, gate_logits, pool, expand)


def _mem_kernel(q_ref, kv_ref, o_ref):
    kv = kv_ref[0]
    outs = []
    for h in range(MEM_HEADS):
        sl = slice(h * HEAD_DIM, (h + 1) * HEAD_DIM)
        s = _dot_nt(q_ref[h, 0], kv[:, sl]) * SCALE
        p = jnp.exp(s - jnp.max(s, axis=1, keepdims=True))
        p = p / jnp.sum(p, axis=1, keepdims=True)
        outs.append(_dot(p.astype(BF16), kv[:, MEM_W + h * HEAD_DIM:MEM_W + (h + 1) * HEAD_DIM]))
    o_ref[0] = jnp.concatenate(outs, axis=1).astype(o_ref.dtype)


def _mem_attention(hd, q0, mkv, tq=512):
    _, b, t, dh = hd.shape
    w = MEM_HEADS * dh
    m = mkv.shape[1]
    assert q0 % MEM_HEADS == 0
    return pl.pallas_call(
        _mem_kernel,
        grid=(b, t // tq),
        in_specs=[
            pl.BlockSpec((MEM_HEADS, 1, tq, dh), lambda bi, qi: (q0 // MEM_HEADS, bi, qi, 0)),
            pl.BlockSpec((1, m, 2 * w), lambda bi, qi: (bi, 0, 0)),
        ],
        out_specs=pl.BlockSpec((1, tq, w), lambda bi, qi: (bi, qi, 0)),
        out_shape=jax.ShapeDtypeStruct((b, t, w), BF16),
        compiler_params=_params("parallel", "parallel"),
        name="mem_attn",
    )(hd, mkv)


def _merge_kernel(sb_ref, nsa0_ref, nsa1_ref, mem_ref, mg_ref, bm_ref, x_ref, wsb_ref, wnsa_ref, wmem_ref,
                  wout_ref, fg_ref, wq_ref, x1_ref, h2_ref, q_ref):
    d = x_ref.shape[1]
    gw = nsa0_ref.shape[-1]
    nsa = _dot(nsa0_ref[0, 0], wnsa_ref[0:gw, :]) + _dot(nsa1_ref[0, 0], wnsa_ref[gw:2 * gw, :])
    branches = (_dot(sb_ref[...], wsb_ref[...]), nsa, _dot(mem_ref[...], wmem_ref[...]))
    merged = jnp.zeros_like(branches[0])
    for j in range(N_BRANCHES):
        gate = _sigmoid(mg_ref[:, j * d:(j + 1) * d] + bm_ref[:, j * d:(j + 1) * d])
        merged = merged + gate * branches[j]
    x1 = x_ref[...] + _dot(merged.astype(BF16), wout_ref[...])
    x1_ref[...] = x1
    h2 = _rms(x1, fg_ref[...])
    h2_ref[...] = h2
    q = _dot(h2.astype(BF16), wq_ref[...]).astype(q_ref.dtype)
    for c in range(q_ref.shape[0]):
        q_ref[c] = q[:, c * LANES:(c + 1) * LANES]


def _merge(sb, nsa, mem, gates_slab, b_merge, x2d, w_sb, w_nsa, w_mem, w_out, ffn_g, w_q, tm=256):
    n, d = x2d.shape
    nq = w_q.shape[1] // LANES
    _, g, t, gw = nsa.shape
    assert g == 2 and t % tm == 0
    per_row = t // tm
    row = lambda w: pl.BlockSpec((tm, w), lambda i: (i, 0))
    nsa_group = lambda gi: pl.BlockSpec((1, 1, tm, gw), lambda i: (i // per_row, gi, i % per_row, 0))
    full = lambda a: pl.BlockSpec(a.shape, lambda i: (0,) * a.ndim)
    bm = b_merge.reshape(1, -1)
    fg = ffn_g.reshape(1, d)
    return pl.pallas_call(
        _merge_kernel,
        grid=(n // tm,),
        in_specs=[row(sb.shape[1]), nsa_group(0), nsa_group(1), row(mem.shape[1]), row(N_BRANCHES * d),
                  full(bm), row(d), full(w_sb), full(w_nsa), full(w_mem), full(w_out), full(fg),
                  full(w_q)],
        out_specs=[row(d), row(d), pl.BlockSpec((nq, tm, LANES), lambda i: (0, i, 0))],
        out_shape=[jax.ShapeDtypeStruct((n, d), F32), jax.ShapeDtypeStruct((n, d), F32),
                   jax.ShapeDtypeStruct((nq, n, LANES), BF16)],
        compiler_params=_params("parallel"),
        name="merge",
    )(sb, nsa, nsa, mem, gates_slab, bm, x2d, w_sb, w_nsa, w_mem, w_out, fg, w_q)


def _peer_topk_kernel(q_ref, sk_ref, idx_ref, gw_ref, tv_ref, ti_ref, bv_ref):
    tt = q_ref.shape[1]
    kk = PEER_TOPK
    nk = PEER_N_KEYS
    rid = lax.broadcasted_iota(I32, (nk, tt), 0).astype(F32)
    for h in range(PEER_HEADS):
        scores = tuple(_dot_nt(sk_ref[2 * h + p], q_ref[2 * h + p]) for p in range(2))

        def pick(k, ss, h=h):
            out = []
            for p, s in enumerate(ss):
                best = jnp.max(s, axis=0, keepdims=True)
                first = jnp.min(jnp.where(s == best, rid, float(nk)), axis=0, keepdims=True)
                tv_ref[2 * h + p, pl.ds(k, 1), :] = best
                ti_ref[2 * h + p, pl.ds(k, 1), :] = first
                out.append(jnp.where(rid == first, -jnp.inf, s))
            return tuple(out)

        lax.fori_loop(0, kk, pick, scores)

    sub = 8
    widths = [min(kk, -(-(kk // (i + 1)) // sub) * sub) for i in range(kk // 2)]
    n_cand = sum(widths) + kk // 2
    pos = lax.broadcasted_iota(I32, (n_cand, tt), 0).astype(F32)
    def candidates(h):
        s0, s1 = tv_ref[2 * h], tv_ref[2 * h + 1]
        i0, i1 = ti_ref[2 * h], ti_ref[2 * h + 1]
        cand, cidx = [], []
        for i, wd in enumerate(widths):
            keep = lax.broadcasted_iota(I32, (wd, 1), 0) < kk // (i + 1)
            cand.append(jnp.where(keep, s0[i:i + 1, :] + s1[0:wd, :], -jnp.inf))
            cidx.append(i0[i:i + 1, :] * float(nk) + i1[0:wd, :])
        cand.append(s0[kk // 2:, :] + s1[0:1, :])
        cidx.append(i0[kk // 2:, :] * float(nk) + i1[0:1, :])
        return jnp.concatenate(cand, axis=0), jnp.concatenate(cidx, axis=0)

    for h0 in range(0, PEER_HEADS, 2):
        pairs = [candidates(h0 + j) for j in range(2)]

        def pick2(k, ss, h0=h0, pairs=pairs):
            out = []
            for j, s in enumerate(ss):
                best = jnp.max(s, axis=0, keepdims=True)
                first = jnp.min(jnp.where(s == best, pos, float(n_cand)), axis=0, keepdims=True)
                hit = pos == first
                bv_ref[j, pl.ds(k, 1), :] = best
                expert = jnp.max(jnp.where(hit, pairs[j][1], -1.0), axis=0, keepdims=True)
                idx_ref[pl.ds((h0 + j) * kk + k, 1), :] = expert.astype(I32)
                out.append(jnp.where(hit, -jnp.inf, s))
            return tuple(out)

        lax.fori_loop(0, kk, pick2, tuple(c for c, _ in pairs))
        for j in range(2):
            best = bv_ref[j]
            e = jnp.exp(best - best[0:1, :])
            gw_ref[(h0 + j) * kk:(h0 + j + 1) * kk, :] = e / jnp.sum(e, axis=0, keepdims=True)


def _peer_topk(q_chunks, subkeys, tt=256):
    nchunk, n, half = q_chunks.shape
    slots = PEER_HEADS * PEER_TOPK
    return pl.pallas_call(
        _peer_topk_kernel,
        grid=(n // tt,),
        in_specs=[
            pl.BlockSpec((nchunk, tt, half), lambda i: (0, i, 0)),
            pl.BlockSpec(subkeys.shape, lambda i: (0, 0, 0)),
        ],
        out_specs=[pl.BlockSpec((slots, tt), lambda i: (0, i)),
                   pl.BlockSpec((slots, tt), lambda i: (0, i))],
        out_shape=[jax.ShapeDtypeStruct((slots, n), I32), jax.ShapeDtypeStruct((slots, n), F32)],
        scratch_shapes=[pltpu.VMEM((nchunk, PEER_TOPK, tt), F32),
                        pltpu.VMEM((nchunk, PEER_TOPK, tt), F32),
                        pltpu.VMEM((2, PEER_TOPK, tt), F32)],
        compiler_params=_params("parallel"),
        name="peer_topk",
    )(q_chunks, subkeys)


def _peer_ffn_kernel(*refs, first_tiles, staged):
    ns = len(staged)
    idx_hbm, tables = refs[0], refs[1:1 + ns]
    io = [refs[1 + ns + 3 * s:4 + ns + 3 * s] for s in range(ns)]
    fg_ref = refs[1 + 4 * ns]
    ys = refs[2 + 4 * ns:2 + 5 * ns]
    idx_smem, bufs = refs[2 + 5 * ns], refs[3 + 5 * ns:3 + 6 * ns]
    idx_sem, sems = refs[3 + 6 * ns], refs[4 + 6 * ns:4 + 7 * ns]
    tt, d = io[0][1].shape
    slots = io[0][0].shape[0]
    rows = d // (2 * LANES)
    tile = pl.program_id(0)
    nbuf = bufs[0].shape[0]
    tok_lane = lax.broadcasted_iota(I32, (1, tt), 1)

    def halves(words):
        return tuple(pltpu.unpack_elementwise(words, index=i, packed_dtype=BF16, unpacked_dtype=F32)
                     for i in range(2))

    def make_stream(s):
        table, buf, row_sem, y_ref = tables[s], bufs[s], sems[s], ys[s]
        gw_ref, h_ref, _ = io[s]

        if staged[s]:
            def start(tok, slot):
                for r in range(buf.shape[1]):
                    pltpu.make_async_copy(table.at[tile * tt + tok, :, r, :], buf.at[slot, r],
                                          row_sem.at[slot]).start(priority=r % 2)
        else:
            idx_copy = pltpu.make_async_copy(idx_hbm.at[tile + first_tiles[s]], idx_smem, idx_sem)
            idx_copy.start()
            idx_copy.wait()

            def start(tok, slot):
                for e in range(slots):
                    pltpu.make_async_copy(table.at[idx_smem[tok * slots + e]], buf.at[slot, :, e, :],
                                          row_sem.at[slot]).start(priority=e % 2)

        def wait(slot):
            pltpu.make_async_copy(buf.at[slot], buf.at[slot], row_sem.at[slot]).wait()

        def expert_weights(tok, slot):
            h = h_ref[pl.ds(tok, 1), :]
            prod = jnp.zeros((slots, LANES), F32)
            for r in range(rows):
                lo, hi = halves(buf[slot, r])
                prod = prod + lo * h[:, r * LANES:(r + 1) * LANES]
                prod = prod + hi * h[:, d // 2 + r * LANES:d // 2 + (r + 1) * LANES]
            a = jnp.sum(prod, axis=1, keepdims=True)
            act = 0.5 * a * (1.0 + lax.erf(a * (2.0 ** -0.5)))
            gate = jnp.sum(jnp.where(tok_lane == tok, gw_ref[...], 0.0), axis=1, keepdims=True)
            return gate * act

        def weighted_values(tok, slot, w):
            parts = [halves(buf[slot, rows + r]) for r in range(rows)]
            y_ref[pl.ds(tok, 1), :] = jnp.concatenate(
                [jnp.sum(parts[r][i] * w, axis=0, keepdims=True) for i in range(2) for r in range(rows)],
                axis=1)

        def step(tok, slot, w, refill):
            nxt = (slot + 1) % nbuf
            wait(nxt)
            w_next = expert_weights(tok + 1, nxt)
            weighted_values(tok, slot, w)
            if refill:
                start(tok + nbuf, slot)
            return w_next

        return start, wait, expert_weights, weighted_values, step

    streams = [make_stream(s) for s in range(ns)]

    ws = []
    for start, wait, expert_weights, _, _ in streams:
        for tok in range(nbuf):
            start(tok, tok)
    for start, wait, expert_weights, _, _ in streams:
        wait(0)
        ws.append(expert_weights(0, 0))

    def round_of_steps(i, ws):
        ws = list(ws)
        for slot in range(nbuf):
            for s in range(ns):
                ws[s] = streams[s][4](i * nbuf + slot, slot, ws[s], True)
        return tuple(ws)

    n_rounds = tt // nbuf - 1
    ws = list(lax.fori_loop(0, n_rounds, round_of_steps, tuple(ws)))
    for slot in range(nbuf - 1):
        for s in range(ns):
            ws[s] = streams[s][4](n_rounds * nbuf + slot, slot, ws[s], False)
    for s in range(ns):
        streams[s][3](tt - 1, nbuf - 1, ws[s])
        ys[s][...] = _rms(io[s][2][...] + ys[s][...], fg_ref[...])


def _pack_tables_kernel(u_ref, v_ref, o_ref):
    d = u_ref.shape[1]
    rows = d // (2 * LANES)
    for base, src in ((0, u_ref), (rows, v_ref)):
        for r in range(rows):
            lo = src[:, r * LANES:(r + 1) * LANES]
            hi = src[:, d // 2 + r * LANES:d // 2 + (r + 1) * LANES]
            o_ref[:, base + r, :] = pltpu.pack_elementwise([lo, hi], packed_dtype=BF16)


def _pack_tables(u, v, te=512):
    n_exp, d = u.shape
    rows = d // LANES
    return pl.pallas_call(
        _pack_tables_kernel,
        grid=(n_exp // te,),
        in_specs=[pl.BlockSpec((te, d), lambda i: (i, 0)), pl.BlockSpec((te, d), lambda i: (i, 0))],
        out_specs=pl.BlockSpec((te, rows, LANES), lambda i: (i, 0, 0)),
        out_shape=jax.ShapeDtypeStruct((n_exp, rows, LANES), jnp.uint32),
        compiler_params=_params("parallel"),
        name="pack_tables",
    )(u, v)


def _sc_gather_slabs(slabs, slab_idx):
    m = slab_idx.shape[0]
    mesh = plsc.VectorSubcoreMesh(core_axis_name="core", subcore_axis_name="subcore")
    idx_rows = jnp.pad(slab_idx.reshape(m // SC_GATHER_WINDOW, SC_GATHER_WINDOW),
                       ((0, 0), (0, LANES - SC_GATHER_WINDOW)))

    @pl.kernel(out_type=jax.ShapeDtypeStruct((m,) + slabs.shape[1:], slabs.dtype), mesh=mesh)
    def gather(slabs_hbm, idx_hbm, out_hbm):
        def window(idx_vmem, out_vmem):
            pltpu.sync_copy(slabs_hbm.at[idx_vmem.at[0, pl.ds(0, SC_GATHER_WINDOW)]], out_vmem)

        pltpu.emit_pipeline(
            window,
            grid=(m // SC_GATHER_WINDOW,),
            in_specs=[pl.BlockSpec((1, LANES), index_map=lambda i: (i, 0))],
            out_specs=[pl.BlockSpec((SC_GATHER_WINDOW,) + slabs.shape[1:], index_map=lambda i: (i, 0, 0))],
            core_axis_name=("core", "subcore"),
            dimension_semantics=(pltpu.PARALLEL,),
            trace_scopes=False,
        )(idx_hbm, out_hbm)

    return gather(slabs, idx_rows)


def _peer_ffn(idx_tiles, tables, gw, h2, x1, final_g, tt, first_tiles, n_tiles, staged):
    _, d = h2.shape
    slots = gw.shape[0]
    rows = d // LANES
    ns = len(staged)
    assert sum(not st for st in staged) <= 1, "one index buffer: at most one gathering stream"
    hbm = pl.BlockSpec(memory_space=pl.ANY)
    io_specs, io_args = [], []
    for s in range(ns):
        io_specs += [pl.BlockSpec((slots, tt), lambda i, f=first_tiles[s]: (0, i + f)),
                     pl.BlockSpec((tt, d), lambda i, f=first_tiles[s]: (i + f, 0)),
                     pl.BlockSpec((tt, d), lambda i, f=first_tiles[s]: (i + f, 0))]
        io_args += [gw, h2, x1]
    buf = pltpu.VMEM((PEER_GATHER_BUFFERS, rows, slots, LANES), tables[0].dtype)
    sem = pltpu.SemaphoreType.DMA((PEER_GATHER_BUFFERS,))
    outs = pl.pallas_call(
        functools.partial(_peer_ffn_kernel, first_tiles=tuple(first_tiles), staged=tuple(staged)),
        grid=(n_tiles,),
        in_specs=[hbm] + [hbm] * ns + io_specs + [pl.BlockSpec((1, d), lambda i: (0, 0))],
        out_specs=[pl.BlockSpec((tt, d), lambda i: (i, 0))] * ns,
        out_shape=[jax.ShapeDtypeStruct((n_tiles * tt, d), F32)] * ns,
        scratch_shapes=[pltpu.SMEM((tt * slots,), I32)] + [buf] * ns + [pltpu.SemaphoreType.DMA] + [sem] * ns,
        compiler_params=_params("arbitrary"),
        name="peer_ffn_" + "_".join("staged" if st else "gather" for st in staged),
    )(idx_tiles, *tables, *io_args, final_g.reshape(1, d))
    return list(outs)


def _mixers(x, mem, mix_g, mem_g, w_in, b_merge, pe_k, pe_v, cw_k, cw_v, w_mem_kv, w_sb_br, w_nsa_br,
            w_mem_br, w_out, ffn_g, peer_w_q, subkeys):
    b, t, d = x.shape
    m = mem.shape[1]
    g = NSA_KV_GROUPS
    n = b * t
    x2d = x.reshape(n, d)

    o_sbq, o_sbk, o_sbv = 0, SB_W, 2 * SB_W
    o_nq = 3 * SB_W
    o_nkv = o_nq + NSA_W
    o_ng = o_nkv + 6 * NSA_KV_W
    o_mq = o_ng + NSA_HEADS * 3
    o_mg = o_mq + MEM_W
    w_act = jnp.concatenate([w_in[:, :o_ng], w_in[:, o_mq:o_mg]], axis=1).astype(BF16)
    gate_pad = LANES - NSA_HEADS * 3
    w_gate = jnp.concatenate([w_in[:, o_mg:], w_in[:, o_ng:o_mq], jnp.zeros((d, gate_pad), w_in.dtype)],
                             axis=1).astype(BF16)
    hd = _norm_matmul_heads(x2d, mix_g, w_act, tm=256, tn=512).reshape(-1, b, t, HEAD_DIM)
    gates_slab = _norm_matmul(x2d, mix_g, w_gate, F32, tm=256, tn=640)
    head_of = lambda col: col // HEAD_DIM

    sb_out = _sb_attention(hd, head_of(o_sbq), head_of(o_sbk), head_of(o_sbv), SB_HEADS)

    kc = _nsa_compress(hd, head_of(o_nkv), pe_k, cw_k)
    vc = _nsa_compress(hd, head_of(o_nkv) + g, pe_v, cw_v)
    gl = gates_slab[:, N_BRANCHES * d:N_BRANCHES * d + NSA_HEADS * 3]
    gl = gl.reshape(b, t, g, NSA_GROUP * 3).transpose(0, 2, 1, 3)
    gl = jnp.pad(gl, ((0, 0), (0, 0), (0, 0), (0, LANES - NSA_GROUP * 3)))
    slopes = jnp.asarray([2.0 ** (-8.0 * (h + 1) / NSA_HEADS) for h in range(NSA_HEADS)], F32)
    nsa_out = _nsa_attention(hd, head_of(o_nq), head_of(o_nkv) + 2 * g, kc, vc, gl, slopes)

    mkv = _norm_matmul(mem.reshape(b * m, d), mem_g, w_mem_kv.astype(BF16), BF16, tm=256, tn=512)
    mem_out = _mem_attention(hd, head_of(o_nkv + 6 * NSA_KV_W), mkv.reshape(b, m, 2 * MEM_W))

    x1, h2, q_chunks = _merge(
        sb_out.reshape(n, SB_W), nsa_out, mem_out.reshape(n, MEM_W), gates_slab, b_merge, x2d,
        w_sb_br.astype(BF16), w_nsa_br.astype(BF16), w_mem_br.astype(BF16), w_out.astype(BF16), ffn_g,
        peer_w_q.astype(BF16))

    half = PEER_QUERY_DIM // 2
    idx, gw = _peer_topk(q_chunks, subkeys.reshape(2 * PEER_HEADS, PEER_N_KEYS, half).astype(BF16))
    return x1, h2, idx.T, gw


def _layer(x, mem, *params_and_tables):
    uv, out_g = params_and_tables[-2:]
    b, t, d = x.shape
    tt = 128
    x1, h2, idx_tok, gw = _mixers(x, mem, *params_and_tables[:-2])
    n, slots = idx_tok.shape
    n_tiles = n // tt
    idx_tiles = idx_tok.reshape(n_tiles, tt * slots)
    chunk = SC_CHUNK_TILES
    sc_first = n_tiles - SC_CHUNKS * chunk
    lead = sc_first - SC_CHUNKS * chunk
    assert lead > 0

    def stage(c):
        lo = (sc_first + c * chunk) * tt
        slabs = _sc_gather_slabs(uv, idx_tok[lo:lo + chunk * tt].reshape(-1))
        return slabs.reshape((chunk * tt, slots) + uv.shape[1:])

    staged = [stage(c) for c in range(SC_CHUNKS)]
    ffn = functools.partial(_peer_ffn, idx_tiles, gw=gw, h2=h2, x1=x1, final_g=out_g, tt=tt)
    own = ffn(tables=[uv], first_tiles=[0], n_tiles=lead, staged=[False])
    theirs = []
    for c in range(SC_CHUNKS):
        mine, other = ffn(tables=[uv, staged[c]], first_tiles=[lead + c * chunk, sc_first + c * chunk],
                          n_tiles=chunk, staged=[False, True])
        own.append(mine)
        theirs.append(other)
    return jnp.concatenate(own + theirs, axis=0).reshape(b, t, d)


def kernel(x, mem, mix_norm_g, mem_norm_g, w_in, b_merge, cmp_pe_k, cmp_pe_v, cmp_w_k, cmp_w_v, w_mem_kv, w_sb_br, w_nsa_br, w_mem_br, w_out, ffn_norm_g, peer_w_q, peer_subkeys, peer_u, peer_v, final_norm_g):
    depth = w_in.shape[0]
    assert depth == 1, "the final rmsnorm is fused into the last layer's PEER kernel"
    l = 0
    uv = _pack_tables(peer_u[l], peer_v[l])
    return _layer(x, mem, mix_norm_g[l], mem_norm_g[l], w_in[l], b_merge[l], cmp_pe_k[l], cmp_pe_v[l],
                  cmp_w_k[l], cmp_w_v[l], w_mem_kv[l], w_sb_br[l], w_nsa_br[l], w_mem_br[l], w_out[l],
                  ffn_norm_g[l], peer_w_q[l], peer_subkeys[l], uv, final_norm_g)
```
